```python
import math
import jax
import jax.numpy as jnp
from jax import lax
import numpy as np

D_MODEL = 1024
BATCH = 4
SEQ = 8192
DEPTH = 1
DEC_BATCH = 8
DEC_SEQ = 32
PAST_LEN = 4096

CHUNK = 64
EPS = 1e-6
NEG = -1e30

D_CONV = D_MODEL
CONV_WIDTH = 3
N_HEADS = 16
N_KV_HEADS = 2
HEAD_DIM = 64
Q_PER_KV = N_HEADS // N_KV_HEADS
D_ATTN = N_HEADS * HEAD_DIM
D_KV = N_KV_HEADS * HEAD_DIM
WINDOW = 128
WIN_CHUNKS = WINDOW // CHUNK
N_BRANCH = 2
IN_COLS = 3 * D_CONV + D_ATTN + 2 * D_KV + N_BRANCH * D_MODEL
IN_SPLITS = (D_CONV, 2 * D_CONV, 3 * D_CONV, 3 * D_CONV + D_ATTN,
             3 * D_CONV + D_ATTN + D_KV, 3 * D_CONV + D_ATTN + 2 * D_KV)
N_GROUPS = 4
EXPERTS_PER_GROUP = 8
N_EXPERTS = N_GROUPS * EXPERTS_PER_GROUP
TOP_K = 2
D_EXPERT = 256

kernel_name = 'hybrid_conv_swa_hiermoe_stream_step'


def rms_norm(x, g):
    xf = x.astype(jnp.float32)
    y = xf * lax.rsqrt(jnp.mean(xf * xf, axis=-1, keepdims=True) + EPS)
    return (y * g.astype(jnp.float32)).astype(x.dtype)


def mixer_inputs(x, norm_g, w_in, q_g, k_g):
    n, s = x.shape[0], x.shape[1]
    h = rms_norm(x, norm_g)
    z = h @ w_in
    b, c, xc, q, k, v, gl = jnp.split(z, IN_SPLITS, axis=-1)
    q = rms_norm(q.reshape(n, s, N_HEADS, HEAD_DIM), q_g)
    k = rms_norm(k.reshape(n, s, N_KV_HEADS, HEAD_DIM), k_g)
    v = v.reshape(n, s, N_KV_HEADS, HEAD_DIM)
    return b, c * xc, q, k, v, gl


def causal_short_conv(u_ext, w):
    s = u_ext.shape[1] - (CONV_WIDTH - 1)
    y = u_ext[:, 0:s] * w[0]
    for j in range(1, CONV_WIDTH):
        y = y + u_ext[:, j:j + s] * w[j]
    return y


def sink_probs(s, sink):
    sk = sink.astype(jnp.float32).reshape(N_KV_HEADS, Q_PER_KV, 1, 1)
    m = jnp.maximum(jnp.max(s, axis=-1, keepdims=True), sk)
    p = jnp.exp(s - m)
    return p / (jnp.sum(p, axis=-1, keepdims=True) + jnp.exp(sk - m))


def band_attention(q, k, v, sink):
    n, s = q.shape[0], q.shape[1]
    nc = s // CHUNK
    qc = q.reshape(n, nc, CHUNK, N_KV_HEADS, Q_PER_KV, HEAD_DIM).astype(jnp.float32)

    def band(t):
        tc = t.reshape(n, nc, CHUNK, N_KV_HEADS, HEAD_DIM).astype(jnp.float32)
        tp = jnp.pad(tc, ((0, 0), (WIN_CHUNKS, 0), (0, 0), (0, 0), (0, 0)))
        return jnp.concatenate([tp[:, j:j + nc] for j in range(WIN_CHUNKS + 1)], axis=2)

    kb, vb = band(k), band(v)
    sc = jnp.einsum('bcqgrd,bckgd->bcgrqk', qc, kb) * (1.0 / math.sqrt(HEAD_DIM))
    key_chunk = (jnp.arange(nc)[:, None]
                 + jnp.repeat(jnp.arange(WIN_CHUNKS + 1), CHUNK)[None, :] - WIN_CHUNKS)
    valid = key_chunk >= 0
    sc = jnp.where(valid[None, :, None, None, None, :], sc, NEG)
    p = sink_probs(sc, sink)
    o = jnp.einsum('bcgrqk,bckgd->bcqgrd', p, vb)
    return o.reshape(n, s, N_HEADS, HEAD_DIM)


def window_attention_step(q, k_all, v_all, sink):
    n, s = q.shape[0], q.shape[1]
    qs = q.reshape(n, s, N_KV_HEADS, Q_PER_KV, HEAD_DIM).astype(jnp.float32)
    sc = jnp.einsum('bqgrd,bkgd->bgrqk', qs, k_all.astype(jnp.float32)) * (1.0 / math.sqrt(HEAD_DIM))
    p = sink_probs(sc, sink)
    o = jnp.einsum('bgrqk,bkgd->bqgrd', p, v_all.astype(jnp.float32))
    return o.reshape(n, s, N_HEADS, HEAD_DIM)


def hier_moe(h, w_group, b_group, w_router, b_router, w_gate, w_up, w_down):
    hf = h.astype(jnp.float32)
    g_logits = hf @ w_group.astype(jnp.float32) + b_group.astype(jnp.float32)
    g_w = jnp.max(jax.nn.softmax(g_logits, axis=-1), axis=-1, keepdims=True)
    g_sel = jnp.argmax(g_logits, axis=-1)
    e_logits = (hf @ w_router.astype(jnp.float32) + b_router.astype(jnp.float32)).reshape(
        h.shape[0], h.shape[1], N_GROUPS, EXPERTS_PER_GROUP)
    e_sel = jnp.sum(e_logits * jax.nn.one_hot(g_sel, N_GROUPS, dtype=jnp.float32)[..., None], axis=-2)
    top_v, top_i = lax.top_k(e_sel, TOP_K)
    top_w = jax.nn.softmax(top_v, axis=-1) * g_w
    expert_id = g_sel[..., None] * EXPERTS_PER_GROUP + top_i
    combine = jnp.sum(jax.nn.one_hot(expert_id, N_EXPERTS, dtype=jnp.float32) * top_w[..., None],
                      axis=-2)

    def one_stream(args):
        hs, cs = args
        gt = jnp.einsum('sd,edf->sef', hs, w_gate)
        up = jnp.einsum('sd,edf->sef', hs, w_up)
        hid = jax.nn.silu(gt) * up * cs[..., None].astype(hs.dtype)
        return jnp.einsum('sef,efd->sd', hid, w_down)

    return lax.map(one_stream, (h, combine))


def merge_and_channel_mix(x, b, conv, att, gl, w_conv_out, w_attn_out, w_out, norm_ffn_g,
                          w_group, b_group, w_router, b_router, w_gate, w_up, w_down):
    n, s = x.shape[0], x.shape[1]
    o_c = (b * conv) @ w_conv_out
    o_a = att.astype(x.dtype).reshape(n, s, D_ATTN) @ w_attn_out
    g = jax.nn.sigmoid(gl.astype(jnp.float32)).astype(x.dtype)
    g_c, g_a = jnp.split(g, N_BRANCH, axis=-1)
    x = x + (g_c * o_c + g_a * o_a) @ w_out
    h2 = rms_norm(x, norm_ffn_g)
    return x + hier_moe(h2, w_group, b_group, w_router, b_router, w_gate, w_up, w_down)


def setup_inputs(seed: int = 0) -> dict:
    key = jax.random.key(seed)
    ks = jax.random.split(key, 24)
    f32 = jnp.float32

    def nrm(k, shape, scale):
        return jax.random.normal(k, shape, f32) * scale

    return {
        'x_prompt': nrm(ks[0], (BATCH, SEQ, D_MODEL), 1.0),
        'x_sample': nrm(ks[1], (DEC_BATCH, DEC_SEQ, D_MODEL), 1.0),
        'cache_k': nrm(ks[2], (DEPTH, DEC_BATCH, WINDOW, N_KV_HEADS, HEAD_DIM), 1.0),
        'cache_v': nrm(ks[3], (DEPTH, DEC_BATCH, WINDOW, N_KV_HEADS, HEAD_DIM), 1.0),
        'state_conv': nrm(ks[4], (DEPTH, DEC_BATCH, CONV_WIDTH - 1, D_CONV), 1.0),
        'norm_mix_g': 1.0 + nrm(ks[5], (DEPTH, D_MODEL), 0.02),
        'w_in': nrm(ks[6], (DEPTH, D_MODEL, IN_COLS), D_MODEL ** -0.5),
        'conv_w': nrm(ks[7], (DEPTH, CONV_WIDTH, D_CONV), CONV_WIDTH ** -0.5),
        'q_norm_g': 1.0 + nrm(ks[8], (DEPTH, HEAD_DIM), 0.02),
        'k_norm_g': 1.0 + nrm(ks[9], (DEPTH, HEAD_DIM), 0.02),
        'attn_sinks': nrm(ks[10], (DEPTH, N_HEADS), 1.0),
        'w_conv_out': nrm(ks[11], (DEPTH, D_CONV, D_MODEL), D_CONV ** -0.5),
        'w_attn_out': nrm(ks[12], (DEPTH, D_ATTN, D_MODEL), D_ATTN ** -0.5),
        'w_out': nrm(ks[13], (DEPTH, D_MODEL, D_MODEL), D_MODEL ** -0.5),
        'norm_ffn_g': 1.0 + nrm(ks[14], (DEPTH, D_MODEL), 0.02),
        'w_group': nrm(ks[15], (DEPTH, D_MODEL, N_GROUPS), D_MODEL ** -0.5),
        'b_group': nrm(ks[16], (DEPTH, N_GROUPS), 0.01),
        'w_router': nrm(ks[17], (DEPTH, D_MODEL, N_EXPERTS), D_MODEL ** -0.5),
        'b_router': nrm(ks[18], (DEPTH, N_EXPERTS), 0.01),
        'w_gate': nrm(ks[19], (DEPTH, N_EXPERTS, D_MODEL, D_EXPERT), D_MODEL ** -0.5),
        'w_up': nrm(ks[20], (DEPTH, N_EXPERTS, D_MODEL, D_EXPERT), D_MODEL ** -0.5),
        'w_down': nrm(ks[21], (DEPTH, N_EXPERTS, D_EXPERT, D_MODEL), D_EXPERT ** -0.5),
    }


def reference(x_prompt, x_sample, cache_k, cache_v, state_conv,
              norm_mix_g, w_in, conv_w, q_norm_g, k_norm_g, attn_sinks,
              w_conv_out, w_attn_out, w_out, norm_ffn_g,
              w_group, b_group, w_router, b_router, w_gate, w_up, w_down):
    xp, xs = x_prompt, x_sample
    kp_l, vp_l, cp_l, ks_l, vs_l, cs_l = [], [], [], [], [], []
    for l in range(DEPTH):
        tail = (w_conv_out[l], w_attn_out[l], w_out[l], norm_ffn_g[l],
                w_group[l], b_group[l], w_router[l], b_router[l],
                w_gate[l], w_up[l], w_down[l])

        b, u, q, k, v, gl = mixer_inputs(xp, norm_mix_g[l], w_in[l], q_norm_g[l], k_norm_g[l])
        u_ext = jnp.pad(u, ((0, 0), (CONV_WIDTH - 1, 0), (0, 0)))
        conv = causal_short_conv(u_ext, conv_w[l])
        att = band_attention(q, k, v, attn_sinks[l])
        xp = merge_and_channel_mix(xp, b, conv, att, gl, *tail)
        kp_l.append(k[:, -WINDOW:])
        vp_l.append(v[:, -WINDOW:])
        cp_l.append(u_ext[:, -(CONV_WIDTH - 1):])

        b, u, q, k, v, gl = mixer_inputs(xs, norm_mix_g[l], w_in[l], q_norm_g[l], k_norm_g[l])
        u_ext = jnp.concatenate([state_conv[l].astype(u.dtype), u], axis=1)
        conv = causal_short_conv(u_ext, conv_w[l])
        k_all = jnp.concatenate([cache_k[l].astype(k.dtype), k], axis=1)
        v_all = jnp.concatenate([cache_v[l].astype(v.dtype), v], axis=1)
        att = window_attention_step(q, k_all, v_all, attn_sinks[l])
        xs = merge_and_channel_mix(xs, b, conv, att, gl, *tail)
        ks_l.append(k_all[:, -WINDOW:])
        vs_l.append(v_all[:, -WINDOW:])
        cs_l.append(u_ext[:, -(CONV_WIDTH - 1):])

    return (xp, xs,
            jnp.stack(kp_l), jnp.stack(vp_l), jnp.stack(cp_l),
            jnp.stack(ks_l), jnp.stack(vs_l), jnp.stack(cs_l))
```

```python
import functools
import math

import jax
import jax.numpy as jnp
from jax import lax
from jax.experimental import pallas as pl
from jax.experimental.pallas import tpu as pltpu

D = 1024
CHUNK = 64
HEAD_DIM = 64
N_HEADS = 16
N_KV = 2
WINDOW = 128
N_GROUPS = 4
EPG = 8
N_EXPERTS = 32
D_EXPERT = 256
EPS = 1e-6
NEG = -1e30

LANES = 128
QUAD = 4
QUAD_W = QUAD * HEAD_DIM
SLOTS = 4
KEYS = SLOTS * CHUNK
RT_W = LANES

C_B, C_C, C_X, C_Q, C_K, C_V, C_G = 0, 1024, 2048, 3072, 4096, 4224, 4352

VMEM_LIMIT = 60 * 1024 * 1024

_NT = (((1,), (1,)), ((), ()))


def _bf(x):
    return x.astype(jnp.bfloat16)


def _dot(a, b):
    return jnp.dot(a, b, preferred_element_type=jnp.float32)


def _rms(x, g):
    return x * lax.rsqrt(jnp.mean(x * x, axis=-1, keepdims=True) + EPS) * g


def _sigmoid(x):
    return 1.0 / (1.0 + jnp.exp(-x))


def _head_norm_q(q, seg_ref, exp_ref, qg):
    ssq = _dot(_bf(q * q), seg_ref[...])
    inv = lax.rsqrt(ssq * (1.0 / HEAD_DIM) + EPS)
    hi = _bf(inv)
    lo = _bf(inv - hi.astype(jnp.float32))
    full = _dot(jnp.concatenate([hi, lo], axis=-1), exp_ref[...])
    return q * full * qg


def _head_norm_k(k, kg):
    lane = lax.broadcasted_iota(jnp.int32, k.shape, 1)
    lo_half = lane < HEAD_DIM
    sq = k * k
    s0 = jnp.sum(jnp.where(lo_half, sq, 0.0), axis=-1, keepdims=True)
    s1 = jnp.sum(jnp.where(lo_half, 0.0, sq), axis=-1, keepdims=True)
    i0 = lax.rsqrt(s0 * (1.0 / HEAD_DIM) + EPS)
    i1 = lax.rsqrt(s1 * (1.0 / HEAD_DIM) + EPS)
    return k * jnp.where(lo_half, i0, i1) * kg


def _rep4(x):
    lane = lax.broadcasted_iota(jnp.int32, x.shape, 1)
    sw = pltpu.roll(x, HEAD_DIM, axis=1)
    a2 = jnp.where(lane < HEAD_DIM, x, sw)
    b2 = jnp.where(lane < HEAD_DIM, sw, x)
    return jnp.concatenate([a2, a2], axis=-1), jnp.concatenate([b2, b2], axis=-1)


def _write_bd(ref, g, row0, x4):
    rows = x4.shape[0]
    lane = lax.broadcasted_iota(jnp.int32, x4.shape, 1)
    for i in range(QUAD):
        keep = (lane >= i * HEAD_DIM) & (lane < (i + 1) * HEAD_DIM)
        ref[g, pl.ds(i * KEYS + row0, rows), :] = _bf(jnp.where(keep, x4, 0.0))


def _attend(qq, kbd, vbd, valid, sinks):
    s = lax.dot_general(qq, kbd, _NT, preferred_element_type=jnp.float32)
    s = jnp.where(valid, s, NEG)
    ps = []
    for i in range(QUAD):
        sh = s[:, i * KEYS:(i + 1) * KEYS]
        m = jnp.maximum(jnp.max(sh, axis=-1, keepdims=True), sinks[i])
        p = jnp.exp(sh - m)
        den = jnp.sum(p, axis=-1, keepdims=True) + jnp.exp(sinks[i] - m)
        ps.append(_bf(p / den))
    return _dot(jnp.concatenate(ps, axis=-1), vbd)


def _route(logits):
    lane = lax.broadcasted_iota(jnp.int32, logits.shape, 1)
    lane_f = lane.astype(jnp.float32)
    big = float(RT_W)
    is_g = (lane >= N_EXPERTS) & (lane < N_EXPERTS + N_GROUPS)
    gl = jnp.where(is_g, logits, NEG)
    gmax = jnp.max(gl, axis=-1, keepdims=True)
    gsum = jnp.sum(jnp.where(is_g, jnp.exp(gl - gmax), 0.0), axis=-1, keepdims=True)
    g_w = 1.0 / gsum
    g_sel = jnp.min(jnp.where(is_g & (gl == gmax), lane_f, big), axis=-1, keepdims=True) - N_EXPERTS
    lo = g_sel * EPG
    in_grp = (lane_f >= lo) & (lane_f < lo + EPG)
    el = jnp.where(in_grp, logits, NEG)
    v1 = jnp.max(el, axis=-1, keepdims=True)
    i1 = jnp.min(jnp.where(in_grp & (el == v1), lane_f, big), axis=-1, keepdims=True)
    rest = in_grp & (lane_f != i1)
    el2 = jnp.where(rest, logits, NEG)
    v2 = jnp.max(el2, axis=-1, keepdims=True)
    i2 = jnp.min(jnp.where(rest & (el2 == v2), lane_f, big), axis=-1, keepdims=True)
    e2 = jnp.exp(v2 - v1)
    den = 1.0 + e2
    w1 = (1.0 / den) * g_w
    w2 = (e2 / den) * g_w
    return jnp.where(lane_f == i1, w1, jnp.where(lane_f == i2, w2, 0.0))


def _merge_tail(x, macc, wo_ref, g2_ref, wr_ref, br_ref, x1_ref, h2_ref, comb_ref):
    x1 = x + _dot(_bf(macc), wo_ref[...])
    x1_ref[...] = x1
    h2 = _rms(x1, g2_ref[...])
    h2_ref[...] = _bf(h2)
    logits = _dot(_bf(h2), wr_ref[...]) + br_ref[...]
    comb_ref[...] = _route(logits)


def _prompt_mixer_kernel(sink_ref, x_ref, g1_ref, win_ref, cw_ref, qg_ref, kg_ref, seg_ref, exp_ref,
                         wco_ref, wao_ref, wo_ref, g2_ref, wr_ref, br_ref,
                         x1_ref, h2_ref, comb_ref, kwin_ref, vwin_ref, cst_ref,
                         uext, qn_s, att_s, kn_s, kbd, vbd, *, tile):
    b = pl.program_id(0)
    it = pl.program_id(1)
    n_it = pl.num_programs(1)
    n_chunk = tile // CHUNK

    @pl.when((b == 0) & (it == 0))
    def _():
        kbd[...] = jnp.zeros_like(kbd)
        vbd[...] = jnp.zeros_like(vbd)

    @pl.when(it == 0)
    def _():
        uext[0:8, :] = jnp.zeros((8, D), jnp.float32)

    x = x_ref[0]
    h = _bf(_rms(x, g1_ref[...]))

    u = _dot(h, win_ref[:, C_C:C_C + D]) * _dot(h, win_ref[:, C_X:C_X + D])
    uext[8:8 + tile, :] = u
    conv = (uext[6:6 + tile, :] * cw_ref[0:1, :] + uext[7:7 + tile, :] * cw_ref[1:2, :]
            + u * cw_ref[2:3, :])
    yc = _bf(_dot(h, win_ref[:, C_B:C_B + D]) * conv)
    o_c = _dot(yc, wco_ref[...])
    macc = _sigmoid(_dot(h, win_ref[:, C_G:C_G + D])) * o_c
    last2 = uext[tile + 6:tile + 8, :]
    uext[6:8, :] = last2

    @pl.when(it == n_it - 1)
    def _():
        cst_ref[0] = last2

    q = _dot(h, win_ref[:, C_Q:C_Q + D])
    qn_s[...] = _bf(_head_norm_q(q, seg_ref, exp_ref, qg_ref[...]) * (1.0 / math.sqrt(HEAD_DIM)))
    kv = _dot(h, win_ref[:, C_K:C_K + 2 * LANES])
    kn = _head_norm_k(kv[:, 0:LANES], kg_ref[...])
    vv = kv[:, LANES:2 * LANES]
    kn_s[0] = kn
    kn_s[1] = vv

    @pl.when(it == n_it - 1)
    def _():
        kwin_ref[0] = kn[tile - WINDOW:, :]
        vwin_ref[0] = vv[tile - WINDOW:, :]

    lane = lax.broadcasted_iota(jnp.int32, (1, QUAD * KEYS), 1)
    slot_of_lane = (lane % KEYS) // CHUNK

    def chunk_body(c, carry):
        r0 = pl.multiple_of(c * CHUNK, CHUNK)
        slot = c % SLOTS
        row0 = pl.multiple_of(slot * CHUNK, CHUNK)
        k4 = _rep4(kn_s[0, pl.ds(r0, CHUNK), :])
        v4 = _rep4(kn_s[1, pl.ds(r0, CHUNK), :])
        for g in range(N_KV):
            _write_bd(kbd, g, row0, k4[g])
            _write_bd(vbd, g, row0, v4[g])
        cg = it * n_chunk + c
        dist = (cg - slot_of_lane) & (SLOTS - 1)
        valid = dist <= jnp.minimum(cg, SLOTS - 2)
        for g in range(N_KV):
            kb = kbd[g]
            vb = vbd[g]
            for j in range(N_HEADS // N_KV // QUAD):
                h0 = g * (N_HEADS // N_KV) + j * QUAD
                qq = qn_s[pl.ds(r0, CHUNK), h0 * HEAD_DIM:(h0 + QUAD) * HEAD_DIM]
                sinks = [sink_ref[h0 + i] for i in range(QUAD)]
                o = _attend(qq, kb, vb, valid, sinks)
                att_s[pl.ds(r0, CHUNK), h0 * HEAD_DIM:(h0 + QUAD) * HEAD_DIM] = _bf(o)
        return carry

    lax.fori_loop(0, n_chunk, chunk_body, 0)

    o_a = _dot(att_s[...], wao_ref[...])
    macc = macc + _sigmoid(_dot(h, win_ref[:, C_G + D:C_G + 2 * D])) * o_a
    _merge_tail(x, macc, wo_ref, g2_ref, wr_ref, br_ref,
                x1_ref.at[0], h2_ref.at[0], comb_ref.at[0])


def _const_spec(shape):
    nd = len(shape)
    return pl.BlockSpec(shape, lambda *_: (0,) * nd, pipeline_mode=pl.Buffered(1))


def _prompt_mixer(x, sinks, wts, tile):
    nb, seq, _ = x.shape
    n_it = seq // tile
    (g1, win, cw, qg, kg, seg, expm, wco, wao, wo, g2, wr, br) = wts
    row_spec = lambda w: pl.BlockSpec((1, tile, w), lambda b, i, *_: (b, i, 0))
    str_spec = lambda r, w: pl.BlockSpec((1, r, w), lambda b, i, *_: (b, 0, 0))
    grid_spec = pltpu.PrefetchScalarGridSpec(
        num_scalar_prefetch=1,
        grid=(nb, n_it),
        in_specs=[row_spec(D)] + [_const_spec(w.shape) for w in wts],
        out_specs=[row_spec(D), row_spec(D), row_spec(RT_W),
                   str_spec(WINDOW, LANES), str_spec(WINDOW, LANES), str_spec(2, D)],
        scratch_shapes=[
            pltpu.VMEM((tile + 8, D), jnp.float32),
            pltpu.VMEM((tile, D), jnp.bfloat16),
            pltpu.VMEM((tile, D), jnp.bfloat16),
            pltpu.VMEM((2, tile, LANES), jnp.float32),
            pltpu.VMEM((N_KV, QUAD * KEYS, QUAD_W), jnp.bfloat16),
            pltpu.VMEM((N_KV, QUAD * KEYS, QUAD_W), jnp.bfloat16),
        ])
    out_shape = [
        jax.ShapeDtypeStruct((nb, seq, D), jnp.float32),
        jax.ShapeDtypeStruct((nb, seq, D), jnp.bfloat16),
        jax.ShapeDtypeStruct((nb, seq, RT_W), jnp.float32),
        jax.ShapeDtypeStruct((nb, WINDOW, LANES), jnp.float32),
        jax.ShapeDtypeStruct((nb, WINDOW, LANES), jnp.float32),
        jax.ShapeDtypeStruct((nb, 2, D), jnp.float32),
    ]
    return pl.pallas_call(
        functools.partial(_prompt_mixer_kernel, tile=tile),
        grid_spec=grid_spec,
        out_shape=out_shape,
        compiler_params=pltpu.CompilerParams(
            dimension_semantics=("arbitrary", "arbitrary"), vmem_limit_bytes=VMEM_LIMIT),
        name="prompt_mixer",
    )(sinks, x, *wts)


def _sample_mixer_kernel(sink_ref, x_ref, ck_ref, cv_ref, sc_ref, g1_ref, win_ref, cw_ref, qg_ref,
                         kg_ref, seg_ref, exp_ref, wco_ref, wao_ref, wo_ref, g2_ref, wr_ref, br_ref,
                         x1_ref, h2_ref, comb_ref, kwin_ref, vwin_ref, cst_ref,
                         uext, att_s, kbd, vbd, *, n_stream, seq):
    ext = seq + 8
    x = x_ref[...]
    h = _bf(_rms(x, g1_ref[...]))

    u = _dot(h, win_ref[:, C_C:C_C + D]) * _dot(h, win_ref[:, C_X:C_X + D])
    for s in range(n_stream):
        uext[s * ext + 6:s * ext + 8, :] = sc_ref[s]
        uext[s * ext + 8:(s + 1) * ext, :] = u[s * seq:(s + 1) * seq, :]
        cst_ref[s] = u[(s + 1) * seq - 2:(s + 1) * seq, :]
    conv = jnp.concatenate(
        [uext[s * ext + 6:s * ext + 6 + seq, :] * cw_ref[0:1, :]
         + uext[s * ext + 7:s * ext + 7 + seq, :] * cw_ref[1:2, :] for s in range(n_stream)],
        axis=0) + u * cw_ref[2:3, :]
    yc = _bf(_dot(h, win_ref[:, C_B:C_B + D]) * conv)
    o_c = _dot(yc, wco_ref[...])
    macc = _sigmoid(_dot(h, win_ref[:, C_G:C_G + D])) * o_c

    q = _dot(h, win_ref[:, C_Q:C_Q + D])
    qn = _bf(_head_norm_q(q, seg_ref, exp_ref, qg_ref[...]) * (1.0 / math.sqrt(HEAD_DIM)))
    kv = _dot(h, win_ref[:, C_K:C_K + 2 * LANES])
    kn = _head_norm_k(kv[:, 0:LANES], kg_ref[...])
    vv = kv[:, LANES:2 * LANES]

    kbd[...] = jnp.zeros_like(kbd)
    vbd[...] = jnp.zeros_like(vbd)
    lane = lax.broadcasted_iota(jnp.int32, (1, QUAD * KEYS), 1)
    valid = (lane % KEYS) < WINDOW + seq
    for s in range(n_stream):
        rows = slice(s * seq, (s + 1) * seq)
        kwin_ref[s, 0:WINDOW - seq, :] = ck_ref[s, seq:WINDOW, :]
        kwin_ref[s, WINDOW - seq:WINDOW, :] = kn[rows, :]
        vwin_ref[s, 0:WINDOW - seq, :] = cv_ref[s, seq:WINDOW, :]
        vwin_ref[s, WINDOW - seq:WINDOW, :] = vv[rows, :]
        kc4 = _rep4(ck_ref[s])
        vc4 = _rep4(cv_ref[s])
        kn4 = _rep4(kn[rows, :])
        vn4 = _rep4(vv[rows, :])
        for g in range(N_KV):
            _write_bd(kbd, g, 0, kc4[g])
            _write_bd(kbd, g, WINDOW, kn4[g])
            _write_bd(vbd, g, 0, vc4[g])
            _write_bd(vbd, g, WINDOW, vn4[g])
        for g in range(N_KV):
            kb = kbd[g]
            vb = vbd[g]
            for j in range(N_HEADS // N_KV // QUAD):
                h0 = g * (N_HEADS // N_KV) + j * QUAD
                qq = qn[rows, h0 * HEAD_DIM:(h0 + QUAD) * HEAD_DIM]
                sinks = [sink_ref[h0 + i] for i in range(QUAD)]
                o = _attend(qq, kb, vb, valid, sinks)
                att_s[rows, h0 * HEAD_DIM:(h0 + QUAD) * HEAD_DIM] = _bf(o)

    o_a = _dot(att_s[...], wao_ref[...])
    macc = macc + _sigmoid(_dot(h, win_ref[:, C_G + D:C_G + 2 * D])) * o_a
    _merge_tail(x, macc, wo_ref, g2_ref, wr_ref, br_ref, x1_ref, h2_ref, comb_ref)


def _sample_mixer(x, ck, cv, sc, sinks, wts):
    nb, seq, _ = x.shape
    rows = nb * seq
    x2 = x.reshape(rows, D)
    ins = (x2, ck, cv, sc) + tuple(wts)
    full = lambda a: pl.BlockSpec(a.shape, lambda *_, nd=a.ndim: (0,) * nd)
    out_shape = [
        jax.ShapeDtypeStruct((rows, D), jnp.float32),
        jax.ShapeDtypeStruct((rows, D), jnp.bfloat16),
        jax.ShapeDtypeStruct((rows, RT_W), jnp.float32),
        jax.ShapeDtypeStruct((nb, WINDOW, LANES), jnp.float32),
        jax.ShapeDtypeStruct((nb, WINDOW, LANES), jnp.float32),
        jax.ShapeDtypeStruct((nb, 2, D), jnp.float32),
    ]
    grid_spec = pltpu.PrefetchScalarGridSpec(
        num_scalar_prefetch=1,
        grid=(1,),
        in_specs=[full(a) for a in ins],
        out_specs=[full(a) for a in out_shape],
        scratch_shapes=[
            pltpu.VMEM((nb * (seq + 8), D), jnp.float32),
            pltpu.VMEM((rows, D), jnp.bfloat16),
            pltpu.VMEM((N_KV, QUAD * KEYS, QUAD_W), jnp.bfloat16),
            pltpu.VMEM((N_KV, QUAD * KEYS, QUAD_W), jnp.bfloat16),
        ])
    return pl.pallas_call(
        functools.partial(_sample_mixer_kernel, n_stream=nb, seq=seq),
        grid_spec=grid_spec,
        out_shape=out_shape,
        compiler_params=pltpu.CompilerParams(
            dimension_semantics=("arbitrary",), vmem_limit_bytes=VMEM_LIMIT),
        name="sample_mixer",
    )(sinks, *ins)


def _moe_kernel(x1_ref, h2_ref, comb_ref, wg_ref, wu_ref, wd_ref, y_ref, acc):
    e = pl.program_id(1)

    @pl.when(e == 0)
    def _():
        acc[...] = x1_ref[...]

    h2 = h2_ref[...]
    comb = comb_ref[...]
    lane = lax.broadcasted_iota(jnp.int32, comb.shape, 1)
    ce = jnp.sum(jnp.where(lane == e, comb, 0.0), axis=-1, keepdims=True)
    gt = _dot(h2, wg_ref[0])
    up = _dot(h2, wu_ref[0])
    hid = gt * _sigmoid(gt) * up * ce
    acc[...] += _dot(_bf(hid), wd_ref[0])

    @pl.when(e == pl.num_programs(1) - 1)
    def _():
        y_ref[...] = acc[...]


def _moe(x1, h2, comb, wg, wu, wd, tile):
    n = x1.shape[0]
    tok = lambda w: pl.BlockSpec((tile, w), lambda t, e: (t, 0))
    return pl.pallas_call(
        _moe_kernel,
        grid=(n // tile, N_EXPERTS),
        in_specs=[tok(D), tok(D), tok(RT_W),
                  pl.BlockSpec((1, D, D_EXPERT), lambda t, e: (e, 0, 0)),
                  pl.BlockSpec((1, D, D_EXPERT), lambda t, e: (e, 0, 0)),
                  pl.BlockSpec((1, D_EXPERT, D), lambda t, e: (e, 0, 0))],
        out_specs=tok(D),
        out_shape=jax.ShapeDtypeStruct((n, D), jnp.float32),
        scratch_shapes=[pltpu.VMEM((tile, D), jnp.float32)],
        compiler_params=pltpu.CompilerParams(
            dimension_semantics=("arbitrary", "arbitrary"), vmem_limit_bytes=VMEM_LIMIT),
        name="moe",
    )(x1, h2, comb, wg, wu, wd)


def _segment_matrices():
    head = jnp.arange(D) // HEAD_DIM
    col = jnp.arange(LANES)
    seg = (head[:, None] == col[None, :]).astype(jnp.bfloat16)
    row = jnp.arange(2 * LANES) % LANES
    expm = (row[:, None] == head[None, :]).astype(jnp.bfloat16)
    return seg, expm


def kernel(x_prompt, x_sample, cache_k, cache_v, state_conv, norm_mix_g, w_in, conv_w, q_norm_g,
           k_norm_g, attn_sinks, w_conv_out, w_attn_out, w_out, norm_ffn_g, w_group, b_group,
           w_router, b_router, w_gate, w_up, w_down):
    nb, seq, _ = x_prompt.shape
    ns, sseq, _ = x_sample.shape
    l = 0
    seg, expm = _segment_matrices()
    wr = jnp.zeros((D, RT_W), jnp.float32)
    wr = wr.at[:, 0:N_EXPERTS].set(w_router[l]).at[:, N_EXPERTS:N_EXPERTS + N_GROUPS].set(w_group[l])
    br = jnp.zeros((1, RT_W), jnp.float32)
    br = br.at[0, 0:N_EXPERTS].set(b_router[l]).at[0, N_EXPERTS:N_EXPERTS + N_GROUPS].set(b_group[l])
    wts = (norm_mix_g[l].reshape(1, D), _bf(w_in[l]), conv_w[l],
           jnp.tile(q_norm_g[l], N_HEADS).reshape(1, D), jnp.tile(k_norm_g[l], N_KV).reshape(1, LANES),
           seg, expm, _bf(w_conv_out[l]), _bf(w_attn_out[l]), _bf(w_out[l]),
           norm_ffn_g[l].reshape(1, D), _bf(wr), br)
    sinks = attn_sinks[l]

    x1p, h2p, cbp, kwp, vwp, cstp = _prompt_mixer(x_prompt, sinks, wts, tile=512)
    x1s, h2s, cbs, kws, vws, csts = _sample_mixer(
        x_sample, cache_k[l].reshape(ns, WINDOW, LANES), cache_v[l].reshape(ns, WINDOW, LANES),
        state_conv[l], sinks, wts)

    wg, wu, wd = _bf(w_gate[l]), _bf(w_up[l]), _bf(w_down[l])
    yp = _moe(x1p.reshape(nb * seq, D), h2p.reshape(nb * seq, D), cbp.reshape(nb * seq, RT_W),
              wg, wu, wd, tile=2048)
    ys = _moe(x1s, h2s, cbs, wg, wu, wd, tile=ns * sseq)

    kv_shape = lambda n: (1, n, WINDOW, N_KV, HEAD_DIM)
    return (yp.reshape(nb, seq, D), ys.reshape(ns, sseq, D),
            kwp.reshape(kv_shape(nb)), vwp.reshape(kv_shape(nb)), cstp.reshape(1, nb, 2, D),
            kws.reshape(kv_shape(ns)), vws.reshape(kv_shape(ns)), csts.reshape(1, ns, 2, D))
```

```python
import functools
import math

import jax
import jax.numpy as jnp
from jax import lax
from jax.experimental import pallas as pl
from jax.experimental.pallas import tpu as pltpu

D = 1024
CHUNK = 64
HEAD_DIM = 64
N_HEADS = 16
N_KV = 2
WINDOW = 128
N_GROUPS = 4
EPG = 8
N_EXPERTS = 32
D_EXPERT = 256
EPS = 1e-6
NEG = -1e30

LANES = 128
QUAD = 4
QUAD_W = QUAD * HEAD_DIM
SLOTS = 4
KEYS = SLOTS * CHUNK
RT_W = LANES
HALF = D // 2
PK = HALF // LANES

R_E1, R_E2, R_W1, R_W2, R_RANK1, R_RANK2 = 0, 1, 2, 3, 4, 5

C_B, C_C, C_X, C_Q, C_K, C_V, C_G = 0, 1024, 2048, 3072, 4096, 4224, 4352

MOE_TILE = 256
MOE_PAD = 128
TOK_TILE = 512
BLOCK_TILES = 8
UNROLL = 16

VMEM_LIMIT = 60 * 1024 * 1024

_NT = (((1,), (1,)), ((), ()))


def _bf(x):
    return x.astype(jnp.bfloat16)


def _dot(a, b):
    return jnp.dot(a, b, preferred_element_type=jnp.float32)


def _rms(x, g):
    return x * lax.rsqrt(jnp.mean(x * x, axis=-1, keepdims=True) + EPS) * g


def _sigmoid(x):
    return 1.0 / (1.0 + jnp.exp(-x))


def _pack_rows(x):
    return pltpu.pack_elementwise([x[:, :HALF], x[:, HALF:]], packed_dtype=jnp.bfloat16)


def _unpack_rows(parts):
    def half(i):
        return [pltpu.unpack_elementwise(p, index=i, packed_dtype=jnp.bfloat16,
                                         unpacked_dtype=jnp.float32) for p in parts]
    return jnp.concatenate(half(0) + half(1), axis=-1)


def _head_norm_q(q, seg_ref, exp_ref, qg):
    ssq = _dot(_bf(q * q), seg_ref[...])
    inv = lax.rsqrt(ssq * (1.0 / HEAD_DIM) + EPS)
    hi = _bf(inv)
    lo = _bf(inv - hi.astype(jnp.float32))
    full = _dot(jnp.concatenate([hi, lo], axis=-1), exp_ref[...])
    return q * full * qg


def _head_norm_k(k, kg):
    lane = lax.broadcasted_iota(jnp.int32, k.shape, 1)
    lo_half = lane < HEAD_DIM
    sq = k * k
    s0 = jnp.sum(jnp.where(lo_half, sq, 0.0), axis=-1, keepdims=True)
    s1 = jnp.sum(jnp.where(lo_half, 0.0, sq), axis=-1, keepdims=True)
    i0 = lax.rsqrt(s0 * (1.0 / HEAD_DIM) + EPS)
    i1 = lax.rsqrt(s1 * (1.0 / HEAD_DIM) + EPS)
    return k * jnp.where(lo_half, i0, i1) * kg


def _rep4(x):
    lane = lax.broadcasted_iota(jnp.int32, x.shape, 1)
    sw = pltpu.roll(x, HEAD_DIM, axis=1)
    a2 = jnp.where(lane < HEAD_DIM, x, sw)
    b2 = jnp.where(lane < HEAD_DIM, sw, x)
    return jnp.concatenate([a2, a2], axis=-1), jnp.concatenate([b2, b2], axis=-1)


def _write_bd(ref, g, row0, x4):
    rows = x4.shape[0]
    lane = lax.broadcasted_iota(jnp.int32, x4.shape, 1)
    for i in range(QUAD):
        keep = (lane >= i * HEAD_DIM) & (lane < (i + 1) * HEAD_DIM)
        ref[g, pl.ds(i * KEYS + row0, rows), :] = _bf(jnp.where(keep, x4, 0.0))


def _attend(qq, kbd, vbd, valid, sinks):
    s = lax.dot_general(qq, kbd, _NT, preferred_element_type=jnp.float32)
    s = jnp.where(valid, s, NEG)
    ps = []
    for i in range(QUAD):
        sh = s[:, i * KEYS:(i + 1) * KEYS]
        m = jnp.maximum(jnp.max(sh, axis=-1, keepdims=True), sinks[i])
        p = jnp.exp(sh - m)
        den = jnp.sum(p, axis=-1, keepdims=True) + jnp.exp(sinks[i] - m)
        ps.append(_bf(p / den))
    return _dot(jnp.concatenate(ps, axis=-1), vbd)


def _route(logits, tri_ref, carry):
    lane = lax.broadcasted_iota(jnp.int32, logits.shape, 1)
    lane_f = lane.astype(jnp.float32)
    big = float(RT_W)
    is_g = (lane >= N_EXPERTS) & (lane < N_EXPERTS + N_GROUPS)
    gl = jnp.where(is_g, logits, NEG)
    gmax = jnp.max(gl, axis=-1, keepdims=True)
    gsum = jnp.sum(jnp.where(is_g, jnp.exp(gl - gmax), 0.0), axis=-1, keepdims=True)
    g_w = 1.0 / gsum
    g_sel = jnp.min(jnp.where(is_g & (gl == gmax), lane_f, big), axis=-1, keepdims=True) - N_EXPERTS
    lo = g_sel * EPG
    in_grp = (lane_f >= lo) & (lane_f < lo + EPG)
    el = jnp.where(in_grp, logits, NEG)
    v1 = jnp.max(el, axis=-1, keepdims=True)
    i1 = jnp.min(jnp.where(in_grp & (el == v1), lane_f, big), axis=-1, keepdims=True)
    rest = in_grp & (lane_f != i1)
    el2 = jnp.where(rest, logits, NEG)
    v2 = jnp.max(el2, axis=-1, keepdims=True)
    i2 = jnp.min(jnp.where(rest & (el2 == v2), lane_f, big), axis=-1, keepdims=True)
    e2 = jnp.exp(v2 - v1)
    den = 1.0 + e2
    w1 = (1.0 / den) * g_w
    w2 = (e2 / den) * g_w
    oh1 = jnp.where(lane_f == i1, 1.0, 0.0)
    oh2 = jnp.where(lane_f == i2, 1.0, 0.0)
    both = oh1 + oh2
    before = _dot(tri_ref[...], _bf(both)) + carry
    r1 = jnp.sum(before * oh1, axis=-1, keepdims=True)
    r2 = jnp.sum(before * oh2, axis=-1, keepdims=True)
    rec = jnp.zeros_like(logits)
    for ln, val in ((R_E1, i1), (R_E2, i2), (R_W1, w1), (R_W2, w2), (R_RANK1, r1), (R_RANK2, r2)):
        rec = jnp.where(lane == ln, val, rec)
    return rec, carry + jnp.sum(both, axis=0, keepdims=True)


def _merge_tail(x, macc, wo_ref, g2_ref, wr_ref, br_ref, tri_ref, carry, x1_ref, h2p_ref, rt_ref):
    rows = x.shape[0]
    x1 = x + _dot(_bf(macc), wo_ref[...])
    x1_ref[...] = x1
    h2 = _rms(x1, g2_ref[...])
    pk = _pack_rows(h2)
    for j in range(PK):
        h2p_ref[pl.ds(j, rows, stride=PK), :] = pk[:, j * LANES:(j + 1) * LANES]
    logits = _dot(_bf(h2), wr_ref[...]) + br_ref[...]
    rec, carry = _route(logits, tri_ref, carry)
    rt_ref[...] = rec
    return carry


def _prompt_mixer_kernel(sink_ref, x_ref, g1_ref, win_ref, cw_ref, qg_ref, kg_ref, seg_ref, exp_ref,
                         wco_ref, wao_ref, wo_ref, g2_ref, wr_ref, br_ref, tri_ref,
                         x1_ref, h2p_ref, rt_ref, cnt_ref, kwin_ref, vwin_ref, cst_ref,
                         uext, qn_s, att_s, kn_s, kbd, vbd, carry_s, *, tile, blk_tiles):
    b = pl.program_id(0)
    it = pl.program_id(1)
    n_it = pl.num_programs(1)
    n_chunk = tile // CHUNK

    @pl.when((b == 0) & (it == 0))
    def _():
        kbd[...] = jnp.zeros_like(kbd)
        vbd[...] = jnp.zeros_like(vbd)

    @pl.when(it == 0)
    def _():
        uext[0:8, :] = jnp.zeros((8, D), jnp.float32)

    @pl.when(it % blk_tiles == 0)
    def _():
        carry_s[...] = jnp.zeros_like(carry_s)

    x = x_ref[0]
    h = _bf(_rms(x, g1_ref[...]))

    u = _dot(h, win_ref[:, C_C:C_C + D]) * _dot(h, win_ref[:, C_X:C_X + D])
    uext[8:8 + tile, :] = u
    conv = (uext[6:6 + tile, :] * cw_ref[0:1, :] + uext[7:7 + tile, :] * cw_ref[1:2, :]
            + u * cw_ref[2:3, :])
    yc = _bf(_dot(h, win_ref[:, C_B:C_B + D]) * conv)
    o_c = _dot(yc, wco_ref[...])
    macc = _sigmoid(_dot(h, win_ref[:, C_G:C_G + D])) * o_c
    last2 = uext[tile + 6:tile + 8, :]
    uext[6:8, :] = last2

    @pl.when(it == n_it - 1)
    def _():
        cst_ref[0] = last2

    q = _dot(h, win_ref[:, C_Q:C_Q + D])
    qn_s[...] = _bf(_head_norm_q(q, seg_ref, exp_ref, qg_ref[...]) * (1.0 / math.sqrt(HEAD_DIM)))
    kv = _dot(h, win_ref[:, C_K:C_K + 2 * LANES])
    kn = _head_norm_k(kv[:, 0:LANES], kg_ref[...])
    vv = kv[:, LANES:2 * LANES]
    kn_s[0] = kn
    kn_s[1] = vv

    @pl.when(it == n_it - 1)
    def _():
        kwin_ref[0] = kn[tile - WINDOW:, :]
        vwin_ref[0] = vv[tile - WINDOW:, :]

    lane = lax.broadcasted_iota(jnp.int32, (1, QUAD * KEYS), 1)
    slot_of_lane = (lane % KEYS) // CHUNK

    def chunk_body(c, carry):
        r0 = pl.multiple_of(c * CHUNK, CHUNK)
        slot = c % SLOTS
        row0 = pl.multiple_of(slot * CHUNK, CHUNK)
        k4 = _rep4(kn_s[0, pl.ds(r0, CHUNK), :])
        v4 = _rep4(kn_s[1, pl.ds(r0, CHUNK), :])
        for g in range(N_KV):
            _write_bd(kbd, g, row0, k4[g])
            _write_bd(vbd, g, row0, v4[g])
        cg = it * n_chunk + c
        dist = (cg - slot_of_lane) & (SLOTS - 1)
        valid = dist <= jnp.minimum(cg, SLOTS - 2)
        for g in range(N_KV):
            kb = kbd[g]
            vb = vbd[g]
            for j in range(N_HEADS // N_KV // QUAD):
                h0 = g * (N_HEADS // N_KV) + j * QUAD
                qq = qn_s[pl.ds(r0, CHUNK), h0 * HEAD_DIM:(h0 + QUAD) * HEAD_DIM]
                sinks = [sink_ref[h0 + i] for i in range(QUAD)]
                o = _attend(qq, kb, vb, valid, sinks)
                att_s[pl.ds(r0, CHUNK), h0 * HEAD_DIM:(h0 + QUAD) * HEAD_DIM] = _bf(o)
        return carry

    lax.fori_loop(0, n_chunk, chunk_body, 0)

    o_a = _dot(att_s[...], wao_ref[...])
    macc = macc + _sigmoid(_dot(h, win_ref[:, C_G + D:C_G + 2 * D])) * o_a
    carry = _merge_tail(x, macc, wo_ref, g2_ref, wr_ref, br_ref, tri_ref, carry_s[...],
                        x1_ref.at[0], h2p_ref, rt_ref.at[0])
    carry_s[...] = carry

    @pl.when(it % blk_tiles == blk_tiles - 1)
    def _():
        cnt_ref[0] = jnp.broadcast_to(carry, (8, RT_W))


def _const_spec(shape):
    nd = len(shape)
    return pl.BlockSpec(shape, lambda *_: (0,) * nd, pipeline_mode=pl.Buffered(1))


def _prompt_mixer(x, sinks, wts, tile, blk_tiles):
    nb, seq, _ = x.shape
    n_it = seq // tile
    bps = n_it // blk_tiles
    row_spec = lambda w: pl.BlockSpec((1, tile, w), lambda b, i, *_: (b, i, 0))
    str_spec = lambda r, w: pl.BlockSpec((1, r, w), lambda b, i, *_: (b, 0, 0))
    grid_spec = pltpu.PrefetchScalarGridSpec(
        num_scalar_prefetch=1,
        grid=(nb, n_it),
        in_specs=[row_spec(D)] + [_const_spec(w.shape) for w in wts],
        out_specs=[row_spec(D),
                   pl.BlockSpec((tile * PK, LANES), lambda b, i, *_: (b * n_it + i, 0)),
                   row_spec(RT_W),
                   pl.BlockSpec((1, 8, RT_W), lambda b, i, *_: (b * bps + i // blk_tiles, 0, 0)),
                   str_spec(WINDOW, LANES), str_spec(WINDOW, LANES), str_spec(2, D)],
        scratch_shapes=[
            pltpu.VMEM((tile + 8, D), jnp.float32),
            pltpu.VMEM((tile, D), jnp.bfloat16),
            pltpu.VMEM((tile, D), jnp.bfloat16),
            pltpu.VMEM((2, tile, LANES), jnp.float32),
            pltpu.VMEM((N_KV, QUAD * KEYS, QUAD_W), jnp.bfloat16),
            pltpu.VMEM((N_KV, QUAD * KEYS, QUAD_W), jnp.bfloat16),
            pltpu.VMEM((1, RT_W), jnp.float32),
        ])
    out_shape = [
        jax.ShapeDtypeStruct((nb, seq, D), jnp.float32),
        jax.ShapeDtypeStruct((nb * seq * PK, LANES), jnp.uint32),
        jax.ShapeDtypeStruct((nb, seq, RT_W), jnp.float32),
        jax.ShapeDtypeStruct((nb * bps, 8, RT_W), jnp.float32),
        jax.ShapeDtypeStruct((nb, WINDOW, LANES), jnp.float32),
        jax.ShapeDtypeStruct((nb, WINDOW, LANES), jnp.float32),
        jax.ShapeDtypeStruct((nb, 2, D), jnp.float32),
    ]
    return pl.pallas_call(
        functools.partial(_prompt_mixer_kernel, tile=tile, blk_tiles=blk_tiles),
        grid_spec=grid_spec,
        out_shape=out_shape,
        compiler_params=pltpu.CompilerParams(
            dimension_semantics=("arbitrary", "arbitrary"), vmem_limit_bytes=VMEM_LIMIT),
        name="prompt_mixer",
    )(sinks, x, *wts)


def _sample_mixer_kernel(sink_ref, x_ref, ck_ref, cv_ref, sc_ref, g1_ref, win_ref, cw_ref, qg_ref,
                         kg_ref, seg_ref, exp_ref, wco_ref, wao_ref, wo_ref, g2_ref, wr_ref, br_ref,
                         tri_ref, x1_ref, h2p_ref, rt_ref, cnt_ref, kwin_ref, vwin_ref, cst_ref,
                         uext, att_s, kbd, vbd, *, n_stream, seq):
    ext = seq + 8
    x = x_ref[...]
    h = _bf(_rms(x, g1_ref[...]))

    u = _dot(h, win_ref[:, C_C:C_C + D]) * _dot(h, win_ref[:, C_X:C_X + D])
    for s in range(n_stream):
        uext[s * ext + 6:s * ext + 8, :] = sc_ref[s]
        uext[s * ext + 8:(s + 1) * ext, :] = u[s * seq:(s + 1) * seq, :]
        cst_ref[s] = u[(s + 1) * seq - 2:(s + 1) * seq, :]
    conv = jnp.concatenate(
        [uext[s * ext + 6:s * ext + 6 + seq, :] * cw_ref[0:1, :]
         + uext[s * ext + 7:s * ext + 7 + seq, :] * cw_ref[1:2, :] for s in range(n_stream)],
        axis=0) + u * cw_ref[2:3, :]
    yc = _bf(_dot(h, win_ref[:, C_B:C_B + D]) * conv)
    o_c = _dot(yc, wco_ref[...])
    macc = _sigmoid(_dot(h, win_ref[:, C_G:C_G + D])) * o_c

    q = _dot(h, win_ref[:, C_Q:C_Q + D])
    qn = _bf(_head_norm_q(q, seg_ref, exp_ref, qg_ref[...]) * (1.0 / math.sqrt(HEAD_DIM)))
    kv = _dot(h, win_ref[:, C_K:C_K + 2 * LANES])
    kn = _head_norm_k(kv[:, 0:LANES], kg_ref[...])
    vv = kv[:, LANES:2 * LANES]

    kbd[...] = jnp.zeros_like(kbd)
    vbd[...] = jnp.zeros_like(vbd)
    lane = lax.broadcasted_iota(jnp.int32, (1, QUAD * KEYS), 1)
    valid = (lane % KEYS) < WINDOW + seq
    for s in range(n_stream):
        rows = slice(s * seq, (s + 1) * seq)
        kwin_ref[s, 0:WINDOW - seq, :] = ck_ref[s, seq:WINDOW, :]
        kwin_ref[s, WINDOW - seq:WINDOW, :] = kn[rows, :]
        vwin_ref[s, 0:WINDOW - seq, :] = cv_ref[s, seq:WINDOW, :]
        vwin_ref[s, WINDOW - seq:WINDOW, :] = vv[rows, :]
        kc4 = _rep4(ck_ref[s])
        vc4 = _rep4(cv_ref[s])
        kn4 = _rep4(kn[rows, :])
        vn4 = _rep4(vv[rows, :])
        for g in range(N_KV):
            _write_bd(kbd, g, 0, kc4[g])
            _write_bd(kbd, g, WINDOW, kn4[g])
            _write_bd(vbd, g, 0, vc4[g])
            _write_bd(vbd, g, WINDOW, vn4[g])
        for g in range(N_KV):
            kb = kbd[g]
            vb = vbd[g]
            for j in range(N_HEADS // N_KV // QUAD):
                h0 = g * (N_HEADS // N_KV) + j * QUAD
                qq = qn[rows, h0 * HEAD_DIM:(h0 + QUAD) * HEAD_DIM]
                sinks = [sink_ref[h0 + i] for i in range(QUAD)]
                o = _attend(qq, kb, vb, valid, sinks)
                att_s[rows, h0 * HEAD_DIM:(h0 + QUAD) * HEAD_DIM] = _bf(o)

    o_a = _dot(att_s[...], wao_ref[...])
    macc = macc + _sigmoid(_dot(h, win_ref[:, C_G + D:C_G + 2 * D])) * o_a
    carry = _merge_tail(x, macc, wo_ref, g2_ref, wr_ref, br_ref, tri_ref,
                        jnp.zeros((1, RT_W), jnp.float32), x1_ref, h2p_ref, rt_ref)
    cnt_ref[0] = jnp.broadcast_to(carry, (8, RT_W))


def _sample_mixer(x, ck, cv, sc, sinks, wts):
    nb, seq, _ = x.shape
    rows = nb * seq
    x2 = x.reshape(rows, D)
    ins = (x2, ck, cv, sc) + tuple(wts)
    full = lambda a: pl.BlockSpec(a.shape, lambda *_, nd=a.ndim: (0,) * nd)
    out_shape = [
        jax.ShapeDtypeStruct((rows, D), jnp.float32),
        jax.ShapeDtypeStruct((rows * PK, LANES), jnp.uint32),
        jax.ShapeDtypeStruct((rows, RT_W), jnp.float32),
        jax.ShapeDtypeStruct((1, 8, RT_W), jnp.float32),
        jax.ShapeDtypeStruct((nb, WINDOW, LANES), jnp.float32),
        jax.ShapeDtypeStruct((nb, WINDOW, LANES), jnp.float32),
        jax.ShapeDtypeStruct((nb, 2, D), jnp.float32),
    ]
    grid_spec = pltpu.PrefetchScalarGridSpec(
        num_scalar_prefetch=1,
        grid=(1,),
        in_specs=[full(a) for a in ins],
        out_specs=[full(a) for a in out_shape],
        scratch_shapes=[
            pltpu.VMEM((nb * (seq + 8), D), jnp.float32),
            pltpu.VMEM((rows, D), jnp.bfloat16),
            pltpu.VMEM((N_KV, QUAD * KEYS, QUAD_W), jnp.bfloat16),
            pltpu.VMEM((N_KV, QUAD * KEYS, QUAD_W), jnp.bfloat16),
        ])
    return pl.pallas_call(
        functools.partial(_sample_mixer_kernel, n_stream=nb, seq=seq),
        grid_spec=grid_spec,
        out_shape=out_shape,
        compiler_params=pltpu.CompilerParams(
            dimension_semantics=("arbitrary",), vmem_limit_bytes=VMEM_LIMIT),
        name="sample_mixer",
    )(sinks, *ins)


def _sorted_rows(blk_tokens):
    rows = 2 * blk_tokens + N_EXPERTS * (MOE_PAD - 1) + MOE_TILE
    return -(-rows // MOE_TILE) * MOE_TILE


def _moe_kernel(pos_ref, seg_ref, h2p_ref, x1_ref, rt_ref, wg_ref, wu_ref, wd_ref, y_ref,
                xo, g1, g2, *, tile, n_tiles):
    b = pl.program_id(0)
    s = pl.program_id(1)
    n_groups = tile // UNROLL

    @pl.when((b == 0) & (s == 0))
    def _():
        xo[...] = jnp.zeros_like(xo)

    @pl.when(s < n_tiles)
    def _dispatch():
        base = (b * n_tiles + s) * (2 * tile)

        def group(gi, carry):
            for k in range(UNROLL):
                t = gi * UNROLL + k
                row = h2p_ref[pl.ds(pl.multiple_of(t * PK, PK), PK), :]
                p1 = pos_ref[base + 2 * t]
                p2 = pos_ref[base + 2 * t + 1]
                xo[pl.ds(pl.multiple_of(p1 * PK, PK), PK), :] = row
                xo[pl.ds(pl.multiple_of(p2 * PK, PK), PK), :] = row
            return carry

        lax.fori_loop(0, n_groups, group, 0)

    @pl.when((s >= n_tiles) & (s < n_tiles + N_EXPERTS))
    def _experts():
        e = s - n_tiles
        start = seg_ref[(b * N_EXPERTS + e) * 2]
        n_rows = seg_ref[(b * N_EXPERTS + e) * 2 + 1]
        wg = wg_ref[0]
        wu = wu_ref[0]
        wd = wd_ref[0]

        def mm(j, carry):
            r0 = start + j * MOE_TILE
            base = pl.multiple_of(r0 * PK, MOE_PAD * PK)
            xs = _bf(_unpack_rows(
                [xo[pl.ds(base + q, MOE_TILE, stride=PK), :] for q in range(PK)]))
            gt = _dot(xs, wg)
            hid = _bf(gt * _sigmoid(gt) * _dot(xs, wu))
            out = _pack_rows(_dot(hid, wd))
            half = MOE_TILE // 2

            def put(lo):
                for q in range(PK):
                    xo[pl.ds(base + lo * PK + q, half, stride=PK), :] = (
                        out[lo:lo + half, q * LANES:(q + 1) * LANES])

            put(0)

            @pl.when(n_rows - j * MOE_TILE > half)
            def _():
                put(half)

            return carry

        lax.fori_loop(0, (n_rows + MOE_TILE - 1) // MOE_TILE, mm, 0)

    @pl.when(s >= n_tiles + N_EXPERTS)
    def _combine():
        i = s - n_tiles - N_EXPERTS
        base = (b * n_tiles + i) * (2 * tile)

        def group(gi, carry):
            for k in range(UNROLL):
                t = gi * UNROLL + k
                p1 = pos_ref[base + 2 * t]
                p2 = pos_ref[base + 2 * t + 1]
                dst = pl.ds(pl.multiple_of(t * PK, PK), PK)
                g1[dst, :] = xo[pl.ds(pl.multiple_of(p1 * PK, PK), PK), :]
                g2[dst, :] = xo[pl.ds(pl.multiple_of(p2 * PK, PK), PK), :]
            return carry

        lax.fori_loop(0, n_groups, group, 0)
        o1 = _unpack_rows([g1[pl.ds(q, tile, stride=PK), :] for q in range(PK)])
        o2 = _unpack_rows([g2[pl.ds(q, tile, stride=PK), :] for q in range(PK)])
        rt = rt_ref[...]
        y_ref[...] = x1_ref[...] + rt[:, R_W1:R_W1 + 1] * o1 + rt[:, R_W2:R_W2 + 1] * o2


def _moe(pos, seg, h2p, x1, rt, wg, wu, wd, tile, n_tiles):
    n = x1.shape[0]
    n_blocks = n // (tile * n_tiles)
    n_steps = 2 * n_tiles + N_EXPERTS
    last = n_tiles - 1

    def disp(b, s, *_):
        return (b * n_tiles + jnp.minimum(s, last), 0)

    def comb(b, s, *_):
        return (b * n_tiles + jnp.clip(s - n_tiles - N_EXPERTS, 0, last), 0)

    def wmap(b, s, *_):
        return (jnp.clip(s - n_tiles, 0, N_EXPERTS - 1), 0, 0)

    grid_spec = pltpu.PrefetchScalarGridSpec(
        num_scalar_prefetch=2,
        grid=(n_blocks, n_steps),
        in_specs=[pl.BlockSpec((tile * PK, LANES), disp),
                  pl.BlockSpec((tile, D), comb),
                  pl.BlockSpec((tile, RT_W), comb),
                  pl.BlockSpec((1, D, D_EXPERT), wmap),
                  pl.BlockSpec((1, D, D_EXPERT), wmap),
                  pl.BlockSpec((1, D_EXPERT, D), wmap)],
        out_specs=pl.BlockSpec((tile, D), comb),
        scratch_shapes=[
            pltpu.VMEM((_sorted_rows(tile * n_tiles) * PK, LANES), jnp.uint32),
            pltpu.VMEM((tile * PK, LANES), jnp.uint32),
            pltpu.VMEM((tile * PK, LANES), jnp.uint32),
        ])
    return pl.pallas_call(
        functools.partial(_moe_kernel, tile=tile, n_tiles=n_tiles),
        grid_spec=grid_spec,
        out_shape=jax.ShapeDtypeStruct((n, D), jnp.float32),
        compiler_params=pltpu.CompilerParams(
            dimension_semantics=("arbitrary", "arbitrary"), vmem_limit_bytes=VMEM_LIMIT),
        name="moe",
    )(pos, seg, h2p, x1, rt, wg, wu, wd)


def _sorted_positions(rt, cnt, blk_tokens):
    n = rt.shape[0]
    counts = cnt[:, 0, :N_EXPERTS].astype(jnp.int32)
    padded = (counts + MOE_PAD - 1) // MOE_PAD * MOE_PAD
    off = jnp.cumsum(padded, axis=1) - padded
    blk = jnp.arange(n, dtype=jnp.int32) // blk_tokens
    e = rt[:, R_E1:R_E2 + 1].astype(jnp.int32)
    rank = rt[:, R_RANK1:R_RANK2 + 1].astype(jnp.int32)
    pos = off[blk[:, None], e] + rank
    seg = jnp.stack([off, counts], axis=-1)
    return pos.reshape(-1), seg.reshape(-1)


def _segment_matrices():
    head = jnp.arange(D) // HEAD_DIM
    col = jnp.arange(LANES)
    seg = (head[:, None] == col[None, :]).astype(jnp.bfloat16)
    row = jnp.arange(2 * LANES) % LANES
    expm = (row[:, None] == head[None, :]).astype(jnp.bfloat16)
    return seg, expm


def _strict_lower(n):
    r = jnp.arange(n)
    return (r[None, :] < r[:, None]).astype(jnp.bfloat16)


def kernel(x_prompt, x_sample, cache_k, cache_v, state_conv, norm_mix_g, w_in, conv_w, q_norm_g,
           k_norm_g, attn_sinks, w_conv_out, w_attn_out, w_out, norm_ffn_g, w_group, b_group,
           w_router, b_router, w_gate, w_up, w_down):
    nb, seq, _ = x_prompt.shape
    ns, sseq, _ = x_sample.shape
    l = 0
    seg, expm = _segment_matrices()
    wr = jnp.zeros((D, RT_W), jnp.float32)
    wr = wr.at[:, 0:N_EXPERTS].set(w_router[l]).at[:, N_EXPERTS:N_EXPERTS + N_GROUPS].set(w_group[l])
    br = jnp.zeros((1, RT_W), jnp.float32)
    br = br.at[0, 0:N_EXPERTS].set(b_router[l]).at[0, N_EXPERTS:N_EXPERTS + N_GROUPS].set(b_group[l])
    wts = (norm_mix_g[l].reshape(1, D), _bf(w_in[l]), conv_w[l],
           jnp.tile(q_norm_g[l], N_HEADS).reshape(1, D), jnp.tile(k_norm_g[l], N_KV).reshape(1, LANES),
           seg, expm, _bf(w_conv_out[l]), _bf(w_attn_out[l]), _bf(w_out[l]),
           norm_ffn_g[l].reshape(1, D), _bf(wr), br)
    sinks = attn_sinks[l]
    n_s = ns * sseq

    x1p, h2pp, rtp, cntp, kwp, vwp, cstp = _prompt_mixer(
        x_prompt, sinks, wts + (_strict_lower(TOK_TILE),), tile=TOK_TILE, blk_tiles=BLOCK_TILES)
    x1s, h2ps, rts, cnts, kws, vws, csts = _sample_mixer(
        x_sample, cache_k[l].reshape(ns, WINDOW, LANES), cache_v[l].reshape(ns, WINDOW, LANES),
        state_conv[l], sinks, wts + (_strict_lower(n_s),))

    wg, wu, wd = _bf(w_gate[l]), _bf(w_up[l]), _bf(w_down[l])
    rtp = rtp.reshape(nb * seq, RT_W)
    posp, segp = _sorted_positions(rtp, cntp, TOK_TILE * BLOCK_TILES)
    yp = _moe(posp, segp, h2pp, x1p.reshape(nb * seq, D), rtp, wg, wu, wd,
              tile=TOK_TILE, n_tiles=BLOCK_TILES)
    poss, segs = _sorted_positions(rts, cnts, n_s)
    ys = _moe(poss, segs, h2ps, x1s, rts, wg, wu, wd, tile=n_s, n_tiles=1)

    kv_shape = lambda n: (1, n, WINDOW, N_KV, HEAD_DIM)
    return (yp.reshape(nb, seq, D), ys.reshape(ns, sseq, D),
            kwp.reshape(kv_shape(nb)), vwp.reshape(kv_shape(nb)), cstp.reshape(1, nb, 2, D),
            kws.reshape(kv_shape(ns)), vws.reshape(kv_shape(ns)), csts.reshape(1, ns, 2, D))
```

```python
import functools
import math

import jax
import jax.numpy as jnp
from jax import lax
from jax.experimental import pallas as pl
from jax.experimental.pallas import tpu as pltpu

D = 1024
CHUNK = 64
HEAD_DIM = 64
N_HEADS = 16
N_KV = 2
WINDOW = 128
N_GROUPS = 4
EPG = 8
N_EXPERTS = 32
D_EXPERT = 256
EPS = 1e-6
NEG = -1e30

LANES = 128
QUAD = 4
QUAD_W = QUAD * HEAD_DIM
SLOTS = 4
KEYS = SLOTS * CHUNK
GROUP_HEADS = N_HEADS // N_KV
STACK = GROUP_HEADS * CHUNK
RT_W = LANES
HALF = D // 2
PK = HALF // LANES

R_E1, R_E2, R_W1, R_W2, R_RANK1, R_RANK2 = 0, 1, 2, 3, 4, 5

C_B, C_C, C_X, C_Q, C_K, C_V, C_G = 0, 1024, 2048, 3072, 4096, 4224, 4352

MOE_TILE = 256
MOE_PAD = 128
TOK_TILE = 512
BLOCK_TILES = 8
UNROLL = 16

VMEM_LIMIT = 60 * 1024 * 1024

_NT = (((1,), (1,)), ((), ()))


def _bf(x):
    return x.astype(jnp.bfloat16)


def _dot(a, b):
    return jnp.dot(a, b, preferred_element_type=jnp.float32)


def _rms(x, g):
    return x * lax.rsqrt(jnp.mean(x * x, axis=-1, keepdims=True) + EPS) * g


def _sigmoid(x):
    return 1.0 / (1.0 + jnp.exp(-x))


def _pack_rows(x):
    return pltpu.pack_elementwise([x[:, :HALF], x[:, HALF:]], packed_dtype=jnp.bfloat16)


def _unpack_rows(parts):
    def half(i):
        return [pltpu.unpack_elementwise(p, index=i, packed_dtype=jnp.bfloat16,
                                         unpacked_dtype=jnp.float32) for p in parts]
    return jnp.concatenate(half(0) + half(1), axis=-1)


def _head_norm_q(q, seg_ref, exp_ref, qg):
    ssq = _dot(_bf(q * q), seg_ref[...])
    inv = lax.rsqrt(ssq * (1.0 / HEAD_DIM) + EPS)
    hi = _bf(inv)
    lo = _bf(inv - hi.astype(jnp.float32))
    full = _dot(jnp.concatenate([hi, lo], axis=-1), exp_ref[...])
    return q * full * qg


def _head_norm_k(k, kg):
    lane = lax.broadcasted_iota(jnp.int32, k.shape, 1)
    lo_half = lane < HEAD_DIM
    sq = k * k
    s0 = jnp.sum(jnp.where(lo_half, sq, 0.0), axis=-1, keepdims=True)
    s1 = jnp.sum(jnp.where(lo_half, 0.0, sq), axis=-1, keepdims=True)
    i0 = lax.rsqrt(s0 * (1.0 / HEAD_DIM) + EPS)
    i1 = lax.rsqrt(s1 * (1.0 / HEAD_DIM) + EPS)
    return k * jnp.where(lo_half, i0, i1) * kg


def _rep4(x):
    lane = lax.broadcasted_iota(jnp.int32, x.shape, 1)
    sw = pltpu.roll(x, HEAD_DIM, axis=1)
    a2 = jnp.where(lane < HEAD_DIM, x, sw)
    b2 = jnp.where(lane < HEAD_DIM, sw, x)
    return jnp.concatenate([a2, a2], axis=-1), jnp.concatenate([b2, b2], axis=-1)


def _write_bd(ref, g, row0, x4):
    rows = x4.shape[0]
    lane = lax.broadcasted_iota(jnp.int32, x4.shape, 1)
    for i in range(QUAD):
        keep = (lane >= i * HEAD_DIM) & (lane < (i + 1) * HEAD_DIM)
        ref[g, pl.ds(i * KEYS + row0, rows), :] = _bf(jnp.where(keep, x4, 0.0))


def _attend(qq, kbd, vbd, valid, sinks):
    s = lax.dot_general(qq, kbd, _NT, preferred_element_type=jnp.float32)
    s = jnp.where(valid, s, NEG)
    ps = []
    for i in range(QUAD):
        sh = s[:, i * KEYS:(i + 1) * KEYS]
        m = jnp.maximum(jnp.max(sh, axis=-1, keepdims=True), sinks[i])
        p = jnp.exp(sh - m)
        den = jnp.sum(p, axis=-1, keepdims=True) + jnp.exp(sinks[i] - m)
        ps.append(_bf(p / den))
    return _dot(jnp.concatenate(ps, axis=-1), vbd)


def _route(logits, tri_ref, carry):
    lane = lax.broadcasted_iota(jnp.int32, logits.shape, 1)
    lane_f = lane.astype(jnp.float32)
    big = float(RT_W)
    is_g = (lane >= N_EXPERTS) & (lane < N_EXPERTS + N_GROUPS)
    gl = jnp.where(is_g, logits, NEG)
    gmax = jnp.max(gl, axis=-1, keepdims=True)
    gsum = jnp.sum(jnp.where(is_g, jnp.exp(gl - gmax), 0.0), axis=-1, keepdims=True)
    g_w = 1.0 / gsum
    g_sel = jnp.min(jnp.where(is_g & (gl == gmax), lane_f, big), axis=-1, keepdims=True) - N_EXPERTS
    lo = g_sel * EPG
    in_grp = (lane_f >= lo) & (lane_f < lo + EPG)
    el = jnp.where(in_grp, logits, NEG)
    v1 = jnp.max(el, axis=-1, keepdims=True)
    i1 = jnp.min(jnp.where(in_grp & (el == v1), lane_f, big), axis=-1, keepdims=True)
    rest = in_grp & (lane_f != i1)
    el2 = jnp.where(rest, logits, NEG)
    v2 = jnp.max(el2, axis=-1, keepdims=True)
    i2 = jnp.min(jnp.where(rest & (el2 == v2), lane_f, big), axis=-1, keepdims=True)
    e2 = jnp.exp(v2 - v1)
    den = 1.0 + e2
    w1 = (1.0 / den) * g_w
    w2 = (e2 / den) * g_w
    oh1 = jnp.where(lane_f == i1, 1.0, 0.0)
    oh2 = jnp.where(lane_f == i2, 1.0, 0.0)
    both = oh1 + oh2
    before = _dot(tri_ref[...], _bf(both)) + carry
    r1 = jnp.sum(before * oh1, axis=-1, keepdims=True)
    r2 = jnp.sum(before * oh2, axis=-1, keepdims=True)
    rec = jnp.zeros_like(logits)
    for ln, val in ((R_E1, i1), (R_E2, i2), (R_W1, w1), (R_W2, w2), (R_RANK1, r1), (R_RANK2, r2)):
        rec = jnp.where(lane == ln, val, rec)
    return rec, carry + jnp.sum(both, axis=0, keepdims=True)


def _merge_tail(x, macc, wo_ref, g2_ref, wr_ref, br_ref, tri_ref, carry, x1_ref, h2p_ref, rt_ref):
    rows = x.shape[0]
    x1 = x + _dot(_bf(macc), wo_ref[...])
    x1_ref[...] = x1
    h2 = _rms(x1, g2_ref[...])
    pk = _pack_rows(h2)
    for j in range(PK):
        h2p_ref[pl.ds(j, rows, stride=PK), :] = pk[:, j * LANES:(j + 1) * LANES]
    logits = _dot(_bf(h2), wr_ref[...]) + br_ref[...]
    rec, carry = _route(logits, tri_ref, carry)
    rt_ref[...] = rec
    return carry


def _prompt_mixer_kernel(x_ref, sk_ref, g1_ref, win_ref, cw_ref, qg_ref, kg_ref, seg_ref, exp_ref,
                         wco_ref, wao_ref, wo_ref, g2_ref, wr_ref, br_ref, tri_ref,
                         x1_ref, h2p_ref, rt_ref, cnt_ref, kwin_ref, vwin_ref, cst_ref,
                         uext, qs_s, att_s, kn_s, kring, vring, carry_s, *, tile, blk_tiles):
    b = pl.program_id(0)
    it = pl.program_id(1)
    n_it = pl.num_programs(1)
    n_chunk = tile // CHUNK

    @pl.when((b == 0) & (it == 0))
    def _():
        kring[...] = jnp.zeros_like(kring)
        vring[...] = jnp.zeros_like(vring)

    @pl.when(it == 0)
    def _():
        uext[0:8, :] = jnp.zeros((8, D), jnp.float32)

    @pl.when(it % blk_tiles == 0)
    def _():
        carry_s[...] = jnp.zeros_like(carry_s)

    x = x_ref[0]
    h = _bf(_rms(x, g1_ref[...]))

    u = _dot(h, win_ref[:, C_C:C_C + D]) * _dot(h, win_ref[:, C_X:C_X + D])
    uext[8:8 + tile, :] = u
    conv = (uext[6:6 + tile, :] * cw_ref[0:1, :] + uext[7:7 + tile, :] * cw_ref[1:2, :]
            + u * cw_ref[2:3, :])
    yc = _bf(_dot(h, win_ref[:, C_B:C_B + D]) * conv)
    o_c = _dot(yc, wco_ref[...])
    macc = _sigmoid(_dot(h, win_ref[:, C_G:C_G + D])) * o_c
    last2 = uext[tile + 6:tile + 8, :]
    uext[6:8, :] = last2

    @pl.when(it == n_it - 1)
    def _():
        cst_ref[0] = last2

    q = _dot(h, win_ref[:, C_Q:C_Q + D])
    qn = _head_norm_q(q, seg_ref, exp_ref, qg_ref[...]) * (1.0 / math.sqrt(HEAD_DIM))
    for hd in range(N_HEADS):
        g, r = divmod(hd, GROUP_HEADS)
        piece = _bf(qn[:, hd * HEAD_DIM:(hd + 1) * HEAD_DIM])
        for c in range(n_chunk):
            qs_s[g, c * STACK + r * CHUNK:c * STACK + (r + 1) * CHUNK, :] = (
                piece[c * CHUNK:(c + 1) * CHUNK, :])
    kv = _dot(h, win_ref[:, C_K:C_K + 2 * LANES])
    kn = _head_norm_k(kv[:, 0:LANES], kg_ref[...])
    vv = kv[:, LANES:2 * LANES]
    kn_s[0] = kn
    kn_s[1] = vv

    @pl.when(it == n_it - 1)
    def _():
        kwin_ref[0] = kn[tile - WINDOW:, :]
        vwin_ref[0] = vv[tile - WINDOW:, :]

    slot_of_lane = lax.broadcasted_iota(jnp.int32, (1, KEYS), 1) // CHUNK
    lane = lax.broadcasted_iota(jnp.int32, (CHUNK, LANES), 1)
    lo_half = lane < HEAD_DIM
    ones_blk = jnp.ones((CHUNK, LANES), jnp.bfloat16)

    def chunk_body(c, carry):
        r0 = pl.multiple_of(c * CHUNK, CHUNK)
        row0 = pl.multiple_of((c % SLOTS) * CHUNK, CHUNK)
        kc = kn_s[0, pl.ds(r0, CHUNK), :]
        vc = kn_s[1, pl.ds(r0, CHUNK), :]
        vsw = pltpu.roll(vc, HEAD_DIM, axis=1)
        for g in range(N_KV):
            kring[g, pl.ds(row0, CHUNK), :] = _bf(kc[:, g * HEAD_DIM:(g + 1) * HEAD_DIM])
            vv2 = jnp.where(lo_half, vc, vsw) if g == 0 else jnp.where(lo_half, vsw, vc)
            vring[g, pl.ds(row0, CHUNK), 0:LANES] = _bf(vv2)
            vring[g, pl.ds(row0, CHUNK), LANES:2 * LANES] = ones_blk
        cg = it * n_chunk + c
        dist = (cg - slot_of_lane) & (SLOTS - 1)
        valid = dist <= jnp.minimum(cg, SLOTS - 2)
        q0 = pl.multiple_of(c * STACK, STACK)
        for g in range(N_KV):
            s = lax.dot_general(qs_s[g, pl.ds(q0, STACK), :], kring[g], _NT,
                                preferred_element_type=jnp.float32)
            s = jnp.where(valid, s, NEG)
            sk = sk_ref[g]
            m = jnp.maximum(jnp.max(s, axis=-1, keepdims=True), sk)
            o = _dot(_bf(jnp.exp(s - jnp.concatenate([m, m], axis=-1))), vring[g])
            out = o[:, 0:LANES] / (o[:, LANES:2 * LANES] + jnp.exp(sk - m))
            for r in range(0, GROUP_HEADS, 2):
                pair = jnp.where(lo_half, out[r * CHUNK:(r + 1) * CHUNK, :],
                                 out[(r + 1) * CHUNK:(r + 2) * CHUNK, :])
                col = (g * GROUP_HEADS + r) * HEAD_DIM
                att_s[pl.ds(r0, CHUNK), col:col + LANES] = _bf(pair)
        return carry

    lax.fori_loop(0, n_chunk, chunk_body, 0)

    o_a = _dot(att_s[...], wao_ref[...])
    macc = macc + _sigmoid(_dot(h, win_ref[:, C_G + D:C_G + 2 * D])) * o_a
    carry = _merge_tail(x, macc, wo_ref, g2_ref, wr_ref, br_ref, tri_ref, carry_s[...],
                        x1_ref.at[0], h2p_ref, rt_ref.at[0])
    carry_s[...] = carry

    @pl.when(it % blk_tiles == blk_tiles - 1)
    def _():
        cnt_ref[0] = jnp.broadcast_to(carry, (8, RT_W))


def _const_spec(shape):
    nd = len(shape)
    return pl.BlockSpec(shape, lambda *_: (0,) * nd, pipeline_mode=pl.Buffered(1))


def _prompt_mixer(x, sink_rows, wts, tile, blk_tiles):
    nb, seq, _ = x.shape
    n_it = seq // tile
    bps = n_it // blk_tiles
    row_spec = lambda w: pl.BlockSpec((1, tile, w), lambda b, i: (b, i, 0))
    str_spec = lambda r, w: pl.BlockSpec((1, r, w), lambda b, i: (b, 0, 0))
    consts = (sink_rows,) + tuple(wts)
    out_shape = [
        jax.ShapeDtypeStruct((nb, seq, D), jnp.float32),
        jax.ShapeDtypeStruct((nb * seq * PK, LANES), jnp.uint32),
        jax.ShapeDtypeStruct((nb, seq, RT_W), jnp.float32),
        jax.ShapeDtypeStruct((nb * bps, 8, RT_W), jnp.float32),
        jax.ShapeDtypeStruct((nb, WINDOW, LANES), jnp.float32),
        jax.ShapeDtypeStruct((nb, WINDOW, LANES), jnp.float32),
        jax.ShapeDtypeStruct((nb, 2, D), jnp.float32),
    ]
    return pl.pallas_call(
        functools.partial(_prompt_mixer_kernel, tile=tile, blk_tiles=blk_tiles),
        grid=(nb, n_it),
        in_specs=[row_spec(D)] + [_const_spec(w.shape) for w in consts],
        out_specs=[row_spec(D),
                   pl.BlockSpec((tile * PK, LANES), lambda b, i: (b * n_it + i, 0)),
                   row_spec(RT_W),
                   pl.BlockSpec((1, 8, RT_W), lambda b, i: (b * bps + i // blk_tiles, 0, 0)),
                   str_spec(WINDOW, LANES), str_spec(WINDOW, LANES), str_spec(2, D)],
        out_shape=out_shape,
        scratch_shapes=[
            pltpu.VMEM((tile + 8, D), jnp.float32),
            pltpu.VMEM((N_KV, tile * GROUP_HEADS, HEAD_DIM), jnp.bfloat16),
            pltpu.VMEM((tile, D), jnp.bfloat16),
            pltpu.VMEM((2, tile, LANES), jnp.float32),
            pltpu.VMEM((N_KV, KEYS, HEAD_DIM), jnp.bfloat16),
            pltpu.VMEM((N_KV, KEYS, 2 * LANES), jnp.bfloat16),
            pltpu.VMEM((1, RT_W), jnp.float32),
        ],
        compiler_params=pltpu.CompilerParams(
            dimension_semantics=("arbitrary", "arbitrary"), vmem_limit_bytes=VMEM_LIMIT),
        name="prompt_mixer",
    )(x, *consts)


def _sample_mixer_kernel(sink_ref, x_ref, ck_ref, cv_ref, sc_ref, g1_ref, win_ref, cw_ref, qg_ref,
                         kg_ref, seg_ref, exp_ref, wco_ref, wao_ref, wo_ref, g2_ref, wr_ref, br_ref,
                         tri_ref, x1_ref, h2p_ref, rt_ref, cnt_ref, kwin_ref, vwin_ref, cst_ref,
                         uext, att_s, kbd, vbd, *, n_stream, seq):
    ext = seq + 8
    x = x_ref[...]
    h = _bf(_rms(x, g1_ref[...]))

    u = _dot(h, win_ref[:, C_C:C_C + D]) * _dot(h, win_ref[:, C_X:C_X + D])
    for s in range(n_stream):
        uext[s * ext + 6:s * ext + 8, :] = sc_ref[s]
        uext[s * ext + 8:(s + 1) * ext, :] = u[s * seq:(s + 1) * seq, :]
        cst_ref[s] = u[(s + 1) * seq - 2:(s + 1) * seq, :]
    conv = jnp.concatenate(
        [uext[s * ext + 6:s * ext + 6 + seq, :] * cw_ref[0:1, :]
         + uext[s * ext + 7:s * ext + 7 + seq, :] * cw_ref[1:2, :] for s in range(n_stream)],
        axis=0) + u * cw_ref[2:3, :]
    yc = _bf(_dot(h, win_ref[:, C_B:C_B + D]) * conv)
    o_c = _dot(yc, wco_ref[...])
    macc = _sigmoid(_dot(h, win_ref[:, C_G:C_G + D])) * o_c

    q = _dot(h, win_ref[:, C_Q:C_Q + D])
    qn = _bf(_head_norm_q(q, seg_ref, exp_ref, qg_ref[...]) * (1.0 / math.sqrt(HEAD_DIM)))
    kv = _dot(h, win_ref[:, C_K:C_K + 2 * LANES])
    kn = _head_norm_k(kv[:, 0:LANES], kg_ref[...])
    vv = kv[:, LANES:2 * LANES]

    kbd[...] = jnp.zeros_like(kbd)
    vbd[...] = jnp.zeros_like(vbd)
    lane = lax.broadcasted_iota(jnp.int32, (1, QUAD * KEYS), 1)
    valid = (lane % KEYS) < WINDOW + seq
    for s in range(n_stream):
        rows = slice(s * seq, (s + 1) * seq)
        kwin_ref[s, 0:WINDOW - seq, :] = ck_ref[s, seq:WINDOW, :]
        kwin_ref[s, WINDOW - seq:WINDOW, :] = kn[rows, :]
        vwin_ref[s, 0:WINDOW - seq, :] = cv_ref[s, seq:WINDOW, :]
        vwin_ref[s, WINDOW - seq:WINDOW, :] = vv[rows, :]
        kc4 = _rep4(ck_ref[s])
        vc4 = _rep4(cv_ref[s])
        kn4 = _rep4(kn[rows, :])
        vn4 = _rep4(vv[rows, :])
        for g in range(N_KV):
            _write_bd(kbd, g, 0, kc4[g])
            _write_bd(kbd, g, WINDOW, kn4[g])
            _write_bd(vbd, g, 0, vc4[g])
            _write_bd(vbd, g, WINDOW, vn4[g])
        for g in range(N_KV):
            kb = kbd[g]
            vb = vbd[g]
            for j in range(N_HEADS // N_KV // QUAD):
                h0 = g * (N_HEADS // N_KV) + j * QUAD
                qq = qn[rows, h0 * HEAD_DIM:(h0 + QUAD) * HEAD_DIM]
                sinks = [sink_ref[h0 + i] for i in range(QUAD)]
                o = _attend(qq, kb, vb, valid, sinks)
                att_s[rows, h0 * HEAD_DIM:(h0 + QUAD) * HEAD_DIM] = _bf(o)

    o_a = _dot(att_s[...], wao_ref[...])
    macc = macc + _sigmoid(_dot(h, win_ref[:, C_G + D:C_G + 2 * D])) * o_a
    carry = _merge_tail(x, macc, wo_ref, g2_ref, wr_ref, br_ref, tri_ref,
                        jnp.zeros((1, RT_W), jnp.float32), x1_ref, h2p_ref, rt_ref)
    cnt_ref[0] = jnp.broadcast_to(carry, (8, RT_W))


def _sample_mixer(x, ck, cv, sc, sinks, wts):
    nb, seq, _ = x.shape
    rows = nb * seq
    x2 = x.reshape(rows, D)
    ins = (x2, ck, cv, sc) + tuple(wts)
    full = lambda a: pl.BlockSpec(a.shape, lambda *_, nd=a.ndim: (0,) * nd)
    out_shape = [
        jax.ShapeDtypeStruct((rows, D), jnp.float32),
        jax.ShapeDtypeStruct((rows * PK, LANES), jnp.uint32),
        jax.ShapeDtypeStruct((rows, RT_W), jnp.float32),
        jax.ShapeDtypeStruct((1, 8, RT_W), jnp.float32),
        jax.ShapeDtypeStruct((nb, WINDOW, LANES), jnp.float32),
        jax.ShapeDtypeStruct((nb, WINDOW, LANES), jnp.float32),
        jax.ShapeDtypeStruct((nb, 2, D), jnp.float32),
    ]
    grid_spec = pltpu.PrefetchScalarGridSpec(
        num_scalar_prefetch=1,
        grid=(1,),
        in_specs=[full(a) for a in ins],
        out_specs=[full(a) for a in out_shape],
        scratch_shapes=[
            pltpu.VMEM((nb * (seq + 8), D), jnp.float32),
            pltpu.VMEM((rows, D), jnp.bfloat16),
            pltpu.VMEM((N_KV, QUAD * KEYS, QUAD_W), jnp.bfloat16),
            pltpu.VMEM((N_KV, QUAD * KEYS, QUAD_W), jnp.bfloat16),
        ])
    return pl.pallas_call(
        functools.partial(_sample_mixer_kernel, n_stream=nb, seq=seq),
        grid_spec=grid_spec,
        out_shape=out_shape,
        compiler_params=pltpu.CompilerParams(
            dimension_semantics=("arbitrary",), vmem_limit_bytes=VMEM_LIMIT),
        name="sample_mixer",
    )(sinks, *ins)


def _sorted_rows(blk_tokens):
    rows = 2 * blk_tokens + N_EXPERTS * (MOE_PAD - 1) + MOE_TILE
    return -(-rows // MOE_TILE) * MOE_TILE


def _moe_kernel(pos_ref, seg_ref, h2p_ref, x1_ref, rt_ref, wg_ref, wu_ref, wd_ref, y_ref,
                xo, g1, g2, *, tile, n_tiles):
    b = pl.program_id(0)
    s = pl.program_id(1)
    n_groups = tile // UNROLL

    @pl.when((b == 0) & (s == 0))
    def _():
        xo[...] = jnp.zeros_like(xo)

    @pl.when(s < n_tiles)
    def _dispatch():
        base = (b * n_tiles + s) * (2 * tile)

        def group(gi, carry):
            for k in range(UNROLL):
                t = gi * UNROLL + k
                row = h2p_ref[pl.ds(pl.multiple_of(t * PK, PK), PK), :]
                p1 = pos_ref[base + 2 * t]
                p2 = pos_ref[base + 2 * t + 1]
                xo[pl.ds(pl.multiple_of(p1 * PK, PK), PK), :] = row
                xo[pl.ds(pl.multiple_of(p2 * PK, PK), PK), :] = row
            return carry

        lax.fori_loop(0, n_groups, group, 0)

    @pl.when((s >= n_tiles) & (s < n_tiles + N_EXPERTS))
    def _experts():
        e = s - n_tiles
        start = seg_ref[(b * N_EXPERTS + e) * 2]
        n_rows = seg_ref[(b * N_EXPERTS + e) * 2 + 1]
        wg = wg_ref[0]
        wu = wu_ref[0]
        wd = wd_ref[0]

        def mm(j, carry):
            r0 = start + j * MOE_TILE
            base = pl.multiple_of(r0 * PK, MOE_PAD * PK)
            xs = _bf(_unpack_rows(
                [xo[pl.ds(base + q, MOE_TILE, stride=PK), :] for q in range(PK)]))
            gt = _dot(xs, wg)
            hid = _bf(gt * _sigmoid(gt) * _dot(xs, wu))
            out = _pack_rows(_dot(hid, wd))
            half = MOE_TILE // 2

            def put(lo):
                for q in range(PK):
                    xo[pl.ds(base + lo * PK + q, half, stride=PK), :] = (
                        out[lo:lo + half, q * LANES:(q + 1) * LANES])

            put(0)

            @pl.when(n_rows - j * MOE_TILE > half)
            def _():
                put(half)

            return carry

        lax.fori_loop(0, (n_rows + MOE_TILE - 1) // MOE_TILE, mm, 0)

    @pl.when(s >= n_tiles + N_EXPERTS)
    def _combine():
        i = s - n_tiles - N_EXPERTS
        base = (b * n_tiles + i) * (2 * tile)

        def group(gi, carry):
            for k in range(UNROLL):
                t = gi * UNROLL + k
                p1 = pos_ref[base + 2 * t]
                p2 = pos_ref[base + 2 * t + 1]
                dst = pl.ds(pl.multiple_of(t * PK, PK), PK)
                g1[dst, :] = xo[pl.ds(pl.multiple_of(p1 * PK, PK), PK), :]
                g2[dst, :] = xo[pl.ds(pl.multiple_of(p2 * PK, PK), PK), :]
            return carry

        lax.fori_loop(0, n_groups, group, 0)
        o1 = _unpack_rows([g1[pl.ds(q, tile, stride=PK), :] for q in range(PK)])
        o2 = _unpack_rows([g2[pl.ds(q, tile, stride=PK), :] for q in range(PK)])
        rt = rt_ref[...]
        y_ref[...] = x1_ref[...] + rt[:, R_W1:R_W1 + 1] * o1 + rt[:, R_W2:R_W2 + 1] * o2


def _moe(pos, seg, h2p, x1, rt, wg, wu, wd, tile, n_tiles):
    n = x1.shape[0]
    n_blocks = n // (tile * n_tiles)
    n_steps = 2 * n_tiles + N_EXPERTS
    last = n_tiles - 1

    def disp(b, s, *_):
        return (b * n_tiles + jnp.minimum(s, last), 0)

    def comb(b, s, *_):
        return (b * n_tiles + jnp.clip(s - n_tiles - N_EXPERTS, 0, last), 0)

    def wmap(b, s, *_):
        return (jnp.clip(s - n_tiles, 0, N_EXPERTS - 1), 0, 0)

    grid_spec = pltpu.PrefetchScalarGridSpec(
        num_scalar_prefetch=2,
        grid=(n_blocks, n_steps),
        in_specs=[pl.BlockSpec((tile * PK, LANES), disp),
                  pl.BlockSpec((tile, D), comb),
                  pl.BlockSpec((tile, RT_W), comb),
                  pl.BlockSpec((1, D, D_EXPERT), wmap),
                  pl.BlockSpec((1, D, D_EXPERT), wmap),
                  pl.BlockSpec((1, D_EXPERT, D), wmap)],
        out_specs=pl.BlockSpec((tile, D), comb),
        scratch_shapes=[
            pltpu.VMEM((_sorted_rows(tile * n_tiles) * PK, LANES), jnp.uint32),
            pltpu.VMEM((tile * PK, LANES), jnp.uint32),
            pltpu.VMEM((tile * PK, LANES), jnp.uint32),
        ])
    return pl.pallas_call(
        functools.partial(_moe_kernel, tile=tile, n_tiles=n_tiles),
        grid_spec=grid_spec,
        out_shape=jax.ShapeDtypeStruct((n, D), jnp.float32),
        compiler_params=pltpu.CompilerParams(
            dimension_semantics=("arbitrary", "arbitrary"), vmem_limit_bytes=VMEM_LIMIT),
        name="moe",
    )(pos, seg, h2p, x1, rt, wg, wu, wd)


def _sorted_positions(rt, cnt, blk_tokens):
    n = rt.shape[0]
    counts = cnt[:, 0, :N_EXPERTS].astype(jnp.int32)
    padded = (counts + MOE_PAD - 1) // MOE_PAD * MOE_PAD
    off = jnp.cumsum(padded, axis=1) - padded
    e = rt[:, R_E1:R_E2 + 1].astype(jnp.int32)
    rank = rt[:, R_RANK1:R_RANK2 + 1].astype(jnp.int32)
    off_tok = jnp.repeat(off, blk_tokens, axis=0)
    hit = e[:, :, None] == jnp.arange(N_EXPERTS, dtype=jnp.int32)
    pos = jnp.sum(jnp.where(hit, off_tok[:, None, :], 0), axis=-1) + rank
    seg = jnp.stack([off, counts], axis=-1)
    return pos.reshape(-1), seg.reshape(-1)


def _segment_matrices():
    head = jnp.arange(D) // HEAD_DIM
    col = jnp.arange(LANES)
    seg = (head[:, None] == col[None, :]).astype(jnp.bfloat16)
    row = jnp.arange(2 * LANES) % LANES
    expm = (row[:, None] == head[None, :]).astype(jnp.bfloat16)
    return seg, expm


def _strict_lower(n):
    r = jnp.arange(n)
    return (r[None, :] < r[:, None]).astype(jnp.bfloat16)


def kernel(x_prompt, x_sample, cache_k, cache_v, state_conv, norm_mix_g, w_in, conv_w, q_norm_g,
           k_norm_g, attn_sinks, w_conv_out, w_attn_out, w_out, norm_ffn_g, w_group, b_group,
           w_router, b_router, w_gate, w_up, w_down):
    nb, seq, _ = x_prompt.shape
    ns, sseq, _ = x_sample.shape
    l = 0
    seg, expm = _segment_matrices()
    wr = jnp.zeros((D, RT_W), jnp.float32)
    wr = wr.at[:, 0:N_EXPERTS].set(w_router[l]).at[:, N_EXPERTS:N_EXPERTS + N_GROUPS].set(w_group[l])
    br = jnp.zeros((1, RT_W), jnp.float32)
    br = br.at[0, 0:N_EXPERTS].set(b_router[l]).at[0, N_EXPERTS:N_EXPERTS + N_GROUPS].set(b_group[l])
    wts = (norm_mix_g[l].reshape(1, D), _bf(w_in[l]), conv_w[l],
           jnp.tile(q_norm_g[l], N_HEADS).reshape(1, D), jnp.tile(k_norm_g[l], N_KV).reshape(1, LANES),
           seg, expm, _bf(w_conv_out[l]), _bf(w_attn_out[l]), _bf(w_out[l]),
           norm_ffn_g[l].reshape(1, D), _bf(wr), br)
    sinks = attn_sinks[l]
    n_s = ns * sseq

    sink_rows = jnp.broadcast_to(
        jnp.repeat(sinks.reshape(N_KV, GROUP_HEADS), CHUNK, axis=1)[:, :, None], (N_KV, STACK, LANES))
    x1p, h2pp, rtp, cntp, kwp, vwp, cstp = _prompt_mixer(
        x_prompt, sink_rows, wts + (_strict_lower(TOK_TILE),), tile=TOK_TILE, blk_tiles=BLOCK_TILES)
    x1s, h2ps, rts, cnts, kws, vws, csts = _sample_mixer(
        x_sample, cache_k[l].reshape(ns, WINDOW, LANES), cache_v[l].reshape(ns, WINDOW, LANES),
        state_conv[l], sinks, wts + (_strict_lower(n_s),))

    wg, wu, wd = _bf(w_gate[l]), _bf(w_up[l]), _bf(w_down[l])
    rtp = rtp.reshape(nb * seq, RT_W)
    posp, segp = _sorted_positions(rtp, cntp, TOK_TILE * BLOCK_TILES)
    yp = _moe(posp, segp, h2pp, x1p.reshape(nb * seq, D), rtp, wg, wu, wd,
              tile=TOK_TILE, n_tiles=BLOCK_TILES)
    poss, segs = _sorted_positions(rts, cnts, n_s)
    ys = _moe(poss, segs, h2ps, x1s, rts, wg, wu, wd, tile=n_s, n_tiles=1)

    kv_shape = lambda n: (1, n, WINDOW, N_KV, HEAD_DIM)
    return (yp.reshape(nb, seq, D), ys.reshape(ns, sseq, D),
            kwp.reshape(kv_shape(nb)), vwp.reshape(kv_shape(nb)), cstp.reshape(1, nb, 2, D),
            kws.reshape(kv_shape(ns)), vws.reshape(kv_shape(ns)), csts.reshape(1, ns, 2, D))
```

```python
import functools
import math

import jax
import jax.numpy as jnp
from jax import lax
from jax.experimental import pallas as pl
from jax.experimental.pallas import tpu as pltpu

D = 1024
CHUNK = 64
HEAD_DIM = 64
N_HEADS = 16
N_KV = 2
WINDOW = 128
N_GROUPS = 4
EPG = 8
N_EXPERTS = 32
D_EXPERT = 256
EPS = 1e-6
NEG = -1e30

LANES = 128
QUAD = 4
QUAD_W = QUAD * HEAD_DIM
SLOTS = 4
KEYS = SLOTS * CHUNK
GROUP_HEADS = N_HEADS // N_KV
STACK = GROUP_HEADS * CHUNK
RT_W = LANES
HALF = D // 2
PK = HALF // LANES

R_E1, R_E2, R_W1, R_W2, R_RANK1, R_RANK2 = 0, 1, 2, 3, 4, 5

C_B, C_C, C_X, C_Q, C_K, C_V, C_G = 0, 1024, 2048, 3072, 4096, 4224, 4352

MOE_TILE = 256
MOE_PAD = 128
TOK_TILE = 512
BLOCK_TILES = 8
UNROLL = 16
STEP_EXPERTS = 4
EXPERT_STEPS = N_EXPERTS // STEP_EXPERTS

VMEM_LIMIT = 60 * 1024 * 1024

_NT = (((1,), (1,)), ((), ()))


def _bf(x):
    return x.astype(jnp.bfloat16)


def _dot(a, b):
    return jnp.dot(a, b, preferred_element_type=jnp.float32)


def _rms(x, g):
    return x * lax.rsqrt(jnp.mean(x * x, axis=-1, keepdims=True) + EPS) * g


def _sigmoid(x):
    return 1.0 / (1.0 + jnp.exp(-x))


def _pack_rows(x):
    return pltpu.pack_elementwise([x[:, :HALF], x[:, HALF:]], packed_dtype=jnp.bfloat16)


def _unpack_rows(parts):
    def half(i):
        return [pltpu.unpack_elementwise(p, index=i, packed_dtype=jnp.bfloat16,
                                         unpacked_dtype=jnp.float32) for p in parts]
    return jnp.concatenate(half(0) + half(1), axis=-1)


def _head_norm_q(q, seg_ref, exp_ref, qg):
    ssq = _dot(_bf(q * q), seg_ref[...])
    inv = lax.rsqrt(ssq * (1.0 / HEAD_DIM) + EPS)
    hi = _bf(inv)
    lo = _bf(inv - hi.astype(jnp.float32))
    full = _dot(jnp.concatenate([hi, lo], axis=-1), exp_ref[...])
    return q * full * qg


def _head_norm_k(k, kg):
    lane = lax.broadcasted_iota(jnp.int32, k.shape, 1)
    lo_half = lane < HEAD_DIM
    sq = k * k
    s0 = jnp.sum(jnp.where(lo_half, sq, 0.0), axis=-1, keepdims=True)
    s1 = jnp.sum(jnp.where(lo_half, 0.0, sq), axis=-1, keepdims=True)
    i0 = lax.rsqrt(s0 * (1.0 / HEAD_DIM) + EPS)
    i1 = lax.rsqrt(s1 * (1.0 / HEAD_DIM) + EPS)
    return k * jnp.where(lo_half, i0, i1) * kg


def _rep4(x):
    lane = lax.broadcasted_iota(jnp.int32, x.shape, 1)
    sw = pltpu.roll(x, HEAD_DIM, axis=1)
    a2 = jnp.where(lane < HEAD_DIM, x, sw)
    b2 = jnp.where(lane < HEAD_DIM, sw, x)
    return jnp.concatenate([a2, a2], axis=-1), jnp.concatenate([b2, b2], axis=-1)


def _write_bd(ref, g, row0, x4):
    rows = x4.shape[0]
    lane = lax.broadcasted_iota(jnp.int32, x4.shape, 1)
    for i in range(QUAD):
        keep = (lane >= i * HEAD_DIM) & (lane < (i + 1) * HEAD_DIM)
        ref[g, pl.ds(i * KEYS + row0, rows), :] = _bf(jnp.where(keep, x4, 0.0))


def _attend(qq, kbd, vbd, valid, sinks):
    s = lax.dot_general(qq, kbd, _NT, preferred_element_type=jnp.float32)
    s = jnp.where(valid, s, NEG)
    ps = []
    for i in range(QUAD):
        sh = s[:, i * KEYS:(i + 1) * KEYS]
        m = jnp.maximum(jnp.max(sh, axis=-1, keepdims=True), sinks[i])
        p = jnp.exp(sh - m)
        den = jnp.sum(p, axis=-1, keepdims=True) + jnp.exp(sinks[i] - m)
        ps.append(_bf(p / den))
    return _dot(jnp.concatenate(ps, axis=-1), vbd)


def _route(logits, tri_ref, carry):
    lane = lax.broadcasted_iota(jnp.int32, logits.shape, 1)
    lane_f = lane.astype(jnp.float32)
    big = float(RT_W)
    is_g = (lane >= N_EXPERTS) & (lane < N_EXPERTS + N_GROUPS)
    gl = jnp.where(is_g, logits, NEG)
    gmax = jnp.max(gl, axis=-1, keepdims=True)
    gsum = jnp.sum(jnp.where(is_g, jnp.exp(gl - gmax), 0.0), axis=-1, keepdims=True)
    g_w = 1.0 / gsum
    g_sel = jnp.min(jnp.where(is_g & (gl == gmax), lane_f, big), axis=-1, keepdims=True) - N_EXPERTS
    lo = g_sel * EPG
    in_grp = (lane_f >= lo) & (lane_f < lo + EPG)
    el = jnp.where(in_grp, logits, NEG)
    v1 = jnp.max(el, axis=-1, keepdims=True)
    i1 = jnp.min(jnp.where(in_grp & (el == v1), lane_f, big), axis=-1, keepdims=True)
    rest = in_grp & (lane_f != i1)
    el2 = jnp.where(rest, logits, NEG)
    v2 = jnp.max(el2, axis=-1, keepdims=True)
    i2 = jnp.min(jnp.where(rest & (el2 == v2), lane_f, big), axis=-1, keepdims=True)
    e2 = jnp.exp(v2 - v1)
    den = 1.0 + e2
    w1 = (1.0 / den) * g_w
    w2 = (e2 / den) * g_w
    oh1 = jnp.where(lane_f == i1, 1.0, 0.0)
    oh2 = jnp.where(lane_f == i2, 1.0, 0.0)
    both = oh1 + oh2
    before = _dot(tri_ref[...], _bf(both)) + carry
    r1 = jnp.sum(before * oh1, axis=-1, keepdims=True)
    r2 = jnp.sum(before * oh2, axis=-1, keepdims=True)
    rec = jnp.zeros_like(logits)
    for ln, val in ((R_E1, i1), (R_E2, i2), (R_W1, w1), (R_W2, w2), (R_RANK1, r1), (R_RANK2, r2)):
        rec = jnp.where(lane == ln, val, rec)
    return rec, carry + jnp.sum(both, axis=0, keepdims=True)


def _merge_tail(x, macc, wo_ref, g2_ref, wr_ref, br_ref, tri_ref, carry, x1_ref, h2p_ref, rt_ref):
    rows = x.shape[0]
    x1 = x + _dot(_bf(macc), wo_ref[...])
    x1_ref[...] = x1
    h2 = _rms(x1, g2_ref[...])
    pk = _pack_rows(h2)
    for j in range(PK):
        h2p_ref[pl.ds(j, rows, stride=PK), :] = pk[:, j * LANES:(j + 1) * LANES]
    logits = _dot(_bf(h2), wr_ref[...]) + br_ref[...]
    rec, carry = _route(logits, tri_ref, carry)
    rt_ref[...] = rec
    return carry


def _prompt_mixer_kernel(x_ref, sk_ref, g1_ref, win_ref, cw_ref, qg_ref, kg_ref, seg_ref, exp_ref,
                         wco_ref, wao_ref, wo_ref, g2_ref, wr_ref, br_ref, tri_ref,
                         x1_ref, h2p_ref, rt_ref, cnt_ref, kwin_ref, vwin_ref, cst_ref,
                         uext, qs_s, att_s, kn_s, kring, vring, carry_s, *, tile, blk_tiles):
    b = pl.program_id(0)
    it = pl.program_id(1)
    n_it = pl.num_programs(1)
    n_chunk = tile // CHUNK

    @pl.when((b == 0) & (it == 0))
    def _():
        kring[...] = jnp.zeros_like(kring)
        vring[...] = jnp.zeros_like(vring)

    @pl.when(it == 0)
    def _():
        uext[0:8, :] = jnp.zeros((8, D), jnp.float32)

    @pl.when(it % blk_tiles == 0)
    def _():
        carry_s[...] = jnp.zeros_like(carry_s)

    x = x_ref[0]
    h = _bf(_rms(x, g1_ref[...]))

    u = _dot(h, win_ref[:, C_C:C_C + D]) * _dot(h, win_ref[:, C_X:C_X + D])
    uext[8:8 + tile, :] = u
    conv = (uext[6:6 + tile, :] * cw_ref[0:1, :] + uext[7:7 + tile, :] * cw_ref[1:2, :]
            + u * cw_ref[2:3, :])
    yc = _bf(_dot(h, win_ref[:, C_B:C_B + D]) * conv)
    o_c = _dot(yc, wco_ref[...])
    macc = _sigmoid(_dot(h, win_ref[:, C_G:C_G + D])) * o_c
    last2 = uext[tile + 6:tile + 8, :]
    uext[6:8, :] = last2

    @pl.when(it == n_it - 1)
    def _():
        cst_ref[0] = last2

    q = _dot(h, win_ref[:, C_Q:C_Q + D])
    qn = _head_norm_q(q, seg_ref, exp_ref, qg_ref[...]) * (1.0 / math.sqrt(HEAD_DIM))
    for hd in range(N_HEADS):
        g, r = divmod(hd, GROUP_HEADS)
        piece = _bf(qn[:, hd * HEAD_DIM:(hd + 1) * HEAD_DIM])
        for c in range(n_chunk):
            qs_s[g, c * STACK + r * CHUNK:c * STACK + (r + 1) * CHUNK, :] = (
                piece[c * CHUNK:(c + 1) * CHUNK, :])
    kv = _dot(h, win_ref[:, C_K:C_K + 2 * LANES])
    kn = _head_norm_k(kv[:, 0:LANES], kg_ref[...])
    vv = kv[:, LANES:2 * LANES]
    kn_s[0] = kn
    kn_s[1] = vv

    @pl.when(it == n_it - 1)
    def _():
        kwin_ref[0] = kn[tile - WINDOW:, :]
        vwin_ref[0] = vv[tile - WINDOW:, :]

    slot_of_lane = lax.broadcasted_iota(jnp.int32, (1, KEYS), 1) // CHUNK
    lane = lax.broadcasted_iota(jnp.int32, (CHUNK, LANES), 1)
    lo_half = lane < HEAD_DIM
    ones_blk = jnp.ones((CHUNK, LANES), jnp.bfloat16)

    def chunk_body(c, carry):
        r0 = pl.multiple_of(c * CHUNK, CHUNK)
        row0 = pl.multiple_of((c % SLOTS) * CHUNK, CHUNK)
        kc = kn_s[0, pl.ds(r0, CHUNK), :]
        vc = kn_s[1, pl.ds(r0, CHUNK), :]
        vsw = pltpu.roll(vc, HEAD_DIM, axis=1)
        for g in range(N_KV):
            kring[g, pl.ds(row0, CHUNK), :] = _bf(kc[:, g * HEAD_DIM:(g + 1) * HEAD_DIM])
            vv2 = jnp.where(lo_half, vc, vsw) if g == 0 else jnp.where(lo_half, vsw, vc)
            vring[g, pl.ds(row0, CHUNK), 0:LANES] = _bf(vv2)
            vring[g, pl.ds(row0, CHUNK), LANES:2 * LANES] = ones_blk
        cg = it * n_chunk + c
        dist = (cg - slot_of_lane) & (SLOTS - 1)
        valid = dist <= jnp.minimum(cg, SLOTS - 2)
        q0 = pl.multiple_of(c * STACK, STACK)
        for g in range(N_KV):
            s = lax.dot_general(qs_s[g, pl.ds(q0, STACK), :], kring[g], _NT,
                                preferred_element_type=jnp.float32)
            s = jnp.where(valid, s, NEG)
            sk = sk_ref[g]
            m = jnp.maximum(jnp.max(s, axis=-1, keepdims=True), sk)
            o = _dot(_bf(jnp.exp(s - jnp.concatenate([m, m], axis=-1))), vring[g])
            out = o[:, 0:LANES] / (o[:, LANES:2 * LANES] + jnp.exp(sk - m))
            for r in range(0, GROUP_HEADS, 2):
                pair = jnp.where(lo_half, out[r * CHUNK:(r + 1) * CHUNK, :],
                                 out[(r + 1) * CHUNK:(r + 2) * CHUNK, :])
                col = (g * GROUP_HEADS + r) * HEAD_DIM
                att_s[pl.ds(r0, CHUNK), col:col + LANES] = _bf(pair)
        return carry

    lax.fori_loop(0, n_chunk, chunk_body, 0)

    o_a = _dot(att_s[...], wao_ref[...])
    macc = macc + _sigmoid(_dot(h, win_ref[:, C_G + D:C_G + 2 * D])) * o_a
    carry = _merge_tail(x, macc, wo_ref, g2_ref, wr_ref, br_ref, tri_ref, carry_s[...],
                        x1_ref.at[0], h2p_ref, rt_ref.at[0])
    carry_s[...] = carry

    @pl.when(it % blk_tiles == blk_tiles - 1)
    def _():
        cnt_ref[0] = jnp.broadcast_to(carry, (8, RT_W))


def _const_spec(shape):
    nd = len(shape)
    return pl.BlockSpec(shape, lambda *_: (0,) * nd, pipeline_mode=pl.Buffered(1))


def _prompt_mixer(x, sink_rows, wts, tile, blk_tiles):
    nb, seq, _ = x.shape
    n_it = seq // tile
    bps = n_it // blk_tiles
    row_spec = lambda w: pl.BlockSpec((1, tile, w), lambda b, i: (b, i, 0))
    str_spec = lambda r, w: pl.BlockSpec((1, r, w), lambda b, i: (b, 0, 0))
    consts = (sink_rows,) + tuple(wts)
    out_shape = [
        jax.ShapeDtypeStruct((nb, seq, D), jnp.float32),
        jax.ShapeDtypeStruct((nb * seq * PK, LANES), jnp.uint32),
        jax.ShapeDtypeStruct((nb, seq, RT_W), jnp.float32),
        jax.ShapeDtypeStruct((nb * bps, 8, RT_W), jnp.float32),
        jax.ShapeDtypeStruct((nb, WINDOW, LANES), jnp.float32),
        jax.ShapeDtypeStruct((nb, WINDOW, LANES), jnp.float32),
        jax.ShapeDtypeStruct((nb, 2, D), jnp.float32),
    ]
    return pl.pallas_call(
        functools.partial(_prompt_mixer_kernel, tile=tile, blk_tiles=blk_tiles),
        grid=(nb, n_it),
        in_specs=[row_spec(D)] + [_const_spec(w.shape) for w in consts],
        out_specs=[row_spec(D),
                   pl.BlockSpec((tile * PK, LANES), lambda b, i: (b * n_it + i, 0)),
                   row_spec(RT_W),
                   pl.BlockSpec((1, 8, RT_W), lambda b, i: (b * bps + i // blk_tiles, 0, 0)),
                   str_spec(WINDOW, LANES), str_spec(WINDOW, LANES), str_spec(2, D)],
        out_shape=out_shape,
        scratch_shapes=[
            pltpu.VMEM((tile + 8, D), jnp.float32),
            pltpu.VMEM((N_KV, tile * GROUP_HEADS, HEAD_DIM), jnp.bfloat16),
            pltpu.VMEM((tile, D), jnp.bfloat16),
            pltpu.VMEM((2, tile, LANES), jnp.float32),
            pltpu.VMEM((N_KV, KEYS, HEAD_DIM), jnp.bfloat16),
            pltpu.VMEM((N_KV, KEYS, 2 * LANES), jnp.bfloat16),
            pltpu.VMEM((1, RT_W), jnp.float32),
        ],
        compiler_params=pltpu.CompilerParams(
            dimension_semantics=("arbitrary", "arbitrary"), vmem_limit_bytes=VMEM_LIMIT),
        name="prompt_mixer",
    )(x, *consts)


def _sample_mixer_kernel(sink_ref, x_ref, ck_ref, cv_ref, sc_ref, g1_ref, win_ref, cw_ref, qg_ref,
                         kg_ref, seg_ref, exp_ref, wco_ref, wao_ref, wo_ref, g2_ref, wr_ref, br_ref,
                         tri_ref, x1_ref, h2p_ref, rt_ref, cnt_ref, kwin_ref, vwin_ref, cst_ref,
                         uext, att_s, kbd, vbd, *, n_stream, seq):
    ext = seq + 8
    x = x_ref[...]
    h = _bf(_rms(x, g1_ref[...]))

    u = _dot(h, win_ref[:, C_C:C_C + D]) * _dot(h, win_ref[:, C_X:C_X + D])
    for s in range(n_stream):
        uext[s * ext + 6:s * ext + 8, :] = sc_ref[s]
        uext[s * ext + 8:(s + 1) * ext, :] = u[s * seq:(s + 1) * seq, :]
        cst_ref[s] = u[(s + 1) * seq - 2:(s + 1) * seq, :]
    conv = jnp.concatenate(
        [uext[s * ext + 6:s * ext + 6 + seq, :] * cw_ref[0:1, :]
         + uext[s * ext + 7:s * ext + 7 + seq, :] * cw_ref[1:2, :] for s in range(n_stream)],
        axis=0) + u * cw_ref[2:3, :]
    yc = _bf(_dot(h, win_ref[:, C_B:C_B + D]) * conv)
    o_c = _dot(yc, wco_ref[...])
    macc = _sigmoid(_dot(h, win_ref[:, C_G:C_G + D])) * o_c

    q = _dot(h, win_ref[:, C_Q:C_Q + D])
    qn = _bf(_head_norm_q(q, seg_ref, exp_ref, qg_ref[...]) * (1.0 / math.sqrt(HEAD_DIM)))
    kv = _dot(h, win_ref[:, C_K:C_K + 2 * LANES])
    kn = _head_norm_k(kv[:, 0:LANES], kg_ref[...])
    vv = kv[:, LANES:2 * LANES]

    kbd[...] = jnp.zeros_like(kbd)
    vbd[...] = jnp.zeros_like(vbd)
    lane = lax.broadcasted_iota(jnp.int32, (1, QUAD * KEYS), 1)
    valid = (lane % KEYS) < WINDOW + seq
    for s in range(n_stream):
        rows = slice(s * seq, (s + 1) * seq)
        kwin_ref[s, 0:WINDOW - seq, :] = ck_ref[s, seq:WINDOW, :]
        kwin_ref[s, WINDOW - seq:WINDOW, :] = kn[rows, :]
        vwin_ref[s, 0:WINDOW - seq, :] = cv_ref[s, seq:WINDOW, :]
        vwin_ref[s, WINDOW - seq:WINDOW, :] = vv[rows, :]
        kc4 = _rep4(ck_ref[s])
        vc4 = _rep4(cv_ref[s])
        kn4 = _rep4(kn[rows, :])
        vn4 = _rep4(vv[rows, :])
        for g in range(N_KV):
            _write_bd(kbd, g, 0, kc4[g])
            _write_bd(kbd, g, WINDOW, kn4[g])
            _write_bd(vbd, g, 0, vc4[g])
            _write_bd(vbd, g, WINDOW, vn4[g])
        for g in range(N_KV):
            kb = kbd[g]
            vb = vbd[g]
            for j in range(N_HEADS // N_KV // QUAD):
                h0 = g * (N_HEADS // N_KV) + j * QUAD
                qq = qn[rows, h0 * HEAD_DIM:(h0 + QUAD) * HEAD_DIM]
                sinks = [sink_ref[h0 + i] for i in range(QUAD)]
                o = _attend(qq, kb, vb, valid, sinks)
                att_s[rows, h0 * HEAD_DIM:(h0 + QUAD) * HEAD_DIM] = _bf(o)

    o_a = _dot(att_s[...], wao_ref[...])
    macc = macc + _sigmoid(_dot(h, win_ref[:, C_G + D:C_G + 2 * D])) * o_a
    carry = _merge_tail(x, macc, wo_ref, g2_ref, wr_ref, br_ref, tri_ref,
                        jnp.zeros((1, RT_W), jnp.float32), x1_ref, h2p_ref, rt_ref)
    cnt_ref[0] = jnp.broadcast_to(carry, (8, RT_W))


def _sample_mixer(x, ck, cv, sc, sinks, wts):
    nb, seq, _ = x.shape
    rows = nb * seq
    x2 = x.reshape(rows, D)
    ins = (x2, ck, cv, sc) + tuple(wts)
    full = lambda a: pl.BlockSpec(a.shape, lambda *_, nd=a.ndim: (0,) * nd)
    out_shape = [
        jax.ShapeDtypeStruct((rows, D), jnp.float32),
        jax.ShapeDtypeStruct((rows * PK, LANES), jnp.uint32),
        jax.ShapeDtypeStruct((rows, RT_W), jnp.float32),
        jax.ShapeDtypeStruct((1, 8, RT_W), jnp.float32),
        jax.ShapeDtypeStruct((nb, WINDOW, LANES), jnp.float32),
        jax.ShapeDtypeStruct((nb, WINDOW, LANES), jnp.float32),
        jax.ShapeDtypeStruct((nb, 2, D), jnp.float32),
    ]
    grid_spec = pltpu.PrefetchScalarGridSpec(
        num_scalar_prefetch=1,
        grid=(1,),
        in_specs=[full(a) for a in ins],
        out_specs=[full(a) for a in out_shape],
        scratch_shapes=[
            pltpu.VMEM((nb * (seq + 8), D), jnp.float32),
            pltpu.VMEM((rows, D), jnp.bfloat16),
            pltpu.VMEM((N_KV, QUAD * KEYS, QUAD_W), jnp.bfloat16),
            pltpu.VMEM((N_KV, QUAD * KEYS, QUAD_W), jnp.bfloat16),
        ])
    return pl.pallas_call(
        functools.partial(_sample_mixer_kernel, n_stream=nb, seq=seq),
        grid_spec=grid_spec,
        out_shape=out_shape,
        compiler_params=pltpu.CompilerParams(
            dimension_semantics=("arbitrary",), vmem_limit_bytes=VMEM_LIMIT),
        name="sample_mixer",
    )(sinks, *ins)


def _sorted_rows(blk_tokens):
    rows = 2 * blk_tokens + N_EXPERTS * (MOE_PAD - 1) + MOE_TILE
    return -(-rows // MOE_TILE) * MOE_TILE


def _moe_kernel(pos_ref, tab_ref, stp_ref, h2p_ref, x1_ref, rt_ref, wg_ref, wu_ref, wd_ref, y_ref,
                xo, g1, g2, *, tile, n_tiles, max_tiles):
    b = pl.program_id(0)
    s = pl.program_id(1)
    n_groups = tile // UNROLL

    @pl.when((b == 0) & (s == 0))
    def _():
        xo[...] = jnp.zeros_like(xo)

    @pl.when(s < n_tiles)
    def _dispatch():
        base = (b * n_tiles + s) * (2 * tile)

        def group(gi, carry):
            i0 = base + gi * (2 * UNROLL)
            t0 = gi * (UNROLL * PK)
            for k in range(UNROLL):
                row = h2p_ref[pl.ds(pl.multiple_of(t0 + k * PK, PK), PK), :]
                xo[pl.ds(pl.multiple_of(pos_ref[i0 + 2 * k], PK), PK), :] = row
                xo[pl.ds(pl.multiple_of(pos_ref[i0 + 2 * k + 1], PK), PK), :] = row
            return carry

        lax.fori_loop(0, n_groups, group, 0)

    @pl.when((s >= n_tiles) & (s < n_tiles + EXPERT_STEPS))
    def _experts():
        st = b * EXPERT_STEPS + s - n_tiles
        first = stp_ref[2 * st]
        last = stp_ref[2 * st + 1]
        half = MOE_TILE // 2

        def entry(j):
            t = (b * max_tiles + j) * 3
            return pl.multiple_of(tab_ref[t], MOE_PAD * PK), tab_ref[t + 1], tab_ref[t + 2]

        def up(j):
            base, k, _ = entry(j)
            xs = _bf(_unpack_rows(
                [xo[pl.ds(base + q, MOE_TILE, stride=PK), :] for q in range(PK)]))
            gt = _dot(xs, wg_ref[k])
            return _bf(gt * _sigmoid(gt) * _dot(xs, wu_ref[k]))

        def down(j, hid):
            base, k, full = entry(j)
            out = _pack_rows(_dot(hid, wd_ref[k]))
            def put(lo):
                for q in range(PK):
                    xo[pl.ds(base + lo * PK + q, half, stride=PK), :] = (
                        out[lo:lo + half, q * LANES:(q + 1) * LANES])

            put(0)

            @pl.when(full > 0)
            def _():
                put(half)

        @pl.when(last > first)
        def _():
            def both(j, hid):
                nxt = up(j)
                down(j - 1, hid)
                return nxt

            down(last - 1, lax.fori_loop(first + 1, last, both, up(first)))

    @pl.when(s >= n_tiles + EXPERT_STEPS)
    def _combine():
        i = s - n_tiles - EXPERT_STEPS
        base = (b * n_tiles + i) * (2 * tile)

        def group(gi, carry):
            i0 = base + gi * (2 * UNROLL)
            t0 = gi * (UNROLL * PK)
            for k in range(UNROLL):
                dst = pl.ds(pl.multiple_of(t0 + k * PK, PK), PK)
                g1[dst, :] = xo[pl.ds(pl.multiple_of(pos_ref[i0 + 2 * k], PK), PK), :]
                g2[dst, :] = xo[pl.ds(pl.multiple_of(pos_ref[i0 + 2 * k + 1], PK), PK), :]
            return carry

        lax.fori_loop(0, n_groups, group, 0)
        o1 = _unpack_rows([g1[pl.ds(q, tile, stride=PK), :] for q in range(PK)])
        o2 = _unpack_rows([g2[pl.ds(q, tile, stride=PK), :] for q in range(PK)])
        rt = rt_ref[...]
        y_ref[...] = x1_ref[...] + rt[:, R_W1:R_W1 + 1] * o1 + rt[:, R_W2:R_W2 + 1] * o2


def _moe(pos, tab, stp, h2p, x1, rt, wg, wu, wd, tile, n_tiles):
    n = x1.shape[0]
    n_blocks = n // (tile * n_tiles)
    n_steps = 2 * n_tiles + EXPERT_STEPS
    last = n_tiles - 1

    def disp(b, s, *_):
        return (b * n_tiles + jnp.minimum(s, last), 0)

    def comb(b, s, *_):
        return (b * n_tiles + jnp.clip(s - n_tiles - EXPERT_STEPS, 0, last), 0)

    def wmap(b, s, *_):
        return (jnp.clip(s - n_tiles, 0, EXPERT_STEPS - 1), 0, 0)

    grid_spec = pltpu.PrefetchScalarGridSpec(
        num_scalar_prefetch=3,
        grid=(n_blocks, n_steps),
        in_specs=[pl.BlockSpec((tile * PK, LANES), disp),
                  pl.BlockSpec((tile, D), comb),
                  pl.BlockSpec((tile, RT_W), comb),
                  pl.BlockSpec((STEP_EXPERTS, D, D_EXPERT), wmap),
                  pl.BlockSpec((STEP_EXPERTS, D, D_EXPERT), wmap),
                  pl.BlockSpec((STEP_EXPERTS, D_EXPERT, D), wmap)],
        out_specs=pl.BlockSpec((tile, D), comb),
        scratch_shapes=[
            pltpu.VMEM((_sorted_rows(tile * n_tiles) * PK, LANES), jnp.uint32),
            pltpu.VMEM((tile * PK, LANES), jnp.uint32),
            pltpu.VMEM((tile * PK, LANES), jnp.uint32),
        ])
    return pl.pallas_call(
        functools.partial(_moe_kernel, tile=tile, n_tiles=n_tiles,
                          max_tiles=_max_tiles(tile * n_tiles)),
        grid_spec=grid_spec,
        out_shape=jax.ShapeDtypeStruct((n, D), jnp.float32),
        compiler_params=pltpu.CompilerParams(
            dimension_semantics=("arbitrary", "arbitrary"), vmem_limit_bytes=VMEM_LIMIT),
        name="moe",
    )(pos, tab, stp, h2p, x1, rt, wg, wu, wd)


def _max_tiles(blk_tokens):
    return N_EXPERTS + 2 * blk_tokens // MOE_TILE


def _sorted_positions(rt, cnt, blk_tokens):
    experts = jnp.arange(N_EXPERTS, dtype=jnp.int32)
    counts = cnt[:, 0, :N_EXPERTS].astype(jnp.int32)
    padded = (counts + MOE_PAD - 1) // MOE_PAD * MOE_PAD
    off = jnp.cumsum(padded, axis=1) - padded
    e = rt[:, R_E1:R_E2 + 1].astype(jnp.int32)
    rank = rt[:, R_RANK1:R_RANK2 + 1].astype(jnp.int32)
    off_tok = jnp.repeat(off, blk_tokens, axis=0)
    hit = e[:, :, None] == experts
    pos = (jnp.sum(jnp.where(hit, off_tok[:, None, :], 0), axis=-1) + rank) * PK

    tiles = (counts + MOE_TILE - 1) // MOE_TILE
    t_end = jnp.cumsum(tiles, axis=1)
    t_beg = t_end - tiles
    j = jnp.arange(_max_tiles(blk_tokens), dtype=jnp.int32)
    e_of = jnp.minimum(jnp.sum(t_end[:, None, :] <= j[None, :, None], axis=-1), N_EXPERTS - 1)
    pick = lambda a: jnp.sum(jnp.where(e_of[:, :, None] == experts, a[:, None, :], 0), axis=-1)
    local = (j[None, :] - pick(t_beg)) * MOE_TILE
    tab = jnp.stack([(pick(off) + local) * PK, e_of % STEP_EXPERTS,
                     (pick(counts) - local > MOE_TILE // 2).astype(jnp.int32)], axis=-1)
    stp = jnp.stack([t_beg[:, ::STEP_EXPERTS], t_end[:, STEP_EXPERTS - 1::STEP_EXPERTS]], axis=-1)
    return pos.reshape(-1), tab.reshape(-1), stp.reshape(-1)


def _segment_matrices():
    head = jnp.arange(D) // HEAD_DIM
    col = jnp.arange(LANES)
    seg = (head[:, None] == col[None, :]).astype(jnp.bfloat16)
    row = jnp.arange(2 * LANES) % LANES
    expm = (row[:, None] == head[None, :]).astype(jnp.bfloat16)
    return seg, expm


def _strict_lower(n):
    r = jnp.arange(n)
    return (r[None, :] < r[:, None]).astype(jnp.bfloat16)


def kernel(x_prompt, x_sample, cache_k, cache_v, state_conv, norm_mix_g, w_in, conv_w, q_norm_g,
           k_norm_g, attn_sinks, w_conv_out, w_attn_out, w_out, norm_ffn_g, w_group, b_group,
           w_router, b_router, w_gate, w_up, w_down):
    nb, seq, _ = x_prompt.shape
    ns, sseq, _ = x_sample.shape
    l = 0
    seg, expm = _segment_matrices()
    wr = jnp.zeros((D, RT_W), jnp.float32)
    wr = wr.at[:, 0:N_EXPERTS].set(w_router[l]).at[:, N_EXPERTS:N_EXPERTS + N_GROUPS].set(w_group[l])
    br = jnp.zeros((1, RT_W), jnp.float32)
    br = br.at[0, 0:N_EXPERTS].set(b_router[l]).at[0, N_EXPERTS:N_EXPERTS + N_GROUPS].set(b_group[l])
    wts = (norm_mix_g[l].reshape(1, D), _bf(w_in[l]), conv_w[l],
           jnp.tile(q_norm_g[l], N_HEADS).reshape(1, D), jnp.tile(k_norm_g[l], N_KV).reshape(1, LANES),
           seg, expm, _bf(w_conv_out[l]), _bf(w_attn_out[l]), _bf(w_out[l]),
           norm_ffn_g[l].reshape(1, D), _bf(wr), br)
    sinks = attn_sinks[l]
    n_s = ns * sseq

    sink_rows = jnp.broadcast_to(
        jnp.repeat(sinks.reshape(N_KV, GROUP_HEADS), CHUNK, axis=1)[:, :, None], (N_KV, STACK, LANES))
    x1p, h2pp, rtp, cntp, kwp, vwp, cstp = _prompt_mixer(
        x_prompt, sink_rows, wts + (_strict_lower(TOK_TILE),), tile=TOK_TILE, blk_tiles=BLOCK_TILES)
    x1s, h2ps, rts, cnts, kws, vws, csts = _sample_mixer(
        x_sample, cache_k[l].reshape(ns, WINDOW, LANES), cache_v[l].reshape(ns, WINDOW, LANES),
        state_conv[l], sinks, wts + (_strict_lower(n_s),))

    wg, wu, wd = _bf(w_gate[l]), _bf(w_up[l]), _bf(w_down[l])
    rtp = rtp.reshape(nb * seq, RT_W)
    yp = _moe(*_sorted_positions(rtp, cntp, TOK_TILE * BLOCK_TILES), h2pp, x1p.reshape(nb * seq, D), rtp, wg, wu, wd,
              tile=TOK_TILE, n_tiles=BLOCK_TILES)
    ys = _moe(*_sorted_positions(rts, cnts, n_s), h2ps, x1s, rts, wg, wu, wd, tile=n_s, n_tiles=1)

    kv_shape = lambda n: (1, n, WINDOW, N_KV, HEAD_DIM)
    return (yp.reshape(nb, seq, D), ys.reshape(ns, sseq, D),
            kwp.reshape(kv_shape(nb)), vwp.reshape(kv_shape(nb)), cstp.reshape(1, nb, 2, D),
            kws.reshape(kv_shape(ns)), vws.reshape(kv_shape(ns)), csts.reshape(1, ns, 2, D))
```

```python
import functools
import math

import jax
import jax.numpy as jnp
from jax import lax
from jax.experimental import pallas as pl
from jax.experimental.pallas import tpu as pltpu

D = 1024
CHUNK = 64
HEAD_DIM = 64
N_HEADS = 16
N_KV = 2
WINDOW = 128
N_GROUPS = 4
EPG = 8
N_EXPERTS = 32
D_EXPERT = 256
EPS = 1e-6
NEG = -1e30

LANES = 128
QUAD = 4
QUAD_W = QUAD * HEAD_DIM
SLOTS = 4
KEYS = SLOTS * CHUNK
GROUP_HEADS = N_HEADS // N_KV
STACK = GROUP_HEADS * CHUNK
BIAS_LANE = HEAD_DIM
SINK_LANE = HEAD_DIM + 1
LOG2E = 1.0 / math.log(2.0)
RT_W = LANES
HALF = D // 2
PK = HALF // LANES

R_E1, R_E2, R_W1, R_W2, R_RANK1, R_RANK2 = 0, 1, 2, 3, 4, 5

C_B, C_C, C_X, C_Q, C_K, C_V, C_G = 0, 1024, 2048, 3072, 4096, 4224, 4352

MOE_TILE = 256
MOE_PAD = 128
TOK_TILE = 512
BLOCK_TILES = 8
UNROLL = 16
STEP_EXPERTS = 4
EXPERT_STEPS = N_EXPERTS // STEP_EXPERTS

VMEM_LIMIT = 60 * 1024 * 1024

_NT = (((1,), (1,)), ((), ()))


def _bf(x):
    return x.astype(jnp.bfloat16)


def _dot(a, b):
    return jnp.dot(a, b, preferred_element_type=jnp.float32)


def _rms(x, g):
    return x * lax.rsqrt(jnp.mean(x * x, axis=-1, keepdims=True) + EPS) * g


def _sigmoid(x):
    return 1.0 / (1.0 + jnp.exp(-x))


def _pack_rows(x):
    return pltpu.pack_elementwise([x[:, :HALF], x[:, HALF:]], packed_dtype=jnp.bfloat16)


def _unpack_rows(parts):
    def half(i):
        return [pltpu.unpack_elementwise(p, index=i, packed_dtype=jnp.bfloat16,
                                         unpacked_dtype=jnp.float32) for p in parts]
    return jnp.concatenate(half(0) + half(1), axis=-1)


def _head_norm_q(q, seg_ref, exp_ref, qg):
    ssq = _dot(_bf(q * q), seg_ref[...])
    inv = lax.rsqrt(ssq * (1.0 / HEAD_DIM) + EPS)
    hi = _bf(inv)
    lo = _bf(inv - hi.astype(jnp.float32))
    full = _dot(jnp.concatenate([hi, lo], axis=-1), exp_ref[...])
    return q * full * qg


def _head_norm_k(k, kg):
    lane = lax.broadcasted_iota(jnp.int32, k.shape, 1)
    lo_half = lane < HEAD_DIM
    sq = k * k
    s0 = jnp.sum(jnp.where(lo_half, sq, 0.0), axis=-1, keepdims=True)
    s1 = jnp.sum(jnp.where(lo_half, 0.0, sq), axis=-1, keepdims=True)
    i0 = lax.rsqrt(s0 * (1.0 / HEAD_DIM) + EPS)
    i1 = lax.rsqrt(s1 * (1.0 / HEAD_DIM) + EPS)
    return k * jnp.where(lo_half, i0, i1) * kg


def _rep4(x):
    lane = lax.broadcasted_iota(jnp.int32, x.shape, 1)
    sw = pltpu.roll(x, HEAD_DIM, axis=1)
    a2 = jnp.where(lane < HEAD_DIM, x, sw)
    b2 = jnp.where(lane < HEAD_DIM, sw, x)
    return jnp.concatenate([a2, a2], axis=-1), jnp.concatenate([b2, b2], axis=-1)


def _write_bd(ref, g, row0, x4):
    rows = x4.shape[0]
    lane = lax.broadcasted_iota(jnp.int32, x4.shape, 1)
    for i in range(QUAD):
        keep = (lane >= i * HEAD_DIM) & (lane < (i + 1) * HEAD_DIM)
        ref[g, pl.ds(i * KEYS + row0, rows), :] = _bf(jnp.where(keep, x4, 0.0))


def _attend(qq, kbd, vbd, valid, sinks):
    s = lax.dot_general(qq, kbd, _NT, preferred_element_type=jnp.float32)
    s = jnp.where(valid, s, NEG)
    ps = []
    for i in range(QUAD):
        sh = s[:, i * KEYS:(i + 1) * KEYS]
        m = jnp.maximum(jnp.max(sh, axis=-1, keepdims=True), sinks[i])
        p = jnp.exp(sh - m)
        den = jnp.sum(p, axis=-1, keepdims=True) + jnp.exp(sinks[i] - m)
        ps.append(_bf(p / den))
    return _dot(jnp.concatenate(ps, axis=-1), vbd)


def _route(logits, tri_ref, carry):
    lane = lax.broadcasted_iota(jnp.int32, logits.shape, 1)
    lane_f = lane.astype(jnp.float32)
    big = float(RT_W)
    is_g = (lane >= N_EXPERTS) & (lane < N_EXPERTS + N_GROUPS)
    gl = jnp.where(is_g, logits, NEG)
    gmax = jnp.max(gl, axis=-1, keepdims=True)
    gsum = jnp.sum(jnp.where(is_g, jnp.exp(gl - gmax), 0.0), axis=-1, keepdims=True)
    g_w = 1.0 / gsum
    g_sel = jnp.min(jnp.where(is_g & (gl == gmax), lane_f, big), axis=-1, keepdims=True) - N_EXPERTS
    lo = g_sel * EPG
    in_grp = (lane_f >= lo) & (lane_f < lo + EPG)
    el = jnp.where(in_grp, logits, NEG)
    v1 = jnp.max(el, axis=-1, keepdims=True)
    i1 = jnp.min(jnp.where(in_grp & (el == v1), lane_f, big), axis=-1, keepdims=True)
    rest = in_grp & (lane_f != i1)
    el2 = jnp.where(rest, logits, NEG)
    v2 = jnp.max(el2, axis=-1, keepdims=True)
    i2 = jnp.min(jnp.where(rest & (el2 == v2), lane_f, big), axis=-1, keepdims=True)
    e2 = jnp.exp(v2 - v1)
    den = 1.0 + e2
    w1 = (1.0 / den) * g_w
    w2 = (e2 / den) * g_w
    oh1 = jnp.where(lane_f == i1, 1.0, 0.0)
    oh2 = jnp.where(lane_f == i2, 1.0, 0.0)
    both = oh1 + oh2
    before = _dot(tri_ref[...], _bf(both)) + carry
    r1 = jnp.sum(before * oh1, axis=-1, keepdims=True)
    r2 = jnp.sum(before * oh2, axis=-1, keepdims=True)
    rec = jnp.zeros_like(logits)
    for ln, val in ((R_E1, i1), (R_E2, i2), (R_W1, w1), (R_W2, w2), (R_RANK1, r1), (R_RANK2, r2)):
        rec = jnp.where(lane == ln, val, rec)
    return rec, carry + jnp.sum(both, axis=0, keepdims=True)


def _merge_tail(x, macc, wo_ref, g2_ref, wr_ref, br_ref, tri_ref, carry, x1_ref, h2p_ref, rt_ref):
    rows = x.shape[0]
    x1 = x + _dot(_bf(macc), wo_ref[...])
    x1_ref[...] = x1
    h2 = _rms(x1, g2_ref[...])
    pk = _pack_rows(h2)
    for j in range(PK):
        h2p_ref[pl.ds(j, rows, stride=PK), :] = pk[:, j * LANES:(j + 1) * LANES]
    logits = _dot(_bf(h2), wr_ref[...]) + br_ref[...]
    rec, carry = _route(logits, tri_ref, carry)
    rt_ref[...] = rec
    return carry


def _prompt_mixer_kernel(x_ref, qx_ref, g1_ref, win_ref, cw_ref, qg_ref, kg_ref, seg_ref, exp_ref,
                         wco_ref, wao_ref, wo_ref, g2_ref, wr_ref, br_ref, tri_ref,
                         x1_ref, h2p_ref, rt_ref, cnt_ref, kwin_ref, vwin_ref, cst_ref,
                         uext, qs_s, att_s, kn_s, kop, vop, khist, vhist, carry_s, *, tile, blk_tiles):
    b = pl.program_id(0)
    it = pl.program_id(1)
    n_it = pl.num_programs(1)
    n_chunk = tile // CHUNK

    @pl.when((b == 0) & (it == 0))
    def _():
        shape = (n_chunk, N_KV, CHUNK, LANES)
        row = lax.broadcasted_iota(jnp.int32, shape, 2)
        ln = lax.broadcasted_iota(jnp.int32, shape, 3)
        is_sink = (row == 0) & ((ln == SINK_LANE) | (ln == SINK_LANE + 1))
        tail = pl.ds((SLOTS - 1) * CHUNK, CHUNK)
        kop[:, :, tail, :] = _bf(jnp.where(is_sink, 1.0, jnp.where((row > 0) & (ln == BIAS_LANE), NEG, 0.0)))
        vop[:, :, tail, 0:LANES] = jnp.zeros(shape, jnp.bfloat16)
        vop[:, :, tail, LANES:2 * LANES] = jnp.ones(shape, jnp.bfloat16)

    @pl.when(it == 0)
    def _():
        uext[0:8, :] = jnp.zeros((8, D), jnp.float32)
        ln = lax.broadcasted_iota(jnp.int32, khist.shape, 2)
        khist[...] = _bf(jnp.where(ln == BIAS_LANE, NEG, 0.0))
        vhist[...] = jnp.zeros_like(vhist)

    @pl.when(it % blk_tiles == 0)
    def _():
        carry_s[...] = jnp.zeros_like(carry_s)

    x = x_ref[0]
    h = _bf(_rms(x, g1_ref[...]))

    u = _dot(h, win_ref[:, C_C:C_C + D]) * _dot(h, win_ref[:, C_X:C_X + D])
    uext[8:8 + tile, :] = u
    conv = (uext[6:6 + tile, :] * cw_ref[0:1, :] + uext[7:7 + tile, :] * cw_ref[1:2, :]
            + u * cw_ref[2:3, :])
    yc = _bf(_dot(h, win_ref[:, C_B:C_B + D]) * conv)
    o_c = _dot(yc, wco_ref[...])
    macc = _sigmoid(_dot(h, win_ref[:, C_G:C_G + D])) * o_c
    last2 = uext[tile + 6:tile + 8, :]
    uext[6:8, :] = last2

    @pl.when(it == n_it - 1)
    def _():
        cst_ref[0] = last2

    q = _dot(h, win_ref[:, C_Q:C_Q + D])
    qn = _head_norm_q(q, seg_ref, exp_ref, qg_ref[...]) * (LOG2E / math.sqrt(HEAD_DIM))
    for hd in range(N_HEADS):
        g, r = divmod(hd, GROUP_HEADS)
        ext = jnp.broadcast_to(qx_ref[hd], (tile, HEAD_DIM))
        piece = _bf(jnp.concatenate([qn[:, hd * HEAD_DIM:(hd + 1) * HEAD_DIM], ext], axis=-1))
        for c in range(n_chunk):
            qs_s[g, c * STACK + r * CHUNK:c * STACK + (r + 1) * CHUNK, :] = (
                piece[c * CHUNK:(c + 1) * CHUNK, :])
    kv = _dot(h, win_ref[:, C_K:C_K + 2 * LANES])
    kn = _head_norm_k(kv[:, 0:LANES], kg_ref[...])
    vv = kv[:, LANES:2 * LANES]
    kn_s[0] = kn
    kn_s[1] = vv

    @pl.when(it == n_it - 1)
    def _():
        kwin_ref[0] = kn[tile - WINDOW:, :]
        vwin_ref[0] = vv[tile - WINDOW:, :]

    lane = lax.broadcasted_iota(jnp.int32, (CHUNK, LANES), 1)
    lo_half = lane < HEAD_DIM
    ones_blk = jnp.ones((CHUNK, LANES), jnp.bfloat16)
    own = SLOTS - 2

    for c in range(own):
        for sl in range(own - c):
            src = pl.ds((c + sl) * CHUNK, CHUNK)
            kop[c, :, pl.ds(sl * CHUNK, CHUNK), :] = khist[:, src, :]
            vop[c, :, pl.ds(sl * CHUNK, CHUNK), :] = vhist[:, src, :]

    def chunk_body(c):
        r0 = c * CHUNK
        kc = kn_s[0, pl.ds(r0, CHUNK), :]
        vc = kn_s[1, pl.ds(r0, CHUNK), :]
        ksw = pltpu.roll(kc, HEAD_DIM, axis=1)
        vsw = pltpu.roll(vc, HEAD_DIM, axis=1)
        for g in range(N_KV):
            kb = _bf(jnp.where(lo_half, kc if g == 0 else ksw, 0.0))
            vb = _bf(jnp.where(lo_half, vc, vsw) if g == 0 else jnp.where(lo_half, vsw, vc))
            for d in range(own + 1):
                if c + d < n_chunk:
                    dst = pl.ds((own - d) * CHUNK, CHUNK)
                    kop[c + d, g, dst, :] = kb
                    vop[c + d, g, dst, 0:LANES] = vb
                    vop[c + d, g, dst, LANES:2 * LANES] = ones_blk
            if c >= n_chunk - own:
                dst = pl.ds((c - (n_chunk - own)) * CHUNK, CHUNK)
                khist[g, dst, :] = kb
                vhist[g, dst, 0:LANES] = vb
                vhist[g, dst, LANES:2 * LANES] = ones_blk
        for g in range(N_KV):
            s = lax.dot_general(qs_s[g, pl.ds(c * STACK, STACK), :], kop[c, g], _NT,
                                preferred_element_type=jnp.float32)
            m = jnp.max(s, axis=-1, keepdims=True)
            o = _dot(_bf(jnp.exp2(s - m)), vop[c, g])
            out = o[:, 0:LANES] / o[:, LANES:2 * LANES]
            for r in range(0, GROUP_HEADS, 2):
                pair = jnp.where(lo_half, out[r * CHUNK:(r + 1) * CHUNK, :],
                                 out[(r + 1) * CHUNK:(r + 2) * CHUNK, :])
                col = (g * GROUP_HEADS + r) * HEAD_DIM
                att_s[pl.ds(r0, CHUNK), col:col + LANES] = _bf(pair)

    for c in range(n_chunk):
        chunk_body(c)

    o_a = _dot(att_s[...], wao_ref[...])
    macc = macc + _sigmoid(_dot(h, win_ref[:, C_G + D:C_G + 2 * D])) * o_a
    carry = _merge_tail(x, macc, wo_ref, g2_ref, wr_ref, br_ref, tri_ref, carry_s[...],
                        x1_ref.at[0], h2p_ref, rt_ref.at[0])
    carry_s[...] = carry

    @pl.when(it % blk_tiles == blk_tiles - 1)
    def _():
        cnt_ref[0] = jnp.broadcast_to(carry, (8, RT_W))


def _const_spec(shape):
    nd = len(shape)
    return pl.BlockSpec(shape, lambda *_: (0,) * nd, pipeline_mode=pl.Buffered(1))


def _prompt_mixer(x, q_extra, wts, tile, blk_tiles):
    nb, seq, _ = x.shape
    n_it = seq // tile
    bps = n_it // blk_tiles
    row_spec = lambda w: pl.BlockSpec((1, tile, w), lambda b, i: (b, i, 0))
    str_spec = lambda r, w: pl.BlockSpec((1, r, w), lambda b, i: (b, 0, 0))
    consts = (q_extra,) + tuple(wts)
    out_shape = [
        jax.ShapeDtypeStruct((nb, seq, D), jnp.float32),
        jax.ShapeDtypeStruct((nb * seq * PK, LANES), jnp.uint32),
        jax.ShapeDtypeStruct((nb, seq, RT_W), jnp.float32),
        jax.ShapeDtypeStruct((nb * bps, 8, RT_W), jnp.float32),
        jax.ShapeDtypeStruct((nb, WINDOW, LANES), jnp.float32),
        jax.ShapeDtypeStruct((nb, WINDOW, LANES), jnp.float32),
        jax.ShapeDtypeStruct((nb, 2, D), jnp.float32),
    ]
    return pl.pallas_call(
        functools.partial(_prompt_mixer_kernel, tile=tile, blk_tiles=blk_tiles),
        grid=(nb, n_it),
        in_specs=[row_spec(D)] + [_const_spec(w.shape) for w in consts],
        out_specs=[row_spec(D),
                   pl.BlockSpec((tile * PK, LANES), lambda b, i: (b * n_it + i, 0)),
                   row_spec(RT_W),
                   pl.BlockSpec((1, 8, RT_W), lambda b, i: (b * bps + i // blk_tiles, 0, 0)),
                   str_spec(WINDOW, LANES), str_spec(WINDOW, LANES), str_spec(2, D)],
        out_shape=out_shape,
        scratch_shapes=[
            pltpu.VMEM((tile + 8, D), jnp.float32),
            pltpu.VMEM((N_KV, tile * GROUP_HEADS, LANES), jnp.bfloat16),
            pltpu.VMEM((tile, D), jnp.bfloat16),
            pltpu.VMEM((2, tile, LANES), jnp.float32),
            pltpu.VMEM((tile // CHUNK, N_KV, KEYS, LANES), jnp.bfloat16),
            pltpu.VMEM((tile // CHUNK, N_KV, KEYS, 2 * LANES), jnp.bfloat16),
            pltpu.VMEM((N_KV, (SLOTS - 2) * CHUNK, LANES), jnp.bfloat16),
            pltpu.VMEM((N_KV, (SLOTS - 2) * CHUNK, 2 * LANES), jnp.bfloat16),
            pltpu.VMEM((1, RT_W), jnp.float32),
        ],
        compiler_params=pltpu.CompilerParams(
            dimension_semantics=("arbitrary", "arbitrary"), vmem_limit_bytes=VMEM_LIMIT),
        name="prompt_mixer",
    )(x, *consts)


def _sample_mixer_kernel(sink_ref, x_ref, ck_ref, cv_ref, sc_ref, g1_ref, win_ref, cw_ref, qg_ref,
                         kg_ref, seg_ref, exp_ref, wco_ref, wao_ref, wo_ref, g2_ref, wr_ref, br_ref,
                         tri_ref, x1_ref, h2p_ref, rt_ref, cnt_ref, kwin_ref, vwin_ref, cst_ref,
                         uext, att_s, kbd, vbd, *, n_stream, seq):
    ext = seq + 8
    x = x_ref[...]
    h = _bf(_rms(x, g1_ref[...]))

    u = _dot(h, win_ref[:, C_C:C_C + D]) * _dot(h, win_ref[:, C_X:C_X + D])
    for s in range(n_stream):
        uext[s * ext + 6:s * ext + 8, :] = sc_ref[s]
        uext[s * ext + 8:(s + 1) * ext, :] = u[s * seq:(s + 1) * seq, :]
        cst_ref[s] = u[(s + 1) * seq - 2:(s + 1) * seq, :]
    conv = jnp.concatenate(
        [uext[s * ext + 6:s * ext + 6 + seq, :] * cw_ref[0:1, :]
         + uext[s * ext + 7:s * ext + 7 + seq, :] * cw_ref[1:2, :] for s in range(n_stream)],
        axis=0) + u * cw_ref[2:3, :]
    yc = _bf(_dot(h, win_ref[:, C_B:C_B + D]) * conv)
    o_c = _dot(yc, wco_ref[...])
    macc = _sigmoid(_dot(h, win_ref[:, C_G:C_G + D])) * o_c

    q = _dot(h, win_ref[:, C_Q:C_Q + D])
    qn = _bf(_head_norm_q(q, seg_ref, exp_ref, qg_ref[...]) * (1.0 / math.sqrt(HEAD_DIM)))
    kv = _dot(h, win_ref[:, C_K:C_K + 2 * LANES])
    kn = _head_norm_k(kv[:, 0:LANES], kg_ref[...])
    vv = kv[:, LANES:2 * LANES]

    kbd[...] = jnp.zeros_like(kbd)
    vbd[...] = jnp.zeros_like(vbd)
    lane = lax.broadcasted_iota(jnp.int32, (1, QUAD * KEYS), 1)
    valid = (lane % KEYS) < WINDOW + seq
    for s in range(n_stream):
        rows = slice(s * seq, (s + 1) * seq)
        kwin_ref[s, 0:WINDOW - seq, :] = ck_ref[s, seq:WINDOW, :]
        kwin_ref[s, WINDOW - seq:WINDOW, :] = kn[rows, :]
        vwin_ref[s, 0:WINDOW - seq, :] = cv_ref[s, seq:WINDOW, :]
        vwin_ref[s, WINDOW - seq:WINDOW, :] = vv[rows, :]
        kc4 = _rep4(ck_ref[s])
        vc4 = _rep4(cv_ref[s])
        kn4 = _rep4(kn[rows, :])
        vn4 = _rep4(vv[rows, :])
        for g in range(N_KV):
            _write_bd(kbd, g, 0, kc4[g])
            _write_bd(kbd, g, WINDOW, kn4[g])
            _write_bd(vbd, g, 0, vc4[g])
            _write_bd(vbd, g, WINDOW, vn4[g])
        for g in range(N_KV):
            kb = kbd[g]
            vb = vbd[g]
            for j in range(N_HEADS // N_KV // QUAD):
                h0 = g * (N_HEADS // N_KV) + j * QUAD
                qq = qn[rows, h0 * HEAD_DIM:(h0 + QUAD) * HEAD_DIM]
                sinks = [sink_ref[h0 + i] for i in range(QUAD)]
                o = _attend(qq, kb, vb, valid, sinks)
                att_s[rows, h0 * HEAD_DIM:(h0 + QUAD) * HEAD_DIM] = _bf(o)

    o_a = _dot(att_s[...], wao_ref[...])
    macc = macc + _sigmoid(_dot(h, win_ref[:, C_G + D:C_G + 2 * D])) * o_a
    carry = _merge_tail(x, macc, wo_ref, g2_ref, wr_ref, br_ref, tri_ref,
                        jnp.zeros((1, RT_W), jnp.float32), x1_ref, h2p_ref, rt_ref)
    cnt_ref[0] = jnp.broadcast_to(carry, (8, RT_W))


def _sample_mixer(x, ck, cv, sc, sinks, wts):
    nb, seq, _ = x.shape
    rows = nb * seq
    x2 = x.reshape(rows, D)
    ins = (x2, ck, cv, sc) + tuple(wts)
    full = lambda a: pl.BlockSpec(a.shape, lambda *_, nd=a.ndim: (0,) * nd)
    out_shape = [
        jax.ShapeDtypeStruct((rows, D), jnp.float32),
        jax.ShapeDtypeStruct((rows * PK, LANES), jnp.uint32),
        jax.ShapeDtypeStruct((rows, RT_W), jnp.float32),
        jax.ShapeDtypeStruct((1, 8, RT_W), jnp.float32),
        jax.ShapeDtypeStruct((nb, WINDOW, LANES), jnp.float32),
        jax.ShapeDtypeStruct((nb, WINDOW, LANES), jnp.float32),
        jax.ShapeDtypeStruct((nb, 2, D), jnp.float32),
    ]
    grid_spec = pltpu.PrefetchScalarGridSpec(
        num_scalar_prefetch=1,
        grid=(1,),
        in_specs=[full(a) for a in ins],
        out_specs=[full(a) for a in out_shape],
        scratch_shapes=[
            pltpu.VMEM((nb * (seq + 8), D), jnp.float32),
            pltpu.VMEM((rows, D), jnp.bfloat16),
            pltpu.VMEM((N_KV, QUAD * KEYS, QUAD_W), jnp.bfloat16),
            pltpu.VMEM((N_KV, QUAD * KEYS, QUAD_W), jnp.bfloat16),
        ])
    return pl.pallas_call(
        functools.partial(_sample_mixer_kernel, n_stream=nb, seq=seq),
        grid_spec=grid_spec,
        out_shape=out_shape,
        compiler_params=pltpu.CompilerParams(
            dimension_semantics=("arbitrary",), vmem_limit_bytes=VMEM_LIMIT),
        name="sample_mixer",
    )(sinks, *ins)


def _sorted_rows(blk_tokens):
    rows = 2 * blk_tokens + N_EXPERTS * (MOE_PAD - 1) + MOE_TILE
    return -(-rows // MOE_TILE) * MOE_TILE


def _moe_kernel(pos_ref, tab_ref, stp_ref, h2p_ref, x1_ref, rt_ref, wg_ref, wu_ref, wd_ref, y_ref,
                xo, g1, g2, *, tile, n_tiles, max_tiles):
    b = pl.program_id(0)
    s = pl.program_id(1)
    n_groups = tile // UNROLL

    @pl.when((b == 0) & (s == 0))
    def _():
        xo[...] = jnp.zeros_like(xo)

    @pl.when(s < n_tiles)
    def _dispatch():
        base = (b * n_tiles + s) * (2 * tile)

        def group(gi, carry):
            i0 = base + gi * (2 * UNROLL)
            t0 = gi * (UNROLL * PK)
            for k in range(UNROLL):
                row = h2p_ref[pl.ds(pl.multiple_of(t0 + k * PK, PK), PK), :]
                xo[pl.ds(pl.multiple_of(pos_ref[i0 + 2 * k], PK), PK), :] = row
                xo[pl.ds(pl.multiple_of(pos_ref[i0 + 2 * k + 1], PK), PK), :] = row
            return carry

        lax.fori_loop(0, n_groups, group, 0)

    @pl.when((s >= n_tiles) & (s < n_tiles + EXPERT_STEPS))
    def _experts():
        st = b * EXPERT_STEPS + s - n_tiles
        first = stp_ref[2 * st]
        last = stp_ref[2 * st + 1]
        half = MOE_TILE // 2

        def entry(j):
            t = (b * max_tiles + j) * 3
            return pl.multiple_of(tab_ref[t], MOE_PAD * PK), tab_ref[t + 1], tab_ref[t + 2]

        def up(j):
            base, k, _ = entry(j)
            xs = _bf(_unpack_rows(
                [xo[pl.ds(base + q, MOE_TILE, stride=PK), :] for q in range(PK)]))
            gt = _dot(xs, wg_ref[k])
            return _bf(gt * _sigmoid(gt) * _dot(xs, wu_ref[k]))

        def down(j, hid):
            base, k, full = entry(j)
            out = _pack_rows(_dot(hid, wd_ref[k]))
            def put(lo):
                for q in range(PK):
                    xo[pl.ds(base + lo * PK + q, half, stride=PK), :] = (
                        out[lo:lo + half, q * LANES:(q + 1) * LANES])

            put(0)

            @pl.when(full > 0)
            def _():
                put(half)

        @pl.when(last > first)
        def _():
            def both(j, hid):
                nxt = up(j)
                down(j - 1, hid)
                return nxt

            down(last - 1, lax.fori_loop(first + 1, last, both, up(first)))

    @pl.when(s >= n_tiles + EXPERT_STEPS)
    def _combine():
        i = s - n_tiles - EXPERT_STEPS
        base = (b * n_tiles + i) * (2 * tile)

        def group(gi, carry):
            i0 = base + gi * (2 * UNROLL)
            t0 = gi * (UNROLL * PK)
            for k in range(UNROLL):
                dst = pl.ds(pl.multiple_of(t0 + k * PK, PK), PK)
                g1[dst, :] = xo[pl.ds(pl.multiple_of(pos_ref[i0 + 2 * k], PK), PK), :]
                g2[dst, :] = xo[pl.ds(pl.multiple_of(pos_ref[i0 + 2 * k + 1], PK), PK), :]
            return carry

        lax.fori_loop(0, n_groups, group, 0)
        o1 = _unpack_rows([g1[pl.ds(q, tile, stride=PK), :] for q in range(PK)])
        o2 = _unpack_rows([g2[pl.ds(q, tile, stride=PK), :] for q in range(PK)])
        rt = rt_ref[...]
        y_ref[...] = x1_ref[...] + rt[:, R_W1:R_W1 + 1] * o1 + rt[:, R_W2:R_W2 + 1] * o2


def _moe(pos, tab, stp, h2p, x1, rt, wg, wu, wd, tile, n_tiles):
    n = x1.shape[0]
    n_blocks = n // (tile * n_tiles)
    n_steps = 2 * n_tiles + EXPERT_STEPS
    last = n_tiles - 1

    def disp(b, s, *_):
        return (b * n_tiles + jnp.minimum(s, last), 0)

    def comb(b, s, *_):
        return (b * n_tiles + jnp.clip(s - n_tiles - EXPERT_STEPS, 0, last), 0)

    def wmap(b, s, *_):
        return (jnp.clip(s - n_tiles, 0, EXPERT_STEPS - 1), 0, 0)

    grid_spec = pltpu.PrefetchScalarGridSpec(
        num_scalar_prefetch=3,
        grid=(n_blocks, n_steps),
        in_specs=[pl.BlockSpec((tile * PK, LANES), disp),
                  pl.BlockSpec((tile, D), comb),
                  pl.BlockSpec((tile, RT_W), comb),
                  pl.BlockSpec((STEP_EXPERTS, D, D_EXPERT), wmap),
                  pl.BlockSpec((STEP_EXPERTS, D, D_EXPERT), wmap),
                  pl.BlockSpec((STEP_EXPERTS, D_EXPERT, D), wmap)],
        out_specs=pl.BlockSpec((tile, D), comb),
        scratch_shapes=[
            pltpu.VMEM((_sorted_rows(tile * n_tiles) * PK, LANES), jnp.uint32),
            pltpu.VMEM((tile * PK, LANES), jnp.uint32),
            pltpu.VMEM((tile * PK, LANES), jnp.uint32),
        ])
    return pl.pallas_call(
        functools.partial(_moe_kernel, tile=tile, n_tiles=n_tiles,
                          max_tiles=_max_tiles(tile * n_tiles)),
        grid_spec=grid_spec,
        out_shape=jax.ShapeDtypeStruct((n, D), jnp.float32),
        compiler_params=pltpu.CompilerParams(
            dimension_semantics=("arbitrary", "arbitrary"), vmem_limit_bytes=VMEM_LIMIT),
        name="moe",
    )(pos, tab, stp, h2p, x1, rt, wg, wu, wd)


def _max_tiles(blk_tokens):
    return N_EXPERTS + 2 * blk_tokens // MOE_TILE


def _sorted_positions(rt, cnt, blk_tokens):
    experts = jnp.arange(N_EXPERTS, dtype=jnp.int32)
    counts = cnt[:, 0, :N_EXPERTS].astype(jnp.int32)
    padded = (counts + MOE_PAD - 1) // MOE_PAD * MOE_PAD
    off = jnp.cumsum(padded, axis=1) - padded
    e = rt[:, R_E1:R_E2 + 1].astype(jnp.int32)
    rank = rt[:, R_RANK1:R_RANK2 + 1].astype(jnp.int32)
    off_tok = jnp.repeat(off, blk_tokens, axis=0)
    hit = e[:, :, None] == experts
    pos = (jnp.sum(jnp.where(hit, off_tok[:, None, :], 0), axis=-1) + rank) * PK

    tiles = (counts + MOE_TILE - 1) // MOE_TILE
    t_end = jnp.cumsum(tiles, axis=1)
    t_beg = t_end - tiles
    j = jnp.arange(_max_tiles(blk_tokens), dtype=jnp.int32)
    e_of = jnp.minimum(jnp.sum(t_end[:, None, :] <= j[None, :, None], axis=-1), N_EXPERTS - 1)
    pick = lambda a: jnp.sum(jnp.where(e_of[:, :, None] == experts, a[:, None, :], 0), axis=-1)
    local = (j[None, :] - pick(t_beg)) * MOE_TILE
    tab = jnp.stack([(pick(off) + local) * PK, e_of % STEP_EXPERTS,
                     (pick(counts) - local > MOE_TILE // 2).astype(jnp.int32)], axis=-1)
    stp = jnp.stack([t_beg[:, ::STEP_EXPERTS], t_end[:, STEP_EXPERTS - 1::STEP_EXPERTS]], axis=-1)
    return pos.reshape(-1), tab.reshape(-1), stp.reshape(-1)


def _segment_matrices():
    head = jnp.arange(D) // HEAD_DIM
    col = jnp.arange(LANES)
    seg = (head[:, None] == col[None, :]).astype(jnp.bfloat16)
    row = jnp.arange(2 * LANES) % LANES
    expm = (row[:, None] == head[None, :]).astype(jnp.bfloat16)
    return seg, expm


def _strict_lower(n):
    r = jnp.arange(n)
    return (r[None, :] < r[:, None]).astype(jnp.bfloat16)


def kernel(x_prompt, x_sample, cache_k, cache_v, state_conv, norm_mix_g, w_in, conv_w, q_norm_g,
           k_norm_g, attn_sinks, w_conv_out, w_attn_out, w_out, norm_ffn_g, w_group, b_group,
           w_router, b_router, w_gate, w_up, w_down):
    nb, seq, _ = x_prompt.shape
    ns, sseq, _ = x_sample.shape
    l = 0
    seg, expm = _segment_matrices()
    wr = jnp.zeros((D, RT_W), jnp.float32)
    wr = wr.at[:, 0:N_EXPERTS].set(w_router[l]).at[:, N_EXPERTS:N_EXPERTS + N_GROUPS].set(w_group[l])
    br = jnp.zeros((1, RT_W), jnp.float32)
    br = br.at[0, 0:N_EXPERTS].set(b_router[l]).at[0, N_EXPERTS:N_EXPERTS + N_GROUPS].set(b_group[l])
    wts = (norm_mix_g[l].reshape(1, D), _bf(w_in[l]), conv_w[l],
           jnp.tile(q_norm_g[l], N_HEADS).reshape(1, D), jnp.tile(k_norm_g[l], N_KV).reshape(1, LANES),
           seg, expm, _bf(w_conv_out[l]), _bf(w_attn_out[l]), _bf(w_out[l]),
           norm_ffn_g[l].reshape(1, D), _bf(wr), br)
    sinks = attn_sinks[l]
    n_s = ns * sseq

    sink2 = sinks * LOG2E
    sink_hi = _bf(sink2).astype(jnp.float32)
    q_extra = jnp.zeros((N_HEADS, 1, HEAD_DIM), jnp.float32)
    q_extra = q_extra.at[:, 0, BIAS_LANE - HEAD_DIM].set(1.0)
    q_extra = q_extra.at[:, 0, SINK_LANE - HEAD_DIM].set(sink_hi)
    q_extra = q_extra.at[:, 0, SINK_LANE + 1 - HEAD_DIM].set(sink2 - sink_hi)
    x1p, h2pp, rtp, cntp, kwp, vwp, cstp = _prompt_mixer(
        x_prompt, q_extra, wts + (_strict_lower(TOK_TILE),), tile=TOK_TILE, blk_tiles=BLOCK_TILES)
    x1s, h2ps, rts, cnts, kws, vws, csts = _sample_mixer(
        x_sample, cache_k[l].reshape(ns, WINDOW, LANES), cache_v[l].reshape(ns, WINDOW, LANES),
        state_conv[l], sinks, wts + (_strict_lower(n_s),))

    wg, wu, wd = _bf(w_gate[l]), _bf(w_up[l]), _bf(w_down[l])
    rtp = rtp.reshape(nb * seq, RT_W)
    yp = _moe(*_sorted_positions(rtp, cntp, TOK_TILE * BLOCK_TILES), h2pp, x1p.reshape(nb * seq, D), rtp, wg, wu, wd,
              tile=TOK_TILE, n_tiles=BLOCK_TILES)
    ys = _moe(*_sorted_positions(rts, cnts, n_s), h2ps, x1s, rts, wg, wu, wd, tile=n_s, n_tiles=1)

    kv_shape = lambda n: (1, n, WINDOW, N_KV, HEAD_DIM)
    return (yp.reshape(nb, seq, D), ys.reshape(ns, sseq, D),
            kwp.reshape(kv_shape(nb)), vwp.reshape(kv_shape(nb)), cstp.reshape(1, nb, 2, D),
            kws.reshape(kv_shape(ns)), vws.reshape(kv_shape(ns)), csts.reshape(1, ns, 2, D))
```

```python
import functools
import math

import jax
import jax.numpy as jnp
from jax import lax
from jax.experimental import pallas as pl
from jax.experimental.pallas import tpu as pltpu

D = 1024
CHUNK = 64
HEAD_DIM = 64
N_HEADS = 16
N_KV = 2
WINDOW = 128
N_GROUPS = 4
EPG = 8
N_EXPERTS = 32
TOP_K = 2
D_EXPERT = 256
EPS = 1e-6
NEG = -1e30

LANES = 128
QUAD = 4
QUAD_W = QUAD * HEAD_DIM
SLOTS = 4
KEYS = SLOTS * CHUNK
GROUP_HEADS = N_HEADS // N_KV
STACK = GROUP_HEADS * CHUNK
BIAS_LANE = HEAD_DIM
SINK_LANE = HEAD_DIM + 1
LOG2E = 1.0 / math.log(2.0)
RT_ROWS = LANES
REC_ROWS = 8
HALF = D // 2
PK = HALF // LANES

R_E1, R_E2, R_W1, R_W2, R_RANK1, R_RANK2 = 0, 1, 2, 3, 4, 5

C_B, C_C, C_X, C_Q, C_K, C_V, C_G = 0, 1024, 2048, 3072, 4096, 4224, 4352

MOE_TILE = 256
MOE_PAD = 128
TOK_TILE = 512
BLOCK_TILES = 8
UNROLL = 16
STEP_EXPERTS = 4
EXPERT_STEPS = N_EXPERTS // STEP_EXPERTS

VMEM_LIMIT = 60 * 1024 * 1024

_NT = (((1,), (1,)), ((), ()))


def _bf(x):
    return x.astype(jnp.bfloat16)


def _dot(a, b):
    return jnp.dot(a, b, preferred_element_type=jnp.float32)


def _rms(x, g):
    return x * lax.rsqrt(jnp.mean(x * x, axis=-1, keepdims=True) + EPS) * g


def _sigmoid(x):
    return 1.0 / (1.0 + jnp.exp(-x))


def _pack_rows(x):
    return pltpu.pack_elementwise([x[:, :HALF], x[:, HALF:]], packed_dtype=jnp.bfloat16)


def _unpack_rows(parts):
    def half(i):
        return [pltpu.unpack_elementwise(p, index=i, packed_dtype=jnp.bfloat16,
                                         unpacked_dtype=jnp.float32) for p in parts]
    return jnp.concatenate(half(0) + half(1), axis=-1)


def _head_norm_q(q, seg_ref, exp_ref, qg):
    ssq = _dot(_bf(q * q), seg_ref[...])
    inv = lax.rsqrt(ssq * (1.0 / HEAD_DIM) + EPS)
    hi = _bf(inv)
    lo = _bf(inv - hi.astype(jnp.float32))
    full = _dot(jnp.concatenate([hi, lo], axis=-1), exp_ref[...])
    return q * full * qg


def _head_norm_k(k, kg):
    lane = lax.broadcasted_iota(jnp.int32, k.shape, 1)
    lo_half = lane < HEAD_DIM
    sq = k * k
    s0 = jnp.sum(jnp.where(lo_half, sq, 0.0), axis=-1, keepdims=True)
    s1 = jnp.sum(jnp.where(lo_half, 0.0, sq), axis=-1, keepdims=True)
    i0 = lax.rsqrt(s0 * (1.0 / HEAD_DIM) + EPS)
    i1 = lax.rsqrt(s1 * (1.0 / HEAD_DIM) + EPS)
    return k * jnp.where(lo_half, i0, i1) * kg


def _rep4(x):
    lane = lax.broadcasted_iota(jnp.int32, x.shape, 1)
    sw = pltpu.roll(x, HEAD_DIM, axis=1)
    a2 = jnp.where(lane < HEAD_DIM, x, sw)
    b2 = jnp.where(lane < HEAD_DIM, sw, x)
    return jnp.concatenate([a2, a2], axis=-1), jnp.concatenate([b2, b2], axis=-1)


def _write_bd(ref, g, row0, x4):
    rows = x4.shape[0]
    lane = lax.broadcasted_iota(jnp.int32, x4.shape, 1)
    for i in range(QUAD):
        keep = (lane >= i * HEAD_DIM) & (lane < (i + 1) * HEAD_DIM)
        ref[g, pl.ds(i * KEYS + row0, rows), :] = _bf(jnp.where(keep, x4, 0.0))


def _attend(qq, kbd, vbd, valid, sinks):
    s = lax.dot_general(qq, kbd, _NT, preferred_element_type=jnp.float32)
    s = jnp.where(valid, s, NEG)
    ps = []
    for i in range(QUAD):
        sh = s[:, i * KEYS:(i + 1) * KEYS]
        m = jnp.maximum(jnp.max(sh, axis=-1, keepdims=True), sinks[i])
        p = jnp.exp(sh - m)
        den = jnp.sum(p, axis=-1, keepdims=True) + jnp.exp(sinks[i] - m)
        ps.append(_bf(p / den))
    return _dot(jnp.concatenate(ps, axis=-1), vbd)


def _route(lt, upper_ref, carry):
    t = lt.shape[1]
    big = float(RT_ROWS)
    grow = lax.broadcasted_iota(jnp.int32, (REC_ROWS, t), 0).astype(jnp.float32)
    is_g = grow < N_GROUPS
    gl = jnp.where(is_g, lt[N_EXPERTS:N_EXPERTS + REC_ROWS], NEG)
    gmax = jnp.max(gl, axis=0, keepdims=True)
    gsum = jnp.sum(jnp.where(is_g, jnp.exp(gl - gmax), 0.0), axis=0, keepdims=True)
    g_w = 1.0 / gsum
    g_sel = jnp.min(jnp.where(is_g & (gl == gmax), grow, big), axis=0, keepdims=True)
    logits = lt[0:N_EXPERTS]
    erow = lax.broadcasted_iota(jnp.int32, (N_EXPERTS, t), 0).astype(jnp.float32)
    lo = g_sel * EPG
    in_grp = (erow >= lo) & (erow < lo + EPG)
    el = jnp.where(in_grp, logits, NEG)
    v1 = jnp.max(el, axis=0, keepdims=True)
    i1 = jnp.min(jnp.where(in_grp & (el == v1), erow, big), axis=0, keepdims=True)
    rest = in_grp & (erow != i1)
    el2 = jnp.where(rest, logits, NEG)
    v2 = jnp.max(el2, axis=0, keepdims=True)
    i2 = jnp.min(jnp.where(rest & (el2 == v2), erow, big), axis=0, keepdims=True)
    e2 = jnp.exp(v2 - v1)
    den = 1.0 + e2
    w1 = (1.0 / den) * g_w
    w2 = (e2 / den) * g_w
    oh1 = jnp.where(erow == i1, 1.0, 0.0)
    oh2 = jnp.where(erow == i2, 1.0, 0.0)
    both = oh1 + oh2
    before = _dot(_bf(both), upper_ref[...]) + carry
    r1 = jnp.sum(before * oh1, axis=0, keepdims=True)
    r2 = jnp.sum(before * oh2, axis=0, keepdims=True)
    rec = jnp.zeros((REC_ROWS, t), jnp.float32)
    for rw, val in ((R_E1, i1), (R_E2, i2), (R_W1, w1), (R_W2, w2), (R_RANK1, r1), (R_RANK2, r2)):
        rec = jnp.where(grow == rw, val, rec)
    return rec, carry + jnp.sum(both, axis=1, keepdims=True)


def _merge_tail(x, macc, wo_ref, g2_ref, wr_ref, br_ref, upper_ref, carry, x1_ref, h2p_ref, rt_ref):
    rows = x.shape[0]
    x1 = x + _dot(_bf(macc), wo_ref[...])
    x1_ref[...] = x1
    h2 = _rms(x1, g2_ref[...])
    pk = _pack_rows(h2)
    for j in range(PK):
        h2p_ref[pl.ds(j, rows, stride=PK), :] = pk[:, j * LANES:(j + 1) * LANES]
    lt = lax.dot_general(wr_ref[...], _bf(h2), _NT, preferred_element_type=jnp.float32) + br_ref[...]
    rec, carry = _route(lt, upper_ref, carry)
    rt_ref[...] = rec
    return carry


def _prompt_mixer_kernel(x_ref, qx_ref, g1_ref, win_ref, cw_ref, qg_ref, kg_ref, seg_ref, exp_ref,
                         wco_ref, wao_ref, wo_ref, g2_ref, wr_ref, br_ref, upper_ref,
                         x1_ref, h2p_ref, rt_ref, cnt_ref, kwin_ref, vwin_ref, cst_ref,
                         uext, qs_s, att_s, kn_s, kop, vop, khist, vhist, carry_s, *, tile, blk_tiles):
    b = pl.program_id(0)
    it = pl.program_id(1)
    n_it = pl.num_programs(1)
    n_chunk = tile // CHUNK

    @pl.when((b == 0) & (it == 0))
    def _():
        shape = (n_chunk, N_KV, CHUNK, LANES)
        row = lax.broadcasted_iota(jnp.int32, shape, 2)
        ln = lax.broadcasted_iota(jnp.int32, shape, 3)
        is_sink = (row == 0) & ((ln == SINK_LANE) | (ln == SINK_LANE + 1))
        tail = pl.ds((SLOTS - 1) * CHUNK, CHUNK)
        kop[:, :, tail, :] = _bf(jnp.where(is_sink, 1.0, jnp.where((row > 0) & (ln == BIAS_LANE), NEG, 0.0)))
        vop[:, :, tail, 0:LANES] = jnp.zeros(shape, jnp.bfloat16)
        vop[:, :, tail, LANES:2 * LANES] = jnp.ones(shape, jnp.bfloat16)

    @pl.when(it == 0)
    def _():
        uext[0:8, :] = jnp.zeros((8, D), jnp.float32)
        ln = lax.broadcasted_iota(jnp.int32, khist.shape, 2)
        khist[...] = _bf(jnp.where(ln == BIAS_LANE, NEG, 0.0))
        vhist[...] = jnp.zeros_like(vhist)

    @pl.when(it % blk_tiles == 0)
    def _():
        carry_s[...] = jnp.zeros_like(carry_s)

    x = x_ref[0]
    h = _bf(_rms(x, g1_ref[...]))

    u = _dot(h, win_ref[:, C_C:C_C + D]) * _dot(h, win_ref[:, C_X:C_X + D])
    uext[8:8 + tile, :] = u
    conv = (uext[6:6 + tile, :] * cw_ref[0:1, :] + uext[7:7 + tile, :] * cw_ref[1:2, :]
            + u * cw_ref[2:3, :])
    yc = _bf(_dot(h, win_ref[:, C_B:C_B + D]) * conv)
    o_c = _dot(yc, wco_ref[...])
    macc = _sigmoid(_dot(h, win_ref[:, C_G:C_G + D])) * o_c
    last2 = uext[tile + 6:tile + 8, :]
    uext[6:8, :] = last2

    @pl.when(it == n_it - 1)
    def _():
        cst_ref[0] = last2

    q = _dot(h, win_ref[:, C_Q:C_Q + D])
    qn = _head_norm_q(q, seg_ref, exp_ref, qg_ref[...]) * (LOG2E / math.sqrt(HEAD_DIM))
    for hd in range(N_HEADS):
        g, r = divmod(hd, GROUP_HEADS)
        ext = jnp.broadcast_to(qx_ref[hd], (tile, HEAD_DIM))
        piece = _bf(jnp.concatenate([qn[:, hd * HEAD_DIM:(hd + 1) * HEAD_DIM], ext], axis=-1))
        for c in range(n_chunk):
            qs_s[g, c * STACK + r * CHUNK:c * STACK + (r + 1) * CHUNK, :] = (
                piece[c * CHUNK:(c + 1) * CHUNK, :])
    kv = _dot(h, win_ref[:, C_K:C_K + 2 * LANES])
    kn = _head_norm_k(kv[:, 0:LANES], kg_ref[...])
    vv = kv[:, LANES:2 * LANES]
    kn_s[0] = kn
    kn_s[1] = vv

    @pl.when(it == n_it - 1)
    def _():
        kwin_ref[0] = kn[tile - WINDOW:, :]
        vwin_ref[0] = vv[tile - WINDOW:, :]

    lane = lax.broadcasted_iota(jnp.int32, (CHUNK, LANES), 1)
    lo_half = lane < HEAD_DIM
    ones_blk = jnp.ones((CHUNK, LANES), jnp.bfloat16)
    own = SLOTS - 2

    for c in range(own):
        for sl in range(own - c):
            src = pl.ds((c + sl) * CHUNK, CHUNK)
            kop[c, :, pl.ds(sl * CHUNK, CHUNK), :] = khist[:, src, :]
            vop[c, :, pl.ds(sl * CHUNK, CHUNK), :] = vhist[:, src, :]

    def chunk_body(c):
        r0 = c * CHUNK
        kc = kn_s[0, pl.ds(r0, CHUNK), :]
        vc = kn_s[1, pl.ds(r0, CHUNK), :]
        ksw = pltpu.roll(kc, HEAD_DIM, axis=1)
        vsw = pltpu.roll(vc, HEAD_DIM, axis=1)
        for g in range(N_KV):
            kb = _bf(jnp.where(lo_half, kc if g == 0 else ksw, 0.0))
            vb = _bf(jnp.where(lo_half, vc, vsw) if g == 0 else jnp.where(lo_half, vsw, vc))
            for d in range(own + 1):
                if c + d < n_chunk:
                    dst = pl.ds((own - d) * CHUNK, CHUNK)
                    kop[c + d, g, dst, :] = kb
                    vop[c + d, g, dst, 0:LANES] = vb
                    vop[c + d, g, dst, LANES:2 * LANES] = ones_blk
            if c >= n_chunk - own:
                dst = pl.ds((c - (n_chunk - own)) * CHUNK, CHUNK)
                khist[g, dst, :] = kb
                vhist[g, dst, 0:LANES] = vb
                vhist[g, dst, LANES:2 * LANES] = ones_blk
        for g in range(N_KV):
            s = lax.dot_general(qs_s[g, pl.ds(c * STACK, STACK), :], kop[c, g], _NT,
                                preferred_element_type=jnp.float32)
            m = jnp.max(s, axis=-1, keepdims=True)
            o = _dot(_bf(jnp.exp2(s - m)), vop[c, g])
            out = o[:, 0:LANES] / o[:, LANES:2 * LANES]
            for r in range(0, GROUP_HEADS, 2):
                pair = jnp.where(lo_half, out[r * CHUNK:(r + 1) * CHUNK, :],
                                 out[(r + 1) * CHUNK:(r + 2) * CHUNK, :])
                col = (g * GROUP_HEADS + r) * HEAD_DIM
                att_s[pl.ds(r0, CHUNK), col:col + LANES] = _bf(pair)

    for c in range(n_chunk):
        chunk_body(c)

    o_a = _dot(att_s[...], wao_ref[...])
    macc = macc + _sigmoid(_dot(h, win_ref[:, C_G + D:C_G + 2 * D])) * o_a
    carry = _merge_tail(x, macc, wo_ref, g2_ref, wr_ref, br_ref, upper_ref, carry_s[...],
                        x1_ref.at[0], h2p_ref, rt_ref.at[0])
    carry_s[...] = carry

    @pl.when(it % blk_tiles == blk_tiles - 1)
    def _():
        cnt_ref[0] = jnp.broadcast_to(carry, (N_EXPERTS, LANES))


def _const_spec(shape):
    nd = len(shape)
    return pl.BlockSpec(shape, lambda *_: (0,) * nd, pipeline_mode=pl.Buffered(1))


def _prompt_mixer(x, q_extra, wts, tile, blk_tiles):
    nb, seq, _ = x.shape
    n_it = seq // tile
    bps = n_it // blk_tiles
    row_spec = lambda w: pl.BlockSpec((1, tile, w), lambda b, i: (b, i, 0))
    str_spec = lambda r, w: pl.BlockSpec((1, r, w), lambda b, i: (b, 0, 0))
    consts = (q_extra,) + tuple(wts)
    out_shape = [
        jax.ShapeDtypeStruct((nb, seq, D), jnp.float32),
        jax.ShapeDtypeStruct((nb * seq * PK, LANES), jnp.uint32),
        jax.ShapeDtypeStruct((nb * n_it, REC_ROWS, tile), jnp.float32),
        jax.ShapeDtypeStruct((nb * bps, N_EXPERTS, LANES), jnp.float32),
        jax.ShapeDtypeStruct((nb, WINDOW, LANES), jnp.float32),
        jax.ShapeDtypeStruct((nb, WINDOW, LANES), jnp.float32),
        jax.ShapeDtypeStruct((nb, 2, D), jnp.float32),
    ]
    return pl.pallas_call(
        functools.partial(_prompt_mixer_kernel, tile=tile, blk_tiles=blk_tiles),
        grid=(nb, n_it),
        in_specs=[row_spec(D)] + [_const_spec(w.shape) for w in consts],
        out_specs=[row_spec(D),
                   pl.BlockSpec((tile * PK, LANES), lambda b, i: (b * n_it + i, 0)),
                   pl.BlockSpec((1, REC_ROWS, tile), lambda b, i: (b * n_it + i, 0, 0)),
                   pl.BlockSpec((1, N_EXPERTS, LANES), lambda b, i: (b * bps + i // blk_tiles, 0, 0)),
                   str_spec(WINDOW, LANES), str_spec(WINDOW, LANES), str_spec(2, D)],
        out_shape=out_shape,
        scratch_shapes=[
            pltpu.VMEM((tile + 8, D), jnp.float32),
            pltpu.VMEM((N_KV, tile * GROUP_HEADS, LANES), jnp.bfloat16),
            pltpu.VMEM((tile, D), jnp.bfloat16),
            pltpu.VMEM((2, tile, LANES), jnp.float32),
            pltpu.VMEM((tile // CHUNK, N_KV, KEYS, LANES), jnp.bfloat16),
            pltpu.VMEM((tile // CHUNK, N_KV, KEYS, 2 * LANES), jnp.bfloat16),
            pltpu.VMEM((N_KV, (SLOTS - 2) * CHUNK, LANES), jnp.bfloat16),
            pltpu.VMEM((N_KV, (SLOTS - 2) * CHUNK, 2 * LANES), jnp.bfloat16),
            pltpu.VMEM((N_EXPERTS, 1), jnp.float32),
        ],
        compiler_params=pltpu.CompilerParams(
            dimension_semantics=("arbitrary", "arbitrary"), vmem_limit_bytes=VMEM_LIMIT),
        name="prompt_mixer",
    )(x, *consts)


def _sample_mixer_kernel(sink_ref, x_ref, ck_ref, cv_ref, sc_ref, g1_ref, win_ref, cw_ref, qg_ref,
                         kg_ref, seg_ref, exp_ref, wco_ref, wao_ref, wo_ref, g2_ref, wr_ref, br_ref,
                         upper_ref, x1_ref, h2p_ref, rt_ref, cnt_ref, kwin_ref, vwin_ref, cst_ref,
                         uext, att_s, kbd, vbd, *, n_stream, seq):
    ext = seq + 8
    x = x_ref[...]
    h = _bf(_rms(x, g1_ref[...]))

    u = _dot(h, win_ref[:, C_C:C_C + D]) * _dot(h, win_ref[:, C_X:C_X + D])
    for s in range(n_stream):
        uext[s * ext + 6:s * ext + 8, :] = sc_ref[s]
        uext[s * ext + 8:(s + 1) * ext, :] = u[s * seq:(s + 1) * seq, :]
        cst_ref[s] = u[(s + 1) * seq - 2:(s + 1) * seq, :]
    conv = jnp.concatenate(
        [uext[s * ext + 6:s * ext + 6 + seq, :] * cw_ref[0:1, :]
         + uext[s * ext + 7:s * ext + 7 + seq, :] * cw_ref[1:2, :] for s in range(n_stream)],
        axis=0) + u * cw_ref[2:3, :]
    yc = _bf(_dot(h, win_ref[:, C_B:C_B + D]) * conv)
    o_c = _dot(yc, wco_ref[...])
    macc = _sigmoid(_dot(h, win_ref[:, C_G:C_G + D])) * o_c

    q = _dot(h, win_ref[:, C_Q:C_Q + D])
    qn = _bf(_head_norm_q(q, seg_ref, exp_ref, qg_ref[...]) * (1.0 / math.sqrt(HEAD_DIM)))
    kv = _dot(h, win_ref[:, C_K:C_K + 2 * LANES])
    kn = _head_norm_k(kv[:, 0:LANES], kg_ref[...])
    vv = kv[:, LANES:2 * LANES]

    kbd[...] = jnp.zeros_like(kbd)
    vbd[...] = jnp.zeros_like(vbd)
    lane = lax.broadcasted_iota(jnp.int32, (1, QUAD * KEYS), 1)
    valid = (lane % KEYS) < WINDOW + seq
    for s in range(n_stream):
        rows = slice(s * seq, (s + 1) * seq)
        kwin_ref[s, 0:WINDOW - seq, :] = ck_ref[s, seq:WINDOW, :]
        kwin_ref[s, WINDOW - seq:WINDOW, :] = kn[rows, :]
        vwin_ref[s, 0:WINDOW - seq, :] = cv_ref[s, seq:WINDOW, :]
        vwin_ref[s, WINDOW - seq:WINDOW, :] = vv[rows, :]
        kc4 = _rep4(ck_ref[s])
        vc4 = _rep4(cv_ref[s])
        kn4 = _rep4(kn[rows, :])
        vn4 = _rep4(vv[rows, :])
        for g in range(N_KV):
            _write_bd(kbd, g, 0, kc4[g])
            _write_bd(kbd, g, WINDOW, kn4[g])
            _write_bd(vbd, g, 0, vc4[g])
            _write_bd(vbd, g, WINDOW, vn4[g])
        for g in range(N_KV):
            kb = kbd[g]
            vb = vbd[g]
            for j in range(N_HEADS // N_KV // QUAD):
                h0 = g * (N_HEADS // N_KV) + j * QUAD
                qq = qn[rows, h0 * HEAD_DIM:(h0 + QUAD) * HEAD_DIM]
                sinks = [sink_ref[h0 + i] for i in range(QUAD)]
                o = _attend(qq, kb, vb, valid, sinks)
                att_s[rows, h0 * HEAD_DIM:(h0 + QUAD) * HEAD_DIM] = _bf(o)

    o_a = _dot(att_s[...], wao_ref[...])
    macc = macc + _sigmoid(_dot(h, win_ref[:, C_G + D:C_G + 2 * D])) * o_a
    carry = _merge_tail(x, macc, wo_ref, g2_ref, wr_ref, br_ref, upper_ref,
                        jnp.zeros((N_EXPERTS, 1), jnp.float32), x1_ref, h2p_ref, rt_ref.at[0])
    cnt_ref[0] = jnp.broadcast_to(carry, (N_EXPERTS, LANES))


def _sample_mixer(x, ck, cv, sc, sinks, wts):
    nb, seq, _ = x.shape
    rows = nb * seq
    x2 = x.reshape(rows, D)
    ins = (x2, ck, cv, sc) + tuple(wts)
    full = lambda a: pl.BlockSpec(a.shape, lambda *_, nd=a.ndim: (0,) * nd)
    out_shape = [
        jax.ShapeDtypeStruct((rows, D), jnp.float32),
        jax.ShapeDtypeStruct((rows * PK, LANES), jnp.uint32),
        jax.ShapeDtypeStruct((1, REC_ROWS, rows), jnp.float32),
        jax.ShapeDtypeStruct((1, N_EXPERTS, LANES), jnp.float32),
        jax.ShapeDtypeStruct((nb, WINDOW, LANES), jnp.float32),
        jax.ShapeDtypeStruct((nb, WINDOW, LANES), jnp.float32),
        jax.ShapeDtypeStruct((nb, 2, D), jnp.float32),
    ]
    grid_spec = pltpu.PrefetchScalarGridSpec(
        num_scalar_prefetch=1,
        grid=(1,),
        in_specs=[full(a) for a in ins],
        out_specs=[full(a) for a in out_shape],
        scratch_shapes=[
            pltpu.VMEM((nb * (seq + 8), D), jnp.float32),
            pltpu.VMEM((rows, D), jnp.bfloat16),
            pltpu.VMEM((N_KV, QUAD * KEYS, QUAD_W), jnp.bfloat16),
            pltpu.VMEM((N_KV, QUAD * KEYS, QUAD_W), jnp.bfloat16),
        ])
    return pl.pallas_call(
        functools.partial(_sample_mixer_kernel, n_stream=nb, seq=seq),
        grid_spec=grid_spec,
        out_shape=out_shape,
        compiler_params=pltpu.CompilerParams(
            dimension_semantics=("arbitrary",), vmem_limit_bytes=VMEM_LIMIT),
        name="sample_mixer",
    )(sinks, *ins)


def _sorted_rows(blk_tokens):
    rows = 2 * blk_tokens + N_EXPERTS * (MOE_PAD - 1) + MOE_TILE
    return -(-rows // MOE_TILE) * MOE_TILE


def _moe_kernel(pos_ref, tab_ref, stp_ref, h2p_ref, x1_ref, cw_ref, wg_ref, wu_ref, wd_ref, y_ref,
                xo, g1, g2, *, tile, n_tiles, max_tiles):
    b = pl.program_id(0)
    s = pl.program_id(1)
    n_groups = tile // UNROLL

    @pl.when((b == 0) & (s == 0))
    def _():
        xo[...] = jnp.zeros_like(xo)

    @pl.when(s < n_tiles)
    def _dispatch():
        base = (b * n_tiles + s) * (2 * tile)

        def group(gi, carry):
            i0 = base + gi * (2 * UNROLL)
            t0 = gi * (UNROLL * PK)
            for k in range(UNROLL):
                row = h2p_ref[pl.ds(pl.multiple_of(t0 + k * PK, PK), PK), :]
                xo[pl.ds(pl.multiple_of(pos_ref[i0 + 2 * k], PK), PK), :] = row
                xo[pl.ds(pl.multiple_of(pos_ref[i0 + 2 * k + 1], PK), PK), :] = row
            return carry

        lax.fori_loop(0, n_groups, group, 0)

    @pl.when((s >= n_tiles) & (s < n_tiles + EXPERT_STEPS))
    def _experts():
        st = b * EXPERT_STEPS + s - n_tiles
        first = stp_ref[2 * st]
        last = stp_ref[2 * st + 1]
        half = MOE_TILE // 2

        def entry(j):
            t = (b * max_tiles + j) * 3
            return pl.multiple_of(tab_ref[t], MOE_PAD * PK), tab_ref[t + 1], tab_ref[t + 2]

        def up(j):
            base, k, _ = entry(j)
            xs = _bf(_unpack_rows(
                [xo[pl.ds(base + q, MOE_TILE, stride=PK), :] for q in range(PK)]))
            gt = _dot(xs, wg_ref[k])
            return _bf(gt * _sigmoid(gt) * _dot(xs, wu_ref[k]))

        def down(j, hid):
            base, k, full = entry(j)
            out = _pack_rows(_dot(hid, wd_ref[k]))
            def put(lo):
                for q in range(PK):
                    xo[pl.ds(base + lo * PK + q, half, stride=PK), :] = (
                        out[lo:lo + half, q * LANES:(q + 1) * LANES])

            put(0)

            @pl.when(full > 0)
            def _():
                put(half)

        @pl.when(last > first)
        def _():
            def both(j, hid):
                nxt = up(j)
                down(j - 1, hid)
                return nxt

            down(last - 1, lax.fori_loop(first + 1, last, both, up(first)))

    @pl.when(s >= n_tiles + EXPERT_STEPS)
    def _combine():
        i = s - n_tiles - EXPERT_STEPS
        base = (b * n_tiles + i) * (2 * tile)

        def group(gi, carry):
            i0 = base + gi * (2 * UNROLL)
            t0 = gi * (UNROLL * PK)
            for k in range(UNROLL):
                dst = pl.ds(pl.multiple_of(t0 + k * PK, PK), PK)
                g1[dst, :] = xo[pl.ds(pl.multiple_of(pos_ref[i0 + 2 * k], PK), PK), :]
                g2[dst, :] = xo[pl.ds(pl.multiple_of(pos_ref[i0 + 2 * k + 1], PK), PK), :]
            return carry

        lax.fori_loop(0, n_groups, group, 0)
        o1 = _unpack_rows([g1[pl.ds(q, tile, stride=PK), :] for q in range(PK)])
        o2 = _unpack_rows([g2[pl.ds(q, tile, stride=PK), :] for q in range(PK)])
        cw = cw_ref[...]
        y_ref[...] = x1_ref[...] + cw[:, 0:1] * o1 + cw[:, 1:2] * o2


def _moe(pos, tab, stp, cw, h2p, x1, wg, wu, wd, tile, n_tiles):
    n = x1.shape[0]
    n_blocks = n // (tile * n_tiles)
    n_steps = 2 * n_tiles + EXPERT_STEPS
    last = n_tiles - 1

    def disp(b, s, *_):
        return (b * n_tiles + jnp.minimum(s, last), 0)

    def comb(b, s, *_):
        return (b * n_tiles + jnp.clip(s - n_tiles - EXPERT_STEPS, 0, last), 0)

    def wmap(b, s, *_):
        return (jnp.clip(s - n_tiles, 0, EXPERT_STEPS - 1), 0, 0)

    grid_spec = pltpu.PrefetchScalarGridSpec(
        num_scalar_prefetch=3,
        grid=(n_blocks, n_steps),
        in_specs=[pl.BlockSpec((tile * PK, LANES), disp),
                  pl.BlockSpec((tile, D), comb),
                  pl.BlockSpec((tile, TOP_K), comb),
                  pl.BlockSpec((STEP_EXPERTS, D, D_EXPERT), wmap),
                  pl.BlockSpec((STEP_EXPERTS, D, D_EXPERT), wmap),
                  pl.BlockSpec((STEP_EXPERTS, D_EXPERT, D), wmap)],
        out_specs=pl.BlockSpec((tile, D), comb),
        scratch_shapes=[
            pltpu.VMEM((_sorted_rows(tile * n_tiles) * PK, LANES), jnp.uint32),
            pltpu.VMEM((tile * PK, LANES), jnp.uint32),
            pltpu.VMEM((tile * PK, LANES), jnp.uint32),
        ])
    return pl.pallas_call(
        functools.partial(_moe_kernel, tile=tile, n_tiles=n_tiles,
                          max_tiles=_max_tiles(tile * n_tiles)),
        grid_spec=grid_spec,
        out_shape=jax.ShapeDtypeStruct((n, D), jnp.float32),
        compiler_params=pltpu.CompilerParams(
            dimension_semantics=("arbitrary", "arbitrary"), vmem_limit_bytes=VMEM_LIMIT),
        name="moe",
    )(pos, tab, stp, h2p, x1, cw, wg, wu, wd)


def _max_tiles(blk_tokens):
    return N_EXPERTS + 2 * blk_tokens // MOE_TILE


def _sorted_positions(rt, cnt, blk_tokens):
    experts = jnp.arange(N_EXPERTS, dtype=jnp.int32)
    by_token = lambda r0: jnp.swapaxes(rt[:, r0:r0 + TOP_K, :], 1, 2).reshape(-1, TOP_K)
    counts = cnt[:, :, 0].astype(jnp.int32)
    padded = (counts + MOE_PAD - 1) // MOE_PAD * MOE_PAD
    off = jnp.cumsum(padded, axis=1) - padded
    e = by_token(R_E1).astype(jnp.int32)
    rank = by_token(R_RANK1).astype(jnp.int32)
    off_tok = jnp.repeat(off, blk_tokens, axis=0)
    hit = e[:, :, None] == experts
    pos = (jnp.sum(jnp.where(hit, off_tok[:, None, :], 0), axis=-1) + rank) * PK

    tiles = (counts + MOE_TILE - 1) // MOE_TILE
    t_end = jnp.cumsum(tiles, axis=1)
    t_beg = t_end - tiles
    j = jnp.arange(_max_tiles(blk_tokens), dtype=jnp.int32)
    e_of = jnp.minimum(jnp.sum(t_end[:, None, :] <= j[None, :, None], axis=-1), N_EXPERTS - 1)
    pick = lambda a: jnp.sum(jnp.where(e_of[:, :, None] == experts, a[:, None, :], 0), axis=-1)
    local = (j[None, :] - pick(t_beg)) * MOE_TILE
    tab = jnp.stack([(pick(off) + local) * PK, e_of % STEP_EXPERTS,
                     (pick(counts) - local > MOE_TILE // 2).astype(jnp.int32)], axis=-1)
    stp = jnp.stack([t_beg[:, ::STEP_EXPERTS], t_end[:, STEP_EXPERTS - 1::STEP_EXPERTS]], axis=-1)
    return pos.reshape(-1), tab.reshape(-1), stp.reshape(-1), by_token(R_W1)


def _segment_matrices():
    head = jnp.arange(D) // HEAD_DIM
    col = jnp.arange(LANES)
    seg = (head[:, None] == col[None, :]).astype(jnp.bfloat16)
    row = jnp.arange(2 * LANES) % LANES
    expm = (row[:, None] == head[None, :]).astype(jnp.bfloat16)
    return seg, expm


def _strict_upper(n):
    r = jnp.arange(n)
    return (r[:, None] < r[None, :]).astype(jnp.bfloat16)


def kernel(x_prompt, x_sample, cache_k, cache_v, state_conv, norm_mix_g, w_in, conv_w, q_norm_g,
           k_norm_g, attn_sinks, w_conv_out, w_attn_out, w_out, norm_ffn_g, w_group, b_group,
           w_router, b_router, w_gate, w_up, w_down):
    nb, seq, _ = x_prompt.shape
    ns, sseq, _ = x_sample.shape
    l = 0
    seg, expm = _segment_matrices()
    pad = RT_ROWS - N_EXPERTS - N_GROUPS
    wr = jnp.concatenate([w_router[l].T, w_group[l].T, jnp.zeros((pad, D), jnp.float32)], axis=0)
    br = jnp.concatenate([b_router[l], b_group[l], jnp.zeros((pad,), jnp.float32)]).reshape(RT_ROWS, 1)
    wts = (norm_mix_g[l].reshape(1, D), _bf(w_in[l]), conv_w[l],
           jnp.tile(q_norm_g[l], N_HEADS).reshape(1, D), jnp.tile(k_norm_g[l], N_KV).reshape(1, LANES),
           seg, expm, _bf(w_conv_out[l]), _bf(w_attn_out[l]), _bf(w_out[l]),
           norm_ffn_g[l].reshape(1, D), _bf(wr), br)
    sinks = attn_sinks[l]
    n_s = ns * sseq

    sink2 = sinks * LOG2E
    sink_hi = _bf(sink2).astype(jnp.float32)
    q_extra = jnp.zeros((N_HEADS, 1, HEAD_DIM), jnp.float32)
    q_extra = q_extra.at[:, 0, BIAS_LANE - HEAD_DIM].set(1.0)
    q_extra = q_extra.at[:, 0, SINK_LANE - HEAD_DIM].set(sink_hi)
    q_extra = q_extra.at[:, 0, SINK_LANE + 1 - HEAD_DIM].set(sink2 - sink_hi)
    x1p, h2pp, rtp, cntp, kwp, vwp, cstp = _prompt_mixer(
        x_prompt, q_extra, wts + (_strict_upper(TOK_TILE),), tile=TOK_TILE, blk_tiles=BLOCK_TILES)
    x1s, h2ps, rts, cnts, kws, vws, csts = _sample_mixer(
        x_sample, cache_k[l].reshape(ns, WINDOW, LANES), cache_v[l].reshape(ns, WINDOW, LANES),
        state_conv[l], sinks, wts + (_strict_upper(n_s),))

    wg, wu, wd = _bf(w_gate[l]), _bf(w_up[l]), _bf(w_down[l])
    yp = _moe(*_sorted_positions(rtp, cntp, TOK_TILE * BLOCK_TILES), h2pp, x1p.reshape(nb * seq, D), wg, wu, wd,
              tile=TOK_TILE, n_tiles=BLOCK_TILES)
    ys = _moe(*_sorted_positions(rts, cnts, n_s), h2ps, x1s, wg, wu, wd, tile=n_s, n_tiles=1)

    kv_shape = lambda n: (1, n, WINDOW, N_KV, HEAD_DIM)
    return (yp.reshape(nb, seq, D), ys.reshape(ns, sseq, D),
            kwp.reshape(kv_shape(nb)), vwp.reshape(kv_shape(nb)), cstp.reshape(1, nb, 2, D),
            kws.reshape(kv_shape(ns)), vws.reshape(kv_shape(ns)), csts.reshape(1, ns, 2, D))
```

```python
import functools
import math

import jax
import jax.numpy as jnp
from jax import lax
from jax.experimental import pallas as pl
from jax.experimental.pallas import tpu as pltpu

D = 1024
CHUNK = 64
HEAD_DIM = 64
N_HEADS = 16
N_KV = 2
WINDOW = 128
N_GROUPS = 4
EPG = 8
N_EXPERTS = 32
TOP_K = 2
D_EXPERT = 256
EPS = 1e-6
NEG = -1e30

LANES = 128
QUAD = 4
QUAD_W = QUAD * HEAD_DIM
SLOTS = 4
KEYS = SLOTS * CHUNK
GROUP_HEADS = N_HEADS // N_KV
STACK = GROUP_HEADS * CHUNK
BIAS_LANE = HEAD_DIM
SINK_LANE = HEAD_DIM + 1
LOG2E = 1.0 / math.log(2.0)
RT_ROWS = LANES
REC_ROWS = 8
HALF = D // 2
PK = HALF // LANES

R_E1, R_E2, R_W1, R_W2, R_RANK1, R_RANK2 = 0, 1, 2, 3, 4, 5

C_B, C_C, C_X, C_Q, C_K, C_V, C_G = 0, 1024, 2048, 3072, 4096, 4224, 4352

MOE_TILE = 256
MOE_PAD = 128
TOK_TILE = 512
BLOCK_TILES = 8
UNROLL = 16
STEP_EXPERTS = 4
EXPERT_STEPS = N_EXPERTS // STEP_EXPERTS

VMEM_LIMIT = 60 * 1024 * 1024

_NT = (((1,), (1,)), ((), ()))


def _bf(x):
    return x.astype(jnp.bfloat16)


def _dot(a, b):
    return jnp.dot(a, b, preferred_element_type=jnp.float32)


def _rms(x, g):
    return x * lax.rsqrt(jnp.mean(x * x, axis=-1, keepdims=True) + EPS) * g


def _sigmoid(x):
    return 1.0 / (1.0 + jnp.exp(-x))


def _pack_rows(x):
    return pltpu.pack_elementwise([x[:, :HALF], x[:, HALF:]], packed_dtype=jnp.bfloat16)


def _unpack_rows(parts):
    def half(i):
        return [pltpu.unpack_elementwise(p, index=i, packed_dtype=jnp.bfloat16,
                                         unpacked_dtype=jnp.float32) for p in parts]
    return jnp.concatenate(half(0) + half(1), axis=-1)


def _head_norm_q(q, seg_ref, exp_ref, qg):
    ssq = _dot(_bf(q * q), seg_ref[...])
    inv = lax.rsqrt(ssq * (1.0 / HEAD_DIM) + EPS)
    hi = _bf(inv)
    lo = _bf(inv - hi.astype(jnp.float32))
    full = _dot(jnp.concatenate([hi, lo], axis=-1), exp_ref[...])
    return q * full * qg


def _head_norm_k(k, kg):
    lane = lax.broadcasted_iota(jnp.int32, k.shape, 1)
    lo_half = lane < HEAD_DIM
    sq = k * k
    s0 = jnp.sum(jnp.where(lo_half, sq, 0.0), axis=-1, keepdims=True)
    s1 = jnp.sum(jnp.where(lo_half, 0.0, sq), axis=-1, keepdims=True)
    i0 = lax.rsqrt(s0 * (1.0 / HEAD_DIM) + EPS)
    i1 = lax.rsqrt(s1 * (1.0 / HEAD_DIM) + EPS)
    return k * jnp.where(lo_half, i0, i1) * kg


def _rep4(x):
    lane = lax.broadcasted_iota(jnp.int32, x.shape, 1)
    sw = pltpu.roll(x, HEAD_DIM, axis=1)
    a2 = jnp.where(lane < HEAD_DIM, x, sw)
    b2 = jnp.where(lane < HEAD_DIM, sw, x)
    return jnp.concatenate([a2, a2], axis=-1), jnp.concatenate([b2, b2], axis=-1)


def _write_bd(ref, g, row0, x4):
    rows = x4.shape[0]
    lane = lax.broadcasted_iota(jnp.int32, x4.shape, 1)
    for i in range(QUAD):
        keep = (lane >= i * HEAD_DIM) & (lane < (i + 1) * HEAD_DIM)
        ref[g, pl.ds(i * KEYS + row0, rows), :] = _bf(jnp.where(keep, x4, 0.0))


def _attend(qq, kbd, vbd, valid, sinks):
    s = lax.dot_general(qq, kbd, _NT, preferred_element_type=jnp.float32)
    s = jnp.where(valid, s, NEG)
    ps = []
    for i in range(QUAD):
        sh = s[:, i * KEYS:(i + 1) * KEYS]
        m = jnp.maximum(jnp.max(sh, axis=-1, keepdims=True), sinks[i])
        p = jnp.exp(sh - m)
        den = jnp.sum(p, axis=-1, keepdims=True) + jnp.exp(sinks[i] - m)
        ps.append(_bf(p / den))
    return _dot(jnp.concatenate(ps, axis=-1), vbd)


def _route(lt, upper_ref, carry):
    t = lt.shape[1]
    big = float(RT_ROWS)
    grow = lax.broadcasted_iota(jnp.int32, (REC_ROWS, t), 0).astype(jnp.float32)
    is_g = grow < N_GROUPS
    gl = jnp.where(is_g, lt[N_EXPERTS:N_EXPERTS + REC_ROWS], NEG)
    gmax = jnp.max(gl, axis=0, keepdims=True)
    gsum = jnp.sum(jnp.where(is_g, jnp.exp(gl - gmax), 0.0), axis=0, keepdims=True)
    g_w = 1.0 / gsum
    g_sel = jnp.min(jnp.where(is_g & (gl == gmax), grow, big), axis=0, keepdims=True)
    logits = lt[0:N_EXPERTS]
    erow = lax.broadcasted_iota(jnp.int32, (N_EXPERTS, t), 0).astype(jnp.float32)
    lo = g_sel * EPG
    in_grp = (erow >= lo) & (erow < lo + EPG)
    el = jnp.where(in_grp, logits, NEG)
    v1 = jnp.max(el, axis=0, keepdims=True)
    i1 = jnp.min(jnp.where(in_grp & (el == v1), erow, big), axis=0, keepdims=True)
    rest = in_grp & (erow != i1)
    el2 = jnp.where(rest, logits, NEG)
    v2 = jnp.max(el2, axis=0, keepdims=True)
    i2 = jnp.min(jnp.where(rest & (el2 == v2), erow, big), axis=0, keepdims=True)
    e2 = jnp.exp(v2 - v1)
    den = 1.0 + e2
    w1 = (1.0 / den) * g_w
    w2 = (e2 / den) * g_w
    oh1 = jnp.where(erow == i1, 1.0, 0.0)
    oh2 = jnp.where(erow == i2, 1.0, 0.0)
    both = oh1 + oh2
    before = _dot(_bf(both), upper_ref[...]) + carry
    r1 = jnp.sum(before * oh1, axis=0, keepdims=True)
    r2 = jnp.sum(before * oh2, axis=0, keepdims=True)
    rec = jnp.zeros((REC_ROWS, t), jnp.float32)
    for rw, val in ((R_E1, i1), (R_E2, i2), (R_W1, w1), (R_W2, w2), (R_RANK1, r1), (R_RANK2, r2)):
        rec = jnp.where(grow == rw, val, rec)
    return rec, carry + jnp.sum(both, axis=1, keepdims=True)


def _merge_tail(x, macc, wo_ref, g2_ref, wr_ref, br_ref, upper_ref, carry, x1_ref, h2p_ref, rt_ref):
    rows = x.shape[0]
    x1 = x + _dot(_bf(macc), wo_ref[...])
    x1_ref[...] = x1
    h2 = _rms(x1, g2_ref[...])
    pk = _pack_rows(h2)
    for j in range(PK):
        h2p_ref[pl.ds(j, rows, stride=PK), :] = pk[:, j * LANES:(j + 1) * LANES]
    lt = lax.dot_general(wr_ref[...], _bf(h2), _NT, preferred_element_type=jnp.float32) + br_ref[...]
    rec, carry = _route(lt, upper_ref, carry)
    rt_ref[...] = rec
    return carry


def _prompt_mixer_kernel(x_ref, wg32_ref, wu32_ref, wd32_ref,
                         qx_ref, g1_ref, win_ref, cw_ref, qg_ref, kg_ref, seg_ref, exp_ref,
                         wco_ref, wao_ref, wo_ref, g2_ref, wr_ref, br_ref, upper_ref,
                         x1_ref, h2p_ref, rt_ref, cnt_ref, kwin_ref, vwin_ref, cst_ref,
                         wg16_ref, wu16_ref, wd16_ref,
                         uext, qs_s, att_s, kn_s, kop, vop, khist, vhist, carry_s, *, tile, blk_tiles):
    b = pl.program_id(0)
    it = pl.program_id(1)
    n_it = pl.num_programs(1)
    n_chunk = tile // CHUNK

    @pl.when((b == 0) & (it == 0))
    def _():
        shape = (n_chunk, N_KV, CHUNK, LANES)
        row = lax.broadcasted_iota(jnp.int32, shape, 2)
        ln = lax.broadcasted_iota(jnp.int32, shape, 3)
        is_sink = (row == 0) & ((ln == SINK_LANE) | (ln == SINK_LANE + 1))
        tail = pl.ds((SLOTS - 1) * CHUNK, CHUNK)
        kop[:, :, tail, :] = _bf(jnp.where(is_sink, 1.0, jnp.where((row > 0) & (ln == BIAS_LANE), NEG, 0.0)))
        vop[:, :, tail, 0:LANES] = jnp.zeros(shape, jnp.bfloat16)
        vop[:, :, tail, LANES:2 * LANES] = jnp.ones(shape, jnp.bfloat16)

    @pl.when(it == 0)
    def _():
        uext[0:8, :] = jnp.zeros((8, D), jnp.float32)
        ln = lax.broadcasted_iota(jnp.int32, khist.shape, 2)
        khist[...] = _bf(jnp.where(ln == BIAS_LANE, NEG, 0.0))
        vhist[...] = jnp.zeros_like(vhist)

    @pl.when(it % blk_tiles == 0)
    def _():
        carry_s[...] = jnp.zeros_like(carry_s)

    wg16_ref[...] = _bf(wg32_ref[...])
    wu16_ref[...] = _bf(wu32_ref[...])
    wd16_ref[...] = _bf(wd32_ref[...])

    x = x_ref[0]
    h = _bf(_rms(x, g1_ref[...]))

    u = _dot(h, win_ref[:, C_C:C_C + D]) * _dot(h, win_ref[:, C_X:C_X + D])
    uext[8:8 + tile, :] = u
    conv = (uext[6:6 + tile, :] * cw_ref[0:1, :] + uext[7:7 + tile, :] * cw_ref[1:2, :]
            + u * cw_ref[2:3, :])
    yc = _bf(_dot(h, win_ref[:, C_B:C_B + D]) * conv)
    o_c = _dot(yc, wco_ref[...])
    macc = _sigmoid(_dot(h, win_ref[:, C_G:C_G + D])) * o_c
    last2 = uext[tile + 6:tile + 8, :]
    uext[6:8, :] = last2

    @pl.when(it == n_it - 1)
    def _():
        cst_ref[0] = last2

    q = _dot(h, win_ref[:, C_Q:C_Q + D])
    qn = _head_norm_q(q, seg_ref, exp_ref, qg_ref[...]) * (LOG2E / math.sqrt(HEAD_DIM))
    for hd in range(N_HEADS):
        g, r = divmod(hd, GROUP_HEADS)
        ext = jnp.broadcast_to(qx_ref[hd], (tile, HEAD_DIM))
        piece = _bf(jnp.concatenate([qn[:, hd * HEAD_DIM:(hd + 1) * HEAD_DIM], ext], axis=-1))
        for c in range(n_chunk):
            qs_s[g, c * STACK + r * CHUNK:c * STACK + (r + 1) * CHUNK, :] = (
                piece[c * CHUNK:(c + 1) * CHUNK, :])
    kv = _dot(h, win_ref[:, C_K:C_K + 2 * LANES])
    kn = _head_norm_k(kv[:, 0:LANES], kg_ref[...])
    vv = kv[:, LANES:2 * LANES]
    kn_s[0] = kn
    kn_s[1] = vv

    @pl.when(it == n_it - 1)
    def _():
        kwin_ref[0] = kn[tile - WINDOW:, :]
        vwin_ref[0] = vv[tile - WINDOW:, :]

    lane = lax.broadcasted_iota(jnp.int32, (CHUNK, LANES), 1)
    lo_half = lane < HEAD_DIM
    ones_blk = jnp.ones((CHUNK, LANES), jnp.bfloat16)
    own = SLOTS - 2

    for c in range(own):
        for sl in range(own - c):
            src = pl.ds((c + sl) * CHUNK, CHUNK)
            kop[c, :, pl.ds(sl * CHUNK, CHUNK), :] = khist[:, src, :]
            vop[c, :, pl.ds(sl * CHUNK, CHUNK), :] = vhist[:, src, :]

    def chunk_body(c):
        r0 = c * CHUNK
        kc = kn_s[0, pl.ds(r0, CHUNK), :]
        vc = kn_s[1, pl.ds(r0, CHUNK), :]
        ksw = pltpu.roll(kc, HEAD_DIM, axis=1)
        vsw = pltpu.roll(vc, HEAD_DIM, axis=1)
        for g in range(N_KV):
            kb = _bf(jnp.where(lo_half, kc if g == 0 else ksw, 0.0))
            vb = _bf(jnp.where(lo_half, vc, vsw) if g == 0 else jnp.where(lo_half, vsw, vc))
            for d in range(own + 1):
                if c + d < n_chunk:
                    dst = pl.ds((own - d) * CHUNK, CHUNK)
                    kop[c + d, g, dst, :] = kb
                    vop[c + d, g, dst, 0:LANES] = vb
                    vop[c + d, g, dst, LANES:2 * LANES] = ones_blk
            if c >= n_chunk - own:
                dst = pl.ds((c - (n_chunk - own)) * CHUNK, CHUNK)
                khist[g, dst, :] = kb
                vhist[g, dst, 0:LANES] = vb
                vhist[g, dst, LANES:2 * LANES] = ones_blk
        for g in range(N_KV):
            s = lax.dot_general(qs_s[g, pl.ds(c * STACK, STACK), :], kop[c, g], _NT,
                                preferred_element_type=jnp.float32)
            m = jnp.max(s, axis=-1, keepdims=True)
            o = _dot(_bf(jnp.exp2(s - m)), vop[c, g])
            out = o[:, 0:LANES] / o[:, LANES:2 * LANES]
            for r in range(0, GROUP_HEADS, 2):
                pair = jnp.where(lo_half, out[r * CHUNK:(r + 1) * CHUNK, :],
                                 out[(r + 1) * CHUNK:(r + 2) * CHUNK, :])
                col = (g * GROUP_HEADS + r) * HEAD_DIM
                att_s[pl.ds(r0, CHUNK), col:col + LANES] = _bf(pair)

    for c in range(n_chunk):
        chunk_body(c)

    o_a = _dot(att_s[...], wao_ref[...])
    macc = macc + _sigmoid(_dot(h, win_ref[:, C_G + D:C_G + 2 * D])) * o_a
    carry = _merge_tail(x, macc, wo_ref, g2_ref, wr_ref, br_ref, upper_ref, carry_s[...],
                        x1_ref.at[0], h2p_ref, rt_ref.at[0])
    carry_s[...] = carry

    @pl.when(it % blk_tiles == blk_tiles - 1)
    def _():
        cnt_ref[0] = jnp.broadcast_to(carry, (N_EXPERTS, LANES))


def _const_spec(shape):
    nd = len(shape)
    return pl.BlockSpec(shape, lambda *_: (0,) * nd, pipeline_mode=pl.Buffered(1))


def _prompt_mixer(x, experts32, q_extra, wts, tile, blk_tiles):
    nb, seq, _ = x.shape
    n_it = seq // tile
    bps = n_it // blk_tiles
    row_spec = lambda w: pl.BlockSpec((1, tile, w), lambda b, i: (b, i, 0))
    str_spec = lambda r, w: pl.BlockSpec((1, r, w), lambda b, i: (b, 0, 0))
    consts = (q_extra,) + tuple(wts)
    parts = nb * n_it // N_EXPERTS
    part_spec = lambda r, w: pl.BlockSpec(
        (1, r // parts, w), lambda b, i: ((b * n_it + i) // parts, (b * n_it + i) % parts, 0))
    expert_specs = [part_spec(D, D_EXPERT), part_spec(D, D_EXPERT), part_spec(D_EXPERT, D)]
    out_shape = [
        jax.ShapeDtypeStruct((nb, seq, D), jnp.float32),
        jax.ShapeDtypeStruct((nb * seq * PK, LANES), jnp.uint32),
        jax.ShapeDtypeStruct((nb * n_it, REC_ROWS, tile), jnp.float32),
        jax.ShapeDtypeStruct((nb * bps, N_EXPERTS, LANES), jnp.float32),
        jax.ShapeDtypeStruct((nb, WINDOW, LANES), jnp.float32),
        jax.ShapeDtypeStruct((nb, WINDOW, LANES), jnp.float32),
        jax.ShapeDtypeStruct((nb, 2, D), jnp.float32),
    ] + [jax.ShapeDtypeStruct(w.shape, jnp.bfloat16) for w in experts32]
    return pl.pallas_call(
        functools.partial(_prompt_mixer_kernel, tile=tile, blk_tiles=blk_tiles),
        grid=(nb, n_it),
        in_specs=[row_spec(D)] + expert_specs + [_const_spec(w.shape) for w in consts],
        out_specs=[row_spec(D),
                   pl.BlockSpec((tile * PK, LANES), lambda b, i: (b * n_it + i, 0)),
                   pl.BlockSpec((1, REC_ROWS, tile), lambda b, i: (b * n_it + i, 0, 0)),
                   pl.BlockSpec((1, N_EXPERTS, LANES), lambda b, i: (b * bps + i // blk_tiles, 0, 0)),
                   str_spec(WINDOW, LANES), str_spec(WINDOW, LANES), str_spec(2, D)] + expert_specs,
        out_shape=out_shape,
        scratch_shapes=[
            pltpu.VMEM((tile + 8, D), jnp.float32),
            pltpu.VMEM((N_KV, tile * GROUP_HEADS, LANES), jnp.bfloat16),
            pltpu.VMEM((tile, D), jnp.bfloat16),
            pltpu.VMEM((2, tile, LANES), jnp.float32),
            pltpu.VMEM((tile // CHUNK, N_KV, KEYS, LANES), jnp.bfloat16),
            pltpu.VMEM((tile // CHUNK, N_KV, KEYS, 2 * LANES), jnp.bfloat16),
            pltpu.VMEM((N_KV, (SLOTS - 2) * CHUNK, LANES), jnp.bfloat16),
            pltpu.VMEM((N_KV, (SLOTS - 2) * CHUNK, 2 * LANES), jnp.bfloat16),
            pltpu.VMEM((N_EXPERTS, 1), jnp.float32),
        ],
        compiler_params=pltpu.CompilerParams(
            dimension_semantics=("arbitrary", "arbitrary"), vmem_limit_bytes=VMEM_LIMIT),
        name="prompt_mixer",
    )(x, *experts32, *consts)


def _sample_mixer_kernel(sink_ref, x_ref, ck_ref, cv_ref, sc_ref, g1_ref, win_ref, cw_ref, qg_ref,
                         kg_ref, seg_ref, exp_ref, wco_ref, wao_ref, wo_ref, g2_ref, wr_ref, br_ref,
                         upper_ref, x1_ref, h2p_ref, rt_ref, cnt_ref, kwin_ref, vwin_ref, cst_ref,
                         uext, att_s, kbd, vbd, *, n_stream, seq):
    ext = seq + 8
    x = x_ref[...]
    h = _bf(_rms(x, g1_ref[...]))

    u = _dot(h, win_ref[:, C_C:C_C + D]) * _dot(h, win_ref[:, C_X:C_X + D])
    for s in range(n_stream):
        uext[s * ext + 6:s * ext + 8, :] = sc_ref[s]
        uext[s * ext + 8:(s + 1) * ext, :] = u[s * seq:(s + 1) * seq, :]
        cst_ref[s] = u[(s + 1) * seq - 2:(s + 1) * seq, :]
    conv = jnp.concatenate(
        [uext[s * ext + 6:s * ext + 6 + seq, :] * cw_ref[0:1, :]
         + uext[s * ext + 7:s * ext + 7 + seq, :] * cw_ref[1:2, :] for s in range(n_stream)],
        axis=0) + u * cw_ref[2:3, :]
    yc = _bf(_dot(h, win_ref[:, C_B:C_B + D]) * conv)
    o_c = _dot(yc, wco_ref[...])
    macc = _sigmoid(_dot(h, win_ref[:, C_G:C_G + D])) * o_c

    q = _dot(h, win_ref[:, C_Q:C_Q + D])
    qn = _bf(_head_norm_q(q, seg_ref, exp_ref, qg_ref[...]) * (1.0 / math.sqrt(HEAD_DIM)))
    kv = _dot(h, win_ref[:, C_K:C_K + 2 * LANES])
    kn = _head_norm_k(kv[:, 0:LANES], kg_ref[...])
    vv = kv[:, LANES:2 * LANES]

    kbd[...] = jnp.zeros_like(kbd)
    vbd[...] = jnp.zeros_like(vbd)
    lane = lax.broadcasted_iota(jnp.int32, (1, QUAD * KEYS), 1)
    valid = (lane % KEYS) < WINDOW + seq
    for s in range(n_stream):
        rows = slice(s * seq, (s + 1) * seq)
        kwin_ref[s, 0:WINDOW - seq, :] = ck_ref[s, seq:WINDOW, :]
        kwin_ref[s, WINDOW - seq:WINDOW, :] = kn[rows, :]
        vwin_ref[s, 0:WINDOW - seq, :] = cv_ref[s, seq:WINDOW, :]
        vwin_ref[s, WINDOW - seq:WINDOW, :] = vv[rows, :]
        kc4 = _rep4(ck_ref[s])
        vc4 = _rep4(cv_ref[s])
        kn4 = _rep4(kn[rows, :])
        vn4 = _rep4(vv[rows, :])
        for g in range(N_KV):
            _write_bd(kbd, g, 0, kc4[g])
            _write_bd(kbd, g, WINDOW, kn4[g])
            _write_bd(vbd, g, 0, vc4[g])
            _write_bd(vbd, g, WINDOW, vn4[g])
        for g in range(N_KV):
            kb = kbd[g]
            vb = vbd[g]
            for j in range(N_HEADS // N_KV // QUAD):
                h0 = g * (N_HEADS // N_KV) + j * QUAD
                qq = qn[rows, h0 * HEAD_DIM:(h0 + QUAD) * HEAD_DIM]
                sinks = [sink_ref[h0 + i] for i in range(QUAD)]
                o = _attend(qq, kb, vb, valid, sinks)
                att_s[rows, h0 * HEAD_DIM:(h0 + QUAD) * HEAD_DIM] = _bf(o)

    o_a = _dot(att_s[...], wao_ref[...])
    macc = macc + _sigmoid(_dot(h, win_ref[:, C_G + D:C_G + 2 * D])) * o_a
    carry = _merge_tail(x, macc, wo_ref, g2_ref, wr_ref, br_ref, upper_ref,
                        jnp.zeros((N_EXPERTS, 1), jnp.float32), x1_ref, h2p_ref, rt_ref.at[0])
    cnt_ref[0] = jnp.broadcast_to(carry, (N_EXPERTS, LANES))


def _sample_mixer(x, ck, cv, sc, sinks, wts):
    nb, seq, _ = x.shape
    rows = nb * seq
    x2 = x.reshape(rows, D)
    ins = (x2, ck, cv, sc) + tuple(wts)
    full = lambda a: pl.BlockSpec(a.shape, lambda *_, nd=a.ndim: (0,) * nd)
    out_shape = [
        jax.ShapeDtypeStruct((rows, D), jnp.float32),
        jax.ShapeDtypeStruct((rows * PK, LANES), jnp.uint32),
        jax.ShapeDtypeStruct((1, REC_ROWS, rows), jnp.float32),
        jax.ShapeDtypeStruct((1, N_EXPERTS, LANES), jnp.float32),
        jax.ShapeDtypeStruct((nb, WINDOW, LANES), jnp.float32),
        jax.ShapeDtypeStruct((nb, WINDOW, LANES), jnp.float32),
        jax.ShapeDtypeStruct((nb, 2, D), jnp.float32),
    ]
    grid_spec = pltpu.PrefetchScalarGridSpec(
        num_scalar_prefetch=1,
        grid=(1,),
        in_specs=[full(a) for a in ins],
        out_specs=[full(a) for a in out_shape],
        scratch_shapes=[
            pltpu.VMEM((nb * (seq + 8), D), jnp.float32),
            pltpu.VMEM((rows, D), jnp.bfloat16),
            pltpu.VMEM((N_KV, QUAD * KEYS, QUAD_W), jnp.bfloat16),
            pltpu.VMEM((N_KV, QUAD * KEYS, QUAD_W), jnp.bfloat16),
        ])
    return pl.pallas_call(
        functools.partial(_sample_mixer_kernel, n_stream=nb, seq=seq),
        grid_spec=grid_spec,
        out_shape=out_shape,
        compiler_params=pltpu.CompilerParams(
            dimension_semantics=("arbitrary",), vmem_limit_bytes=VMEM_LIMIT),
        name="sample_mixer",
    )(sinks, *ins)


def _sorted_rows(blk_tokens):
    rows = 2 * blk_tokens + N_EXPERTS * (MOE_PAD - 1) + MOE_TILE
    return -(-rows // MOE_TILE) * MOE_TILE


def _moe_kernel(pos_ref, tab_ref, stp_ref, h2p_ref, x1_ref, rt_ref, wg_ref, wu_ref, wd_ref, y_ref,
                xo, g1, g2, *, tile, n_tiles, max_tiles):
    b = pl.program_id(0)
    s = pl.program_id(1)
    n_groups = tile // UNROLL

    @pl.when((b == 0) & (s == 0))
    def _():
        xo[...] = jnp.zeros_like(xo)

    @pl.when(s < n_tiles)
    def _dispatch():
        base = (b * n_tiles + s) * (TOP_K * tile)

        def group(gi, carry):
            i0 = base + gi * UNROLL
            t0 = gi * (UNROLL * PK)
            for k in range(UNROLL):
                row = h2p_ref[pl.ds(pl.multiple_of(t0 + k * PK, PK), PK), :]
                xo[pl.ds(pl.multiple_of(pos_ref[i0 + k], PK), PK), :] = row
                xo[pl.ds(pl.multiple_of(pos_ref[i0 + tile + k], PK), PK), :] = row
            return carry

        lax.fori_loop(0, n_groups, group, 0)

    @pl.when((s >= n_tiles) & (s < n_tiles + EXPERT_STEPS))
    def _experts():
        st = b * EXPERT_STEPS + s - n_tiles
        first = stp_ref[2 * st]
        last = stp_ref[2 * st + 1]
        half = MOE_TILE // 2

        def entry(j):
            t = (b * max_tiles + j) * 3
            return pl.multiple_of(tab_ref[t], MOE_PAD * PK), tab_ref[t + 1], tab_ref[t + 2]

        def up(j):
            base, k, _ = entry(j)
            xs = _bf(_unpack_rows(
                [xo[pl.ds(base + q, MOE_TILE, stride=PK), :] for q in range(PK)]))
            gt = _dot(xs, wg_ref[k])
            return _bf(gt * _sigmoid(gt) * _dot(xs, wu_ref[k]))

        def down(j, hid):
            base, k, full = entry(j)
            out = _pack_rows(_dot(hid, wd_ref[k]))
            def put(lo):
                for q in range(PK):
                    xo[pl.ds(base + lo * PK + q, half, stride=PK), :] = (
                        out[lo:lo + half, q * LANES:(q + 1) * LANES])

            put(0)

            @pl.when(full > 0)
            def _():
                put(half)

        @pl.when(last > first)
        def _():
            def both(j, hid):
                nxt = up(j)
                down(j - 1, hid)
                return nxt

            down(last - 1, lax.fori_loop(first + 1, last, both, up(first)))

    @pl.when(s >= n_tiles + EXPERT_STEPS)
    def _combine():
        i = s - n_tiles - EXPERT_STEPS
        base = (b * n_tiles + i) * (TOP_K * tile)

        def group(gi, carry):
            i0 = base + gi * UNROLL
            t0 = gi * (UNROLL * PK)
            for k in range(UNROLL):
                dst = pl.ds(pl.multiple_of(t0 + k * PK, PK), PK)
                g1[dst, :] = xo[pl.ds(pl.multiple_of(pos_ref[i0 + k], PK), PK), :]
                g2[dst, :] = xo[pl.ds(pl.multiple_of(pos_ref[i0 + tile + k], PK), PK), :]
            return carry

        lax.fori_loop(0, n_groups, group, 0)
        o1 = _unpack_rows([g1[pl.ds(q, tile, stride=PK), :] for q in range(PK)])
        o2 = _unpack_rows([g2[pl.ds(q, tile, stride=PK), :] for q in range(PK)])
        rec = jnp.concatenate([rt_ref[0], jnp.zeros((LANES - REC_ROWS, tile), jnp.float32)], axis=0)
        cw = rec.T
        y_ref[...] = x1_ref[...] + cw[:, R_W1:R_W1 + 1] * o1 + cw[:, R_W2:R_W2 + 1] * o2


def _moe(pos, tab, stp, rt, h2p, x1, wg, wu, wd, tile, n_tiles):
    n = x1.shape[0]
    n_blocks = n // (tile * n_tiles)
    n_steps = 2 * n_tiles + EXPERT_STEPS
    last = n_tiles - 1

    def disp(b, s, *_):
        return (b * n_tiles + jnp.minimum(s, last), 0)

    def comb(b, s, *_):
        return (b * n_tiles + jnp.clip(s - n_tiles - EXPERT_STEPS, 0, last), 0)

    def wmap(b, s, *_):
        return (jnp.clip(s - n_tiles, 0, EXPERT_STEPS - 1), 0, 0)

    grid_spec = pltpu.PrefetchScalarGridSpec(
        num_scalar_prefetch=3,
        grid=(n_blocks, n_steps),
        in_specs=[pl.BlockSpec((tile * PK, LANES), disp),
                  pl.BlockSpec((tile, D), comb),
                  pl.BlockSpec((1, REC_ROWS, tile), lambda b, s, *_: comb(b, s) + (0,)),
                  pl.BlockSpec((STEP_EXPERTS, D, D_EXPERT), wmap),
                  pl.BlockSpec((STEP_EXPERTS, D, D_EXPERT), wmap),
                  pl.BlockSpec((STEP_EXPERTS, D_EXPERT, D), wmap)],
        out_specs=pl.BlockSpec((tile, D), comb),
        scratch_shapes=[
            pltpu.VMEM((_sorted_rows(tile * n_tiles) * PK, LANES), jnp.uint32),
            pltpu.VMEM((tile * PK, LANES), jnp.uint32),
            pltpu.VMEM((tile * PK, LANES), jnp.uint32),
        ])
    return pl.pallas_call(
        functools.partial(_moe_kernel, tile=tile, n_tiles=n_tiles,
                          max_tiles=_max_tiles(tile * n_tiles)),
        grid_spec=grid_spec,
        out_shape=jax.ShapeDtypeStruct((n, D), jnp.float32),
        compiler_params=pltpu.CompilerParams(
            dimension_semantics=("arbitrary", "arbitrary"), vmem_limit_bytes=VMEM_LIMIT),
        name="moe",
    )(pos, tab, stp, h2p, x1, rt, wg, wu, wd)


def _max_tiles(blk_tokens):
    return N_EXPERTS + 2 * blk_tokens // MOE_TILE


def _sorted_positions(rt, cnt, blk_tokens):
    experts = jnp.arange(N_EXPERTS, dtype=jnp.int32)
    n_tiles, _, tile = rt.shape
    counts = cnt[:, :, 0].astype(jnp.int32)
    padded = (counts + MOE_PAD - 1) // MOE_PAD * MOE_PAD
    off = jnp.cumsum(padded, axis=1) - padded
    e = rt[:, R_E1:R_E1 + TOP_K, :].astype(jnp.int32)
    rank = rt[:, R_RANK1:R_RANK1 + TOP_K, :].astype(jnp.int32)
    off_tile = jnp.repeat(off, blk_tokens // tile, axis=0)
    pos = rank
    for x in range(N_EXPERTS):
        pos = pos + jnp.where(e == x, off_tile[:, x, None, None], 0)
    pos = pos * PK

    tiles = (counts + MOE_TILE - 1) // MOE_TILE
    t_end = jnp.cumsum(tiles, axis=1)
    t_beg = t_end - tiles
    j = jnp.arange(_max_tiles(blk_tokens), dtype=jnp.int32)
    e_of = jnp.minimum(jnp.sum(t_end[:, None, :] <= j[None, :, None], axis=-1), N_EXPERTS - 1)
    pick = lambda a: jnp.sum(jnp.where(e_of[:, :, None] == experts, a[:, None, :], 0), axis=-1)
    local = (j[None, :] - pick(t_beg)) * MOE_TILE
    tab = jnp.stack([(pick(off) + local) * PK, e_of % STEP_EXPERTS,
                     (pick(counts) - local > MOE_TILE // 2).astype(jnp.int32)], axis=-1)
    stp = jnp.stack([t_beg[:, ::STEP_EXPERTS], t_end[:, STEP_EXPERTS - 1::STEP_EXPERTS]], axis=-1)
    return pos.reshape(-1), tab.reshape(-1), stp.reshape(-1), rt


def _segment_matrices():
    head = jnp.arange(D) // HEAD_DIM
    col = jnp.arange(LANES)
    seg = (head[:, None] == col[None, :]).astype(jnp.bfloat16)
    row = jnp.arange(2 * LANES) % LANES
    expm = (row[:, None] == head[None, :]).astype(jnp.bfloat16)
    return seg, expm


def _strict_upper(n):
    r = jnp.arange(n)
    return (r[:, None] < r[None, :]).astype(jnp.bfloat16)


def kernel(x_prompt, x_sample, cache_k, cache_v, state_conv, norm_mix_g, w_in, conv_w, q_norm_g,
           k_norm_g, attn_sinks, w_conv_out, w_attn_out, w_out, norm_ffn_g, w_group, b_group,
           w_router, b_router, w_gate, w_up, w_down):
    nb, seq, _ = x_prompt.shape
    ns, sseq, _ = x_sample.shape
    l = 0
    seg, expm = _segment_matrices()
    pad = RT_ROWS - N_EXPERTS - N_GROUPS
    wr = jnp.concatenate([w_router[l].T, w_group[l].T, jnp.zeros((pad, D), jnp.float32)], axis=0)
    br = jnp.concatenate([b_router[l], b_group[l], jnp.zeros((pad,), jnp.float32)]).reshape(RT_ROWS, 1)
    wts = (norm_mix_g[l].reshape(1, D), _bf(w_in[l]), conv_w[l],
           jnp.tile(q_norm_g[l], N_HEADS).reshape(1, D), jnp.tile(k_norm_g[l], N_KV).reshape(1, LANES),
           seg, expm, _bf(w_conv_out[l]), _bf(w_attn_out[l]), _bf(w_out[l]),
           norm_ffn_g[l].reshape(1, D), _bf(wr), br)
    sinks = attn_sinks[l]
    n_s = ns * sseq

    sink2 = sinks * LOG2E
    sink_hi = _bf(sink2).astype(jnp.float32)
    q_extra = jnp.zeros((N_HEADS, 1, HEAD_DIM), jnp.float32)
    q_extra = q_extra.at[:, 0, BIAS_LANE - HEAD_DIM].set(1.0)
    q_extra = q_extra.at[:, 0, SINK_LANE - HEAD_DIM].set(sink_hi)
    q_extra = q_extra.at[:, 0, SINK_LANE + 1 - HEAD_DIM].set(sink2 - sink_hi)
    x1p, h2pp, rtp, cntp, kwp, vwp, cstp, wg, wu, wd = _prompt_mixer(
        x_prompt, (w_gate[l], w_up[l], w_down[l]), q_extra, wts + (_strict_upper(TOK_TILE),), tile=TOK_TILE, blk_tiles=BLOCK_TILES)
    x1s, h2ps, rts, cnts, kws, vws, csts = _sample_mixer(
        x_sample, cache_k[l].reshape(ns, WINDOW, LANES), cache_v[l].reshape(ns, WINDOW, LANES),
        state_conv[l], sinks, wts + (_strict_upper(n_s),))

    yp = _moe(*_sorted_positions(rtp, cntp, TOK_TILE * BLOCK_TILES), h2pp, x1p.reshape(nb * seq, D), wg, wu, wd,
              tile=TOK_TILE, n_tiles=BLOCK_TILES)
    ys = _moe(*_sorted_positions(rts, cnts, n_s), h2ps, x1s, wg, wu, wd, tile=n_s, n_tiles=1)

    kv_shape = lambda n: (1, n, WINDOW, N_KV, HEAD_DIM)
    return (yp.reshape(nb, seq, D), ys.reshape(ns, sseq, D),
            kwp.reshape(kv_shape(nb)), vwp.reshape(kv_shape(nb)), cstp.reshape(1, nb, 2, D),
            kws.reshape(kv_shape(ns)), vws.reshape(kv_shape(ns)), csts.reshape(1, ns, 2, D))
```

```python
import functools
import math

import jax
import jax.numpy as jnp
from jax import lax
from jax.experimental import pallas as pl
from jax.experimental.pallas import tpu as pltpu

D = 1024
CHUNK = 64
HEAD_DIM = 64
N_HEADS = 16
N_KV = 2
WINDOW = 128
N_GROUPS = 4
EPG = 8
N_EXPERTS = 32
TOP_K = 2
D_EXPERT = 256
EPS = 1e-6
NEG = -1e30

LANES = 128
QUAD = 4
QUAD_W = QUAD * HEAD_DIM
SLOTS = 4
KEYS = SLOTS * CHUNK
GROUP_HEADS = N_HEADS // N_KV
STACK = GROUP_HEADS * CHUNK
BIAS_LANE = HEAD_DIM
SINK_LANE = HEAD_DIM + 1
LOG2E = 1.0 / math.log(2.0)
RT_ROWS = LANES
REC_ROWS = 8
HALF = D // 2
PK = HALF // LANES

R_E1, R_E2, R_W1, R_W2, R_RANK1, R_RANK2 = 0, 1, 2, 3, 4, 5

C_B, C_C, C_X, C_Q, C_K, C_V, C_G = 0, 1024, 2048, 3072, 4096, 4224, 4352

MOE_TILE = 256
SAMPLE_MOE_TILE = 64
TOK_TILE = 512
BLOCK_TILES = 8
UNROLL = 16
STEP_EXPERTS = 4
EXPERT_STEPS = N_EXPERTS // STEP_EXPERTS

VMEM_LIMIT = 60 * 1024 * 1024

_NT = (((1,), (1,)), ((), ()))


def _bf(x):
    return x.astype(jnp.bfloat16)


def _dot(a, b):
    return jnp.dot(a, b, preferred_element_type=jnp.float32)


def _rms(x, g):
    return x * lax.rsqrt(jnp.mean(x * x, axis=-1, keepdims=True) + EPS) * g


def _sigmoid(x):
    return 1.0 / (1.0 + jnp.exp(-x))


def _pack_rows(x):
    return pltpu.pack_elementwise([x[:, :HALF], x[:, HALF:]], packed_dtype=jnp.bfloat16)


def _unpack_rows(parts):
    def half(i):
        return [pltpu.unpack_elementwise(p, index=i, packed_dtype=jnp.bfloat16,
                                         unpacked_dtype=jnp.float32) for p in parts]
    return jnp.concatenate(half(0) + half(1), axis=-1)


def _head_norm_q(q, seg_ref, exp_ref, qg):
    ssq = _dot(_bf(q * q), seg_ref[...])
    inv = lax.rsqrt(ssq * (1.0 / HEAD_DIM) + EPS)
    hi = _bf(inv)
    lo = _bf(inv - hi.astype(jnp.float32))
    full = _dot(jnp.concatenate([hi, lo], axis=-1), exp_ref[...])
    return q * full * qg


def _head_norm_k(k, kg):
    lane = lax.broadcasted_iota(jnp.int32, k.shape, 1)
    lo_half = lane < HEAD_DIM
    sq = k * k
    s0 = jnp.sum(jnp.where(lo_half, sq, 0.0), axis=-1, keepdims=True)
    s1 = jnp.sum(jnp.where(lo_half, 0.0, sq), axis=-1, keepdims=True)
    i0 = lax.rsqrt(s0 * (1.0 / HEAD_DIM) + EPS)
    i1 = lax.rsqrt(s1 * (1.0 / HEAD_DIM) + EPS)
    return k * jnp.where(lo_half, i0, i1) * kg


def _rep4(x):
    lane = lax.broadcasted_iota(jnp.int32, x.shape, 1)
    sw = pltpu.roll(x, HEAD_DIM, axis=1)
    a2 = jnp.where(lane < HEAD_DIM, x, sw)
    b2 = jnp.where(lane < HEAD_DIM, sw, x)
    return jnp.concatenate([a2, a2], axis=-1), jnp.concatenate([b2, b2], axis=-1)


def _write_bd(ref, g, row0, x4):
    rows = x4.shape[0]
    lane = lax.broadcasted_iota(jnp.int32, x4.shape, 1)
    for i in range(QUAD):
        keep = (lane >= i * HEAD_DIM) & (lane < (i + 1) * HEAD_DIM)
        ref[g, pl.ds(i * KEYS + row0, rows), :] = _bf(jnp.where(keep, x4, 0.0))


def _attend(qq, kbd, vbd, valid, sinks):
    s = lax.dot_general(qq, kbd, _NT, preferred_element_type=jnp.float32)
    s = jnp.where(valid, s, NEG)
    ps = []
    for i in range(QUAD):
        sh = s[:, i * KEYS:(i + 1) * KEYS]
        m = jnp.maximum(jnp.max(sh, axis=-1, keepdims=True), sinks[i])
        p = jnp.exp(sh - m)
        den = jnp.sum(p, axis=-1, keepdims=True) + jnp.exp(sinks[i] - m)
        ps.append(_bf(p / den))
    return _dot(jnp.concatenate(ps, axis=-1), vbd)


def _route(lt, upper_ref, carry):
    t = lt.shape[1]
    big = float(RT_ROWS)
    grow = lax.broadcasted_iota(jnp.int32, (REC_ROWS, t), 0).astype(jnp.float32)
    is_g = grow < N_GROUPS
    gl = jnp.where(is_g, lt[N_EXPERTS:N_EXPERTS + REC_ROWS], NEG)
    gmax = jnp.max(gl, axis=0, keepdims=True)
    gsum = jnp.sum(jnp.where(is_g, jnp.exp(gl - gmax), 0.0), axis=0, keepdims=True)
    g_w = 1.0 / gsum
    g_sel = jnp.min(jnp.where(is_g & (gl == gmax), grow, big), axis=0, keepdims=True)
    logits = lt[0:N_EXPERTS]
    erow = lax.broadcasted_iota(jnp.int32, (N_EXPERTS, t), 0).astype(jnp.float32)
    lo = g_sel * EPG
    in_grp = (erow >= lo) & (erow < lo + EPG)
    el = jnp.where(in_grp, logits, NEG)
    v1 = jnp.max(el, axis=0, keepdims=True)
    i1 = jnp.min(jnp.where(in_grp & (el == v1), erow, big), axis=0, keepdims=True)
    rest = in_grp & (erow != i1)
    el2 = jnp.where(rest, logits, NEG)
    v2 = jnp.max(el2, axis=0, keepdims=True)
    i2 = jnp.min(jnp.where(rest & (el2 == v2), erow, big), axis=0, keepdims=True)
    e2 = jnp.exp(v2 - v1)
    den = 1.0 + e2
    w1 = (1.0 / den) * g_w
    w2 = (e2 / den) * g_w
    oh1 = jnp.where(erow == i1, 1.0, 0.0)
    oh2 = jnp.where(erow == i2, 1.0, 0.0)
    both = oh1 + oh2
    before = _dot(_bf(both), upper_ref[...]) + carry
    r1 = jnp.sum(before * oh1, axis=0, keepdims=True)
    r2 = jnp.sum(before * oh2, axis=0, keepdims=True)
    rec = jnp.zeros((REC_ROWS, t), jnp.float32)
    for rw, val in ((R_E1, i1), (R_E2, i2), (R_W1, w1), (R_W2, w2), (R_RANK1, r1), (R_RANK2, r2)):
        rec = jnp.where(grow == rw, val, rec)
    return rec, carry + jnp.sum(both, axis=1, keepdims=True)


def _route_stage(x1, g2_ref, wr_ref, br_ref, upper_ref, carry, h2p_ref, rt_ref):
    rows = x1.shape[0]
    h2 = _rms(x1, g2_ref[...])
    pk = _pack_rows(h2)
    for j in range(PK):
        h2p_ref[pl.ds(j, rows, stride=PK), :] = pk[:, j * LANES:(j + 1) * LANES]
    lt = lax.dot_general(wr_ref[...], _bf(h2), _NT, preferred_element_type=jnp.float32) + br_ref[...]
    rec, carry = _route(lt, upper_ref, carry)
    rt_ref[...] = rec
    return carry


def _merge_tail(x, macc, wo_ref, g2_ref, wr_ref, br_ref, upper_ref, carry, x1_ref, h2p_ref, rt_ref):
    x1 = x + _dot(_bf(macc), wo_ref[...])
    x1_ref[...] = x1
    return _route_stage(x1, g2_ref, wr_ref, br_ref, upper_ref, carry, h2p_ref, rt_ref)


def _prompt_mixer_kernel(x_ref, wg32_ref, wu32_ref, wd32_ref,
                         qx_ref, g1_ref, win_ref, cw_ref, qg_ref, kg_ref, seg_ref, exp_ref,
                         wco_ref, wao_ref, wo_ref, g2_ref, wr_ref, br_ref, upper_ref,
                         x1_ref, h2p_ref, rt_ref, cnt_ref, kwin_ref, vwin_ref, cst_ref,
                         wg16_ref, wu16_ref, wd16_ref,
                         uext, qs_s, att_s, kn_s, kop, vop, khist, vhist, carry_s, x1_s,
                         *, tile, blk_tiles, n_it, n_main):
    step = pl.program_id(0)
    it = step % n_it
    n_chunk = tile // CHUNK

    @pl.when(step == 0)
    def _():
        shape = (n_chunk, N_KV, CHUNK, LANES)
        row = lax.broadcasted_iota(jnp.int32, shape, 2)
        ln = lax.broadcasted_iota(jnp.int32, shape, 3)
        is_sink = (row == 0) & ((ln == SINK_LANE) | (ln == SINK_LANE + 1))
        tail = pl.ds((SLOTS - 1) * CHUNK, CHUNK)
        kop[:, :, tail, :] = _bf(jnp.where(is_sink, 1.0, jnp.where((row > 0) & (ln == BIAS_LANE), NEG, 0.0)))
        vop[:, :, tail, 0:LANES] = jnp.zeros(shape, jnp.bfloat16)
        vop[:, :, tail, LANES:2 * LANES] = jnp.ones(shape, jnp.bfloat16)
        x1_s[1] = jnp.zeros((tile, D), jnp.float32)
        carry_s[...] = jnp.zeros_like(carry_s)

    def route_previous():
        first_of_block = (step + blk_tiles - 1) % blk_tiles == 0
        carry = jnp.where(first_of_block, 0.0, carry_s[...])
        carry = _route_stage(x1_s[(step + 1) % 2], g2_ref, wr_ref, br_ref, upper_ref, carry,
                             h2p_ref, rt_ref.at[0])
        carry_s[...] = carry
        cnt_ref[0] = jnp.broadcast_to(carry, (N_EXPERTS, LANES))

    @pl.when(step == n_main)
    def _():
        route_previous()

    @pl.when((step < n_main) & (it == 0))
    def _():
        uext[0:8, :] = jnp.zeros((8, D), jnp.float32)
        ln = lax.broadcasted_iota(jnp.int32, khist.shape, 2)
        khist[...] = _bf(jnp.where(ln == BIAS_LANE, NEG, 0.0))
        vhist[...] = jnp.zeros_like(vhist)

    @pl.when(step < n_main)
    def _():
        route_previous()
        _mixer_tile(x_ref, wg32_ref, wu32_ref, wd32_ref, qx_ref, g1_ref, win_ref, cw_ref, qg_ref,
                    kg_ref, seg_ref, exp_ref, wco_ref, wao_ref, wo_ref, x1_ref, kwin_ref, vwin_ref,
                    cst_ref, wg16_ref, wu16_ref, wd16_ref, uext, qs_s, att_s, kn_s, kop, vop, khist,
                    vhist, x1_s.at[step % 2], tile=tile)


def _mixer_tile(x_ref, wg32_ref, wu32_ref, wd32_ref, qx_ref, g1_ref, win_ref, cw_ref, qg_ref,
                kg_ref, seg_ref, exp_ref, wco_ref, wao_ref, wo_ref, x1_ref, kwin_ref, vwin_ref,
                cst_ref, wg16_ref, wu16_ref, wd16_ref, uext, qs_s, att_s, kn_s, kop, vop, khist,
                vhist, x1_keep, *, tile):
    n_chunk = tile // CHUNK
    wg16_ref[...] = _bf(wg32_ref[...])
    wu16_ref[...] = _bf(wu32_ref[...])
    wd16_ref[...] = _bf(wd32_ref[...])

    x = x_ref[0]
    h = _bf(_rms(x, g1_ref[...]))

    u = _dot(h, win_ref[:, C_C:C_C + D]) * _dot(h, win_ref[:, C_X:C_X + D])
    uext[8:8 + tile, :] = u
    conv = (uext[6:6 + tile, :] * cw_ref[0:1, :] + uext[7:7 + tile, :] * cw_ref[1:2, :]
            + u * cw_ref[2:3, :])
    yc = _bf(_dot(h, win_ref[:, C_B:C_B + D]) * conv)
    o_c = _dot(yc, wco_ref[...])
    macc = _sigmoid(_dot(h, win_ref[:, C_G:C_G + D])) * o_c
    last2 = uext[tile + 6:tile + 8, :]
    uext[6:8, :] = last2

    cst_ref[0] = last2

    q = _dot(h, win_ref[:, C_Q:C_Q + D])
    qn = _head_norm_q(q, seg_ref, exp_ref, qg_ref[...]) * (LOG2E / math.sqrt(HEAD_DIM))
    for hd in range(N_HEADS):
        g, r = divmod(hd, GROUP_HEADS)
        ext = jnp.broadcast_to(qx_ref[hd], (tile, HEAD_DIM))
        piece = _bf(jnp.concatenate([qn[:, hd * HEAD_DIM:(hd + 1) * HEAD_DIM], ext], axis=-1))
        for c in range(n_chunk):
            qs_s[g, c * STACK + r * CHUNK:c * STACK + (r + 1) * CHUNK, :] = (
                piece[c * CHUNK:(c + 1) * CHUNK, :])
    kv = _dot(h, win_ref[:, C_K:C_K + 2 * LANES])
    kn = _head_norm_k(kv[:, 0:LANES], kg_ref[...])
    vv = kv[:, LANES:2 * LANES]
    kn_s[0] = kn
    kn_s[1] = vv

    kwin_ref[0] = kn[tile - WINDOW:, :]
    vwin_ref[0] = vv[tile - WINDOW:, :]

    lane = lax.broadcasted_iota(jnp.int32, (CHUNK, LANES), 1)
    lo_half = lane < HEAD_DIM
    ones_blk = jnp.ones((CHUNK, LANES), jnp.bfloat16)
    own = SLOTS - 2

    for c in range(own):
        for sl in range(own - c):
            src = pl.ds((c + sl) * CHUNK, CHUNK)
            kop[c, :, pl.ds(sl * CHUNK, CHUNK), :] = khist[:, src, :]
            vop[c, :, pl.ds(sl * CHUNK, CHUNK), :] = vhist[:, src, :]

    def chunk_body(c):
        r0 = c * CHUNK
        kc = kn_s[0, pl.ds(r0, CHUNK), :]
        vc = kn_s[1, pl.ds(r0, CHUNK), :]
        ksw = pltpu.roll(kc, HEAD_DIM, axis=1)
        vsw = pltpu.roll(vc, HEAD_DIM, axis=1)
        for g in range(N_KV):
            kb = _bf(jnp.where(lo_half, kc if g == 0 else ksw, 0.0))
            vb = _bf(jnp.where(lo_half, vc, vsw) if g == 0 else jnp.where(lo_half, vsw, vc))
            for d in range(own + 1):
                if c + d < n_chunk:
                    dst = pl.ds((own - d) * CHUNK, CHUNK)
                    kop[c + d, g, dst, :] = kb
                    vop[c + d, g, dst, 0:LANES] = vb
                    vop[c + d, g, dst, LANES:2 * LANES] = ones_blk
            if c >= n_chunk - own:
                dst = pl.ds((c - (n_chunk - own)) * CHUNK, CHUNK)
                khist[g, dst, :] = kb
                vhist[g, dst, 0:LANES] = vb
                vhist[g, dst, LANES:2 * LANES] = ones_blk
        for g in range(N_KV):
            s = lax.dot_general(qs_s[g, pl.ds(c * STACK, STACK), :], kop[c, g], _NT,
                                preferred_element_type=jnp.float32)
            m = jnp.max(s, axis=-1, keepdims=True)
            o = _dot(_bf(jnp.exp2(s - m)), vop[c, g])
            out = o[:, 0:LANES] / o[:, LANES:2 * LANES]
            for r in range(0, GROUP_HEADS, 2):
                pair = jnp.where(lo_half, out[r * CHUNK:(r + 1) * CHUNK, :],
                                 out[(r + 1) * CHUNK:(r + 2) * CHUNK, :])
                col = (g * GROUP_HEADS + r) * HEAD_DIM
                att_s[pl.ds(r0, CHUNK), col:col + LANES] = _bf(pair)

    for c in range(n_chunk):
        chunk_body(c)

    o_a = _dot(att_s[...], wao_ref[...])
    macc = macc + _sigmoid(_dot(h, win_ref[:, C_G + D:C_G + 2 * D])) * o_a
    x1 = x + _dot(_bf(macc), wo_ref[...])
    x1_ref[0] = x1
    x1_keep[...] = x1


def _const_spec(shape):
    nd = len(shape)
    return pl.BlockSpec(shape, lambda *_: (0,) * nd, pipeline_mode=pl.Buffered(1))


def _prompt_mixer(x, experts32, q_extra, wts, tile, blk_tiles):
    nb, seq, _ = x.shape
    n_it = seq // tile
    n_main = nb * n_it
    bps = n_it // blk_tiles
    parts = n_main // N_EXPERTS
    consts = (q_extra,) + tuple(wts)

    cur = lambda s: jnp.minimum(s, n_main - 1)
    prev = lambda s: jnp.maximum(s - 1, 0)
    row_spec = lambda w: pl.BlockSpec((1, tile, w), lambda s: (cur(s) // n_it, cur(s) % n_it, 0))
    str_spec = lambda r, w: pl.BlockSpec((1, r, w), lambda s: (cur(s) // n_it, 0, 0))
    part_spec = lambda r, w: pl.BlockSpec(
        (1, r // parts, w), lambda s: (cur(s) // parts, cur(s) % parts, 0))
    expert_specs = [part_spec(D, D_EXPERT), part_spec(D, D_EXPERT), part_spec(D_EXPERT, D)]
    out_shape = [
        jax.ShapeDtypeStruct((nb, seq, D), jnp.float32),
        jax.ShapeDtypeStruct((nb * seq * PK, LANES), jnp.uint32),
        jax.ShapeDtypeStruct((n_main, REC_ROWS, tile), jnp.float32),
        jax.ShapeDtypeStruct((nb * bps, N_EXPERTS, LANES), jnp.float32),
        jax.ShapeDtypeStruct((nb, WINDOW, LANES), jnp.float32),
        jax.ShapeDtypeStruct((nb, WINDOW, LANES), jnp.float32),
        jax.ShapeDtypeStruct((nb, 2, D), jnp.float32),
    ] + [jax.ShapeDtypeStruct(w.shape, jnp.bfloat16) for w in experts32]
    return pl.pallas_call(
        functools.partial(_prompt_mixer_kernel, tile=tile, blk_tiles=blk_tiles, n_it=n_it, n_main=n_main),
        grid=(n_main + 1,),
        in_specs=[row_spec(D)] + expert_specs + [_const_spec(w.shape) for w in consts],
        out_specs=[row_spec(D),
                   pl.BlockSpec((tile * PK, LANES), lambda s: (prev(s), 0)),
                   pl.BlockSpec((1, REC_ROWS, tile), lambda s: (prev(s), 0, 0)),
                   pl.BlockSpec((1, N_EXPERTS, LANES), lambda s: (prev(s) // blk_tiles, 0, 0)),
                   str_spec(WINDOW, LANES), str_spec(WINDOW, LANES), str_spec(2, D)] + expert_specs,
        out_shape=out_shape,
        scratch_shapes=[
            pltpu.VMEM((tile + 8, D), jnp.float32),
            pltpu.VMEM((N_KV, tile * GROUP_HEADS, LANES), jnp.bfloat16),
            pltpu.VMEM((tile, D), jnp.bfloat16),
            pltpu.VMEM((2, tile, LANES), jnp.float32),
            pltpu.VMEM((tile // CHUNK, N_KV, KEYS, LANES), jnp.bfloat16),
            pltpu.VMEM((tile // CHUNK, N_KV, KEYS, 2 * LANES), jnp.bfloat16),
            pltpu.VMEM((N_KV, (SLOTS - 2) * CHUNK, LANES), jnp.bfloat16),
            pltpu.VMEM((N_KV, (SLOTS - 2) * CHUNK, 2 * LANES), jnp.bfloat16),
            pltpu.VMEM((N_EXPERTS, 1), jnp.float32),
            pltpu.VMEM((2, tile, D), jnp.float32),
        ],
        compiler_params=pltpu.CompilerParams(
            dimension_semantics=("arbitrary",), vmem_limit_bytes=VMEM_LIMIT),
        name="prompt_mixer",
    )(x, *experts32, *consts)


def _sample_mixer_kernel(sink_ref, x_ref, ck_ref, cv_ref, sc_ref, g1_ref, win_ref, cw_ref, qg_ref,
                         kg_ref, seg_ref, exp_ref, wco_ref, wao_ref, wo_ref, g2_ref, wr_ref, br_ref,
                         upper_ref, x1_ref, h2p_ref, rt_ref, cnt_ref, kwin_ref, vwin_ref, cst_ref,
                         uext, att_s, kbd, vbd, *, n_stream, seq):
    ext = seq + 8
    x = x_ref[...]
    h = _bf(_rms(x, g1_ref[...]))

    u = _dot(h, win_ref[:, C_C:C_C + D]) * _dot(h, win_ref[:, C_X:C_X + D])
    for s in range(n_stream):
        uext[s * ext + 6:s * ext + 8, :] = sc_ref[s]
        uext[s * ext + 8:(s + 1) * ext, :] = u[s * seq:(s + 1) * seq, :]
        cst_ref[s] = u[(s + 1) * seq - 2:(s + 1) * seq, :]
    conv = jnp.concatenate(
        [uext[s * ext + 6:s * ext + 6 + seq, :] * cw_ref[0:1, :]
         + uext[s * ext + 7:s * ext + 7 + seq, :] * cw_ref[1:2, :] for s in range(n_stream)],
        axis=0) + u * cw_ref[2:3, :]
    yc = _bf(_dot(h, win_ref[:, C_B:C_B + D]) * conv)
    o_c = _dot(yc, wco_ref[...])
    macc = _sigmoid(_dot(h, win_ref[:, C_G:C_G + D])) * o_c

    q = _dot(h, win_ref[:, C_Q:C_Q + D])
    qn = _bf(_head_norm_q(q, seg_ref, exp_ref, qg_ref[...]) * (1.0 / math.sqrt(HEAD_DIM)))
    kv = _dot(h, win_ref[:, C_K:C_K + 2 * LANES])
    kn = _head_norm_k(kv[:, 0:LANES], kg_ref[...])
    vv = kv[:, LANES:2 * LANES]

    kbd[...] = jnp.zeros_like(kbd)
    vbd[...] = jnp.zeros_like(vbd)
    lane = lax.broadcasted_iota(jnp.int32, (1, QUAD * KEYS), 1)
    valid = (lane % KEYS) < WINDOW + seq
    for s in range(n_stream):
        rows = slice(s * seq, (s + 1) * seq)
        kwin_ref[s, 0:WINDOW - seq, :] = ck_ref[s, seq:WINDOW, :]
        kwin_ref[s, WINDOW - seq:WINDOW, :] = kn[rows, :]
        vwin_ref[s, 0:WINDOW - seq, :] = cv_ref[s, seq:WINDOW, :]
        vwin_ref[s, WINDOW - seq:WINDOW, :] = vv[rows, :]
        kc4 = _rep4(ck_ref[s])
        vc4 = _rep4(cv_ref[s])
        kn4 = _rep4(kn[rows, :])
        vn4 = _rep4(vv[rows, :])
        for g in range(N_KV):
            _write_bd(kbd, g, 0, kc4[g])
            _write_bd(kbd, g, WINDOW, kn4[g])
            _write_bd(vbd, g, 0, vc4[g])
            _write_bd(vbd, g, WINDOW, vn4[g])
        for g in range(N_KV):
            kb = kbd[g]
            vb = vbd[g]
            for j in range(N_HEADS // N_KV // QUAD):
                h0 = g * (N_HEADS // N_KV) + j * QUAD
                qq = qn[rows, h0 * HEAD_DIM:(h0 + QUAD) * HEAD_DIM]
                sinks = [sink_ref[h0 + i] for i in range(QUAD)]
                o = _attend(qq, kb, vb, valid, sinks)
                att_s[rows, h0 * HEAD_DIM:(h0 + QUAD) * HEAD_DIM] = _bf(o)

    o_a = _dot(att_s[...], wao_ref[...])
    macc = macc + _sigmoid(_dot(h, win_ref[:, C_G + D:C_G + 2 * D])) * o_a
    carry = _merge_tail(x, macc, wo_ref, g2_ref, wr_ref, br_ref, upper_ref,
                        jnp.zeros((N_EXPERTS, 1), jnp.float32), x1_ref, h2p_ref, rt_ref.at[0])
    cnt_ref[0] = jnp.broadcast_to(carry, (N_EXPERTS, LANES))


def _sample_mixer(x, ck, cv, sc, sinks, wts):
    nb, seq, _ = x.shape
    rows = nb * seq
    x2 = x.reshape(rows, D)
    ins = (x2, ck, cv, sc) + tuple(wts)
    full = lambda a: pl.BlockSpec(a.shape, lambda *_, nd=a.ndim: (0,) * nd)
    out_shape = [
        jax.ShapeDtypeStruct((rows, D), jnp.float32),
        jax.ShapeDtypeStruct((rows * PK, LANES), jnp.uint32),
        jax.ShapeDtypeStruct((1, REC_ROWS, rows), jnp.float32),
        jax.ShapeDtypeStruct((1, N_EXPERTS, LANES), jnp.float32),
        jax.ShapeDtypeStruct((nb, WINDOW, LANES), jnp.float32),
        jax.ShapeDtypeStruct((nb, WINDOW, LANES), jnp.float32),
        jax.ShapeDtypeStruct((nb, 2, D), jnp.float32),
    ]
    grid_spec = pltpu.PrefetchScalarGridSpec(
        num_scalar_prefetch=1,
        grid=(1,),
        in_specs=[full(a) for a in ins],
        out_specs=[full(a) for a in out_shape],
        scratch_shapes=[
            pltpu.VMEM((nb * (seq + 8), D), jnp.float32),
            pltpu.VMEM((rows, D), jnp.bfloat16),
            pltpu.VMEM((N_KV, QUAD * KEYS, QUAD_W), jnp.bfloat16),
            pltpu.VMEM((N_KV, QUAD * KEYS, QUAD_W), jnp.bfloat16),
        ])
    return pl.pallas_call(
        functools.partial(_sample_mixer_kernel, n_stream=nb, seq=seq),
        grid_spec=grid_spec,
        out_shape=out_shape,
        compiler_params=pltpu.CompilerParams(
            dimension_semantics=("arbitrary",), vmem_limit_bytes=VMEM_LIMIT),
        name="sample_mixer",
    )(sinks, *ins)


def _sorted_rows(blk_tokens, mt):
    rows = 2 * blk_tokens + N_EXPERTS * (mt // 2 - 1) + mt
    return -(-rows // mt) * mt


def _moe_kernel(pos_ref, tab_ref, stp_ref, h2p_ref, x1_ref, rt_ref, wg_ref, wu_ref, wd_ref, y_ref,
                xo, g1, g2, *, tile, n_tiles, max_tiles, mt):
    b = pl.program_id(0)
    s = pl.program_id(1)
    n_groups = tile // UNROLL

    @pl.when((b == 0) & (s == 0))
    def _():
        xo[...] = jnp.zeros_like(xo)

    @pl.when(s < n_tiles)
    def _dispatch():
        base = (b * n_tiles + s) * (TOP_K * tile)

        def group(gi, carry):
            i0 = base + gi * UNROLL
            t0 = gi * (UNROLL * PK)
            for k in range(UNROLL):
                row = h2p_ref[pl.ds(pl.multiple_of(t0 + k * PK, PK), PK), :]
                xo[pl.ds(pl.multiple_of(pos_ref[i0 + k], PK), PK), :] = row
                xo[pl.ds(pl.multiple_of(pos_ref[i0 + tile + k], PK), PK), :] = row
            return carry

        lax.fori_loop(0, n_groups, group, 0)

    @pl.when((s >= n_tiles) & (s < n_tiles + EXPERT_STEPS))
    def _experts():
        st = b * EXPERT_STEPS + s - n_tiles
        first = stp_ref[2 * st]
        last = stp_ref[2 * st + 1]
        half = mt // 2

        def entry(j):
            t = (b * max_tiles + j) * 3
            return pl.multiple_of(tab_ref[t], half * PK), tab_ref[t + 1], tab_ref[t + 2]

        def up(j):
            base, k, _ = entry(j)
            xs = _bf(_unpack_rows(
                [xo[pl.ds(base + q, mt, stride=PK), :] for q in range(PK)]))
            gt = _dot(xs, wg_ref[k])
            return _bf(gt * _sigmoid(gt) * _dot(xs, wu_ref[k]))

        def down(j, hid):
            base, k, full = entry(j)
            out = _pack_rows(_dot(hid, wd_ref[k]))
            def put(lo):
                for q in range(PK):
                    xo[pl.ds(base + lo * PK + q, half, stride=PK), :] = (
                        out[lo:lo + half, q * LANES:(q + 1) * LANES])

            put(0)

            @pl.when(full > 0)
            def _():
                put(half)

        @pl.when(last > first)
        def _():
            def both(j, hid):
                nxt = up(j)
                down(j - 1, hid)
                return nxt

            down(last - 1, lax.fori_loop(first + 1, last, both, up(first)))

    @pl.when(s >= n_tiles + EXPERT_STEPS)
    def _combine():
        i = s - n_tiles - EXPERT_STEPS
        base = (b * n_tiles + i) * (TOP_K * tile)

        def group(gi, carry):
            i0 = base + gi * UNROLL
            t0 = gi * (UNROLL * PK)
            for k in range(UNROLL):
                dst = pl.ds(pl.multiple_of(t0 + k * PK, PK), PK)
                g1[dst, :] = xo[pl.ds(pl.multiple_of(pos_ref[i0 + k], PK), PK), :]
                g2[dst, :] = xo[pl.ds(pl.multiple_of(pos_ref[i0 + tile + k], PK), PK), :]
            return carry

        lax.fori_loop(0, n_groups, group, 0)
        o1 = _unpack_rows([g1[pl.ds(q, tile, stride=PK), :] for q in range(PK)])
        o2 = _unpack_rows([g2[pl.ds(q, tile, stride=PK), :] for q in range(PK)])
        rec = jnp.concatenate([rt_ref[0], jnp.zeros((LANES - REC_ROWS, tile), jnp.float32)], axis=0)
        cw = rec.T
        y_ref[...] = x1_ref[...] + cw[:, R_W1:R_W1 + 1] * o1 + cw[:, R_W2:R_W2 + 1] * o2


def _moe(pos, tab, stp, rt, h2p, x1, wg, wu, wd, tile, n_tiles, mt):
    n = x1.shape[0]
    n_blocks = n // (tile * n_tiles)
    n_steps = 2 * n_tiles + EXPERT_STEPS
    last = n_tiles - 1

    def disp(b, s, *_):
        return (b * n_tiles + jnp.minimum(s, last), 0)

    def comb(b, s, *_):
        return (b * n_tiles + jnp.clip(s - n_tiles - EXPERT_STEPS, 0, last), 0)

    def wmap(b, s, *_):
        return (jnp.clip(s - n_tiles, 0, EXPERT_STEPS - 1), 0, 0)

    grid_spec = pltpu.PrefetchScalarGridSpec(
        num_scalar_prefetch=3,
        grid=(n_blocks, n_steps),
        in_specs=[pl.BlockSpec((tile * PK, LANES), disp),
                  pl.BlockSpec((tile, D), comb),
                  pl.BlockSpec((1, REC_ROWS, tile), lambda b, s, *_: comb(b, s) + (0,)),
                  pl.BlockSpec((STEP_EXPERTS, D, D_EXPERT), wmap),
                  pl.BlockSpec((STEP_EXPERTS, D, D_EXPERT), wmap),
                  pl.BlockSpec((STEP_EXPERTS, D_EXPERT, D), wmap)],
        out_specs=pl.BlockSpec((tile, D), comb),
        scratch_shapes=[
            pltpu.VMEM((_sorted_rows(tile * n_tiles, mt) * PK, LANES), jnp.uint32),
            pltpu.VMEM((tile * PK, LANES), jnp.uint32),
            pltpu.VMEM((tile * PK, LANES), jnp.uint32),
        ])
    return pl.pallas_call(
        functools.partial(_moe_kernel, tile=tile, n_tiles=n_tiles,
                          max_tiles=_max_tiles(tile * n_tiles, mt), mt=mt),
        grid_spec=grid_spec,
        out_shape=jax.ShapeDtypeStruct((n, D), jnp.float32),
        compiler_params=pltpu.CompilerParams(
            dimension_semantics=("arbitrary", "arbitrary"), vmem_limit_bytes=VMEM_LIMIT),
        name="moe",
    )(pos, tab, stp, h2p, x1, rt, wg, wu, wd)


def _max_tiles(blk_tokens, mt):
    return N_EXPERTS + 2 * blk_tokens // mt


def _sorted_positions(rt, cnt, blk_tokens, mt):
    experts = jnp.arange(N_EXPERTS, dtype=jnp.int32)
    n_tiles, _, tile = rt.shape
    counts = cnt[:, :, 0].astype(jnp.int32)
    pad = mt // 2
    padded = (counts + pad - 1) // pad * pad
    off = jnp.cumsum(padded, axis=1) - padded
    e = rt[:, R_E1:R_E1 + TOP_K, :].astype(jnp.int32)
    rank = rt[:, R_RANK1:R_RANK1 + TOP_K, :].astype(jnp.int32)
    off_tile = jnp.repeat(off, blk_tokens // tile, axis=0)
    hit = e[:, :, None, :] == experts[:, None]
    pos = (rank + jnp.sum(jnp.where(hit, off_tile[:, None, :, None], 0), axis=2)) * PK

    tiles = (counts + mt - 1) // mt
    t_end = jnp.cumsum(tiles, axis=1)
    t_beg = t_end - tiles
    j = jnp.arange(_max_tiles(blk_tokens, mt), dtype=jnp.int32)
    e_of = jnp.minimum(jnp.sum(t_end[:, None, :] <= j[None, :, None], axis=-1), N_EXPERTS - 1)
    pick = lambda a: jnp.sum(jnp.where(e_of[:, :, None] == experts, a[:, None, :], 0), axis=-1)
    local = (j[None, :] - pick(t_beg)) * mt
    tab = jnp.stack([(pick(off) + local) * PK, e_of % STEP_EXPERTS,
                     (pick(counts) - local > pad).astype(jnp.int32)], axis=-1)
    stp = jnp.stack([t_beg[:, ::STEP_EXPERTS], t_end[:, STEP_EXPERTS - 1::STEP_EXPERTS]], axis=-1)
    return pos.reshape(-1), tab.reshape(-1), stp.reshape(-1), rt


def _segment_matrices():
    head = jnp.arange(D) // HEAD_DIM
    col = jnp.arange(LANES)
    seg = (head[:, None] == col[None, :]).astype(jnp.bfloat16)
    row = jnp.arange(2 * LANES) % LANES
    expm = (row[:, None] == head[None, :]).astype(jnp.bfloat16)
    return seg, expm


def _strict_upper(n):
    r = jnp.arange(n)
    return (r[:, None] < r[None, :]).astype(jnp.bfloat16)


def kernel(x_prompt, x_sample, cache_k, cache_v, state_conv, norm_mix_g, w_in, conv_w, q_norm_g,
           k_norm_g, attn_sinks, w_conv_out, w_attn_out, w_out, norm_ffn_g, w_group, b_group,
           w_router, b_router, w_gate, w_up, w_down):
    nb, seq, _ = x_prompt.shape
    ns, sseq, _ = x_sample.shape
    l = 0
    seg, expm = _segment_matrices()
    pad = RT_ROWS - N_EXPERTS - N_GROUPS
    wr = jnp.concatenate([w_router[l].T, w_group[l].T, jnp.zeros((pad, D), jnp.float32)], axis=0)
    br = jnp.concatenate([b_router[l], b_group[l], jnp.zeros((pad,), jnp.float32)]).reshape(RT_ROWS, 1)
    wts = (norm_mix_g[l].reshape(1, D), _bf(w_in[l]), conv_w[l],
           jnp.tile(q_norm_g[l], N_HEADS).reshape(1, D), jnp.tile(k_norm_g[l], N_KV).reshape(1, LANES),
           seg, expm, _bf(w_conv_out[l]), _bf(w_attn_out[l]), _bf(w_out[l]),
           norm_ffn_g[l].reshape(1, D), _bf(wr), br)
    sinks = attn_sinks[l]
    n_s = ns * sseq

    sink2 = sinks * LOG2E
    sink_hi = _bf(sink2).astype(jnp.float32)
    q_extra = jnp.zeros((N_HEADS, 1, HEAD_DIM), jnp.float32)
    q_extra = q_extra.at[:, 0, BIAS_LANE - HEAD_DIM].set(1.0)
    q_extra = q_extra.at[:, 0, SINK_LANE - HEAD_DIM].set(sink_hi)
    q_extra = q_extra.at[:, 0, SINK_LANE + 1 - HEAD_DIM].set(sink2 - sink_hi)
    x1p, h2pp, rtp, cntp, kwp, vwp, cstp, wg, wu, wd = _prompt_mixer(
        x_prompt, (w_gate[l], w_up[l], w_down[l]), q_extra, wts + (_strict_upper(TOK_TILE),), tile=TOK_TILE, blk_tiles=BLOCK_TILES)
    x1s, h2ps, rts, cnts, kws, vws, csts = _sample_mixer(
        x_sample, cache_k[l].reshape(ns, WINDOW, LANES), cache_v[l].reshape(ns, WINDOW, LANES),
        state_conv[l], sinks, wts + (_strict_upper(n_s),))

    yp = _moe(*_sorted_positions(rtp, cntp, TOK_TILE * BLOCK_TILES, MOE_TILE), h2pp, x1p.reshape(nb * seq, D), wg, wu, wd,
              tile=TOK_TILE, n_tiles=BLOCK_TILES, mt=MOE_TILE)
    ys = _moe(*_sorted_positions(rts, cnts, n_s, SAMPLE_MOE_TILE), h2ps, x1s, wg, wu, wd, tile=n_s, n_tiles=1,
              mt=SAMPLE_MOE_TILE)

    kv_shape = lambda n: (1, n, WINDOW, N_KV, HEAD_DIM)
    return (yp.reshape(nb, seq, D), ys.reshape(ns, sseq, D),
            kwp.reshape(kv_shape(nb)), vwp.reshape(kv_shape(nb)), cstp.reshape(1, nb, 2, D),
            kws.reshape(kv_shape(ns)), vws.reshape(kv_shape(ns)), csts.reshape(1, ns, 2, D))
```

```python
import functools
import math

import jax
import jax.numpy as jnp
import numpy as np
from jax import lax
from jax.experimental import pallas as pl
from jax.experimental.pallas import tpu as pltpu

D = 1024
CHUNK = 64
HEAD_DIM = 64
N_HEADS = 16
N_KV = 2
WINDOW = 128
N_GROUPS = 4
EPG = 8
N_EXPERTS = 32
TOP_K = 2
D_EXPERT = 256
EPS = 1e-6
NEG = -1e30

LANES = 128
QUAD = 4
QUAD_W = QUAD * HEAD_DIM
SLOTS = 4
KEYS = SLOTS * CHUNK
GROUP_HEADS = N_HEADS // N_KV
STACK = GROUP_HEADS * CHUNK
BIAS_LANE = HEAD_DIM
SINK_LANE = HEAD_DIM + 1
LOG2E = 1.0 / math.log(2.0)
RT_ROWS = LANES
REC_ROWS = 8
HALF = D // 2
PK = HALF // LANES

R_E1, R_E2, R_W1, R_W2, R_RANK1, R_RANK2 = 0, 1, 2, 3, 4, 5

C_B, C_C, C_X, C_Q, C_K, C_V, C_G = 0, 1024, 2048, 3072, 4096, 4224, 4352

MOE_TILE = 256
SAMPLE_MOE_TILE = 64
TOK_TILE = 512
BLOCK_TILES = 8
UNROLL = 16
COMBINE_PARTS = 4
STEP_EXPERTS = 4
EXPERT_STEPS = N_EXPERTS // STEP_EXPERTS

VMEM_LIMIT = 60 * 1024 * 1024

_NT = (((1,), (1,)), ((), ()))


def _bf(x):
    return x.astype(jnp.bfloat16)


def _dot(a, b):
    return jnp.dot(a, b, preferred_element_type=jnp.float32)


def _rms(x, g):
    return x * lax.rsqrt(jnp.mean(x * x, axis=-1, keepdims=True) + EPS) * g


def _sigmoid(x):
    return 1.0 / (1.0 + jnp.exp(-x))


def _pack_rows(x):
    return pltpu.pack_elementwise([x[:, :HALF], x[:, HALF:]], packed_dtype=jnp.bfloat16)


def _unpack_rows(parts):
    def half(i):
        return [pltpu.unpack_elementwise(p, index=i, packed_dtype=jnp.bfloat16,
                                         unpacked_dtype=jnp.float32) for p in parts]
    return jnp.concatenate(half(0) + half(1), axis=-1)


def _head_norm_q(q, seg_ref, exp_ref, qg):
    ssq = _dot(_bf(q * q), seg_ref[...])
    inv = lax.rsqrt(ssq * (1.0 / HEAD_DIM) + EPS)
    hi = _bf(inv)
    lo = _bf(inv - hi.astype(jnp.float32))
    full = _dot(jnp.concatenate([hi, lo], axis=-1), exp_ref[...])
    return q * full * qg


def _head_norm_k(k, kg):
    lane = lax.broadcasted_iota(jnp.int32, k.shape, 1)
    lo_half = lane < HEAD_DIM
    sq = k * k
    s0 = jnp.sum(jnp.where(lo_half, sq, 0.0), axis=-1, keepdims=True)
    s1 = jnp.sum(jnp.where(lo_half, 0.0, sq), axis=-1, keepdims=True)
    i0 = lax.rsqrt(s0 * (1.0 / HEAD_DIM) + EPS)
    i1 = lax.rsqrt(s1 * (1.0 / HEAD_DIM) + EPS)
    return k * jnp.where(lo_half, i0, i1) * kg


def _rep4(x):
    lane = lax.broadcasted_iota(jnp.int32, x.shape, 1)
    sw = pltpu.roll(x, HEAD_DIM, axis=1)
    a2 = jnp.where(lane < HEAD_DIM, x, sw)
    b2 = jnp.where(lane < HEAD_DIM, sw, x)
    return jnp.concatenate([a2, a2], axis=-1), jnp.concatenate([b2, b2], axis=-1)


def _write_bd(ref, g, row0, x4):
    rows = x4.shape[0]
    lane = lax.broadcasted_iota(jnp.int32, x4.shape, 1)
    for i in range(QUAD):
        keep = (lane >= i * HEAD_DIM) & (lane < (i + 1) * HEAD_DIM)
        ref[g, pl.ds(i * KEYS + row0, rows), :] = _bf(jnp.where(keep, x4, 0.0))


def _attend(qq, kbd, vbd, valid, sinks):
    s = lax.dot_general(qq, kbd, _NT, preferred_element_type=jnp.float32)
    s = jnp.where(valid, s, NEG)
    ps = []
    for i in range(QUAD):
        sh = s[:, i * KEYS:(i + 1) * KEYS]
        m = jnp.maximum(jnp.max(sh, axis=-1, keepdims=True), sinks[i])
        p = jnp.exp(sh - m)
        den = jnp.sum(p, axis=-1, keepdims=True) + jnp.exp(sinks[i] - m)
        ps.append(_bf(p / den))
    return _dot(jnp.concatenate(ps, axis=-1), vbd)


def _route(lt, upper_ref, carry):
    t = lt.shape[1]
    big = float(RT_ROWS)
    grow = lax.broadcasted_iota(jnp.int32, (REC_ROWS, t), 0).astype(jnp.float32)
    is_g = grow < N_GROUPS
    gl = jnp.where(is_g, lt[N_EXPERTS:N_EXPERTS + REC_ROWS], NEG)
    gmax = jnp.max(gl, axis=0, keepdims=True)
    gsum = jnp.sum(jnp.where(is_g, jnp.exp(gl - gmax), 0.0), axis=0, keepdims=True)
    g_w = 1.0 / gsum
    g_sel = jnp.min(jnp.where(is_g & (gl == gmax), grow, big), axis=0, keepdims=True)
    logits = lt[0:N_EXPERTS]
    erow = lax.broadcasted_iota(jnp.int32, (N_EXPERTS, t), 0).astype(jnp.float32)
    lo = g_sel * EPG
    in_grp = (erow >= lo) & (erow < lo + EPG)
    el = jnp.where(in_grp, logits, NEG)
    v1 = jnp.max(el, axis=0, keepdims=True)
    i1 = jnp.min(jnp.where(in_grp & (el == v1), erow, big), axis=0, keepdims=True)
    rest = in_grp & (erow != i1)
    el2 = jnp.where(rest, logits, NEG)
    v2 = jnp.max(el2, axis=0, keepdims=True)
    i2 = jnp.min(jnp.where(rest & (el2 == v2), erow, big), axis=0, keepdims=True)
    e2 = jnp.exp(v2 - v1)
    den = 1.0 + e2
    w1 = (1.0 / den) * g_w
    w2 = (e2 / den) * g_w
    oh1 = jnp.where(erow == i1, 1.0, 0.0)
    oh2 = jnp.where(erow == i2, 1.0, 0.0)
    both = oh1 + oh2
    before = _dot(_bf(both), upper_ref[...]) + carry
    r1 = jnp.sum(before * oh1, axis=0, keepdims=True)
    r2 = jnp.sum(before * oh2, axis=0, keepdims=True)
    rec = jnp.zeros((REC_ROWS, t), jnp.float32)
    for rw, val in ((R_E1, i1), (R_E2, i2), (R_W1, w1), (R_W2, w2), (R_RANK1, r1), (R_RANK2, r2)):
        rec = jnp.where(grow == rw, val, rec)
    return rec, carry + jnp.sum(both, axis=1, keepdims=True)


def _route_stage(x1, g2_ref, wr_ref, br_ref, upper_ref, carry, h2p_ref, rt_ref):
    rows = x1.shape[0]
    h2 = _rms(x1, g2_ref[...])
    pk = _pack_rows(h2)
    for j in range(PK):
        h2p_ref[pl.ds(j, rows, stride=PK), :] = pk[:, j * LANES:(j + 1) * LANES]
    lt = lax.dot_general(wr_ref[...], _bf(h2), _NT, preferred_element_type=jnp.float32) + br_ref[...]
    rec, carry = _route(lt, upper_ref, carry)
    rt_ref[...] = rec
    return carry


def _merge_tail(x, macc, wo_ref, g2_ref, wr_ref, br_ref, upper_ref, carry, x1_ref, h2p_ref, rt_ref):
    x1 = x + _dot(_bf(macc), wo_ref[...])
    x1_ref[...] = x1
    return _route_stage(x1, g2_ref, wr_ref, br_ref, upper_ref, carry, h2p_ref, rt_ref)


def _prompt_mixer_kernel(x_ref, wg32_ref, wu32_ref, wd32_ref,
                         qx_ref, g1_ref, win_ref, cw_ref, qg_ref, kg_ref, seg_ref, exp_ref,
                         wco_ref, wao_ref, wo_ref, g2_ref, wr_ref, br_ref, upper_ref,
                         x1_ref, h2p_ref, rt_ref, cnt_ref, kwin_ref, vwin_ref, cst_ref,
                         wg16_ref, wu16_ref, wd16_ref,
                         uext, qs_s, att_s, kn_s, kop, vop, khist, vhist, carry_s, x1_s,
                         *, tile, blk_tiles, n_it, n_main):
    step = pl.program_id(0)
    it = step % n_it
    n_chunk = tile // CHUNK

    @pl.when(step == 0)
    def _():
        shape = (n_chunk, N_KV, CHUNK, LANES)
        row = lax.broadcasted_iota(jnp.int32, shape, 2)
        ln = lax.broadcasted_iota(jnp.int32, shape, 3)
        is_sink = (row == 0) & ((ln == SINK_LANE) | (ln == SINK_LANE + 1))
        tail = pl.ds((SLOTS - 1) * CHUNK, CHUNK)
        kop[:, :, tail, :] = _bf(jnp.where(is_sink, 1.0, jnp.where((row > 0) & (ln == BIAS_LANE), NEG, 0.0)))
        vop[:, :, tail, 0:LANES] = jnp.zeros(shape, jnp.bfloat16)
        vop[:, :, tail, LANES:2 * LANES] = jnp.ones(shape, jnp.bfloat16)
        x1_s[1] = jnp.zeros((tile, D), jnp.float32)
        carry_s[...] = jnp.zeros_like(carry_s)

    def route_previous():
        first_of_block = (step + blk_tiles - 1) % blk_tiles == 0
        carry = jnp.where(first_of_block, 0.0, carry_s[...])
        carry = _route_stage(x1_s[(step + 1) % 2], g2_ref, wr_ref, br_ref, upper_ref, carry,
                             h2p_ref, rt_ref.at[0])
        carry_s[...] = carry
        cnt_ref[0] = jnp.broadcast_to(carry, (N_EXPERTS, LANES))

    @pl.when(step == n_main)
    def _():
        route_previous()

    @pl.when((step < n_main) & (it == 0))
    def _():
        uext[0:8, :] = jnp.zeros((8, D), jnp.float32)
        ln = lax.broadcasted_iota(jnp.int32, khist.shape, 2)
        khist[...] = _bf(jnp.where(ln == BIAS_LANE, NEG, 0.0))
        vhist[...] = jnp.zeros_like(vhist)

    @pl.when(step < n_main)
    def _():
        route_previous()
        _mixer_tile(x_ref, wg32_ref, wu32_ref, wd32_ref, qx_ref, g1_ref, win_ref, cw_ref, qg_ref,
                    kg_ref, seg_ref, exp_ref, wco_ref, wao_ref, wo_ref, x1_ref, kwin_ref, vwin_ref,
                    cst_ref, wg16_ref, wu16_ref, wd16_ref, uext, qs_s, att_s, kn_s, kop, vop, khist,
                    vhist, x1_s.at[step % 2], tile=tile)


def _mixer_tile(x_ref, wg32_ref, wu32_ref, wd32_ref, qx_ref, g1_ref, win_ref, cw_ref, qg_ref,
                kg_ref, seg_ref, exp_ref, wco_ref, wao_ref, wo_ref, x1_ref, kwin_ref, vwin_ref,
                cst_ref, wg16_ref, wu16_ref, wd16_ref, uext, qs_s, att_s, kn_s, kop, vop, khist,
                vhist, x1_keep, *, tile):
    n_chunk = tile // CHUNK
    wg16_ref[...] = _bf(wg32_ref[...])
    wu16_ref[...] = _bf(wu32_ref[...])
    wd16_ref[...] = _bf(wd32_ref[...])

    x = x_ref[0]
    h = _bf(_rms(x, g1_ref[...]))

    u = _dot(h, win_ref[:, C_C:C_C + D]) * _dot(h, win_ref[:, C_X:C_X + D])
    uext[8:8 + tile, :] = u
    conv = (uext[6:6 + tile, :] * cw_ref[0:1, :] + uext[7:7 + tile, :] * cw_ref[1:2, :]
            + u * cw_ref[2:3, :])
    yc = _bf(_dot(h, win_ref[:, C_B:C_B + D]) * conv)
    o_c = _dot(yc, wco_ref[...])
    macc = _sigmoid(_dot(h, win_ref[:, C_G:C_G + D])) * o_c
    last2 = uext[tile + 6:tile + 8, :]
    uext[6:8, :] = last2

    cst_ref[0] = last2

    q = _dot(h, win_ref[:, C_Q:C_Q + D])
    qn = _head_norm_q(q, seg_ref, exp_ref, qg_ref[...]) * (LOG2E / math.sqrt(HEAD_DIM))
    for hd in range(N_HEADS):
        g, r = divmod(hd, GROUP_HEADS)
        ext = jnp.broadcast_to(qx_ref[hd], (tile, HEAD_DIM))
        piece = _bf(jnp.concatenate([qn[:, hd * HEAD_DIM:(hd + 1) * HEAD_DIM], ext], axis=-1))
        for c in range(n_chunk):
            qs_s[g, c * STACK + r * CHUNK:c * STACK + (r + 1) * CHUNK, :] = (
                piece[c * CHUNK:(c + 1) * CHUNK, :])
    kv = _dot(h, win_ref[:, C_K:C_K + 2 * LANES])
    kn = _head_norm_k(kv[:, 0:LANES], kg_ref[...])
    vv = kv[:, LANES:2 * LANES]
    kn_s[0] = kn
    kn_s[1] = vv

    kwin_ref[0] = kn[tile - WINDOW:, :]
    vwin_ref[0] = vv[tile - WINDOW:, :]

    lane = lax.broadcasted_iota(jnp.int32, (CHUNK, LANES), 1)
    lo_half = lane < HEAD_DIM
    ones_blk = jnp.ones((CHUNK, LANES), jnp.bfloat16)
    own = SLOTS - 2

    for c in range(own):
        for sl in range(own - c):
            src = pl.ds((c + sl) * CHUNK, CHUNK)
            kop[c, :, pl.ds(sl * CHUNK, CHUNK), :] = khist[:, src, :]
            vop[c, :, pl.ds(sl * CHUNK, CHUNK), :] = vhist[:, src, :]

    def chunk_body(c):
        r0 = c * CHUNK
        kc = kn_s[0, pl.ds(r0, CHUNK), :]
        vc = kn_s[1, pl.ds(r0, CHUNK), :]
        ksw = pltpu.roll(kc, HEAD_DIM, axis=1)
        vsw = pltpu.roll(vc, HEAD_DIM, axis=1)
        for g in range(N_KV):
            kb = _bf(jnp.where(lo_half, kc if g == 0 else ksw, 0.0))
            vb = _bf(jnp.where(lo_half, vc, vsw) if g == 0 else jnp.where(lo_half, vsw, vc))
            for d in range(own + 1):
                if c + d < n_chunk:
                    dst = pl.ds((own - d) * CHUNK, CHUNK)
                    kop[c + d, g, dst, :] = kb
                    vop[c + d, g, dst, 0:LANES] = vb
                    vop[c + d, g, dst, LANES:2 * LANES] = ones_blk
            if c >= n_chunk - own:
                dst = pl.ds((c - (n_chunk - own)) * CHUNK, CHUNK)
                khist[g, dst, :] = kb
                vhist[g, dst, 0:LANES] = vb
                vhist[g, dst, LANES:2 * LANES] = ones_blk
        for g in range(N_KV):
            s = lax.dot_general(qs_s[g, pl.ds(c * STACK, STACK), :], kop[c, g], _NT,
                                preferred_element_type=jnp.float32)
            m = jnp.max(s, axis=-1, keepdims=True)
            o = _dot(_bf(jnp.exp2(s - m)), vop[c, g])
            out = o[:, 0:LANES] / o[:, LANES:2 * LANES]
            for r in range(0, GROUP_HEADS, 2):
                pair = jnp.where(lo_half, out[r * CHUNK:(r + 1) * CHUNK, :],
                                 out[(r + 1) * CHUNK:(r + 2) * CHUNK, :])
                col = (g * GROUP_HEADS + r) * HEAD_DIM
                att_s[pl.ds(r0, CHUNK), col:col + LANES] = _bf(pair)

    for c in range(n_chunk):
        chunk_body(c)

    o_a = _dot(att_s[...], wao_ref[...])
    macc = macc + _sigmoid(_dot(h, win_ref[:, C_G + D:C_G + 2 * D])) * o_a
    x1 = x + _dot(_bf(macc), wo_ref[...])
    x1_ref[0] = x1
    x1_keep[...] = x1


def _const_spec(shape):
    nd = len(shape)
    return pl.BlockSpec(shape, lambda *_: (0,) * nd, pipeline_mode=pl.Buffered(1))


def _prompt_mixer(x, experts32, q_extra, wts, tile, blk_tiles):
    nb, seq, _ = x.shape
    n_it = seq // tile
    n_main = nb * n_it
    bps = n_it // blk_tiles
    parts = n_main // N_EXPERTS
    consts = (q_extra,) + tuple(wts)

    cur = lambda s: jnp.minimum(s, n_main - 1)
    prev = lambda s: jnp.maximum(s - 1, 0)
    row_spec = lambda w: pl.BlockSpec((1, tile, w), lambda s: (cur(s) // n_it, cur(s) % n_it, 0))
    str_spec = lambda r, w: pl.BlockSpec((1, r, w), lambda s: (cur(s) // n_it, 0, 0))
    part_spec = lambda r, w: pl.BlockSpec(
        (1, r // parts, w), lambda s: (cur(s) // parts, cur(s) % parts, 0))
    expert_specs = [part_spec(D, D_EXPERT), part_spec(D, D_EXPERT), part_spec(D_EXPERT, D)]
    out_shape = [
        jax.ShapeDtypeStruct((nb, seq, D), jnp.float32),
        jax.ShapeDtypeStruct((nb * seq * PK, LANES), jnp.uint32),
        jax.ShapeDtypeStruct((n_main, REC_ROWS, tile), jnp.float32),
        jax.ShapeDtypeStruct((nb * bps, N_EXPERTS, LANES), jnp.float32),
        jax.ShapeDtypeStruct((nb, WINDOW, LANES), jnp.float32),
        jax.ShapeDtypeStruct((nb, WINDOW, LANES), jnp.float32),
        jax.ShapeDtypeStruct((nb, 2, D), jnp.float32),
    ] + [jax.ShapeDtypeStruct(w.shape, jnp.bfloat16) for w in experts32]
    return pl.pallas_call(
        functools.partial(_prompt_mixer_kernel, tile=tile, blk_tiles=blk_tiles, n_it=n_it, n_main=n_main),
        grid=(n_main + 1,),
        in_specs=[row_spec(D)] + expert_specs + [_const_spec(w.shape) for w in consts],
        out_specs=[row_spec(D),
                   pl.BlockSpec((tile * PK, LANES), lambda s: (prev(s), 0)),
                   pl.BlockSpec((1, REC_ROWS, tile), lambda s: (prev(s), 0, 0)),
                   pl.BlockSpec((1, N_EXPERTS, LANES), lambda s: (prev(s) // blk_tiles, 0, 0)),
                   str_spec(WINDOW, LANES), str_spec(WINDOW, LANES), str_spec(2, D)] + expert_specs,
        out_shape=out_shape,
        scratch_shapes=[
            pltpu.VMEM((tile + 8, D), jnp.float32),
            pltpu.VMEM((N_KV, tile * GROUP_HEADS, LANES), jnp.bfloat16),
            pltpu.VMEM((tile, D), jnp.bfloat16),
            pltpu.VMEM((2, tile, LANES), jnp.float32),
            pltpu.VMEM((tile // CHUNK, N_KV, KEYS, LANES), jnp.bfloat16),
            pltpu.VMEM((tile // CHUNK, N_KV, KEYS, 2 * LANES), jnp.bfloat16),
            pltpu.VMEM((N_KV, (SLOTS - 2) * CHUNK, LANES), jnp.bfloat16),
            pltpu.VMEM((N_KV, (SLOTS - 2) * CHUNK, 2 * LANES), jnp.bfloat16),
            pltpu.VMEM((N_EXPERTS, 1), jnp.float32),
            pltpu.VMEM((2, tile, D), jnp.float32),
        ],
        compiler_params=pltpu.CompilerParams(
            dimension_semantics=("arbitrary",), vmem_limit_bytes=VMEM_LIMIT),
        name="prompt_mixer",
    )(x, *experts32, *consts)


def _sample_mixer_kernel(sink_ref, x_ref, ck_ref, cv_ref, sc_ref, g1_ref, win_ref, cw_ref, qg_ref,
                         kg_ref, seg_ref, exp_ref, wco_ref, wao_ref, wo_ref, g2_ref, wr_ref, br_ref,
                         upper_ref, x1_ref, h2p_ref, rt_ref, cnt_ref, kwin_ref, vwin_ref, cst_ref,
                         uext, att_s, kbd, vbd, *, n_stream, seq):
    ext = seq + 8
    x = x_ref[...]
    h = _bf(_rms(x, g1_ref[...]))

    u = _dot(h, win_ref[:, C_C:C_C + D]) * _dot(h, win_ref[:, C_X:C_X + D])
    for s in range(n_stream):
        uext[s * ext + 6:s * ext + 8, :] = sc_ref[s]
        uext[s * ext + 8:(s + 1) * ext, :] = u[s * seq:(s + 1) * seq, :]
        cst_ref[s] = u[(s + 1) * seq - 2:(s + 1) * seq, :]
    conv = jnp.concatenate(
        [uext[s * ext + 6:s * ext + 6 + seq, :] * cw_ref[0:1, :]
         + uext[s * ext + 7:s * ext + 7 + seq, :] * cw_ref[1:2, :] for s in range(n_stream)],
        axis=0) + u * cw_ref[2:3, :]
    yc = _bf(_dot(h, win_ref[:, C_B:C_B + D]) * conv)
    o_c = _dot(yc, wco_ref[...])
    macc = _sigmoid(_dot(h, win_ref[:, C_G:C_G + D])) * o_c

    q = _dot(h, win_ref[:, C_Q:C_Q + D])
    qn = _bf(_head_norm_q(q, seg_ref, exp_ref, qg_ref[...]) * (1.0 / math.sqrt(HEAD_DIM)))
    kv = _dot(h, win_ref[:, C_K:C_K + 2 * LANES])
    kn = _head_norm_k(kv[:, 0:LANES], kg_ref[...])
    vv = kv[:, LANES:2 * LANES]

    kbd[...] = jnp.zeros_like(kbd)
    vbd[...] = jnp.zeros_like(vbd)
    lane = lax.broadcasted_iota(jnp.int32, (1, QUAD * KEYS), 1)
    valid = (lane % KEYS) < WINDOW + seq
    for s in range(n_stream):
        rows = slice(s * seq, (s + 1) * seq)
        kwin_ref[s, 0:WINDOW - seq, :] = ck_ref[s, seq:WINDOW, :]
        kwin_ref[s, WINDOW - seq:WINDOW, :] = kn[rows, :]
        vwin_ref[s, 0:WINDOW - seq, :] = cv_ref[s, seq:WINDOW, :]
        vwin_ref[s, WINDOW - seq:WINDOW, :] = vv[rows, :]
        kc4 = _rep4(ck_ref[s])
        vc4 = _rep4(cv_ref[s])
        kn4 = _rep4(kn[rows, :])
        vn4 = _rep4(vv[rows, :])
        for g in range(N_KV):
            _write_bd(kbd, g, 0, kc4[g])
            _write_bd(kbd, g, WINDOW, kn4[g])
            _write_bd(vbd, g, 0, vc4[g])
            _write_bd(vbd, g, WINDOW, vn4[g])
        for g in range(N_KV):
            kb = kbd[g]
            vb = vbd[g]
            for j in range(N_HEADS // N_KV // QUAD):
                h0 = g * (N_HEADS // N_KV) + j * QUAD
                qq = qn[rows, h0 * HEAD_DIM:(h0 + QUAD) * HEAD_DIM]
                sinks = [sink_ref[h0 + i] for i in range(QUAD)]
                o = _attend(qq, kb, vb, valid, sinks)
                att_s[rows, h0 * HEAD_DIM:(h0 + QUAD) * HEAD_DIM] = _bf(o)

    o_a = _dot(att_s[...], wao_ref[...])
    macc = macc + _sigmoid(_dot(h, win_ref[:, C_G + D:C_G + 2 * D])) * o_a
    carry = _merge_tail(x, macc, wo_ref, g2_ref, wr_ref, br_ref, upper_ref,
                        jnp.zeros((N_EXPERTS, 1), jnp.float32), x1_ref, h2p_ref, rt_ref.at[0])
    cnt_ref[0] = jnp.broadcast_to(carry, (N_EXPERTS, LANES))


def _sample_mixer(x, ck, cv, sc, sinks, wts):
    nb, seq, _ = x.shape
    rows = nb * seq
    x2 = x.reshape(rows, D)
    ins = (x2, ck, cv, sc) + tuple(wts)
    full = lambda a: pl.BlockSpec(a.shape, lambda *_, nd=a.ndim: (0,) * nd)
    out_shape = [
        jax.ShapeDtypeStruct((rows, D), jnp.float32),
        jax.ShapeDtypeStruct((rows * PK, LANES), jnp.uint32),
        jax.ShapeDtypeStruct((1, REC_ROWS, rows), jnp.float32),
        jax.ShapeDtypeStruct((1, N_EXPERTS, LANES), jnp.float32),
        jax.ShapeDtypeStruct((nb, WINDOW, LANES), jnp.float32),
        jax.ShapeDtypeStruct((nb, WINDOW, LANES), jnp.float32),
        jax.ShapeDtypeStruct((nb, 2, D), jnp.float32),
    ]
    grid_spec = pltpu.PrefetchScalarGridSpec(
        num_scalar_prefetch=1,
        grid=(1,),
        in_specs=[full(a) for a in ins],
        out_specs=[full(a) for a in out_shape],
        scratch_shapes=[
            pltpu.VMEM((nb * (seq + 8), D), jnp.float32),
            pltpu.VMEM((rows, D), jnp.bfloat16),
            pltpu.VMEM((N_KV, QUAD * KEYS, QUAD_W), jnp.bfloat16),
            pltpu.VMEM((N_KV, QUAD * KEYS, QUAD_W), jnp.bfloat16),
        ])
    return pl.pallas_call(
        functools.partial(_sample_mixer_kernel, n_stream=nb, seq=seq),
        grid_spec=grid_spec,
        out_shape=out_shape,
        compiler_params=pltpu.CompilerParams(
            dimension_semantics=("arbitrary",), vmem_limit_bytes=VMEM_LIMIT),
        name="sample_mixer",
    )(sinks, *ins)


def _sorted_rows(blk_tokens, mt):
    rows = 2 * blk_tokens + N_EXPERTS * (mt // 2 - 1) + mt
    return -(-rows // mt) * mt


def _moe_kernel(pos_ref, tab_ref, stp_ref, h2p_ref, x1_ref, rt_ref, wg_ref, wu_ref, wd_ref, y_ref,
                xo, g1, g2, *, tile, n_tiles, max_tiles, mt):
    b = pl.program_id(0)
    s = pl.program_id(1)
    n_groups = tile // UNROLL

    @pl.when((b == 0) & (s == 0))
    def _():
        xo[...] = jnp.zeros_like(xo)

    @pl.when(s < n_tiles)
    def _dispatch():
        base = (b * n_tiles + s) * (TOP_K * tile)

        def group(gi, carry):
            i0 = base + gi * UNROLL
            t0 = gi * (UNROLL * PK)
            for k in range(UNROLL):
                row = h2p_ref[pl.ds(pl.multiple_of(t0 + k * PK, PK), PK), :]
                xo[pl.ds(pl.multiple_of(pos_ref[i0 + k], PK), PK), :] = row
                xo[pl.ds(pl.multiple_of(pos_ref[i0 + tile + k], PK), PK), :] = row
            return carry

        lax.fori_loop(0, n_groups, group, 0)

    @pl.when((s >= n_tiles) & (s < n_tiles + EXPERT_STEPS))
    def _experts():
        st = b * EXPERT_STEPS + s - n_tiles
        first = stp_ref[2 * st]
        last = stp_ref[2 * st + 1]
        half = mt // 2

        def entry(j):
            t = (b * max_tiles + j) * 3
            return pl.multiple_of(tab_ref[t], half * PK), tab_ref[t + 1], tab_ref[t + 2]

        def up(j):
            base, k, _ = entry(j)
            xs = _bf(_unpack_rows(
                [xo[pl.ds(base + q, mt, stride=PK), :] for q in range(PK)]))
            gt = _dot(xs, wg_ref[k])
            return _bf(gt * _sigmoid(gt) * _dot(xs, wu_ref[k]))

        def down(j, hid):
            base, k, full = entry(j)
            out = _pack_rows(_dot(hid, wd_ref[k]))
            def put(lo):
                for q in range(PK):
                    xo[pl.ds(base + lo * PK + q, half, stride=PK), :] = (
                        out[lo:lo + half, q * LANES:(q + 1) * LANES])

            put(0)

            @pl.when(full > 0)
            def _():
                put(half)

        @pl.when(last > first)
        def _():
            def both(j, hid):
                nxt = up(j)
                down(j - 1, hid)
                return nxt

            down(last - 1, lax.fori_loop(first + 1, last, both, up(first)))

    @pl.when(s >= n_tiles + EXPERT_STEPS)
    def _combine():
        i = s - n_tiles - EXPERT_STEPS
        base = (b * n_tiles + i) * (TOP_K * tile)
        rec = jnp.concatenate([rt_ref[0], jnp.zeros((LANES - REC_ROWS, tile), jnp.float32)], axis=0)
        cw = rec.T
        part = tile // COMBINE_PARTS
        for p in range(COMBINE_PARTS):
            for k in range(part):
                t = p * part + k
                dst = pl.ds(t * PK, PK)
                g1[dst, :] = xo[pl.ds(pl.multiple_of(pos_ref[base + t], PK), PK), :]
                g2[dst, :] = xo[pl.ds(pl.multiple_of(pos_ref[base + tile + t], PK), PK), :]
            rows = pl.ds(p * part, part)
            o1 = _unpack_rows([g1[pl.ds(p * part * PK + q, part, stride=PK), :] for q in range(PK)])
            o2 = _unpack_rows([g2[pl.ds(p * part * PK + q, part, stride=PK), :] for q in range(PK)])
            w = cw[p * part:(p + 1) * part]
            y_ref[rows, :] = x1_ref[rows, :] + w[:, R_W1:R_W1 + 1] * o1 + w[:, R_W2:R_W2 + 1] * o2


def _moe(pos, tab, stp, rt, h2p, x1, wg, wu, wd, tile, n_tiles, mt):
    n = x1.shape[0]
    n_blocks = n // (tile * n_tiles)
    n_steps = 2 * n_tiles + EXPERT_STEPS
    last = n_tiles - 1

    def disp(b, s, *_):
        return (b * n_tiles + jnp.minimum(s, last), 0)

    def comb(b, s, *_):
        return (b * n_tiles + jnp.clip(s - n_tiles - EXPERT_STEPS, 0, last), 0)

    def wmap(b, s, *_):
        return (jnp.clip(s - n_tiles, 0, EXPERT_STEPS - 1), 0, 0)

    grid_spec = pltpu.PrefetchScalarGridSpec(
        num_scalar_prefetch=3,
        grid=(n_blocks, n_steps),
        in_specs=[pl.BlockSpec((tile * PK, LANES), disp),
                  pl.BlockSpec((tile, D), comb),
                  pl.BlockSpec((1, REC_ROWS, tile), lambda b, s, *_: comb(b, s) + (0,)),
                  pl.BlockSpec((STEP_EXPERTS, D, D_EXPERT), wmap),
                  pl.BlockSpec((STEP_EXPERTS, D, D_EXPERT), wmap),
                  pl.BlockSpec((STEP_EXPERTS, D_EXPERT, D), wmap)],
        out_specs=pl.BlockSpec((tile, D), comb),
        scratch_shapes=[
            pltpu.VMEM((_sorted_rows(tile * n_tiles, mt) * PK, LANES), jnp.uint32),
            pltpu.VMEM((tile * PK, LANES), jnp.uint32),
            pltpu.VMEM((tile * PK, LANES), jnp.uint32),
        ])
    return pl.pallas_call(
        functools.partial(_moe_kernel, tile=tile, n_tiles=n_tiles,
                          max_tiles=_max_tiles(tile * n_tiles, mt), mt=mt),
        grid_spec=grid_spec,
        out_shape=jax.ShapeDtypeStruct((n, D), jnp.float32),
        compiler_params=pltpu.CompilerParams(
            dimension_semantics=("arbitrary", "arbitrary"), vmem_limit_bytes=VMEM_LIMIT),
        name="moe",
    )(pos, tab, stp, h2p, x1, rt, wg, wu, wd)


def _max_tiles(blk_tokens, mt):
    return N_EXPERTS + 2 * blk_tokens // mt


def _sorted_positions(rt, cnt, blk_tokens, mt):
    experts = jnp.arange(N_EXPERTS, dtype=jnp.int32)
    n_tiles, _, tile = rt.shape
    counts = cnt[:, :, 0].astype(jnp.int32)
    pad = mt // 2
    padded = (counts + pad - 1) // pad * pad
    off = jnp.cumsum(padded, axis=1) - padded
    e = rt[:, R_E1:R_E1 + TOP_K, :].astype(jnp.int32)
    rank = rt[:, R_RANK1:R_RANK1 + TOP_K, :].astype(jnp.int32)
    off_tile = jnp.repeat(off, blk_tokens // tile, axis=0)
    hit = e[:, :, None, :] == experts[:, None]
    pos = (rank + jnp.sum(jnp.where(hit, off_tile[:, None, :, None], 0), axis=2)) * PK

    tiles = (counts + mt - 1) // mt
    t_end = jnp.cumsum(tiles, axis=1)
    t_beg = t_end - tiles
    j = jnp.arange(_max_tiles(blk_tokens, mt), dtype=jnp.int32)
    e_of = jnp.minimum(jnp.sum(t_end[:, None, :] <= j[None, :, None], axis=-1), N_EXPERTS - 1)
    pick = lambda a: jnp.sum(jnp.where(e_of[:, :, None] == experts, a[:, None, :], 0), axis=-1)
    local = (j[None, :] - pick(t_beg)) * mt
    tab = jnp.stack([(pick(off) + local) * PK, e_of % STEP_EXPERTS,
                     (pick(counts) - local > pad).astype(jnp.int32)], axis=-1)
    stp = jnp.stack([t_beg[:, ::STEP_EXPERTS], t_end[:, STEP_EXPERTS - 1::STEP_EXPERTS]], axis=-1)
    return pos.reshape(-1), tab.reshape(-1), stp.reshape(-1), rt


def _segment_matrices():
    head = np.arange(D) // HEAD_DIM
    seg = head[:, None] == np.arange(LANES)[None, :]
    expm = (np.arange(2 * LANES) % LANES)[:, None] == head[None, :]
    return jnp.asarray(seg, jnp.bfloat16), jnp.asarray(expm, jnp.bfloat16)


def _strict_upper(n):
    r = np.arange(n)
    return jnp.asarray(r[:, None] < r[None, :], jnp.bfloat16)


def kernel(x_prompt, x_sample, cache_k, cache_v, state_conv, norm_mix_g, w_in, conv_w, q_norm_g,
           k_norm_g, attn_sinks, w_conv_out, w_attn_out, w_out, norm_ffn_g, w_group, b_group,
           w_router, b_router, w_gate, w_up, w_down):
    nb, seq, _ = x_prompt.shape
    ns, sseq, _ = x_sample.shape
    l = 0
    seg, expm = _segment_matrices()
    pad = RT_ROWS - N_EXPERTS - N_GROUPS
    wr = jnp.concatenate([w_router[l].T, w_group[l].T, jnp.zeros((pad, D), jnp.float32)], axis=0)
    br = jnp.concatenate([b_router[l], b_group[l], jnp.zeros((pad,), jnp.float32)]).reshape(RT_ROWS, 1)
    wts = (norm_mix_g[l].reshape(1, D), _bf(w_in[l]), conv_w[l],
           jnp.tile(q_norm_g[l], N_HEADS).reshape(1, D), jnp.tile(k_norm_g[l], N_KV).reshape(1, LANES),
           seg, expm, _bf(w_conv_out[l]), _bf(w_attn_out[l]), _bf(w_out[l]),
           norm_ffn_g[l].reshape(1, D), _bf(wr), br)
    sinks = attn_sinks[l]
    n_s = ns * sseq

    sink2 = sinks * LOG2E
    sink_hi = _bf(sink2).astype(jnp.float32)
    q_extra = jnp.concatenate(
        [jnp.ones((N_HEADS, 1), jnp.float32), sink_hi[:, None], (sink2 - sink_hi)[:, None],
         jnp.zeros((N_HEADS, HEAD_DIM - 3), jnp.float32)], axis=1).reshape(N_HEADS, 1, HEAD_DIM)
    x1p, h2pp, rtp, cntp, kwp, vwp, cstp, wg, wu, wd = _prompt_mixer(
        x_prompt, (w_gate[l], w_up[l], w_down[l]), q_extra, wts + (_strict_upper(TOK_TILE),), tile=TOK_TILE, blk_tiles=BLOCK_TILES)
    x1s, h2ps, rts, cnts, kws, vws, csts = _sample_mixer(
        x_sample, cache_k[l].reshape(ns, WINDOW, LANES), cache_v[l].reshape(ns, WINDOW, LANES),
        state_conv[l], sinks, wts + (_strict_upper(n_s),))

    yp = _moe(*_sorted_positions(rtp, cntp, TOK_TILE * BLOCK_TILES, MOE_TILE), h2pp, x1p.reshape(nb * seq, D), wg, wu, wd,
              tile=TOK_TILE, n_tiles=BLOCK_TILES, mt=MOE_TILE)
    ys = _moe(*_sorted_positions(rts, cnts, n_s, SAMPLE_MOE_TILE), h2ps, x1s, wg, wu, wd, tile=n_s, n_tiles=1,
              mt=SAMPLE_MOE_TILE)

    kv_shape = lambda n: (1, n, WINDOW, N_KV, HEAD_DIM)
    return (yp.reshape(nb, seq, D), ys.reshape(ns, sseq, D),
            kwp.reshape(kv_shape(nb)), vwp.reshape(kv_shape(nb)), cstp.reshape(1, nb, 2, D),
            kws.reshape(kv_shape(ns)), vws.reshape(kv_shape(ns)), csts.reshape(1, ns, 2, D))
```

```python
import functools
import math

import jax
import jax.numpy as jnp
import numpy as np
from jax import lax
from jax.experimental import pallas as pl
from jax.experimental.pallas import tpu as pltpu

D = 1024
CHUNK = 64
HEAD_DIM = 64
N_HEADS = 16
N_KV = 2
WINDOW = 128
N_GROUPS = 4
EPG = 8
N_EXPERTS = 32
TOP_K = 2
D_EXPERT = 256
EPS = 1e-6
NEG = -1e30

LANES = 128
QUAD = 4
QUAD_W = QUAD * HEAD_DIM
SLOTS = 4
KEYS = SLOTS * CHUNK
GROUP_HEADS = N_HEADS // N_KV
STACK = GROUP_HEADS * CHUNK
BIAS_LANE = HEAD_DIM
SINK_LANE = HEAD_DIM + 1
LOG2E = 1.0 / math.log(2.0)
RT_ROWS = LANES
REC_ROWS = 8
HALF = D // 2
PK = HALF // LANES

R_E1, R_E2, R_W1, R_W2, R_RANK1, R_RANK2 = 0, 1, 2, 3, 4, 5

C_B, C_C, C_X, C_Q, C_K, C_V, C_G = 0, 1024, 2048, 3072, 4096, 4224, 4352

MOE_TILE = 288
SAMPLE_MOE_TILE = 64
TOK_TILE = 512
BLOCK_TILES = 8
UNROLL = 16
COMBINE_PARTS = 4
STEP_EXPERTS = 4
EXPERT_STEPS = N_EXPERTS // STEP_EXPERTS

VMEM_LIMIT = 60 * 1024 * 1024

_NT = (((1,), (1,)), ((), ()))


def _bf(x):
    return x.astype(jnp.bfloat16)


def _dot(a, b):
    return jnp.dot(a, b, preferred_element_type=jnp.float32)


def _rms(x, g):
    return x * lax.rsqrt(jnp.mean(x * x, axis=-1, keepdims=True) + EPS) * g


def _sigmoid(x):
    return 1.0 / (1.0 + jnp.exp(-x))


def _pack_rows(x):
    return pltpu.pack_elementwise([x[:, :HALF], x[:, HALF:]], packed_dtype=jnp.bfloat16)


def _unpack_rows(parts):
    def half(i):
        return [pltpu.unpack_elementwise(p, index=i, packed_dtype=jnp.bfloat16,
                                         unpacked_dtype=jnp.float32) for p in parts]
    return jnp.concatenate(half(0) + half(1), axis=-1)


def _head_norm_q(q, seg_ref, exp_ref, qg):
    ssq = _dot(_bf(q * q), seg_ref[...])
    inv = lax.rsqrt(ssq * (1.0 / HEAD_DIM) + EPS)
    hi = _bf(inv)
    lo = _bf(inv - hi.astype(jnp.float32))
    full = _dot(jnp.concatenate([hi, lo], axis=-1), exp_ref[...])
    return q * full * qg


def _head_norm_k(k, kg):
    lane = lax.broadcasted_iota(jnp.int32, k.shape, 1)
    lo_half = lane < HEAD_DIM
    sq = k * k
    s0 = jnp.sum(jnp.where(lo_half, sq, 0.0), axis=-1, keepdims=True)
    s1 = jnp.sum(jnp.where(lo_half, 0.0, sq), axis=-1, keepdims=True)
    i0 = lax.rsqrt(s0 * (1.0 / HEAD_DIM) + EPS)
    i1 = lax.rsqrt(s1 * (1.0 / HEAD_DIM) + EPS)
    return k * jnp.where(lo_half, i0, i1) * kg


def _rep4(x):
    lane = lax.broadcasted_iota(jnp.int32, x.shape, 1)
    sw = pltpu.roll(x, HEAD_DIM, axis=1)
    a2 = jnp.where(lane < HEAD_DIM, x, sw)
    b2 = jnp.where(lane < HEAD_DIM, sw, x)
    return jnp.concatenate([a2, a2], axis=-1), jnp.concatenate([b2, b2], axis=-1)


def _write_bd(ref, g, row0, x4):
    rows = x4.shape[0]
    lane = lax.broadcasted_iota(jnp.int32, x4.shape, 1)
    for i in range(QUAD):
        keep = (lane >= i * HEAD_DIM) & (lane < (i + 1) * HEAD_DIM)
        ref[g, pl.ds(i * KEYS + row0, rows), :] = _bf(jnp.where(keep, x4, 0.0))


def _attend(qq, kbd, vbd, valid, sinks):
    s = lax.dot_general(qq, kbd, _NT, preferred_element_type=jnp.float32)
    s = jnp.where(valid, s, NEG)
    ps = []
    for i in range(QUAD):
        sh = s[:, i * KEYS:(i + 1) * KEYS]
        m = jnp.maximum(jnp.max(sh, axis=-1, keepdims=True), sinks[i])
        p = jnp.exp(sh - m)
        den = jnp.sum(p, axis=-1, keepdims=True) + jnp.exp(sinks[i] - m)
        ps.append(_bf(p / den))
    return _dot(jnp.concatenate(ps, axis=-1), vbd)


def _route(lt, upper_ref, carry):
    t = lt.shape[1]
    big = float(RT_ROWS)
    grow = lax.broadcasted_iota(jnp.int32, (REC_ROWS, t), 0).astype(jnp.float32)
    is_g = grow < N_GROUPS
    gl = jnp.where(is_g, lt[N_EXPERTS:N_EXPERTS + REC_ROWS], NEG)
    gmax = jnp.max(gl, axis=0, keepdims=True)
    gsum = jnp.sum(jnp.where(is_g, jnp.exp(gl - gmax), 0.0), axis=0, keepdims=True)
    g_w = 1.0 / gsum
    g_sel = jnp.min(jnp.where(is_g & (gl == gmax), grow, big), axis=0, keepdims=True)
    logits = lt[0:N_EXPERTS]
    erow = lax.broadcasted_iota(jnp.int32, (N_EXPERTS, t), 0).astype(jnp.float32)
    lo = g_sel * EPG
    in_grp = (erow >= lo) & (erow < lo + EPG)
    el = jnp.where(in_grp, logits, NEG)
    v1 = jnp.max(el, axis=0, keepdims=True)
    i1 = jnp.min(jnp.where(in_grp & (el == v1), erow, big), axis=0, keepdims=True)
    rest = in_grp & (erow != i1)
    el2 = jnp.where(rest, logits, NEG)
    v2 = jnp.max(el2, axis=0, keepdims=True)
    i2 = jnp.min(jnp.where(rest & (el2 == v2), erow, big), axis=0, keepdims=True)
    e2 = jnp.exp(v2 - v1)
    den = 1.0 + e2
    w1 = (1.0 / den) * g_w
    w2 = (e2 / den) * g_w
    oh1 = jnp.where(erow == i1, 1.0, 0.0)
    oh2 = jnp.where(erow == i2, 1.0, 0.0)
    both = oh1 + oh2
    before = _dot(_bf(both), upper_ref[...]) + carry
    r1 = jnp.sum(before * oh1, axis=0, keepdims=True)
    r2 = jnp.sum(before * oh2, axis=0, keepdims=True)
    rec = jnp.zeros((REC_ROWS, t), jnp.float32)
    for rw, val in ((R_E1, i1), (R_E2, i2), (R_W1, w1), (R_W2, w2), (R_RANK1, r1), (R_RANK2, r2)):
        rec = jnp.where(grow == rw, val, rec)
    return rec, carry + jnp.sum(both, axis=1, keepdims=True)


def _route_stage(x1, g2_ref, wr_ref, br_ref, upper_ref, carry, h2p_ref, rt_ref):
    rows = x1.shape[0]
    h2 = _rms(x1, g2_ref[...])
    pk = _pack_rows(h2)
    for j in range(PK):
        h2p_ref[pl.ds(j, rows, stride=PK), :] = pk[:, j * LANES:(j + 1) * LANES]
    lt = lax.dot_general(wr_ref[...], _bf(h2), _NT, preferred_element_type=jnp.float32) + br_ref[...]
    rec, carry = _route(lt, upper_ref, carry)
    rt_ref[...] = rec
    return carry


def _merge_tail(x, macc, wo_ref, g2_ref, wr_ref, br_ref, upper_ref, carry, x1_ref, h2p_ref, rt_ref):
    x1 = x + _dot(_bf(macc), wo_ref[...])
    x1_ref[...] = x1
    return _route_stage(x1, g2_ref, wr_ref, br_ref, upper_ref, carry, h2p_ref, rt_ref)


def _prompt_mixer_kernel(x_ref, wg32_ref, wu32_ref, wd32_ref,
                         qx_ref, g1_ref, win_ref, cw_ref, qg_ref, kg_ref, seg_ref, exp_ref,
                         wco_ref, wao_ref, wo_ref, g2_ref, wr_ref, br_ref, upper_ref,
                         x1_ref, h2p_ref, rt_ref, cnt_ref, kwin_ref, vwin_ref, cst_ref,
                         wg16_ref, wu16_ref, wd16_ref,
                         uext, qs_s, att_s, kn_s, kop, vop, khist, vhist, carry_s, x1_s,
                         *, tile, blk_tiles, n_it, n_main):
    step = pl.program_id(0)
    it = step % n_it
    n_chunk = tile // CHUNK

    @pl.when(step == 0)
    def _():
        shape = (n_chunk, N_KV, CHUNK, LANES)
        row = lax.broadcasted_iota(jnp.int32, shape, 2)
        ln = lax.broadcasted_iota(jnp.int32, shape, 3)
        is_sink = (row == 0) & ((ln == SINK_LANE) | (ln == SINK_LANE + 1))
        tail = pl.ds((SLOTS - 1) * CHUNK, CHUNK)
        kop[:, :, tail, :] = _bf(jnp.where(is_sink, 1.0, jnp.where((row > 0) & (ln == BIAS_LANE), NEG, 0.0)))
        vop[:, :, tail, 0:LANES] = jnp.zeros(shape, jnp.bfloat16)
        vop[:, :, tail, LANES:2 * LANES] = jnp.ones(shape, jnp.bfloat16)
        x1_s[1] = jnp.zeros((tile, D), jnp.float32)
        carry_s[...] = jnp.zeros_like(carry_s)

    def route_previous():
        first_of_block = (step + blk_tiles - 1) % blk_tiles == 0
        carry = jnp.where(first_of_block, 0.0, carry_s[...])
        carry = _route_stage(x1_s[(step + 1) % 2], g2_ref, wr_ref, br_ref, upper_ref, carry,
                             h2p_ref, rt_ref.at[0])
        carry_s[...] = carry
        cnt_ref[0] = jnp.broadcast_to(carry, (N_EXPERTS, LANES))

    @pl.when(step == n_main)
    def _():
        route_previous()

    @pl.when((step < n_main) & (it == 0))
    def _():
        uext[0:8, :] = jnp.zeros((8, D), jnp.float32)
        ln = lax.broadcasted_iota(jnp.int32, khist.shape, 2)
        khist[...] = _bf(jnp.where(ln == BIAS_LANE, NEG, 0.0))
        vhist[...] = jnp.zeros_like(vhist)

    @pl.when(step < n_main)
    def _():
        route_previous()
        _mixer_tile(x_ref, wg32_ref, wu32_ref, wd32_ref, qx_ref, g1_ref, win_ref, cw_ref, qg_ref,
                    kg_ref, seg_ref, exp_ref, wco_ref, wao_ref, wo_ref, x1_ref, kwin_ref, vwin_ref,
                    cst_ref, wg16_ref, wu16_ref, wd16_ref, uext, qs_s, att_s, kn_s, kop, vop, khist,
                    vhist, x1_s.at[step % 2], tile=tile)


def _mixer_tile(x_ref, wg32_ref, wu32_ref, wd32_ref, qx_ref, g1_ref, win_ref, cw_ref, qg_ref,
                kg_ref, seg_ref, exp_ref, wco_ref, wao_ref, wo_ref, x1_ref, kwin_ref, vwin_ref,
                cst_ref, wg16_ref, wu16_ref, wd16_ref, uext, qs_s, att_s, kn_s, kop, vop, khist,
                vhist, x1_keep, *, tile):
    n_chunk = tile // CHUNK
    wg16_ref[...] = _bf(wg32_ref[...])
    wu16_ref[...] = _bf(wu32_ref[...])
    wd16_ref[...] = _bf(wd32_ref[...])

    x = x_ref[0]
    h = _bf(_rms(x, g1_ref[...]))

    u = _dot(h, win_ref[:, C_C:C_C + D]) * _dot(h, win_ref[:, C_X:C_X + D])
    uext[8:8 + tile, :] = u
    conv = (uext[6:6 + tile, :] * cw_ref[0:1, :] + uext[7:7 + tile, :] * cw_ref[1:2, :]
            + u * cw_ref[2:3, :])
    yc = _bf(_dot(h, win_ref[:, C_B:C_B + D]) * conv)
    o_c = _dot(yc, wco_ref[...])
    macc = _sigmoid(_dot(h, win_ref[:, C_G:C_G + D])) * o_c
    last2 = uext[tile + 6:tile + 8, :]
    uext[6:8, :] = last2

    cst_ref[0] = last2

    q = _dot(h, win_ref[:, C_Q:C_Q + D])
    qn = _head_norm_q(q, seg_ref, exp_ref, qg_ref[...]) * (LOG2E / math.sqrt(HEAD_DIM))
    for hd in range(N_HEADS):
        g, r = divmod(hd, GROUP_HEADS)
        ext = jnp.broadcast_to(qx_ref[hd], (tile, HEAD_DIM))
        piece = _bf(jnp.concatenate([qn[:, hd * HEAD_DIM:(hd + 1) * HEAD_DIM], ext], axis=-1))
        for c in range(n_chunk):
            qs_s[g, c * STACK + r * CHUNK:c * STACK + (r + 1) * CHUNK, :] = (
                piece[c * CHUNK:(c + 1) * CHUNK, :])
    kv = _dot(h, win_ref[:, C_K:C_K + 2 * LANES])
    kn = _head_norm_k(kv[:, 0:LANES], kg_ref[...])
    vv = kv[:, LANES:2 * LANES]
    kn_s[0] = kn
    kn_s[1] = vv

    kwin_ref[0] = kn[tile - WINDOW:, :]
    vwin_ref[0] = vv[tile - WINDOW:, :]

    lane = lax.broadcasted_iota(jnp.int32, (CHUNK, LANES), 1)
    lo_half = lane < HEAD_DIM
    ones_blk = jnp.ones((CHUNK, LANES), jnp.bfloat16)
    own = SLOTS - 2

    for c in range(own):
        for sl in range(own - c):
            src = pl.ds((c + sl) * CHUNK, CHUNK)
            kop[c, :, pl.ds(sl * CHUNK, CHUNK), :] = khist[:, src, :]
            vop[c, :, pl.ds(sl * CHUNK, CHUNK), :] = vhist[:, src, :]

    def chunk_body(c):
        r0 = c * CHUNK
        kc = kn_s[0, pl.ds(r0, CHUNK), :]
        vc = kn_s[1, pl.ds(r0, CHUNK), :]
        ksw = pltpu.roll(kc, HEAD_DIM, axis=1)
        vsw = pltpu.roll(vc, HEAD_DIM, axis=1)
        for g in range(N_KV):
            kb = _bf(jnp.where(lo_half, kc if g == 0 else ksw, 0.0))
            vb = _bf(jnp.where(lo_half, vc, vsw) if g == 0 else jnp.where(lo_half, vsw, vc))
            for d in range(own + 1):
                if c + d < n_chunk:
                    dst = pl.ds((own - d) * CHUNK, CHUNK)
                    kop[c + d, g, dst, :] = kb
                    vop[c + d, g, dst, 0:LANES] = vb
                    vop[c + d, g, dst, LANES:2 * LANES] = ones_blk
            if c >= n_chunk - own:
                dst = pl.ds((c - (n_chunk - own)) * CHUNK, CHUNK)
                khist[g, dst, :] = kb
                vhist[g, dst, 0:LANES] = vb
                vhist[g, dst, LANES:2 * LANES] = ones_blk
        for g in range(N_KV):
            s = lax.dot_general(qs_s[g, pl.ds(c * STACK, STACK), :], kop[c, g], _NT,
                                preferred_element_type=jnp.float32)
            m = jnp.max(s, axis=-1, keepdims=True)
            o = _dot(_bf(jnp.exp2(s - m)), vop[c, g])
            out = o[:, 0:LANES] / o[:, LANES:2 * LANES]
            for r in range(0, GROUP_HEADS, 2):
                pair = jnp.where(lo_half, out[r * CHUNK:(r + 1) * CHUNK, :],
                                 out[(r + 1) * CHUNK:(r + 2) * CHUNK, :])
                col = (g * GROUP_HEADS + r) * HEAD_DIM
                att_s[pl.ds(r0, CHUNK), col:col + LANES] = _bf(pair)

    for c in range(n_chunk):
        chunk_body(c)

    o_a = _dot(att_s[...], wao_ref[...])
    macc = macc + _sigmoid(_dot(h, win_ref[:, C_G + D:C_G + 2 * D])) * o_a
    x1 = x + _dot(_bf(macc), wo_ref[...])
    x1_ref[0] = x1
    x1_keep[...] = x1


def _const_spec(shape):
    nd = len(shape)
    return pl.BlockSpec(shape, lambda *_: (0,) * nd, pipeline_mode=pl.Buffered(1))


def _prompt_mixer(x, experts32, q_extra, wts, tile, blk_tiles):
    nb, seq, _ = x.shape
    n_it = seq // tile
    n_main = nb * n_it
    bps = n_it // blk_tiles
    parts = n_main // N_EXPERTS
    consts = (q_extra,) + tuple(wts)

    cur = lambda s: jnp.minimum(s, n_main - 1)
    prev = lambda s: jnp.maximum(s - 1, 0)
    row_spec = lambda w: pl.BlockSpec((1, tile, w), lambda s: (cur(s) // n_it, cur(s) % n_it, 0))
    str_spec = lambda r, w: pl.BlockSpec((1, r, w), lambda s: (cur(s) // n_it, 0, 0))
    part_spec = lambda r, w: pl.BlockSpec(
        (1, r // parts, w), lambda s: (cur(s) // parts, cur(s) % parts, 0))
    expert_specs = [part_spec(D, D_EXPERT), part_spec(D, D_EXPERT), part_spec(D_EXPERT, D)]
    out_shape = [
        jax.ShapeDtypeStruct((nb, seq, D), jnp.float32),
        jax.ShapeDtypeStruct((nb * seq * PK, LANES), jnp.uint32),
        jax.ShapeDtypeStruct((n_main, REC_ROWS, tile), jnp.float32),
        jax.ShapeDtypeStruct((nb * bps, N_EXPERTS, LANES), jnp.float32),
        jax.ShapeDtypeStruct((nb, WINDOW, LANES), jnp.float32),
        jax.ShapeDtypeStruct((nb, WINDOW, LANES), jnp.float32),
        jax.ShapeDtypeStruct((nb, 2, D), jnp.float32),
    ] + [jax.ShapeDtypeStruct(w.shape, jnp.bfloat16) for w in experts32]
    return pl.pallas_call(
        functools.partial(_prompt_mixer_kernel, tile=tile, blk_tiles=blk_tiles, n_it=n_it, n_main=n_main),
        grid=(n_main + 1,),
        in_specs=[row_spec(D)] + expert_specs + [_const_spec(w.shape) for w in consts],
        out_specs=[row_spec(D),
                   pl.BlockSpec((tile * PK, LANES), lambda s: (prev(s), 0)),
                   pl.BlockSpec((1, REC_ROWS, tile), lambda s: (prev(s), 0, 0)),
                   pl.BlockSpec((1, N_EXPERTS, LANES), lambda s: (prev(s) // blk_tiles, 0, 0)),
                   str_spec(WINDOW, LANES), str_spec(WINDOW, LANES), str_spec(2, D)] + expert_specs,
        out_shape=out_shape,
        scratch_shapes=[
            pltpu.VMEM((tile + 8, D), jnp.float32),
            pltpu.VMEM((N_KV, tile * GROUP_HEADS, LANES), jnp.bfloat16),
            pltpu.VMEM((tile, D), jnp.bfloat16),
            pltpu.VMEM((2, tile, LANES), jnp.float32),
            pltpu.VMEM((tile // CHUNK, N_KV, KEYS, LANES), jnp.bfloat16),
            pltpu.VMEM((tile // CHUNK, N_KV, KEYS, 2 * LANES), jnp.bfloat16),
            pltpu.VMEM((N_KV, (SLOTS - 2) * CHUNK, LANES), jnp.bfloat16),
            pltpu.VMEM((N_KV, (SLOTS - 2) * CHUNK, 2 * LANES), jnp.bfloat16),
            pltpu.VMEM((N_EXPERTS, 1), jnp.float32),
            pltpu.VMEM((2, tile, D), jnp.float32),
        ],
        compiler_params=pltpu.CompilerParams(
            dimension_semantics=("arbitrary",), vmem_limit_bytes=VMEM_LIMIT),
        name="prompt_mixer",
    )(x, *experts32, *consts)


def _sample_mixer_kernel(sink_ref, x_ref, ck_ref, cv_ref, sc_ref, g1_ref, win_ref, cw_ref, qg_ref,
                         kg_ref, seg_ref, exp_ref, wco_ref, wao_ref, wo_ref, g2_ref, wr_ref, br_ref,
                         upper_ref, x1_ref, h2p_ref, rt_ref, cnt_ref, kwin_ref, vwin_ref, cst_ref,
                         uext, att_s, kbd, vbd, *, n_stream, seq):
    ext = seq + 8
    x = x_ref[...]
    h = _bf(_rms(x, g1_ref[...]))

    u = _dot(h, win_ref[:, C_C:C_C + D]) * _dot(h, win_ref[:, C_X:C_X + D])
    for s in range(n_stream):
        uext[s * ext + 6:s * ext + 8, :] = sc_ref[s]
        uext[s * ext + 8:(s + 1) * ext, :] = u[s * seq:(s + 1) * seq, :]
        cst_ref[s] = u[(s + 1) * seq - 2:(s + 1) * seq, :]
    conv = jnp.concatenate(
        [uext[s * ext + 6:s * ext + 6 + seq, :] * cw_ref[0:1, :]
         + uext[s * ext + 7:s * ext + 7 + seq, :] * cw_ref[1:2, :] for s in range(n_stream)],
        axis=0) + u * cw_ref[2:3, :]
    yc = _bf(_dot(h, win_ref[:, C_B:C_B + D]) * conv)
    o_c = _dot(yc, wco_ref[...])
    macc = _sigmoid(_dot(h, win_ref[:, C_G:C_G + D])) * o_c

    q = _dot(h, win_ref[:, C_Q:C_Q + D])
    qn = _bf(_head_norm_q(q, seg_ref, exp_ref, qg_ref[...]) * (1.0 / math.sqrt(HEAD_DIM)))
    kv = _dot(h, win_ref[:, C_K:C_K + 2 * LANES])
    kn = _head_norm_k(kv[:, 0:LANES], kg_ref[...])
    vv = kv[:, LANES:2 * LANES]

    kbd[...] = jnp.zeros_like(kbd)
    vbd[...] = jnp.zeros_like(vbd)
    lane = lax.broadcasted_iota(jnp.int32, (1, QUAD * KEYS), 1)
    valid = (lane % KEYS) < WINDOW + seq
    for s in range(n_stream):
        rows = slice(s * seq, (s + 1) * seq)
        kwin_ref[s, 0:WINDOW - seq, :] = ck_ref[s, seq:WINDOW, :]
        kwin_ref[s, WINDOW - seq:WINDOW, :] = kn[rows, :]
        vwin_ref[s, 0:WINDOW - seq, :] = cv_ref[s, seq:WINDOW, :]
        vwin_ref[s, WINDOW - seq:WINDOW, :] = vv[rows, :]
        kc4 = _rep4(ck_ref[s])
        vc4 = _rep4(cv_ref[s])
        kn4 = _rep4(kn[rows, :])
        vn4 = _rep4(vv[rows, :])
        for g in range(N_KV):
            _write_bd(kbd, g, 0, kc4[g])
            _write_bd(kbd, g, WINDOW, kn4[g])
            _write_bd(vbd, g, 0, vc4[g])
            _write_bd(vbd, g, WINDOW, vn4[g])
        for g in range(N_KV):
            kb = kbd[g]
            vb = vbd[g]
            for j in range(N_HEADS // N_KV // QUAD):
                h0 = g * (N_HEADS // N_KV) + j * QUAD
                qq = qn[rows, h0 * HEAD_DIM:(h0 + QUAD) * HEAD_DIM]
                sinks = [sink_ref[h0 + i] for i in range(QUAD)]
                o = _attend(qq, kb, vb, valid, sinks)
                att_s[rows, h0 * HEAD_DIM:(h0 + QUAD) * HEAD_DIM] = _bf(o)

    o_a = _dot(att_s[...], wao_ref[...])
    macc = macc + _sigmoid(_dot(h, win_ref[:, C_G + D:C_G + 2 * D])) * o_a
    carry = _merge_tail(x, macc, wo_ref, g2_ref, wr_ref, br_ref, upper_ref,
                        jnp.zeros((N_EXPERTS, 1), jnp.float32), x1_ref, h2p_ref, rt_ref.at[0])
    cnt_ref[0] = jnp.broadcast_to(carry, (N_EXPERTS, LANES))


def _sample_mixer(x, ck, cv, sc, sinks, wts):
    nb, seq, _ = x.shape
    rows = nb * seq
    x2 = x.reshape(rows, D)
    ins = (x2, ck, cv, sc) + tuple(wts)
    full = lambda a: pl.BlockSpec(a.shape, lambda *_, nd=a.ndim: (0,) * nd)
    out_shape = [
        jax.ShapeDtypeStruct((rows, D), jnp.float32),
        jax.ShapeDtypeStruct((rows * PK, LANES), jnp.uint32),
        jax.ShapeDtypeStruct((1, REC_ROWS, rows), jnp.float32),
        jax.ShapeDtypeStruct((1, N_EXPERTS, LANES), jnp.float32),
        jax.ShapeDtypeStruct((nb, WINDOW, LANES), jnp.float32),
        jax.ShapeDtypeStruct((nb, WINDOW, LANES), jnp.float32),
        jax.ShapeDtypeStruct((nb, 2, D), jnp.float32),
    ]
    grid_spec = pltpu.PrefetchScalarGridSpec(
        num_scalar_prefetch=1,
        grid=(1,),
        in_specs=[full(a) for a in ins],
        out_specs=[full(a) for a in out_shape],
        scratch_shapes=[
            pltpu.VMEM((nb * (seq + 8), D), jnp.float32),
            pltpu.VMEM((rows, D), jnp.bfloat16),
            pltpu.VMEM((N_KV, QUAD * KEYS, QUAD_W), jnp.bfloat16),
            pltpu.VMEM((N_KV, QUAD * KEYS, QUAD_W), jnp.bfloat16),
        ])
    return pl.pallas_call(
        functools.partial(_sample_mixer_kernel, n_stream=nb, seq=seq),
        grid_spec=grid_spec,
        out_shape=out_shape,
        compiler_params=pltpu.CompilerParams(
            dimension_semantics=("arbitrary",), vmem_limit_bytes=VMEM_LIMIT),
        name="sample_mixer",
    )(sinks, *ins)


def _sorted_rows(blk_tokens, mt):
    rows = 2 * blk_tokens + N_EXPERTS * (mt // 2 - 1) + mt
    return -(-rows // mt) * mt


def _moe_kernel(pos_ref, tab_ref, stp_ref, h2p_ref, x1_ref, rt_ref, wg_ref, wu_ref, wd_ref, y_ref,
                xo, g1, g2, *, tile, n_tiles, max_tiles, mt):
    b = pl.program_id(0)
    s = pl.program_id(1)
    n_groups = tile // UNROLL

    @pl.when((b == 0) & (s == 0))
    def _():
        xo[...] = jnp.zeros_like(xo)

    @pl.when(s < n_tiles)
    def _dispatch():
        base = (b * n_tiles + s) * (TOP_K * tile)

        def group(gi, carry):
            i0 = base + gi * UNROLL
            t0 = gi * (UNROLL * PK)
            for k in range(UNROLL):
                row = h2p_ref[pl.ds(pl.multiple_of(t0 + k * PK, PK), PK), :]
                xo[pl.ds(pl.multiple_of(pos_ref[i0 + k], PK), PK), :] = row
                xo[pl.ds(pl.multiple_of(pos_ref[i0 + tile + k], PK), PK), :] = row
            return carry

        lax.fori_loop(0, n_groups, group, 0)

    @pl.when((s >= n_tiles) & (s < n_tiles + EXPERT_STEPS))
    def _experts():
        st = b * EXPERT_STEPS + s - n_tiles
        first = stp_ref[2 * st]
        last = stp_ref[2 * st + 1]
        half = mt // 2

        def entry(j):
            t = (b * max_tiles + j) * 3
            return pl.multiple_of(tab_ref[t], half * PK), tab_ref[t + 1], tab_ref[t + 2]

        def up(j):
            base, k, _ = entry(j)
            xs = _bf(_unpack_rows(
                [xo[pl.ds(base + q, mt, stride=PK), :] for q in range(PK)]))
            gt = _dot(xs, wg_ref[k])
            return _bf(gt * _sigmoid(gt) * _dot(xs, wu_ref[k]))

        def down(j, hid):
            base, k, full = entry(j)
            out = _pack_rows(_dot(hid, wd_ref[k]))
            def put(lo):
                for q in range(PK):
                    xo[pl.ds(base + lo * PK + q, half, stride=PK), :] = (
                        out[lo:lo + half, q * LANES:(q + 1) * LANES])

            put(0)

            @pl.when(full > 0)
            def _():
                put(half)

        @pl.when(last > first)
        def _():
            def both(j, hid):
                nxt = up(j)
                down(j - 1, hid)
                return nxt

            down(last - 1, lax.fori_loop(first + 1, last, both, up(first)))

    @pl.when(s >= n_tiles + EXPERT_STEPS)
    def _combine():
        i = s - n_tiles - EXPERT_STEPS
        base = (b * n_tiles + i) * (TOP_K * tile)
        rec = jnp.concatenate([rt_ref[0], jnp.zeros((LANES - REC_ROWS, tile), jnp.float32)], axis=0)
        cw = rec.T
        part = tile // COMBINE_PARTS
        for p in range(COMBINE_PARTS):
            for k in range(part):
                t = p * part + k
                dst = pl.ds(t * PK, PK)
                g1[dst, :] = xo[pl.ds(pl.multiple_of(pos_ref[base + t], PK), PK), :]
                g2[dst, :] = xo[pl.ds(pl.multiple_of(pos_ref[base + tile + t], PK), PK), :]
            rows = pl.ds(p * part, part)
            o1 = _unpack_rows([g1[pl.ds(p * part * PK + q, part, stride=PK), :] for q in range(PK)])
            o2 = _unpack_rows([g2[pl.ds(p * part * PK + q, part, stride=PK), :] for q in range(PK)])
            w = cw[p * part:(p + 1) * part]
            y_ref[rows, :] = x1_ref[rows, :] + w[:, R_W1:R_W1 + 1] * o1 + w[:, R_W2:R_W2 + 1] * o2


def _moe(pos, tab, stp, rt, h2p, x1, wg, wu, wd, tile, n_tiles, mt):
    n = x1.shape[0]
    n_blocks = n // (tile * n_tiles)
    n_steps = 2 * n_tiles + EXPERT_STEPS
    last = n_tiles - 1

    def disp(b, s, *_):
        return (b * n_tiles + jnp.minimum(s, last), 0)

    def comb(b, s, *_):
        return (b * n_tiles + jnp.clip(s - n_tiles - EXPERT_STEPS, 0, last), 0)

    def wmap(b, s, *_):
        return (jnp.clip(s - n_tiles, 0, EXPERT_STEPS - 1), 0, 0)

    grid_spec = pltpu.PrefetchScalarGridSpec(
        num_scalar_prefetch=3,
        grid=(n_blocks, n_steps),
        in_specs=[pl.BlockSpec((tile * PK, LANES), disp),
                  pl.BlockSpec((tile, D), comb),
                  pl.BlockSpec((1, REC_ROWS, tile), lambda b, s, *_: comb(b, s) + (0,)),
                  pl.BlockSpec((STEP_EXPERTS, D, D_EXPERT), wmap),
                  pl.BlockSpec((STEP_EXPERTS, D, D_EXPERT), wmap),
                  pl.BlockSpec((STEP_EXPERTS, D_EXPERT, D), wmap)],
        out_specs=pl.BlockSpec((tile, D), comb),
        scratch_shapes=[
            pltpu.VMEM((_sorted_rows(tile * n_tiles, mt) * PK, LANES), jnp.uint32),
            pltpu.VMEM((tile * PK, LANES), jnp.uint32),
            pltpu.VMEM((tile * PK, LANES), jnp.uint32),
        ])
    return pl.pallas_call(
        functools.partial(_moe_kernel, tile=tile, n_tiles=n_tiles,
                          max_tiles=_max_tiles(tile * n_tiles, mt), mt=mt),
        grid_spec=grid_spec,
        out_shape=jax.ShapeDtypeStruct((n, D), jnp.float32),
        compiler_params=pltpu.CompilerParams(
            dimension_semantics=("arbitrary", "arbitrary"), vmem_limit_bytes=VMEM_LIMIT),
        name="moe",
    )(pos, tab, stp, h2p, x1, rt, wg, wu, wd)


def _max_tiles(blk_tokens, mt):
    return N_EXPERTS + 2 * blk_tokens // mt


def _sorted_positions(rt, cnt, blk_tokens, mt):
    experts = jnp.arange(N_EXPERTS, dtype=jnp.int32)
    n_tiles, _, tile = rt.shape
    counts = cnt[:, :, 0].astype(jnp.int32)
    pad = mt // 2
    padded = (counts + pad - 1) // pad * pad
    off = jnp.cumsum(padded, axis=1) - padded
    e = rt[:, R_E1:R_E1 + TOP_K, :].astype(jnp.int32)
    rank = rt[:, R_RANK1:R_RANK1 + TOP_K, :].astype(jnp.int32)
    off_tile = jnp.repeat(off, blk_tokens // tile, axis=0)
    hit = e[:, :, None, :] == experts[:, None]
    pos = (rank + jnp.sum(jnp.where(hit, off_tile[:, None, :, None], 0), axis=2)) * PK

    tiles = (counts + mt - 1) // mt
    t_end = jnp.cumsum(tiles, axis=1)
    t_beg = t_end - tiles
    j = jnp.arange(_max_tiles(blk_tokens, mt), dtype=jnp.int32)
    e_of = jnp.minimum(jnp.sum(t_end[:, None, :] <= j[None, :, None], axis=-1), N_EXPERTS - 1)
    pick = lambda a: jnp.sum(jnp.where(e_of[:, :, None] == experts, a[:, None, :], 0), axis=-1)
    local = (j[None, :] - pick(t_beg)) * mt
    tab = jnp.stack([(pick(off) + local) * PK, e_of % STEP_EXPERTS,
                     (pick(counts) - local > pad).astype(jnp.int32)], axis=-1)
    stp = jnp.stack([t_beg[:, ::STEP_EXPERTS], t_end[:, STEP_EXPERTS - 1::STEP_EXPERTS]], axis=-1)
    return pos.reshape(-1), tab.reshape(-1), stp.reshape(-1), rt


def _segment_matrices():
    head = np.arange(D) // HEAD_DIM
    seg = head[:, None] == np.arange(LANES)[None, :]
    expm = (np.arange(2 * LANES) % LANES)[:, None] == head[None, :]
    return jnp.asarray(seg, jnp.bfloat16), jnp.asarray(expm, jnp.bfloat16)


def _strict_upper(n):
    r = np.arange(n)
    return jnp.asarray(r[:, None] < r[None, :], jnp.bfloat16)


def kernel(x_prompt, x_sample, cache_k, cache_v, state_conv, norm_mix_g, w_in, conv_w, q_norm_g,
           k_norm_g, attn_sinks, w_conv_out, w_attn_out, w_out, norm_ffn_g, w_group, b_group,
           w_router, b_router, w_gate, w_up, w_down):
    nb, seq, _ = x_prompt.shape
    ns, sseq, _ = x_sample.shape
    l = 0
    seg, expm = _segment_matrices()
    pad = RT_ROWS - N_EXPERTS - N_GROUPS
    wr = jnp.concatenate([w_router[l].T, w_group[l].T, jnp.zeros((pad, D), jnp.float32)], axis=0)
    br = jnp.concatenate([b_router[l], b_group[l], jnp.zeros((pad,), jnp.float32)]).reshape(RT_ROWS, 1)
    wts = (norm_mix_g[l].reshape(1, D), _bf(w_in[l]), conv_w[l],
           jnp.tile(q_norm_g[l], N_HEADS).reshape(1, D), jnp.tile(k_norm_g[l], N_KV).reshape(1, LANES),
           seg, expm, _bf(w_conv_out[l]), _bf(w_attn_out[l]), _bf(w_out[l]),
           norm_ffn_g[l].reshape(1, D), _bf(wr), br)
    sinks = attn_sinks[l]
    n_s = ns * sseq

    sink2 = sinks * LOG2E
    sink_hi = _bf(sink2).astype(jnp.float32)
    q_extra = jnp.concatenate(
        [jnp.ones((N_HEADS, 1), jnp.float32), sink_hi[:, None], (sink2 - sink_hi)[:, None],
         jnp.zeros((N_HEADS, HEAD_DIM - 3), jnp.float32)], axis=1).reshape(N_HEADS, 1, HEAD_DIM)
    x1p, h2pp, rtp, cntp, kwp, vwp, cstp, wg, wu, wd = _prompt_mixer(
        x_prompt, (w_gate[l], w_up[l], w_down[l]), q_extra, wts + (_strict_upper(TOK_TILE),), tile=TOK_TILE, blk_tiles=BLOCK_TILES)
    x1s, h2ps, rts, cnts, kws, vws, csts = _sample_mixer(
        x_sample, cache_k[l].reshape(ns, WINDOW, LANES), cache_v[l].reshape(ns, WINDOW, LANES),
        state_conv[l], sinks, wts + (_strict_upper(n_s),))

    yp = _moe(*_sorted_positions(rtp, cntp, TOK_TILE * BLOCK_TILES, MOE_TILE), h2pp, x1p.reshape(nb * seq, D), wg, wu, wd,
              tile=TOK_TILE, n_tiles=BLOCK_TILES, mt=MOE_TILE)
    ys = _moe(*_sorted_positions(rts, cnts, n_s, SAMPLE_MOE_TILE), h2ps, x1s, wg, wu, wd, tile=n_s, n_tiles=1,
              mt=SAMPLE_MOE_TILE)

    kv_shape = lambda n: (1, n, WINDOW, N_KV, HEAD_DIM)
    return (yp.reshape(nb, seq, D), ys.reshape(ns, sseq, D),
            kwp.reshape(kv_shape(nb)), vwp.reshape(kv_shape(nb)), cstp.reshape(1, nb, 2, D),
            kws.reshape(kv_shape(ns)), vws.reshape(kv_shape(ns)), csts.reshape(1, ns, 2, D))
```

```python
import functools
import math

import jax
import jax.numpy as jnp
import numpy as np
from jax import lax
from jax.experimental import pallas as pl
from jax.experimental.pallas import tpu as pltpu

D = 1024
CHUNK = 64
HEAD_DIM = 64
N_HEADS = 16
N_KV = 2
WINDOW = 128
N_GROUPS = 4
EPG = 8
N_EXPERTS = 32
TOP_K = 2
D_EXPERT = 256
EPS = 1e-6
NEG = -1e30

LANES = 128
QUAD = 4
QUAD_W = QUAD * HEAD_DIM
SLOTS = 4
KEYS = SLOTS * CHUNK
GROUP_HEADS = N_HEADS // N_KV
STACK = GROUP_HEADS * CHUNK
BIAS_LANE = HEAD_DIM
SINK_LANE = HEAD_DIM + 1
LOG2E = 1.0 / math.log(2.0)
RT_ROWS = LANES
REC_ROWS = 8
HALF = D // 2
PK = HALF // LANES

R_E1, R_E2, R_W1, R_W2, R_RANK1, R_RANK2 = 0, 1, 2, 3, 4, 5

C_B, C_C, C_X, C_Q, C_K, C_V, C_G = 0, 1024, 2048, 3072, 4096, 4224, 4352

MOE_TILE = 288
SAMPLE_MOE_TILE = 64
TOK_TILE = 512
BLOCK_TILES = 8
UNROLL = 16
COMBINE_PARTS = 4
DISPATCH_TILES = 2
STEP_EXPERTS = 4
EXPERT_STEPS = N_EXPERTS // STEP_EXPERTS

VMEM_LIMIT = 60 * 1024 * 1024

_NT = (((1,), (1,)), ((), ()))


def _bf(x):
    return x.astype(jnp.bfloat16)


def _dot(a, b):
    return jnp.dot(a, b, preferred_element_type=jnp.float32)


def _rms(x, g):
    return x * lax.rsqrt(jnp.mean(x * x, axis=-1, keepdims=True) + EPS) * g


def _sigmoid(x):
    return 1.0 / (1.0 + jnp.exp(-x))


def _pack_rows(x):
    return pltpu.pack_elementwise([x[:, :HALF], x[:, HALF:]], packed_dtype=jnp.bfloat16)


def _unpack_rows(parts):
    def half(i):
        return [pltpu.unpack_elementwise(p, index=i, packed_dtype=jnp.bfloat16,
                                         unpacked_dtype=jnp.float32) for p in parts]
    return jnp.concatenate(half(0) + half(1), axis=-1)


def _head_norm_q(q, seg_ref, exp_ref, qg):
    ssq = _dot(_bf(q * q), seg_ref[...])
    inv = lax.rsqrt(ssq * (1.0 / HEAD_DIM) + EPS)
    hi = _bf(inv)
    lo = _bf(inv - hi.astype(jnp.float32))
    full = _dot(jnp.concatenate([hi, lo], axis=-1), exp_ref[...])
    return q * full * qg


def _head_norm_k(k, kg):
    lane = lax.broadcasted_iota(jnp.int32, k.shape, 1)
    lo_half = lane < HEAD_DIM
    sq = k * k
    s0 = jnp.sum(jnp.where(lo_half, sq, 0.0), axis=-1, keepdims=True)
    s1 = jnp.sum(jnp.where(lo_half, 0.0, sq), axis=-1, keepdims=True)
    i0 = lax.rsqrt(s0 * (1.0 / HEAD_DIM) + EPS)
    i1 = lax.rsqrt(s1 * (1.0 / HEAD_DIM) + EPS)
    return k * jnp.where(lo_half, i0, i1) * kg


def _rep4(x):
    lane = lax.broadcasted_iota(jnp.int32, x.shape, 1)
    sw = pltpu.roll(x, HEAD_DIM, axis=1)
    a2 = jnp.where(lane < HEAD_DIM, x, sw)
    b2 = jnp.where(lane < HEAD_DIM, sw, x)
    return jnp.concatenate([a2, a2], axis=-1), jnp.concatenate([b2, b2], axis=-1)


def _write_bd(ref, g, row0, x4):
    rows = x4.shape[0]
    lane = lax.broadcasted_iota(jnp.int32, x4.shape, 1)
    for i in range(QUAD):
        keep = (lane >= i * HEAD_DIM) & (lane < (i + 1) * HEAD_DIM)
        ref[g, pl.ds(i * KEYS + row0, rows), :] = _bf(jnp.where(keep, x4, 0.0))


def _attend(qq, kbd, vbd, valid, sinks):
    s = lax.dot_general(qq, kbd, _NT, preferred_element_type=jnp.float32)
    s = jnp.where(valid, s, NEG)
    ps = []
    for i in range(QUAD):
        sh = s[:, i * KEYS:(i + 1) * KEYS]
        m = jnp.maximum(jnp.max(sh, axis=-1, keepdims=True), sinks[i])
        p = jnp.exp(sh - m)
        den = jnp.sum(p, axis=-1, keepdims=True) + jnp.exp(sinks[i] - m)
        ps.append(_bf(p / den))
    return _dot(jnp.concatenate(ps, axis=-1), vbd)


def _route(lt, upper_ref, carry):
    t = lt.shape[1]
    big = float(RT_ROWS)
    grow = lax.broadcasted_iota(jnp.int32, (REC_ROWS, t), 0).astype(jnp.float32)
    is_g = grow < N_GROUPS
    gl = jnp.where(is_g, lt[N_EXPERTS:N_EXPERTS + REC_ROWS], NEG)
    gmax = jnp.max(gl, axis=0, keepdims=True)
    gsum = jnp.sum(jnp.where(is_g, jnp.exp(gl - gmax), 0.0), axis=0, keepdims=True)
    g_w = 1.0 / gsum
    g_sel = jnp.min(jnp.where(is_g & (gl == gmax), grow, big), axis=0, keepdims=True)
    logits = lt[0:N_EXPERTS]
    erow = lax.broadcasted_iota(jnp.int32, (N_EXPERTS, t), 0).astype(jnp.float32)
    lo = g_sel * EPG
    in_grp = (erow >= lo) & (erow < lo + EPG)
    el = jnp.where(in_grp, logits, NEG)
    v1 = jnp.max(el, axis=0, keepdims=True)
    i1 = jnp.min(jnp.where(in_grp & (el == v1), erow, big), axis=0, keepdims=True)
    rest = in_grp & (erow != i1)
    el2 = jnp.where(rest, logits, NEG)
    v2 = jnp.max(el2, axis=0, keepdims=True)
    i2 = jnp.min(jnp.where(rest & (el2 == v2), erow, big), axis=0, keepdims=True)
    e2 = jnp.exp(v2 - v1)
    den = 1.0 + e2
    w1 = (1.0 / den) * g_w
    w2 = (e2 / den) * g_w
    oh1 = jnp.where(erow == i1, 1.0, 0.0)
    oh2 = jnp.where(erow == i2, 1.0, 0.0)
    both = oh1 + oh2
    before = _dot(_bf(both), upper_ref[...]) + carry
    r1 = jnp.sum(before * oh1, axis=0, keepdims=True)
    r2 = jnp.sum(before * oh2, axis=0, keepdims=True)
    rec = jnp.zeros((REC_ROWS, t), jnp.float32)
    for rw, val in ((R_E1, i1), (R_E2, i2), (R_W1, w1), (R_W2, w2), (R_RANK1, r1), (R_RANK2, r2)):
        rec = jnp.where(grow == rw, val, rec)
    return rec, carry + jnp.sum(both, axis=1, keepdims=True)


def _route_stage(x1, g2_ref, wr_ref, br_ref, upper_ref, carry, h2p_ref, rt_ref):
    rows = x1.shape[0]
    h2 = _rms(x1, g2_ref[...])
    pk = _pack_rows(h2)
    for j in range(PK):
        h2p_ref[pl.ds(j, rows, stride=PK), :] = pk[:, j * LANES:(j + 1) * LANES]
    lt = lax.dot_general(wr_ref[...], _bf(h2), _NT, preferred_element_type=jnp.float32) + br_ref[...]
    rec, carry = _route(lt, upper_ref, carry)
    rt_ref[...] = rec
    return carry


def _merge_tail(x, macc, wo_ref, g2_ref, wr_ref, br_ref, upper_ref, carry, x1_ref, h2p_ref, rt_ref):
    x1 = x + _dot(_bf(macc), wo_ref[...])
    x1_ref[...] = x1
    return _route_stage(x1, g2_ref, wr_ref, br_ref, upper_ref, carry, h2p_ref, rt_ref)


def _prompt_mixer_kernel(x_ref, wg32_ref, wu32_ref, wd32_ref,
                         qx_ref, g1_ref, win_ref, cw_ref, qg_ref, kg_ref, seg_ref, exp_ref,
                         wco_ref, wao_ref, wo_ref, g2_ref, wr_ref, br_ref, upper_ref,
                         x1_ref, h2p_ref, rt_ref, cnt_ref, kwin_ref, vwin_ref, cst_ref,
                         wg16_ref, wu16_ref, wd16_ref,
                         uext, qs_s, att_s, kn_s, kop, vop, khist, vhist, carry_s, x1_s,
                         *, tile, blk_tiles, n_it, n_main):
    step = pl.program_id(0)
    it = step % n_it
    n_chunk = tile // CHUNK

    @pl.when(step == 0)
    def _():
        shape = (n_chunk, N_KV, CHUNK, LANES)
        row = lax.broadcasted_iota(jnp.int32, shape, 2)
        ln = lax.broadcasted_iota(jnp.int32, shape, 3)
        is_sink = (row == 0) & ((ln == SINK_LANE) | (ln == SINK_LANE + 1))
        tail = pl.ds((SLOTS - 1) * CHUNK, CHUNK)
        kop[:, :, tail, :] = _bf(jnp.where(is_sink, 1.0, jnp.where((row > 0) & (ln == BIAS_LANE), NEG, 0.0)))
        vop[:, :, tail, 0:LANES] = jnp.zeros(shape, jnp.bfloat16)
        vop[:, :, tail, LANES:2 * LANES] = jnp.ones(shape, jnp.bfloat16)
        x1_s[1] = jnp.zeros((tile, D), jnp.float32)
        carry_s[...] = jnp.zeros_like(carry_s)

    def route_previous():
        first_of_block = (step + blk_tiles - 1) % blk_tiles == 0
        carry = jnp.where(first_of_block, 0.0, carry_s[...])
        carry = _route_stage(x1_s[(step + 1) % 2], g2_ref, wr_ref, br_ref, upper_ref, carry,
                             h2p_ref, rt_ref.at[0])
        carry_s[...] = carry
        cnt_ref[0] = jnp.broadcast_to(carry, (N_EXPERTS, LANES))

    @pl.when(step == n_main)
    def _():
        route_previous()

    @pl.when((step < n_main) & (it == 0))
    def _():
        uext[0:8, :] = jnp.zeros((8, D), jnp.float32)
        ln = lax.broadcasted_iota(jnp.int32, khist.shape, 2)
        khist[...] = _bf(jnp.where(ln == BIAS_LANE, NEG, 0.0))
        vhist[...] = jnp.zeros_like(vhist)

    @pl.when(step < n_main)
    def _():
        route_previous()
        _mixer_tile(x_ref, wg32_ref, wu32_ref, wd32_ref, qx_ref, g1_ref, win_ref, cw_ref, qg_ref,
                    kg_ref, seg_ref, exp_ref, wco_ref, wao_ref, wo_ref, x1_ref, kwin_ref, vwin_ref,
                    cst_ref, wg16_ref, wu16_ref, wd16_ref, uext, qs_s, att_s, kn_s, kop, vop, khist,
                    vhist, x1_s.at[step % 2], tile=tile)


def _mixer_tile(x_ref, wg32_ref, wu32_ref, wd32_ref, qx_ref, g1_ref, win_ref, cw_ref, qg_ref,
                kg_ref, seg_ref, exp_ref, wco_ref, wao_ref, wo_ref, x1_ref, kwin_ref, vwin_ref,
                cst_ref, wg16_ref, wu16_ref, wd16_ref, uext, qs_s, att_s, kn_s, kop, vop, khist,
                vhist, x1_keep, *, tile):
    n_chunk = tile // CHUNK
    wg16_ref[...] = _bf(wg32_ref[...])
    wu16_ref[...] = _bf(wu32_ref[...])
    wd16_ref[...] = _bf(wd32_ref[...])

    x = x_ref[0]
    h = _bf(_rms(x, g1_ref[...]))

    u = _dot(h, win_ref[:, C_C:C_C + D]) * _dot(h, win_ref[:, C_X:C_X + D])
    uext[8:8 + tile, :] = u
    conv = (uext[6:6 + tile, :] * cw_ref[0:1, :] + uext[7:7 + tile, :] * cw_ref[1:2, :]
            + u * cw_ref[2:3, :])
    yc = _bf(_dot(h, win_ref[:, C_B:C_B + D]) * conv)
    o_c = _dot(yc, wco_ref[...])
    macc = _sigmoid(_dot(h, win_ref[:, C_G:C_G + D])) * o_c
    last2 = uext[tile + 6:tile + 8, :]
    uext[6:8, :] = last2

    cst_ref[0] = last2

    q = _dot(h, win_ref[:, C_Q:C_Q + D])
    qn = _head_norm_q(q, seg_ref, exp_ref, qg_ref[...]) * (LOG2E / math.sqrt(HEAD_DIM))
    for hd in range(N_HEADS):
        g, r = divmod(hd, GROUP_HEADS)
        ext = jnp.broadcast_to(qx_ref[hd], (tile, HEAD_DIM))
        piece = _bf(jnp.concatenate([qn[:, hd * HEAD_DIM:(hd + 1) * HEAD_DIM], ext], axis=-1))
        for c in range(n_chunk):
            qs_s[g, c * STACK + r * CHUNK:c * STACK + (r + 1) * CHUNK, :] = (
                piece[c * CHUNK:(c + 1) * CHUNK, :])
    kv = _dot(h, win_ref[:, C_K:C_K + 2 * LANES])
    kn = _head_norm_k(kv[:, 0:LANES], kg_ref[...])
    vv = kv[:, LANES:2 * LANES]
    kn_s[0] = kn
    kn_s[1] = vv

    kwin_ref[0] = kn[tile - WINDOW:, :]
    vwin_ref[0] = vv[tile - WINDOW:, :]

    lane = lax.broadcasted_iota(jnp.int32, (CHUNK, LANES), 1)
    lo_half = lane < HEAD_DIM
    ones_blk = jnp.ones((CHUNK, LANES), jnp.bfloat16)
    own = SLOTS - 2

    for c in range(own):
        for sl in range(own - c):
            src = pl.ds((c + sl) * CHUNK, CHUNK)
            kop[c, :, pl.ds(sl * CHUNK, CHUNK), :] = khist[:, src, :]
            vop[c, :, pl.ds(sl * CHUNK, CHUNK), :] = vhist[:, src, :]

    def chunk_body(c):
        r0 = c * CHUNK
        kc = kn_s[0, pl.ds(r0, CHUNK), :]
        vc = kn_s[1, pl.ds(r0, CHUNK), :]
        ksw = pltpu.roll(kc, HEAD_DIM, axis=1)
        vsw = pltpu.roll(vc, HEAD_DIM, axis=1)
        for g in range(N_KV):
            kb = _bf(jnp.where(lo_half, kc if g == 0 else ksw, 0.0))
            vb = _bf(jnp.where(lo_half, vc, vsw) if g == 0 else jnp.where(lo_half, vsw, vc))
            for d in range(own + 1):
                if c + d < n_chunk:
                    dst = pl.ds((own - d) * CHUNK, CHUNK)
                    kop[c + d, g, dst, :] = kb
                    vop[c + d, g, dst, 0:LANES] = vb
                    vop[c + d, g, dst, LANES:2 * LANES] = ones_blk
            if c >= n_chunk - own:
                dst = pl.ds((c - (n_chunk - own)) * CHUNK, CHUNK)
                khist[g, dst, :] = kb
                vhist[g, dst, 0:LANES] = vb
                vhist[g, dst, LANES:2 * LANES] = ones_blk
        for g in range(N_KV):
            s = lax.dot_general(qs_s[g, pl.ds(c * STACK, STACK), :], kop[c, g], _NT,
                                preferred_element_type=jnp.float32)
            m = jnp.max(s, axis=-1, keepdims=True)
            o = _dot(_bf(jnp.exp2(s - m)), vop[c, g])
            out = o[:, 0:LANES] / o[:, LANES:2 * LANES]
            for r in range(0, GROUP_HEADS, 2):
                pair = jnp.where(lo_half, out[r * CHUNK:(r + 1) * CHUNK, :],
                                 out[(r + 1) * CHUNK:(r + 2) * CHUNK, :])
                col = (g * GROUP_HEADS + r) * HEAD_DIM
                att_s[pl.ds(r0, CHUNK), col:col + LANES] = _bf(pair)

    for c in range(n_chunk):
        chunk_body(c)

    o_a = _dot(att_s[...], wao_ref[...])
    macc = macc + _sigmoid(_dot(h, win_ref[:, C_G + D:C_G + 2 * D])) * o_a
    x1 = x + _dot(_bf(macc), wo_ref[...])
    x1_ref[0] = x1
    x1_keep[...] = x1


def _const_spec(shape):
    nd = len(shape)
    return pl.BlockSpec(shape, lambda *_: (0,) * nd, pipeline_mode=pl.Buffered(1))


def _prompt_mixer(x, experts32, q_extra, wts, tile, blk_tiles):
    nb, seq, _ = x.shape
    n_it = seq // tile
    n_main = nb * n_it
    bps = n_it // blk_tiles
    parts = n_main // N_EXPERTS
    consts = (q_extra,) + tuple(wts)

    cur = lambda s: jnp.minimum(s, n_main - 1)
    prev = lambda s: jnp.maximum(s - 1, 0)
    row_spec = lambda w: pl.BlockSpec((1, tile, w), lambda s: (cur(s) // n_it, cur(s) % n_it, 0))
    str_spec = lambda r, w: pl.BlockSpec((1, r, w), lambda s: (cur(s) // n_it, 0, 0))
    part_spec = lambda r, w: pl.BlockSpec(
        (1, r // parts, w), lambda s: (cur(s) // parts, cur(s) % parts, 0))
    expert_specs = [part_spec(D, D_EXPERT), part_spec(D, D_EXPERT), part_spec(D_EXPERT, D)]
    out_shape = [
        jax.ShapeDtypeStruct((nb, seq, D), jnp.float32),
        jax.ShapeDtypeStruct((nb * seq * PK, LANES), jnp.uint32),
        jax.ShapeDtypeStruct((n_main, REC_ROWS, tile), jnp.float32),
        jax.ShapeDtypeStruct((nb * bps, N_EXPERTS, LANES), jnp.float32),
        jax.ShapeDtypeStruct((nb, WINDOW, LANES), jnp.float32),
        jax.ShapeDtypeStruct((nb, WINDOW, LANES), jnp.float32),
        jax.ShapeDtypeStruct((nb, 2, D), jnp.float32),
    ] + [jax.ShapeDtypeStruct(w.shape, jnp.bfloat16) for w in experts32]
    return pl.pallas_call(
        functools.partial(_prompt_mixer_kernel, tile=tile, blk_tiles=blk_tiles, n_it=n_it, n_main=n_main),
        grid=(n_main + 1,),
        in_specs=[row_spec(D)] + expert_specs + [_const_spec(w.shape) for w in consts],
        out_specs=[row_spec(D),
                   pl.BlockSpec((tile * PK, LANES), lambda s: (prev(s), 0)),
                   pl.BlockSpec((1, REC_ROWS, tile), lambda s: (prev(s), 0, 0)),
                   pl.BlockSpec((1, N_EXPERTS, LANES), lambda s: (prev(s) // blk_tiles, 0, 0)),
                   str_spec(WINDOW, LANES), str_spec(WINDOW, LANES), str_spec(2, D)] + expert_specs,
        out_shape=out_shape,
        scratch_shapes=[
            pltpu.VMEM((tile + 8, D), jnp.float32),
            pltpu.VMEM((N_KV, tile * GROUP_HEADS, LANES), jnp.bfloat16),
            pltpu.VMEM((tile, D), jnp.bfloat16),
            pltpu.VMEM((2, tile, LANES), jnp.float32),
            pltpu.VMEM((tile // CHUNK, N_KV, KEYS, LANES), jnp.bfloat16),
            pltpu.VMEM((tile // CHUNK, N_KV, KEYS, 2 * LANES), jnp.bfloat16),
            pltpu.VMEM((N_KV, (SLOTS - 2) * CHUNK, LANES), jnp.bfloat16),
            pltpu.VMEM((N_KV, (SLOTS - 2) * CHUNK, 2 * LANES), jnp.bfloat16),
            pltpu.VMEM((N_EXPERTS, 1), jnp.float32),
            pltpu.VMEM((2, tile, D), jnp.float32),
        ],
        compiler_params=pltpu.CompilerParams(
            dimension_semantics=("arbitrary",), vmem_limit_bytes=VMEM_LIMIT),
        name="prompt_mixer",
    )(x, *experts32, *consts)


def _sample_mixer_kernel(sink_ref, x_ref, ck_ref, cv_ref, sc_ref, g1_ref, win_ref, cw_ref, qg_ref,
                         kg_ref, seg_ref, exp_ref, wco_ref, wao_ref, wo_ref, g2_ref, wr_ref, br_ref,
                         upper_ref, x1_ref, h2p_ref, rt_ref, cnt_ref, kwin_ref, vwin_ref, cst_ref,
                         uext, att_s, kbd, vbd, *, n_stream, seq):
    ext = seq + 8
    x = x_ref[...]
    h = _bf(_rms(x, g1_ref[...]))

    u = _dot(h, win_ref[:, C_C:C_C + D]) * _dot(h, win_ref[:, C_X:C_X + D])
    for s in range(n_stream):
        uext[s * ext + 6:s * ext + 8, :] = sc_ref[s]
        uext[s * ext + 8:(s + 1) * ext, :] = u[s * seq:(s + 1) * seq, :]
        cst_ref[s] = u[(s + 1) * seq - 2:(s + 1) * seq, :]
    conv = jnp.concatenate(
        [uext[s * ext + 6:s * ext + 6 + seq, :] * cw_ref[0:1, :]
         + uext[s * ext + 7:s * ext + 7 + seq, :] * cw_ref[1:2, :] for s in range(n_stream)],
        axis=0) + u * cw_ref[2:3, :]
    yc = _bf(_dot(h, win_ref[:, C_B:C_B + D]) * conv)
    o_c = _dot(yc, wco_ref[...])
    macc = _sigmoid(_dot(h, win_ref[:, C_G:C_G + D])) * o_c

    q = _dot(h, win_ref[:, C_Q:C_Q + D])
    qn = _bf(_head_norm_q(q, seg_ref, exp_ref, qg_ref[...]) * (1.0 / math.sqrt(HEAD_DIM)))
    kv = _dot(h, win_ref[:, C_K:C_K + 2 * LANES])
    kn = _head_norm_k(kv[:, 0:LANES], kg_ref[...])
    vv = kv[:, LANES:2 * LANES]

    kbd[...] = jnp.zeros_like(kbd)
    vbd[...] = jnp.zeros_like(vbd)
    lane = lax.broadcasted_iota(jnp.int32, (1, QUAD * KEYS), 1)
    valid = (lane % KEYS) < WINDOW + seq
    for s in range(n_stream):
        rows = slice(s * seq, (s + 1) * seq)
        kwin_ref[s, 0:WINDOW - seq, :] = ck_ref[s, seq:WINDOW, :]
        kwin_ref[s, WINDOW - seq:WINDOW, :] = kn[rows, :]
        vwin_ref[s, 0:WINDOW - seq, :] = cv_ref[s, seq:WINDOW, :]
        vwin_ref[s, WINDOW - seq:WINDOW, :] = vv[rows, :]
        kc4 = _rep4(ck_ref[s])
        vc4 = _rep4(cv_ref[s])
        kn4 = _rep4(kn[rows, :])
        vn4 = _rep4(vv[rows, :])
        for g in range(N_KV):
            _write_bd(kbd, g, 0, kc4[g])
            _write_bd(kbd, g, WINDOW, kn4[g])
            _write_bd(vbd, g, 0, vc4[g])
            _write_bd(vbd, g, WINDOW, vn4[g])
        for g in range(N_KV):
            kb = kbd[g]
            vb = vbd[g]
            for j in range(N_HEADS // N_KV // QUAD):
                h0 = g * (N_HEADS // N_KV) + j * QUAD
                qq = qn[rows, h0 * HEAD_DIM:(h0 + QUAD) * HEAD_DIM]
                sinks = [sink_ref[h0 + i] for i in range(QUAD)]
                o = _attend(qq, kb, vb, valid, sinks)
                att_s[rows, h0 * HEAD_DIM:(h0 + QUAD) * HEAD_DIM] = _bf(o)

    o_a = _dot(att_s[...], wao_ref[...])
    macc = macc + _sigmoid(_dot(h, win_ref[:, C_G + D:C_G + 2 * D])) * o_a
    carry = _merge_tail(x, macc, wo_ref, g2_ref, wr_ref, br_ref, upper_ref,
                        jnp.zeros((N_EXPERTS, 1), jnp.float32), x1_ref, h2p_ref, rt_ref.at[0])
    cnt_ref[0] = jnp.broadcast_to(carry, (N_EXPERTS, LANES))


def _sample_mixer(x, ck, cv, sc, sinks, wts):
    nb, seq, _ = x.shape
    rows = nb * seq
    x2 = x.reshape(rows, D)
    ins = (x2, ck, cv, sc) + tuple(wts)
    full = lambda a: pl.BlockSpec(a.shape, lambda *_, nd=a.ndim: (0,) * nd)
    out_shape = [
        jax.ShapeDtypeStruct((rows, D), jnp.float32),
        jax.ShapeDtypeStruct((rows * PK, LANES), jnp.uint32),
        jax.ShapeDtypeStruct((1, REC_ROWS, rows), jnp.float32),
        jax.ShapeDtypeStruct((1, N_EXPERTS, LANES), jnp.float32),
        jax.ShapeDtypeStruct((nb, WINDOW, LANES), jnp.float32),
        jax.ShapeDtypeStruct((nb, WINDOW, LANES), jnp.float32),
        jax.ShapeDtypeStruct((nb, 2, D), jnp.float32),
    ]
    grid_spec = pltpu.PrefetchScalarGridSpec(
        num_scalar_prefetch=1,
        grid=(1,),
        in_specs=[full(a) for a in ins],
        out_specs=[full(a) for a in out_shape],
        scratch_shapes=[
            pltpu.VMEM((nb * (seq + 8), D), jnp.float32),
            pltpu.VMEM((rows, D), jnp.bfloat16),
            pltpu.VMEM((N_KV, QUAD * KEYS, QUAD_W), jnp.bfloat16),
            pltpu.VMEM((N_KV, QUAD * KEYS, QUAD_W), jnp.bfloat16),
        ])
    return pl.pallas_call(
        functools.partial(_sample_mixer_kernel, n_stream=nb, seq=seq),
        grid_spec=grid_spec,
        out_shape=out_shape,
        compiler_params=pltpu.CompilerParams(
            dimension_semantics=("arbitrary",), vmem_limit_bytes=VMEM_LIMIT),
        name="sample_mixer",
    )(sinks, *ins)


def _sorted_rows(blk_tokens, mt):
    rows = 2 * blk_tokens + N_EXPERTS * (mt // 2 - 1) + mt
    return -(-rows // mt) * mt


def _moe_kernel(pos_ref, tab_ref, stp_ref, h2p_ref, x1_ref, rt_ref, wg_ref, wu_ref, wd_ref, y_ref,
                xo, g1, g2, *, tile, n_tiles, disp_tiles, max_tiles, mt):
    b = pl.program_id(0)
    s = pl.program_id(1)
    n_groups = tile // UNROLL
    n_disp = n_tiles // disp_tiles

    @pl.when((b == 0) & (s == 0))
    def _():
        xo[...] = jnp.zeros_like(xo)

    @pl.when(s < n_disp)
    def _dispatch():
        def group(gi, carry):
            sub = gi // n_groups
            i0 = ((b * n_tiles + s * disp_tiles + sub) * TOP_K * tile
                  + (gi - sub * n_groups) * UNROLL)
            t0 = gi * (UNROLL * PK)
            for k in range(UNROLL):
                row = h2p_ref[pl.ds(pl.multiple_of(t0 + k * PK, PK), PK), :]
                xo[pl.ds(pl.multiple_of(pos_ref[i0 + k], PK), PK), :] = row
                xo[pl.ds(pl.multiple_of(pos_ref[i0 + tile + k], PK), PK), :] = row
            return carry

        lax.fori_loop(0, n_groups * disp_tiles, group, 0)

    @pl.when((s >= n_disp) & (s < n_disp + EXPERT_STEPS))
    def _experts():
        st = b * EXPERT_STEPS + s - n_disp
        first = stp_ref[2 * st]
        last = stp_ref[2 * st + 1]
        half = mt // 2

        def entry(j):
            t = (b * max_tiles + j) * 3
            return pl.multiple_of(tab_ref[t], half * PK), tab_ref[t + 1], tab_ref[t + 2]

        def up(j):
            base, k, _ = entry(j)
            xs = _bf(_unpack_rows(
                [xo[pl.ds(base + q, mt, stride=PK), :] for q in range(PK)]))
            gt = _dot(xs, wg_ref[k])
            return _bf(gt * _sigmoid(gt) * _dot(xs, wu_ref[k]))

        def down(j, hid):
            base, k, full = entry(j)
            out = _pack_rows(_dot(hid, wd_ref[k]))
            def put(lo):
                for q in range(PK):
                    xo[pl.ds(base + lo * PK + q, half, stride=PK), :] = (
                        out[lo:lo + half, q * LANES:(q + 1) * LANES])

            put(0)

            @pl.when(full > 0)
            def _():
                put(half)

        @pl.when(last > first)
        def _():
            def both(j, hid):
                nxt = up(j)
                down(j - 1, hid)
                return nxt

            down(last - 1, lax.fori_loop(first + 1, last, both, up(first)))

    @pl.when(s >= n_disp + EXPERT_STEPS)
    def _combine():
        i = s - n_disp - EXPERT_STEPS
        base = (b * n_tiles + i) * (TOP_K * tile)
        rec = jnp.concatenate([rt_ref[0], jnp.zeros((LANES - REC_ROWS, tile), jnp.float32)], axis=0)
        cw = rec.T
        part = tile // COMBINE_PARTS
        for p in range(COMBINE_PARTS):
            for k in range(part):
                t = p * part + k
                dst = pl.ds(t * PK, PK)
                g1[dst, :] = xo[pl.ds(pl.multiple_of(pos_ref[base + t], PK), PK), :]
                g2[dst, :] = xo[pl.ds(pl.multiple_of(pos_ref[base + tile + t], PK), PK), :]
            rows = pl.ds(p * part, part)
            o1 = _unpack_rows([g1[pl.ds(p * part * PK + q, part, stride=PK), :] for q in range(PK)])
            o2 = _unpack_rows([g2[pl.ds(p * part * PK + q, part, stride=PK), :] for q in range(PK)])
            w = cw[p * part:(p + 1) * part]
            y_ref[rows, :] = x1_ref[rows, :] + w[:, R_W1:R_W1 + 1] * o1 + w[:, R_W2:R_W2 + 1] * o2


def _moe(pos, tab, stp, rt, h2p, x1, wg, wu, wd, tile, n_tiles, mt):
    n = x1.shape[0]
    n_blocks = n // (tile * n_tiles)
    disp_tiles = min(DISPATCH_TILES, n_tiles)
    n_disp = n_tiles // disp_tiles
    n_steps = n_disp + EXPERT_STEPS + n_tiles

    def disp(b, s, *_):
        return (b * n_disp + jnp.minimum(s, n_disp - 1), 0)

    def comb(b, s, *_):
        return (b * n_tiles + jnp.clip(s - n_disp - EXPERT_STEPS, 0, n_tiles - 1), 0)

    def wmap(b, s, *_):
        return (jnp.clip(s - n_disp, 0, EXPERT_STEPS - 1), 0, 0)

    grid_spec = pltpu.PrefetchScalarGridSpec(
        num_scalar_prefetch=3,
        grid=(n_blocks, n_steps),
        in_specs=[pl.BlockSpec((disp_tiles * tile * PK, LANES), disp),
                  pl.BlockSpec((tile, D), comb),
                  pl.BlockSpec((1, REC_ROWS, tile), lambda b, s, *_: comb(b, s) + (0,)),
                  pl.BlockSpec((STEP_EXPERTS, D, D_EXPERT), wmap),
                  pl.BlockSpec((STEP_EXPERTS, D, D_EXPERT), wmap),
                  pl.BlockSpec((STEP_EXPERTS, D_EXPERT, D), wmap)],
        out_specs=pl.BlockSpec((tile, D), comb),
        scratch_shapes=[
            pltpu.VMEM((_sorted_rows(tile * n_tiles, mt) * PK, LANES), jnp.uint32),
            pltpu.VMEM((tile * PK, LANES), jnp.uint32),
            pltpu.VMEM((tile * PK, LANES), jnp.uint32),
        ])
    return pl.pallas_call(
        functools.partial(_moe_kernel, tile=tile, n_tiles=n_tiles, disp_tiles=disp_tiles,
                          max_tiles=_max_tiles(tile * n_tiles, mt), mt=mt),
        grid_spec=grid_spec,
        out_shape=jax.ShapeDtypeStruct((n, D), jnp.float32),
        compiler_params=pltpu.CompilerParams(
            dimension_semantics=("arbitrary", "arbitrary"), vmem_limit_bytes=VMEM_LIMIT),
        name="moe",
    )(pos, tab, stp, h2p, x1, rt, wg, wu, wd)


def _max_tiles(blk_tokens, mt):
    return N_EXPERTS + 2 * blk_tokens // mt


def _sorted_positions(rt, cnt, blk_tokens, mt):
    experts = jnp.arange(N_EXPERTS, dtype=jnp.int32)
    n_tiles, _, tile = rt.shape
    counts = cnt[:, :, 0].astype(jnp.int32)
    pad = mt // 2
    padded = (counts + pad - 1) // pad * pad
    off = jnp.cumsum(padded, axis=1) - padded
    e = rt[:, R_E1:R_E1 + TOP_K, :].astype(jnp.int32)
    rank = rt[:, R_RANK1:R_RANK1 + TOP_K, :].astype(jnp.int32)
    off_tile = jnp.repeat(off, blk_tokens // tile, axis=0)
    hit = e[:, :, None, :] == experts[:, None]
    pos = (rank + jnp.sum(jnp.where(hit, off_tile[:, None, :, None], 0), axis=2)) * PK

    tiles = (counts + mt - 1) // mt
    t_end = jnp.cumsum(tiles, axis=1)
    t_beg = t_end - tiles
    j = jnp.arange(_max_tiles(blk_tokens, mt), dtype=jnp.int32)
    e_of = jnp.minimum(jnp.sum(t_end[:, None, :] <= j[None, :, None], axis=-1), N_EXPERTS - 1)
    pick = lambda a: jnp.sum(jnp.where(e_of[:, :, None] == experts, a[:, None, :], 0), axis=-1)
    local = (j[None, :] - pick(t_beg)) * mt
    tab = jnp.stack([(pick(off) + local) * PK, e_of % STEP_EXPERTS,
                     (pick(counts) - local > pad).astype(jnp.int32)], axis=-1)
    stp = jnp.stack([t_beg[:, ::STEP_EXPERTS], t_end[:, STEP_EXPERTS - 1::STEP_EXPERTS]], axis=-1)
    return pos.reshape(-1), tab.reshape(-1), stp.reshape(-1), rt


def _segment_matrices():
    head = np.arange(D) // HEAD_DIM
    seg = head[:, None] == np.arange(LANES)[None, :]
    expm = (np.arange(2 * LANES) % LANES)[:, None] == head[None, :]
    return jnp.asarray(seg, jnp.bfloat16), jnp.asarray(expm, jnp.bfloat16)


def _strict_upper(n):
    r = np.arange(n)
    return jnp.asarray(r[:, None] < r[None, :], jnp.bfloat16)


def kernel(x_prompt, x_sample, cache_k, cache_v, state_conv, norm_mix_g, w_in, conv_w, q_norm_g,
           k_norm_g, attn_sinks, w_conv_out, w_attn_out, w_out, norm_ffn_g, w_group, b_group,
           w_router, b_router, w_gate, w_up, w_down):
    nb, seq, _ = x_prompt.shape
    ns, sseq, _ = x_sample.shape
    l = 0
    seg, expm = _segment_matrices()
    pad = RT_ROWS - N_EXPERTS - N_GROUPS
    wr = jnp.concatenate([w_router[l].T, w_group[l].T, jnp.zeros((pad, D), jnp.float32)], axis=0)
    br = jnp.concatenate([b_router[l], b_group[l], jnp.zeros((pad,), jnp.float32)]).reshape(RT_ROWS, 1)
    wts = (norm_mix_g[l].reshape(1, D), _bf(w_in[l]), conv_w[l],
           jnp.tile(q_norm_g[l], N_HEADS).reshape(1, D), jnp.tile(k_norm_g[l], N_KV).reshape(1, LANES),
           seg, expm, _bf(w_conv_out[l]), _bf(w_attn_out[l]), _bf(w_out[l]),
           norm_ffn_g[l].reshape(1, D), _bf(wr), br)
    sinks = attn_sinks[l]
    n_s = ns * sseq

    sink2 = sinks * LOG2E
    sink_hi = _bf(sink2).astype(jnp.float32)
    q_extra = jnp.concatenate(
        [jnp.ones((N_HEADS, 1), jnp.float32), sink_hi[:, None], (sink2 - sink_hi)[:, None],
         jnp.zeros((N_HEADS, HEAD_DIM - 3), jnp.float32)], axis=1).reshape(N_HEADS, 1, HEAD_DIM)
    x1p, h2pp, rtp, cntp, kwp, vwp, cstp, wg, wu, wd = _prompt_mixer(
        x_prompt, (w_gate[l], w_up[l], w_down[l]), q_extra, wts + (_strict_upper(TOK_TILE),), tile=TOK_TILE, blk_tiles=BLOCK_TILES)
    x1s, h2ps, rts, cnts, kws, vws, csts = _sample_mixer(
        x_sample, cache_k[l].reshape(ns, WINDOW, LANES), cache_v[l].reshape(ns, WINDOW, LANES),
        state_conv[l], sinks, wts + (_strict_upper(n_s),))

    yp = _moe(*_sorted_positions(rtp, cntp, TOK_TILE * BLOCK_TILES, MOE_TILE), h2pp, x1p.reshape(nb * seq, D), wg, wu, wd,
              tile=TOK_TILE, n_tiles=BLOCK_TILES, mt=MOE_TILE)
    ys = _moe(*_sorted_positions(rts, cnts, n_s, SAMPLE_MOE_TILE), h2ps, x1s, wg, wu, wd, tile=n_s, n_tiles=1,
              mt=SAMPLE_MOE_TILE)

    kv_shape = lambda n: (1, n, WINDOW, N_KV, HEAD_DIM)
    return (yp.reshape(nb, seq, D), ys.reshape(ns, sseq, D),
            kwp.reshape(kv_shape(nb)), vwp.reshape(kv_shape(nb)), cstp.reshape(1, nb, 2, D),
            kws.reshape(kv_shape(ns)), vws.reshape(kv_shape(ns)), csts.reshape(1, ns, 2, D))
```

```python
import functools
import math

import jax
import jax.numpy as jnp
import numpy as np
from jax import lax
from jax.experimental import pallas as pl
from jax.experimental.pallas import tpu as pltpu

D = 1024
CHUNK = 64
HEAD_DIM = 64
N_HEADS = 16
N_KV = 2
WINDOW = 128
N_GROUPS = 4
EPG = 8
N_EXPERTS = 32
TOP_K = 2
D_EXPERT = 256
EPS = 1e-6
NEG = -1e30

LANES = 128
QUAD = 4
QUAD_W = QUAD * HEAD_DIM
SLOTS = 4
KEYS = SLOTS * CHUNK
GROUP_HEADS = N_HEADS // N_KV
STACK = GROUP_HEADS * CHUNK
BIAS_LANE = HEAD_DIM
SINK_LANE = HEAD_DIM + 1
LOG2E = 1.0 / math.log(2.0)
RT_ROWS = LANES
REC_ROWS = 8
HALF = D // 2
PK = HALF // LANES

R_E1, R_E2, R_W1, R_W2, R_RANK1, R_RANK2 = 0, 1, 2, 3, 4, 5

C_B, C_C, C_X, C_Q, C_K, C_V, C_G = 0, 1024, 2048, 3072, 4096, 4224, 4352

MOE_TILE = 288
SAMPLE_MOE_TILE = 64
TOK_TILE = 512
BLOCK_TILES = 8
UNROLL = 16
COMBINE_PARTS = 4
STEP_EXPERTS = 4
EXPERT_STEPS = N_EXPERTS // STEP_EXPERTS

VMEM_LIMIT = 60 * 1024 * 1024

_NT = (((1,), (1,)), ((), ()))


def _bf(x):
    return x.astype(jnp.bfloat16)


def _dot(a, b):
    return jnp.dot(a, b, preferred_element_type=jnp.float32)


def _rms(x, g):
    return x * lax.rsqrt(jnp.mean(x * x, axis=-1, keepdims=True) + EPS) * g


def _sigmoid(x):
    return 1.0 / (1.0 + jnp.exp(-x))


def _pack_rows(x):
    return pltpu.pack_elementwise([x[:, :HALF], x[:, HALF:]], packed_dtype=jnp.bfloat16)


def _unpack_rows(parts):
    def half(i):
        return [pltpu.unpack_elementwise(p, index=i, packed_dtype=jnp.bfloat16,
                                         unpacked_dtype=jnp.float32) for p in parts]
    return jnp.concatenate(half(0) + half(1), axis=-1)


def _head_norm_q(q, seg_ref, exp_ref, qg):
    ssq = _dot(_bf(q * q), seg_ref[...])
    inv = lax.rsqrt(ssq * (1.0 / HEAD_DIM) + EPS)
    hi = _bf(inv)
    lo = _bf(inv - hi.astype(jnp.float32))
    full = _dot(jnp.concatenate([hi, lo], axis=-1), exp_ref[...])
    return q * full * qg


def _head_norm_k(k, kg):
    lane = lax.broadcasted_iota(jnp.int32, k.shape, 1)
    lo_half = lane < HEAD_DIM
    sq = k * k
    s0 = jnp.sum(jnp.where(lo_half, sq, 0.0), axis=-1, keepdims=True)
    s1 = jnp.sum(jnp.where(lo_half, 0.0, sq), axis=-1, keepdims=True)
    i0 = lax.rsqrt(s0 * (1.0 / HEAD_DIM) + EPS)
    i1 = lax.rsqrt(s1 * (1.0 / HEAD_DIM) + EPS)
    return k * jnp.where(lo_half, i0, i1) * kg


def _rep4(x):
    lane = lax.broadcasted_iota(jnp.int32, x.shape, 1)
    sw = pltpu.roll(x, HEAD_DIM, axis=1)
    a2 = jnp.where(lane < HEAD_DIM, x, sw)
    b2 = jnp.where(lane < HEAD_DIM, sw, x)
    return jnp.concatenate([a2, a2], axis=-1), jnp.concatenate([b2, b2], axis=-1)


def _write_bd(ref, g, row0, x4):
    rows = x4.shape[0]
    lane = lax.broadcasted_iota(jnp.int32, x4.shape, 1)
    for i in range(QUAD):
        keep = (lane >= i * HEAD_DIM) & (lane < (i + 1) * HEAD_DIM)
        ref[g, pl.ds(i * KEYS + row0, rows), :] = _bf(jnp.where(keep, x4, 0.0))


def _attend(qq, kbd, vbd, valid, sinks):
    s = lax.dot_general(qq, kbd, _NT, preferred_element_type=jnp.float32)
    s = jnp.where(valid, s, NEG)
    ps = []
    for i in range(QUAD):
        sh = s[:, i * KEYS:(i + 1) * KEYS]
        m = jnp.maximum(jnp.max(sh, axis=-1, keepdims=True), sinks[i])
        p = jnp.exp(sh - m)
        den = jnp.sum(p, axis=-1, keepdims=True) + jnp.exp(sinks[i] - m)
        ps.append(_bf(p / den))
    return _dot(jnp.concatenate(ps, axis=-1), vbd)


def _route(lt, upper_ref, carry):
    t = lt.shape[1]
    big = float(RT_ROWS)
    grow = lax.broadcasted_iota(jnp.int32, (REC_ROWS, t), 0).astype(jnp.float32)
    is_g = grow < N_GROUPS
    gl = jnp.where(is_g, lt[N_EXPERTS:N_EXPERTS + REC_ROWS], NEG)
    gmax = jnp.max(gl, axis=0, keepdims=True)
    gsum = jnp.sum(jnp.where(is_g, jnp.exp(gl - gmax), 0.0), axis=0, keepdims=True)
    g_w = 1.0 / gsum
    g_sel = jnp.min(jnp.where(is_g & (gl == gmax), grow, big), axis=0, keepdims=True)
    logits = lt[0:N_EXPERTS]
    erow = lax.broadcasted_iota(jnp.int32, (N_EXPERTS, t), 0).astype(jnp.float32)
    lo = g_sel * EPG
    in_grp = (erow >= lo) & (erow < lo + EPG)
    el = jnp.where(in_grp, logits, NEG)
    v1 = jnp.max(el, axis=0, keepdims=True)
    i1 = jnp.min(jnp.where(in_grp & (el == v1), erow, big), axis=0, keepdims=True)
    rest = in_grp & (erow != i1)
    el2 = jnp.where(rest, logits, NEG)
    v2 = jnp.max(el2, axis=0, keepdims=True)
    i2 = jnp.min(jnp.where(rest & (el2 == v2), erow, big), axis=0, keepdims=True)
    e2 = jnp.exp(v2 - v1)
    den = 1.0 + e2
    w1 = (1.0 / den) * g_w
    w2 = (e2 / den) * g_w
    oh1 = jnp.where(erow == i1, 1.0, 0.0)
    oh2 = jnp.where(erow == i2, 1.0, 0.0)
    both = oh1 + oh2
    before = _dot(_bf(both), upper_ref[...]) + carry
    r1 = jnp.sum(before * oh1, axis=0, keepdims=True)
    r2 = jnp.sum(before * oh2, axis=0, keepdims=True)
    rec = jnp.zeros((REC_ROWS, t), jnp.float32)
    for rw, val in ((R_E1, i1), (R_E2, i2), (R_W1, w1), (R_W2, w2), (R_RANK1, r1), (R_RANK2, r2)):
        rec = jnp.where(grow == rw, val, rec)
    return rec, carry + jnp.sum(both, axis=1, keepdims=True)


def _route_stage(x1, g2_ref, wr_ref, br_ref, upper_ref, carry, h2p_ref, rt_ref):
    rows = x1.shape[0]
    h2 = _rms(x1, g2_ref[...])
    pk = _pack_rows(h2)
    for j in range(PK):
        h2p_ref[pl.ds(j, rows, stride=PK), :] = pk[:, j * LANES:(j + 1) * LANES]
    lt = lax.dot_general(wr_ref[...], _bf(h2), _NT, preferred_element_type=jnp.float32) + br_ref[...]
    rec, carry = _route(lt, upper_ref, carry)
    rt_ref[...] = rec
    return carry


def _merge_tail(x, macc, wo_ref, g2_ref, wr_ref, br_ref, upper_ref, carry, x1_ref, h2p_ref, rt_ref):
    x1 = x + _dot(_bf(macc), wo_ref[...])
    x1_ref[...] = x1
    return _route_stage(x1, g2_ref, wr_ref, br_ref, upper_ref, carry, h2p_ref, rt_ref)


def _prompt_mixer_kernel(x_ref, wg32_ref, wu32_ref, wd32_ref,
                         qx_ref, g1_ref, win_ref, cw_ref, qg_ref, kg_ref, seg_ref, exp_ref,
                         wco_ref, wao_ref, wo_ref, g2_ref, wr_ref, br_ref, upper_ref,
                         x1_ref, h2p_ref, rt_ref, cnt_ref, kwin_ref, vwin_ref, cst_ref,
                         wg16_ref, wu16_ref, wd16_ref,
                         uext, qs_s, att_s, kn_s, kop, vop, khist, vhist, carry_s, x1_s,
                         *, tile, blk_tiles, n_it, n_main):
    step = pl.program_id(0)
    it = step % n_it
    n_chunk = tile // CHUNK

    @pl.when(step == 0)
    def _():
        shape = (n_chunk, N_KV, CHUNK, LANES)
        row = lax.broadcasted_iota(jnp.int32, shape, 2)
        ln = lax.broadcasted_iota(jnp.int32, shape, 3)
        is_sink = (row == 0) & ((ln == SINK_LANE) | (ln == SINK_LANE + 1))
        tail = pl.ds((SLOTS - 1) * CHUNK, CHUNK)
        kop[:, :, tail, :] = _bf(jnp.where(is_sink, 1.0, jnp.where((row > 0) & (ln == BIAS_LANE), NEG, 0.0)))
        vop[:, :, tail, 0:LANES] = jnp.zeros(shape, jnp.bfloat16)
        vop[:, :, tail, LANES:2 * LANES] = jnp.ones(shape, jnp.bfloat16)
        x1_s[1] = jnp.zeros((tile, D), jnp.float32)
        carry_s[...] = jnp.zeros_like(carry_s)

    def route_previous():
        first_of_block = (step + blk_tiles - 1) % blk_tiles == 0
        carry = jnp.where(first_of_block, 0.0, carry_s[...])
        carry = _route_stage(x1_s[(step + 1) % 2], g2_ref, wr_ref, br_ref, upper_ref, carry,
                             h2p_ref, rt_ref.at[0])
        carry_s[...] = carry
        cnt_ref[0] = jnp.broadcast_to(carry, (N_EXPERTS, LANES))

    @pl.when(step == n_main)
    def _():
        route_previous()

    @pl.when((step < n_main) & (it == 0))
    def _():
        uext[0:8, :] = jnp.zeros((8, D), jnp.float32)
        ln = lax.broadcasted_iota(jnp.int32, khist.shape, 2)
        khist[...] = _bf(jnp.where(ln == BIAS_LANE, NEG, 0.0))
        vhist[...] = jnp.zeros_like(vhist)

    @pl.when(step < n_main)
    def _():
        route_previous()
        _mixer_tile(x_ref, wg32_ref, wu32_ref, wd32_ref, qx_ref, g1_ref, win_ref, cw_ref, qg_ref,
                    kg_ref, seg_ref, exp_ref, wco_ref, wao_ref, wo_ref, x1_ref, kwin_ref, vwin_ref,
                    cst_ref, wg16_ref, wu16_ref, wd16_ref, uext, qs_s, att_s, kn_s, kop, vop, khist,
                    vhist, x1_s.at[step % 2], tile=tile)


def _mixer_tile(x_ref, wg32_ref, wu32_ref, wd32_ref, qx_ref, g1_ref, win_ref, cw_ref, qg_ref,
                kg_ref, seg_ref, exp_ref, wco_ref, wao_ref, wo_ref, x1_ref, kwin_ref, vwin_ref,
                cst_ref, wg16_ref, wu16_ref, wd16_ref, uext, qs_s, att_s, kn_s, kop, vop, khist,
                vhist, x1_keep, *, tile):
    n_chunk = tile // CHUNK
    wg16_ref[...] = _bf(wg32_ref[...])
    wu16_ref[...] = _bf(wu32_ref[...])
    wd16_ref[...] = _bf(wd32_ref[...])

    x = x_ref[0]
    h = _bf(_rms(x, g1_ref[...]))

    u = _dot(h, win_ref[:, C_C:C_C + D]) * _dot(h, win_ref[:, C_X:C_X + D])
    uext[8:8 + tile, :] = u
    conv = (uext[6:6 + tile, :] * cw_ref[0:1, :] + uext[7:7 + tile, :] * cw_ref[1:2, :]
            + u * cw_ref[2:3, :])
    yc = _bf(_dot(h, win_ref[:, C_B:C_B + D]) * conv)
    o_c = _dot(yc, wco_ref[...])
    macc = _sigmoid(_dot(h, win_ref[:, C_G:C_G + D])) * o_c
    last2 = uext[tile + 6:tile + 8, :]
    uext[6:8, :] = last2

    cst_ref[0] = last2

    q = _dot(h, win_ref[:, C_Q:C_Q + D])
    qn = _head_norm_q(q, seg_ref, exp_ref, qg_ref[...]) * (LOG2E / math.sqrt(HEAD_DIM))
    for hd in range(N_HEADS):
        g, r = divmod(hd, GROUP_HEADS)
        ext = jnp.broadcast_to(qx_ref[hd], (tile, HEAD_DIM))
        piece = _bf(jnp.concatenate([qn[:, hd * HEAD_DIM:(hd + 1) * HEAD_DIM], ext], axis=-1))
        for c in range(n_chunk):
            qs_s[g, c * STACK + r * CHUNK:c * STACK + (r + 1) * CHUNK, :] = (
                piece[c * CHUNK:(c + 1) * CHUNK, :])
    kv = _dot(h, win_ref[:, C_K:C_K + 2 * LANES])
    kn = _head_norm_k(kv[:, 0:LANES], kg_ref[...])
    vv = kv[:, LANES:2 * LANES]
    kn_s[0] = kn
    kn_s[1] = vv

    kwin_ref[0] = kn[tile - WINDOW:, :]
    vwin_ref[0] = vv[tile - WINDOW:, :]

    lane = lax.broadcasted_iota(jnp.int32, (CHUNK, LANES), 1)
    lo_half = lane < HEAD_DIM
    ones_blk = jnp.ones((CHUNK, LANES), jnp.bfloat16)
    own = SLOTS - 2

    for c in range(own):
        for sl in range(own - c):
            src = pl.ds((c + sl) * CHUNK, CHUNK)
            kop[c, :, pl.ds(sl * CHUNK, CHUNK), :] = khist[:, src, :]
            vop[c, :, pl.ds(sl * CHUNK, CHUNK), :] = vhist[:, src, :]

    def chunk_body(c):
        r0 = c * CHUNK
        kc = kn_s[0, pl.ds(r0, CHUNK), :]
        vc = kn_s[1, pl.ds(r0, CHUNK), :]
        ksw = pltpu.roll(kc, HEAD_DIM, axis=1)
        vsw = pltpu.roll(vc, HEAD_DIM, axis=1)
        for g in range(N_KV):
            kb = _bf(jnp.where(lo_half, kc if g == 0 else ksw, 0.0))
            vb = _bf(jnp.where(lo_half, vc, vsw) if g == 0 else jnp.where(lo_half, vsw, vc))
            for d in range(own + 1):
                if c + d < n_chunk:
                    dst = pl.ds((own - d) * CHUNK, CHUNK)
                    kop[c + d, g, dst, :] = kb
                    vop[c + d, g, dst, 0:LANES] = vb
                    vop[c + d, g, dst, LANES:2 * LANES] = ones_blk
            if c >= n_chunk - own:
                dst = pl.ds((c - (n_chunk - own)) * CHUNK, CHUNK)
                khist[g, dst, :] = kb
                vhist[g, dst, 0:LANES] = vb
                vhist[g, dst, LANES:2 * LANES] = ones_blk
        for g in range(N_KV):
            s = lax.dot_general(qs_s[g, pl.ds(c * STACK, STACK), :], kop[c, g], _NT,
                                preferred_element_type=jnp.float32)
            m = jnp.max(s, axis=-1, keepdims=True)
            o = _dot(_bf(jnp.exp2(s - m)), vop[c, g])
            out = o[:, 0:LANES] / o[:, LANES:2 * LANES]
            for r in range(0, GROUP_HEADS, 2):
                pair = jnp.where(lo_half, out[r * CHUNK:(r + 1) * CHUNK, :],
                                 out[(r + 1) * CHUNK:(r + 2) * CHUNK, :])
                col = (g * GROUP_HEADS + r) * HEAD_DIM
                att_s[pl.ds(r0, CHUNK), col:col + LANES] = _bf(pair)

    for c in range(n_chunk):
        chunk_body(c)

    o_a = _dot(att_s[...], wao_ref[...])
    macc = macc + _sigmoid(_dot(h, win_ref[:, C_G + D:C_G + 2 * D])) * o_a
    x1 = x + _dot(_bf(macc), wo_ref[...])
    x1_ref[0] = x1
    x1_keep[...] = x1


def _const_spec(shape):
    nd = len(shape)
    return pl.BlockSpec(shape, lambda *_: (0,) * nd, pipeline_mode=pl.Buffered(1))


def _prompt_mixer(x, experts32, q_extra, wts, tile, blk_tiles):
    nb, seq, _ = x.shape
    n_it = seq // tile
    n_main = nb * n_it
    bps = n_it // blk_tiles
    parts = n_main // N_EXPERTS
    consts = (q_extra,) + tuple(wts)

    cur = lambda s: jnp.minimum(s, n_main - 1)
    prev = lambda s: jnp.maximum(s - 1, 0)
    row_spec = lambda w: pl.BlockSpec((1, tile, w), lambda s: (cur(s) // n_it, cur(s) % n_it, 0))
    str_spec = lambda r, w: pl.BlockSpec((1, r, w), lambda s: (cur(s) // n_it, 0, 0))
    part_spec = lambda r, w: pl.BlockSpec(
        (1, r // parts, w), lambda s: (cur(s) // parts, cur(s) % parts, 0))
    expert_specs = [part_spec(D, D_EXPERT), part_spec(D, D_EXPERT), part_spec(D_EXPERT, D)]
    out_shape = [
        jax.ShapeDtypeStruct((nb, seq, D), jnp.float32),
        jax.ShapeDtypeStruct((nb * seq * PK, LANES), jnp.uint32),
        jax.ShapeDtypeStruct((n_main, REC_ROWS, tile), jnp.float32),
        jax.ShapeDtypeStruct((nb * bps, N_EXPERTS, LANES), jnp.float32),
        jax.ShapeDtypeStruct((nb, WINDOW, LANES), jnp.float32),
        jax.ShapeDtypeStruct((nb, WINDOW, LANES), jnp.float32),
        jax.ShapeDtypeStruct((nb, 2, D), jnp.float32),
    ] + [jax.ShapeDtypeStruct(w.shape, jnp.bfloat16) for w in experts32]
    return pl.pallas_call(
        functools.partial(_prompt_mixer_kernel, tile=tile, blk_tiles=blk_tiles, n_it=n_it, n_main=n_main),
        grid=(n_main + 1,),
        in_specs=[row_spec(D)] + expert_specs + [_const_spec(w.shape) for w in consts],
        out_specs=[row_spec(D),
                   pl.BlockSpec((tile * PK, LANES), lambda s: (prev(s), 0)),
                   pl.BlockSpec((1, REC_ROWS, tile), lambda s: (prev(s), 0, 0)),
                   pl.BlockSpec((1, N_EXPERTS, LANES), lambda s: (prev(s) // blk_tiles, 0, 0)),
                   str_spec(WINDOW, LANES), str_spec(WINDOW, LANES), str_spec(2, D)] + expert_specs,
        out_shape=out_shape,
        scratch_shapes=[
            pltpu.VMEM((tile + 8, D), jnp.float32),
            pltpu.VMEM((N_KV, tile * GROUP_HEADS, LANES), jnp.bfloat16),
            pltpu.VMEM((tile, D), jnp.bfloat16),
            pltpu.VMEM((2, tile, LANES), jnp.float32),
            pltpu.VMEM((tile // CHUNK, N_KV, KEYS, LANES), jnp.bfloat16),
            pltpu.VMEM((tile // CHUNK, N_KV, KEYS, 2 * LANES), jnp.bfloat16),
            pltpu.VMEM((N_KV, (SLOTS - 2) * CHUNK, LANES), jnp.bfloat16),
            pltpu.VMEM((N_KV, (SLOTS - 2) * CHUNK, 2 * LANES), jnp.bfloat16),
            pltpu.VMEM((N_EXPERTS, 1), jnp.float32),
            pltpu.VMEM((2, tile, D), jnp.float32),
        ],
        compiler_params=pltpu.CompilerParams(
            dimension_semantics=("arbitrary",), vmem_limit_bytes=VMEM_LIMIT),
        name="prompt_mixer",
    )(x, *experts32, *consts)


def _sample_mixer_kernel(sink_ref, x_ref, ck_ref, cv_ref, sc_ref, g1_ref, win_ref, cw_ref, qg_ref,
                         kg_ref, seg_ref, exp_ref, wco_ref, wao_ref, wo_ref, g2_ref, wr_ref, br_ref,
                         upper_ref, x1_ref, h2p_ref, rt_ref, cnt_ref, kwin_ref, vwin_ref, cst_ref,
                         uext, att_s, kbd, vbd, *, n_stream, seq):
    ext = seq + 8
    x = x_ref[...]
    h = _bf(_rms(x, g1_ref[...]))

    u = _dot(h, win_ref[:, C_C:C_C + D]) * _dot(h, win_ref[:, C_X:C_X + D])
    for s in range(n_stream):
        uext[s * ext + 6:s * ext + 8, :] = sc_ref[s]
        uext[s * ext + 8:(s + 1) * ext, :] = u[s * seq:(s + 1) * seq, :]
        cst_ref[s] = u[(s + 1) * seq - 2:(s + 1) * seq, :]
    conv = jnp.concatenate(
        [uext[s * ext + 6:s * ext + 6 + seq, :] * cw_ref[0:1, :]
         + uext[s * ext + 7:s * ext + 7 + seq, :] * cw_ref[1:2, :] for s in range(n_stream)],
        axis=0) + u * cw_ref[2:3, :]
    yc = _bf(_dot(h, win_ref[:, C_B:C_B + D]) * conv)
    o_c = _dot(yc, wco_ref[...])
    macc = _sigmoid(_dot(h, win_ref[:, C_G:C_G + D])) * o_c

    q = _dot(h, win_ref[:, C_Q:C_Q + D])
    qn = _bf(_head_norm_q(q, seg_ref, exp_ref, qg_ref[...]) * (1.0 / math.sqrt(HEAD_DIM)))
    kv = _dot(h, win_ref[:, C_K:C_K + 2 * LANES])
    kn = _head_norm_k(kv[:, 0:LANES], kg_ref[...])
    vv = kv[:, LANES:2 * LANES]

    kbd[...] = jnp.zeros_like(kbd)
    vbd[...] = jnp.zeros_like(vbd)
    lane = lax.broadcasted_iota(jnp.int32, (1, QUAD * KEYS), 1)
    valid = (lane % KEYS) < WINDOW + seq
    for s in range(n_stream):
        rows = slice(s * seq, (s + 1) * seq)
        kwin_ref[s, 0:WINDOW - seq, :] = ck_ref[s, seq:WINDOW, :]
        kwin_ref[s, WINDOW - seq:WINDOW, :] = kn[rows, :]
        vwin_ref[s, 0:WINDOW - seq, :] = cv_ref[s, seq:WINDOW, :]
        vwin_ref[s, WINDOW - seq:WINDOW, :] = vv[rows, :]
        kc4 = _rep4(ck_ref[s])
        vc4 = _rep4(cv_ref[s])
        kn4 = _rep4(kn[rows, :])
        vn4 = _rep4(vv[rows, :])
        for g in range(N_KV):
            _write_bd(kbd, g, 0, kc4[g])
            _write_bd(kbd, g, WINDOW, kn4[g])
            _write_bd(vbd, g, 0, vc4[g])
            _write_bd(vbd, g, WINDOW, vn4[g])
        for g in range(N_KV):
            kb = kbd[g]
            vb = vbd[g]
            for j in range(N_HEADS // N_KV // QUAD):
                h0 = g * (N_HEADS // N_KV) + j * QUAD
                qq = qn[rows, h0 * HEAD_DIM:(h0 + QUAD) * HEAD_DIM]
                sinks = [sink_ref[h0 + i] for i in range(QUAD)]
                o = _attend(qq, kb, vb, valid, sinks)
                att_s[rows, h0 * HEAD_DIM:(h0 + QUAD) * HEAD_DIM] = _bf(o)

    o_a = _dot(att_s[...], wao_ref[...])
    macc = macc + _sigmoid(_dot(h, win_ref[:, C_G + D:C_G + 2 * D])) * o_a
    carry = _merge_tail(x, macc, wo_ref, g2_ref, wr_ref, br_ref, upper_ref,
                        jnp.zeros((N_EXPERTS, 1), jnp.float32), x1_ref, h2p_ref, rt_ref.at[0])
    cnt_ref[0] = jnp.broadcast_to(carry, (N_EXPERTS, LANES))


def _sample_mixer(x, ck, cv, sc, sinks, wts):
    nb, seq, _ = x.shape
    rows = nb * seq
    x2 = x.reshape(rows, D)
    ins = (x2, ck, cv, sc) + tuple(wts)
    full = lambda a: pl.BlockSpec(a.shape, lambda *_, nd=a.ndim: (0,) * nd)
    out_shape = [
        jax.ShapeDtypeStruct((rows, D), jnp.float32),
        jax.ShapeDtypeStruct((rows * PK, LANES), jnp.uint32),
        jax.ShapeDtypeStruct((1, REC_ROWS, rows), jnp.float32),
        jax.ShapeDtypeStruct((1, N_EXPERTS, LANES), jnp.float32),
        jax.ShapeDtypeStruct((nb, WINDOW, LANES), jnp.float32),
        jax.ShapeDtypeStruct((nb, WINDOW, LANES), jnp.float32),
        jax.ShapeDtypeStruct((nb, 2, D), jnp.float32),
    ]
    grid_spec = pltpu.PrefetchScalarGridSpec(
        num_scalar_prefetch=1,
        grid=(1,),
        in_specs=[full(a) for a in ins],
        out_specs=[full(a) for a in out_shape],
        scratch_shapes=[
            pltpu.VMEM((nb * (seq + 8), D), jnp.float32),
            pltpu.VMEM((rows, D), jnp.bfloat16),
            pltpu.VMEM((N_KV, QUAD * KEYS, QUAD_W), jnp.bfloat16),
            pltpu.VMEM((N_KV, QUAD * KEYS, QUAD_W), jnp.bfloat16),
        ])
    return pl.pallas_call(
        functools.partial(_sample_mixer_kernel, n_stream=nb, seq=seq),
        grid_spec=grid_spec,
        out_shape=out_shape,
        compiler_params=pltpu.CompilerParams(
            dimension_semantics=("arbitrary",), vmem_limit_bytes=VMEM_LIMIT),
        name="sample_mixer",
    )(sinks, *ins)


def _sorted_rows(blk_tokens, mt):
    rows = 2 * blk_tokens + N_EXPERTS * (mt // 2 - 1) + mt
    return -(-rows // mt) * mt


def _moe_kernel(pos_ref, tab_ref, stp_ref, h2p_ref, x1_ref, rt_ref, wg_ref, wu_ref, wd_ref, y_ref,
                xo, g1, g2, *, tile, n_tiles, max_tiles, mt):
    b = pl.program_id(0)
    s = pl.program_id(1)
    n_groups = tile // UNROLL
    n_disp = n_tiles

    @pl.when((b == 0) & (s == 0))
    def _():
        xo[...] = jnp.zeros_like(xo)

    @pl.when(s < n_disp)
    def _dispatch():
        base = (b * n_tiles + s) * (TOP_K * tile)

        def group(gi, carry):
            i0 = base + gi * UNROLL
            t0 = gi * (UNROLL * PK)
            for k in range(UNROLL):
                row = h2p_ref[pl.ds(pl.multiple_of(t0 + k * PK, PK), PK), :]
                xo[pl.ds(pl.multiple_of(pos_ref[i0 + k], PK), PK), :] = row
                xo[pl.ds(pl.multiple_of(pos_ref[i0 + tile + k], PK), PK), :] = row
            return carry

        lax.fori_loop(0, n_groups, group, 0)

    @pl.when((s >= n_disp) & (s < n_disp + EXPERT_STEPS))
    def _experts():
        st = b * EXPERT_STEPS + s - n_disp
        first = stp_ref[2 * st]
        last = stp_ref[2 * st + 1]
        half = mt // 2

        def entry(j):
            t = (b * max_tiles + j) * 3
            return pl.multiple_of(tab_ref[t], half * PK), tab_ref[t + 1], tab_ref[t + 2]

        def up(j):
            base, k, _ = entry(j)
            xs = _bf(_unpack_rows(
                [xo[pl.ds(base + q, mt, stride=PK), :] for q in range(PK)]))
            gt = _dot(xs, wg_ref[k])
            return _bf(gt * _sigmoid(gt) * _dot(xs, wu_ref[k]))

        def down(j, hid):
            base, k, full = entry(j)
            out = _pack_rows(_dot(hid, wd_ref[k]))
            def put(lo):
                for q in range(PK):
                    xo[pl.ds(base + lo * PK + q, half, stride=PK), :] = (
                        out[lo:lo + half, q * LANES:(q + 1) * LANES])

            put(0)

            @pl.when(full > 0)
            def _():
                put(half)

        def load(j):
            base, _, _ = entry(j)
            return _bf(_unpack_rows([xo[pl.ds(base + q, mt, stride=PK), :] for q in range(PK)]))

        def swiglu(j, xs):
            _, k, _ = entry(j)
            gt = _dot(xs, wg_ref[k])
            return _bf(gt * _sigmoid(gt) * _dot(xs, wu_ref[k]))

        def store(j, hid):
            base, k, _ = entry(j)
            out = _pack_rows(_dot(hid, wd_ref[k]))
            for q in range(PK):
                xo[pl.ds(base + q, mt, stride=PK), :] = out[:, q * LANES:(q + 1) * LANES]

        n_full = tab_ref[(b * max_tiles + first) * 3 + 2]
        for k in range(1, STEP_EXPERTS):
            n_full = n_full + tab_ref[(b * max_tiles + first + k) * 3 + 2]
        usual = (last - first == STEP_EXPERTS) & (n_full == STEP_EXPERTS)

        @pl.when(usual)
        def _():
            xs = [load(first + k) for k in range(STEP_EXPERTS)]
            for k in range(STEP_EXPERTS):
                store(first + k, swiglu(first + k, xs[k]))

        @pl.when(jnp.logical_not(usual) & (last > first))
        def _():
            def both(j, hid):
                nxt = up(j)
                down(j - 1, hid)
                return nxt

            down(last - 1, lax.fori_loop(first + 1, last, both, up(first)))

    @pl.when(s >= n_disp + EXPERT_STEPS)
    def _combine():
        i = s - n_disp - EXPERT_STEPS
        base = (b * n_tiles + i) * (TOP_K * tile)
        rec = jnp.concatenate([rt_ref[0], jnp.zeros((LANES - REC_ROWS, tile), jnp.float32)], axis=0)
        cw = rec.T
        part = tile // COMBINE_PARTS
        for p in range(COMBINE_PARTS):
            for k in range(part):
                t = p * part + k
                dst = pl.ds(t * PK, PK)
                g1[dst, :] = xo[pl.ds(pl.multiple_of(pos_ref[base + t], PK), PK), :]
                g2[dst, :] = xo[pl.ds(pl.multiple_of(pos_ref[base + tile + t], PK), PK), :]
            rows = pl.ds(p * part, part)
            o1 = _unpack_rows([g1[pl.ds(p * part * PK + q, part, stride=PK), :] for q in range(PK)])
            o2 = _unpack_rows([g2[pl.ds(p * part * PK + q, part, stride=PK), :] for q in range(PK)])
            w = cw[p * part:(p + 1) * part]
            y_ref[rows, :] = x1_ref[rows, :] + w[:, R_W1:R_W1 + 1] * o1 + w[:, R_W2:R_W2 + 1] * o2


def _moe(pos, tab, stp, rt, h2p, x1, wg, wu, wd, tile, n_tiles, mt):
    n = x1.shape[0]
    n_blocks = n // (tile * n_tiles)
    n_disp = n_tiles
    n_steps = n_disp + EXPERT_STEPS + n_tiles

    def disp(b, s, *_):
        return (b * n_disp + jnp.minimum(s, n_disp - 1), 0)

    def comb(b, s, *_):
        return (b * n_tiles + jnp.clip(s - n_disp - EXPERT_STEPS, 0, n_tiles - 1), 0)

    def wmap(b, s, *_):
        return (jnp.clip(s - n_disp, 0, EXPERT_STEPS - 1), 0, 0)

    grid_spec = pltpu.PrefetchScalarGridSpec(
        num_scalar_prefetch=3,
        grid=(n_blocks, n_steps),
        in_specs=[pl.BlockSpec((tile * PK, LANES), disp),
                  pl.BlockSpec((tile, D), comb),
                  pl.BlockSpec((1, REC_ROWS, tile), lambda b, s, *_: comb(b, s) + (0,)),
                  pl.BlockSpec((STEP_EXPERTS, D, D_EXPERT), wmap),
                  pl.BlockSpec((STEP_EXPERTS, D, D_EXPERT), wmap),
                  pl.BlockSpec((STEP_EXPERTS, D_EXPERT, D), wmap)],
        out_specs=pl.BlockSpec((tile, D), comb),
        scratch_shapes=[
            pltpu.VMEM((_sorted_rows(tile * n_tiles, mt) * PK, LANES), jnp.uint32),
            pltpu.VMEM((tile * PK, LANES), jnp.uint32),
            pltpu.VMEM((tile * PK, LANES), jnp.uint32),
        ])
    return pl.pallas_call(
        functools.partial(_moe_kernel, tile=tile, n_tiles=n_tiles,
                          max_tiles=_max_tiles(tile * n_tiles, mt), mt=mt),
        grid_spec=grid_spec,
        out_shape=jax.ShapeDtypeStruct((n, D), jnp.float32),
        compiler_params=pltpu.CompilerParams(
            dimension_semantics=("arbitrary", "arbitrary"), vmem_limit_bytes=VMEM_LIMIT),
        name="moe",
    )(pos, tab, stp, h2p, x1, rt, wg, wu, wd)


def _max_tiles(blk_tokens, mt):
    return N_EXPERTS + 2 * blk_tokens // mt


def _sorted_positions(rt, cnt, blk_tokens, mt):
    experts = jnp.arange(N_EXPERTS, dtype=jnp.int32)
    n_tiles, _, tile = rt.shape
    counts = cnt[:, :, 0].astype(jnp.int32)
    pad = mt // 2
    padded = (counts + pad - 1) // pad * pad
    off = jnp.cumsum(padded, axis=1) - padded
    e = rt[:, R_E1:R_E1 + TOP_K, :].astype(jnp.int32)
    rank = rt[:, R_RANK1:R_RANK1 + TOP_K, :].astype(jnp.int32)
    off_tile = jnp.repeat(off, blk_tokens // tile, axis=0)
    hit = e[:, :, None, :] == experts[:, None]
    pos = (rank + jnp.sum(jnp.where(hit, off_tile[:, None, :, None], 0), axis=2)) * PK

    tiles = (counts + mt - 1) // mt
    t_end = jnp.cumsum(tiles, axis=1)
    t_beg = t_end - tiles
    j = jnp.arange(_max_tiles(blk_tokens, mt), dtype=jnp.int32)
    e_of = jnp.minimum(jnp.sum(t_end[:, None, :] <= j[None, :, None], axis=-1), N_EXPERTS - 1)
    pick = lambda a: jnp.sum(jnp.where(e_of[:, :, None] == experts, a[:, None, :], 0), axis=-1)
    local = (j[None, :] - pick(t_beg)) * mt
    tab = jnp.stack([(pick(off) + local) * PK, e_of % STEP_EXPERTS,
                     (pick(counts) - local > pad).astype(jnp.int32)], axis=-1)
    stp = jnp.stack([t_beg[:, ::STEP_EXPERTS], t_end[:, STEP_EXPERTS - 1::STEP_EXPERTS]], axis=-1)
    return pos.reshape(-1), tab.reshape(-1), stp.reshape(-1), rt


def _segment_matrices():
    head = np.arange(D) // HEAD_DIM
    seg = head[:, None] == np.arange(LANES)[None, :]
    expm = (np.arange(2 * LANES) % LANES)[:, None] == head[None, :]
    return jnp.asarray(seg, jnp.bfloat16), jnp.asarray(expm, jnp.bfloat16)


def _strict_upper(n):
    r = np.arange(n)
    return jnp.asarray(r[:, None] < r[None, :], jnp.bfloat16)


def kernel(x_prompt, x_sample, cache_k, cache_v, state_conv, norm_mix_g, w_in, conv_w, q_norm_g,
           k_norm_g, attn_sinks, w_conv_out, w_attn_out, w_out, norm_ffn_g, w_group, b_group,
           w_router, b_router, w_gate, w_up, w_down):
    nb, seq, _ = x_prompt.shape
    ns, sseq, _ = x_sample.shape
    l = 0
    seg, expm = _segment_matrices()
    pad = RT_ROWS - N_EXPERTS - N_GROUPS
    wr = jnp.concatenate([w_router[l].T, w_group[l].T, jnp.zeros((pad, D), jnp.float32)], axis=0)
    br = jnp.concatenate([b_router[l], b_group[l], jnp.zeros((pad,), jnp.float32)]).reshape(RT_ROWS, 1)
    wts = (norm_mix_g[l].reshape(1, D), _bf(w_in[l]), conv_w[l],
           jnp.tile(q_norm_g[l], N_HEADS).reshape(1, D), jnp.tile(k_norm_g[l], N_KV).reshape(1, LANES),
           seg, expm, _bf(w_conv_out[l]), _bf(w_attn_out[l]), _bf(w_out[l]),
           norm_ffn_g[l].reshape(1, D), _bf(wr), br)
    sinks = attn_sinks[l]
    n_s = ns * sseq

    sink2 = sinks * LOG2E
    sink_hi = _bf(sink2).astype(jnp.float32)
    q_extra = jnp.concatenate(
        [jnp.ones((N_HEADS, 1), jnp.float32), sink_hi[:, None], (sink2 - sink_hi)[:, None],
         jnp.zeros((N_HEADS, HEAD_DIM - 3), jnp.float32)], axis=1).reshape(N_HEADS, 1, HEAD_DIM)
    x1p, h2pp, rtp, cntp, kwp, vwp, cstp, wg, wu, wd = _prompt_mixer(
        x_prompt, (w_gate[l], w_up[l], w_down[l]), q_extra, wts + (_strict_upper(TOK_TILE),), tile=TOK_TILE, blk_tiles=BLOCK_TILES)
    x1s, h2ps, rts, cnts, kws, vws, csts = _sample_mixer(
        x_sample, cache_k[l].reshape(ns, WINDOW, LANES), cache_v[l].reshape(ns, WINDOW, LANES),
        state_conv[l], sinks, wts + (_strict_upper(n_s),))

    yp = _moe(*_sorted_positions(rtp, cntp, TOK_TILE * BLOCK_TILES, MOE_TILE), h2pp, x1p.reshape(nb * seq, D), wg, wu, wd,
              tile=TOK_TILE, n_tiles=BLOCK_TILES, mt=MOE_TILE)
    ys = _moe(*_sorted_positions(rts, cnts, n_s, SAMPLE_MOE_TILE), h2ps, x1s, wg, wu, wd, tile=n_s, n_tiles=1,
              mt=SAMPLE_MOE_TILE)

    kv_shape = lambda n: (1, n, WINDOW, N_KV, HEAD_DIM)
    return (yp.reshape(nb, seq, D), ys.reshape(ns, sseq, D),
            kwp.reshape(kv_shape(nb)), vwp.reshape(kv_shape(nb)), cstp.reshape(1, nb, 2, D),
            kws.reshape(kv_shape(ns)), vws.reshape(kv_shape(ns)), csts.reshape(1, ns, 2, D))
```

```python
import functools
import math

import jax
import jax.numpy as jnp
import numpy as np
from jax import lax
from jax.experimental import pallas as pl
from jax.experimental.pallas import tpu as pltpu

D = 1024
CHUNK = 64
HEAD_DIM = 64
N_HEADS = 16
N_KV = 2
WINDOW = 128
N_GROUPS = 4
EPG = 8
N_EXPERTS = 32
TOP_K = 2
D_EXPERT = 256
EPS = 1e-6
NEG = -1e30

LANES = 128
SUBLANES = 8
CONV_HIST = 2
U0 = SUBLANES
SLOTS = 4
KEYS = SLOTS * CHUNK
GROUP_HEADS = N_HEADS // N_KV
STACK = GROUP_HEADS * CHUNK
BIAS_LANE = HEAD_DIM
SINK_LANE = HEAD_DIM + 1
Q_FEATS = 3
LOG2E = 1.0 / math.log(2.0)
RT_ROWS = LANES
REC_ROWS = 8
HALF = D // 2
PK = HALF // LANES

R_E1, R_E2, R_W1, R_W2, R_RANK1, R_RANK2 = 0, 1, 2, 3, 4, 5

C_B, C_C, C_X, C_Q, C_K, C_V, C_G = 0, 1024, 2048, 3072, 4096, 4224, 4352

MOE_TILE = 288
SAMPLE_MOE_TILE = 64
TOK_TILE = 512
BLOCK_TILES = 8
UNROLL = 32
COMBINE_PARTS = 4
STEP_EXPERTS = 4
TAB_W = 3
EXPERT_STEPS = N_EXPERTS // STEP_EXPERTS

VMEM_LIMIT = 60 * 1024 * 1024

_NT = (((1,), (1,)), ((), ()))


def _bf(x):
    return x.astype(jnp.bfloat16)


def _dot(a, b):
    return jnp.dot(a, b, preferred_element_type=jnp.float32)


def _rms(x, g):
    return x * lax.rsqrt(jnp.mean(x * x, axis=-1, keepdims=True) + EPS) * g


def _sigmoid(x):
    return 1.0 / (1.0 + jnp.exp(-x))


def _pack_rows(x):
    return pltpu.pack_elementwise([x[:, :HALF], x[:, HALF:]], packed_dtype=jnp.bfloat16)


def _unpack_rows(parts):
    def half(i):
        return [pltpu.unpack_elementwise(p, index=i, packed_dtype=jnp.bfloat16,
                                         unpacked_dtype=jnp.float32) for p in parts]
    return jnp.concatenate(half(0) + half(1), axis=-1)


def _head_norm_q(q, seg_ref, exp_ref, qg):
    ssq = _dot(_bf(q * q), seg_ref[...])
    inv = lax.rsqrt(ssq * (1.0 / HEAD_DIM) + EPS)
    hi = _bf(inv)
    lo = _bf(inv - hi.astype(jnp.float32))
    full = _dot(jnp.concatenate([hi, lo], axis=-1), exp_ref[...])
    return q * full * qg


def _head_norm_k(k, kg):
    lane = lax.broadcasted_iota(jnp.int32, k.shape, 1)
    lo_half = lane < HEAD_DIM
    sq = k * k
    s0 = jnp.sum(jnp.where(lo_half, sq, 0.0), axis=-1, keepdims=True)
    s1 = jnp.sum(jnp.where(lo_half, 0.0, sq), axis=-1, keepdims=True)
    i0 = lax.rsqrt(s0 * (1.0 / HEAD_DIM) + EPS)
    i1 = lax.rsqrt(s1 * (1.0 / HEAD_DIM) + EPS)
    return k * jnp.where(lo_half, i0, i1) * kg


def _attend_group(qs, keys, vals):
    s = lax.dot_general(qs, keys, _NT, preferred_element_type=jnp.float32)
    m = jnp.max(s, axis=-1, keepdims=True)
    o = _dot(_bf(jnp.exp2(s - m)), vals)
    return o[:, 0:LANES] / o[:, LANES:2 * LANES]


def _key_rows(k, v, g, lo_half):
    ksw = pltpu.roll(k, HEAD_DIM, axis=1)
    vsw = pltpu.roll(v, HEAD_DIM, axis=1)
    kb = jnp.where(lo_half, k if g == 0 else ksw, 0.0)
    vb = jnp.where(lo_half, v, vsw) if g == 0 else jnp.where(lo_half, vsw, v)
    return _bf(kb), _bf(vb)


def _sink_block(shape):
    row = lax.broadcasted_iota(jnp.int32, shape, len(shape) - 2)
    ln = lax.broadcasted_iota(jnp.int32, shape, len(shape) - 1)
    is_sink = (row == 0) & ((ln == SINK_LANE) | (ln == SINK_LANE + 1))
    return _bf(jnp.where(is_sink, 1.0, jnp.where((row > 0) & (ln == BIAS_LANE), NEG, 0.0)))


def _head_pairs(out, rows_per_head, lo_half):
    blk = lambda r: out[r * rows_per_head:(r + 1) * rows_per_head, :]
    return [_bf(jnp.where(lo_half, blk(r), blk(r + 1))) for r in range(0, GROUP_HEADS, 2)]


def _route(lt, upper_ref, carry):
    t = lt.shape[1]
    big = float(RT_ROWS)
    grow = lax.broadcasted_iota(jnp.int32, (REC_ROWS, t), 0).astype(jnp.float32)
    is_g = grow < N_GROUPS
    gl = jnp.where(is_g, lt[N_EXPERTS:N_EXPERTS + REC_ROWS], NEG)
    gmax = jnp.max(gl, axis=0, keepdims=True)
    gsum = jnp.sum(jnp.where(is_g, jnp.exp(gl - gmax), 0.0), axis=0, keepdims=True)
    g_w = 1.0 / gsum
    g_sel = jnp.min(jnp.where(is_g & (gl == gmax), grow, big), axis=0, keepdims=True)
    logits = lt[0:N_EXPERTS]
    erow = lax.broadcasted_iota(jnp.int32, (N_EXPERTS, t), 0).astype(jnp.float32)
    lo = g_sel * EPG
    in_grp = (erow >= lo) & (erow < lo + EPG)
    el = jnp.where(in_grp, logits, NEG)
    v1 = jnp.max(el, axis=0, keepdims=True)
    i1 = jnp.min(jnp.where(in_grp & (el == v1), erow, big), axis=0, keepdims=True)
    rest = in_grp & (erow != i1)
    el2 = jnp.where(rest, logits, NEG)
    v2 = jnp.max(el2, axis=0, keepdims=True)
    i2 = jnp.min(jnp.where(rest & (el2 == v2), erow, big), axis=0, keepdims=True)
    e2 = jnp.exp(v2 - v1)
    den = 1.0 + e2
    w1 = (1.0 / den) * g_w
    w2 = (e2 / den) * g_w
    oh1 = jnp.where(erow == i1, 1.0, 0.0)
    oh2 = jnp.where(erow == i2, 1.0, 0.0)
    both = oh1 + oh2
    before = _dot(_bf(both), upper_ref[...]) + carry
    r1 = jnp.sum(before * oh1, axis=0, keepdims=True)
    r2 = jnp.sum(before * oh2, axis=0, keepdims=True)
    rec = jnp.zeros((REC_ROWS, t), jnp.float32)
    for rw, val in ((R_E1, i1), (R_E2, i2), (R_W1, w1), (R_W2, w2), (R_RANK1, r1), (R_RANK2, r2)):
        rec = jnp.where(grow == rw, val, rec)
    return rec, carry + jnp.sum(both, axis=1, keepdims=True)


def _route_stage(x1, g2_ref, wr_ref, br_ref, upper_ref, carry, h2p_ref, rt_ref):
    rows = x1.shape[0]
    h2 = _rms(x1, g2_ref[...])
    pk = _pack_rows(h2)
    for j in range(PK):
        h2p_ref[pl.ds(j, rows, stride=PK), :] = pk[:, j * LANES:(j + 1) * LANES]
    lt = lax.dot_general(wr_ref[...], _bf(h2), _NT, preferred_element_type=jnp.float32) + br_ref[...]
    rec, carry = _route(lt, upper_ref, carry)
    rt_ref[...] = rec
    return carry


def _merge_tail(x, macc, wo_ref, g2_ref, wr_ref, br_ref, upper_ref, carry, x1_ref, h2p_ref, rt_ref):
    x1 = x + _dot(_bf(macc), wo_ref[...])
    x1_ref[...] = x1
    return _route_stage(x1, g2_ref, wr_ref, br_ref, upper_ref, carry, h2p_ref, rt_ref)


def _prompt_mixer_kernel(x_ref, wg32_ref, wu32_ref, wd32_ref,
                         qx_ref, g1_ref, win_ref, cw_ref, qg_ref, kg_ref, seg_ref, exp_ref,
                         wco_ref, wao_ref, wo_ref, g2_ref, wr_ref, br_ref, upper_ref,
                         x1_ref, h2p_ref, rt_ref, cnt_ref, kwin_ref, vwin_ref, cst_ref,
                         wg16_ref, wu16_ref, wd16_ref,
                         uext, qs_s, att_s, kn_s, kop, vop, khist, vhist, carry_s, x1_s,
                         *, tile, blk_tiles, n_it, n_main):
    step = pl.program_id(0)
    it = step % n_it
    n_chunk = tile // CHUNK

    @pl.when(step == 0)
    def _():
        shape = (n_chunk, N_KV, CHUNK, LANES)
        tail = pl.ds((SLOTS - 1) * CHUNK, CHUNK)
        kop[:, :, tail, :] = _sink_block(shape)
        vop[:, :, tail, 0:LANES] = jnp.zeros(shape, jnp.bfloat16)
        vop[:, :, tail, LANES:2 * LANES] = jnp.ones(shape, jnp.bfloat16)
        x1_s[1] = jnp.zeros((tile, D), jnp.float32)
        carry_s[...] = jnp.zeros_like(carry_s)

    def route_previous():
        first_of_block = (step + blk_tiles - 1) % blk_tiles == 0
        carry = jnp.where(first_of_block, 0.0, carry_s[...])
        carry = _route_stage(x1_s[(step + 1) % 2], g2_ref, wr_ref, br_ref, upper_ref, carry,
                             h2p_ref, rt_ref.at[0])
        carry_s[...] = carry
        cnt_ref[0] = jnp.broadcast_to(carry, (N_EXPERTS, LANES))

    @pl.when(step == n_main)
    def _():
        route_previous()

    @pl.when((step < n_main) & (it == 0))
    def _():
        uext[0:U0, :] = jnp.zeros((U0, D), jnp.float32)
        ln = lax.broadcasted_iota(jnp.int32, khist.shape, 2)
        khist[...] = _bf(jnp.where(ln == BIAS_LANE, NEG, 0.0))
        vhist[...] = jnp.zeros_like(vhist)

    @pl.when(step < n_main)
    def _():
        route_previous()
        _mixer_tile(x_ref, wg32_ref, wu32_ref, wd32_ref, qx_ref, g1_ref, win_ref, cw_ref, qg_ref,
                    kg_ref, seg_ref, exp_ref, wco_ref, wao_ref, wo_ref, x1_ref, kwin_ref, vwin_ref,
                    cst_ref, wg16_ref, wu16_ref, wd16_ref, uext, qs_s, att_s, kn_s, kop, vop, khist,
                    vhist, x1_s.at[step % 2], tile=tile)


def _mixer_tile(x_ref, wg32_ref, wu32_ref, wd32_ref, qx_ref, g1_ref, win_ref, cw_ref, qg_ref,
                kg_ref, seg_ref, exp_ref, wco_ref, wao_ref, wo_ref, x1_ref, kwin_ref, vwin_ref,
                cst_ref, wg16_ref, wu16_ref, wd16_ref, uext, qs_s, att_s, kn_s, kop, vop, khist,
                vhist, x1_keep, *, tile):
    n_chunk = tile // CHUNK
    wg16_ref[...] = _bf(wg32_ref[...])
    wu16_ref[...] = _bf(wu32_ref[...])
    wd16_ref[...] = _bf(wd32_ref[...])

    x = x_ref[0]
    h = _bf(_rms(x, g1_ref[...]))

    u = _dot(h, win_ref[:, C_C:C_C + D]) * _dot(h, win_ref[:, C_X:C_X + D])
    uext[U0:U0 + tile, :] = u
    conv = (uext[U0 - 2:U0 - 2 + tile, :] * cw_ref[0:1, :] + uext[U0 - 1:U0 - 1 + tile, :] * cw_ref[1:2, :]
            + u * cw_ref[2:3, :])
    yc = _bf(_dot(h, win_ref[:, C_B:C_B + D]) * conv)
    o_c = _dot(yc, wco_ref[...])
    macc = _sigmoid(_dot(h, win_ref[:, C_G:C_G + D])) * o_c
    last2 = uext[U0 + tile - CONV_HIST:U0 + tile, :]
    uext[U0 - CONV_HIST:U0, :] = last2

    cst_ref[0] = last2

    q = _dot(h, win_ref[:, C_Q:C_Q + D])
    qn = _head_norm_q(q, seg_ref, exp_ref, qg_ref[...]) * (LOG2E / math.sqrt(HEAD_DIM))
    for hd in range(N_HEADS):
        g, r = divmod(hd, GROUP_HEADS)
        ext = jnp.broadcast_to(qx_ref[hd], (tile, HEAD_DIM))
        piece = _bf(jnp.concatenate([qn[:, hd * HEAD_DIM:(hd + 1) * HEAD_DIM], ext], axis=-1))
        for c in range(n_chunk):
            qs_s[g, c * STACK + r * CHUNK:c * STACK + (r + 1) * CHUNK, :] = (
                piece[c * CHUNK:(c + 1) * CHUNK, :])
    kv = _dot(h, win_ref[:, C_K:C_K + 2 * LANES])
    kn = _head_norm_k(kv[:, 0:LANES], kg_ref[...])
    vv = kv[:, LANES:2 * LANES]
    kn_s[0] = kn
    kn_s[1] = vv

    kwin_ref[0] = kn[tile - WINDOW:, :]
    vwin_ref[0] = vv[tile - WINDOW:, :]

    lane = lax.broadcasted_iota(jnp.int32, (CHUNK, LANES), 1)
    lo_half = lane < HEAD_DIM
    ones_blk = jnp.ones((CHUNK, LANES), jnp.bfloat16)
    own = SLOTS - 2

    for c in range(own):
        for sl in range(own - c):
            src = pl.ds((c + sl) * CHUNK, CHUNK)
            kop[c, :, pl.ds(sl * CHUNK, CHUNK), :] = khist[:, src, :]
            vop[c, :, pl.ds(sl * CHUNK, CHUNK), :] = vhist[:, src, :]

    def chunk_body(c):
        r0 = c * CHUNK
        kc = kn_s[0, pl.ds(r0, CHUNK), :]
        vc = kn_s[1, pl.ds(r0, CHUNK), :]
        for g in range(N_KV):
            kb, vb = _key_rows(kc, vc, g, lo_half)
            for d in range(own + 1):
                if c + d < n_chunk:
                    dst = pl.ds((own - d) * CHUNK, CHUNK)
                    kop[c + d, g, dst, :] = kb
                    vop[c + d, g, dst, 0:LANES] = vb
                    vop[c + d, g, dst, LANES:2 * LANES] = ones_blk
            if c >= n_chunk - own:
                dst = pl.ds((c - (n_chunk - own)) * CHUNK, CHUNK)
                khist[g, dst, :] = kb
                vhist[g, dst, 0:LANES] = vb
                vhist[g, dst, LANES:2 * LANES] = ones_blk
        for g in range(N_KV):
            out = _attend_group(qs_s[g, pl.ds(c * STACK, STACK), :], kop[c, g], vop[c, g])
            for i, pair in enumerate(_head_pairs(out, CHUNK, lo_half)):
                col = (g * GROUP_HEADS + 2 * i) * HEAD_DIM
                att_s[pl.ds(r0, CHUNK), col:col + LANES] = pair

    for c in range(n_chunk):
        chunk_body(c)

    o_a = _dot(att_s[...], wao_ref[...])
    macc = macc + _sigmoid(_dot(h, win_ref[:, C_G + D:C_G + 2 * D])) * o_a
    x1 = x + _dot(_bf(macc), wo_ref[...])
    x1_ref[0] = x1
    x1_keep[...] = x1


def _const_spec(shape):
    nd = len(shape)
    return pl.BlockSpec(shape, lambda *_: (0,) * nd, pipeline_mode=pl.Buffered(1))


def _prompt_mixer(x, experts32, q_extra, wts, tile, blk_tiles):
    nb, seq, _ = x.shape
    n_it = seq // tile
    n_main = nb * n_it
    bps = n_it // blk_tiles
    parts = n_main // N_EXPERTS
    consts = (q_extra,) + tuple(wts)

    cur = lambda s: jnp.minimum(s, n_main - 1)
    prev = lambda s: jnp.maximum(s - 1, 0)
    row_spec = lambda w: pl.BlockSpec((1, tile, w), lambda s: (cur(s) // n_it, cur(s) % n_it, 0))
    str_spec = lambda r, w: pl.BlockSpec((1, r, w), lambda s: (cur(s) // n_it, 0, 0))
    part_spec = lambda r, w: pl.BlockSpec(
        (1, r // parts, w), lambda s: (cur(s) // parts, cur(s) % parts, 0))
    expert_specs = [part_spec(D, D_EXPERT), part_spec(D, D_EXPERT), part_spec(D_EXPERT, D)]
    out_shape = [
        jax.ShapeDtypeStruct((nb, seq, D), jnp.float32),
        jax.ShapeDtypeStruct((nb * seq * PK, LANES), jnp.uint32),
        jax.ShapeDtypeStruct((n_main, REC_ROWS, tile), jnp.float32),
        jax.ShapeDtypeStruct((nb * bps, N_EXPERTS, LANES), jnp.float32),
        jax.ShapeDtypeStruct((nb, WINDOW, LANES), jnp.float32),
        jax.ShapeDtypeStruct((nb, WINDOW, LANES), jnp.float32),
        jax.ShapeDtypeStruct((nb, CONV_HIST, D), jnp.float32),
    ] + [jax.ShapeDtypeStruct(w.shape, jnp.bfloat16) for w in experts32]
    return pl.pallas_call(
        functools.partial(_prompt_mixer_kernel, tile=tile, blk_tiles=blk_tiles, n_it=n_it, n_main=n_main),
        grid=(n_main + 1,),
        in_specs=[row_spec(D)] + expert_specs + [_const_spec(w.shape) for w in consts],
        out_specs=[row_spec(D),
                   pl.BlockSpec((tile * PK, LANES), lambda s: (prev(s), 0)),
                   pl.BlockSpec((1, REC_ROWS, tile), lambda s: (prev(s), 0, 0)),
                   pl.BlockSpec((1, N_EXPERTS, LANES), lambda s: (prev(s) // blk_tiles, 0, 0)),
                   str_spec(WINDOW, LANES), str_spec(WINDOW, LANES), str_spec(CONV_HIST, D)] + expert_specs,
        out_shape=out_shape,
        scratch_shapes=[
            pltpu.VMEM((U0 + tile, D), jnp.float32),
            pltpu.VMEM((N_KV, tile * GROUP_HEADS, LANES), jnp.bfloat16),
            pltpu.VMEM((tile, D), jnp.bfloat16),
            pltpu.VMEM((2, tile, LANES), jnp.float32),
            pltpu.VMEM((tile // CHUNK, N_KV, KEYS, LANES), jnp.bfloat16),
            pltpu.VMEM((tile // CHUNK, N_KV, KEYS, 2 * LANES), jnp.bfloat16),
            pltpu.VMEM((N_KV, (SLOTS - 2) * CHUNK, LANES), jnp.bfloat16),
            pltpu.VMEM((N_KV, (SLOTS - 2) * CHUNK, 2 * LANES), jnp.bfloat16),
            pltpu.VMEM((N_EXPERTS, 1), jnp.float32),
            pltpu.VMEM((2, tile, D), jnp.float32),
        ],
        compiler_params=pltpu.CompilerParams(
            dimension_semantics=("arbitrary",), vmem_limit_bytes=VMEM_LIMIT),
        name="prompt_mixer",
    )(x, *experts32, *consts)


def _sample_mixer_kernel(x_ref, ck_ref, cv_ref, sc_ref, qx_ref, g1_ref, win_ref, cw_ref, qg_ref,
                         kg_ref, seg_ref, exp_ref, wco_ref, wao_ref, wo_ref, g2_ref, wr_ref, br_ref,
                         upper_ref, x1_ref, h2p_ref, rt_ref, cnt_ref, kwin_ref, vwin_ref, cst_ref,
                         uext, att_s, qs_s, kop, vop, *, n_stream, seq):
    ext = U0 + seq
    n_keys = WINDOW + seq
    x = x_ref[...]
    h = _bf(_rms(x, g1_ref[...]))

    u = _dot(h, win_ref[:, C_C:C_C + D]) * _dot(h, win_ref[:, C_X:C_X + D])
    for s in range(n_stream):
        r0 = s * ext + U0
        uext[r0 - CONV_HIST:r0, :] = sc_ref[s]
        uext[r0:r0 + seq, :] = u[s * seq:(s + 1) * seq, :]
        cst_ref[s] = u[(s + 1) * seq - CONV_HIST:(s + 1) * seq, :]
    conv = jnp.concatenate(
        [uext[s * ext + U0 - 2:s * ext + U0 - 2 + seq, :] * cw_ref[0:1, :]
         + uext[s * ext + U0 - 1:s * ext + U0 - 1 + seq, :] * cw_ref[1:2, :] for s in range(n_stream)],
        axis=0) + u * cw_ref[2:3, :]
    yc = _bf(_dot(h, win_ref[:, C_B:C_B + D]) * conv)
    o_c = _dot(yc, wco_ref[...])
    macc = _sigmoid(_dot(h, win_ref[:, C_G:C_G + D])) * o_c

    q = _dot(h, win_ref[:, C_Q:C_Q + D])
    qn = _head_norm_q(q, seg_ref, exp_ref, qg_ref[...]) * (LOG2E / math.sqrt(HEAD_DIM))
    kv = _dot(h, win_ref[:, C_K:C_K + 2 * LANES])
    kn = _head_norm_k(kv[:, 0:LANES], kg_ref[...])
    vv = kv[:, LANES:2 * LANES]

    lo_win = lax.broadcasted_iota(jnp.int32, (WINDOW, LANES), 1) < HEAD_DIM
    lo_new = lax.broadcasted_iota(jnp.int32, (seq, LANES), 1) < HEAD_DIM
    tail = KEYS - n_keys
    sink_rows = _sink_block((tail, LANES))
    for s in range(n_stream):
        for g in range(N_KV):
            kop[s, g, n_keys:KEYS, :] = sink_rows
            vop[s, g, n_keys:KEYS, 0:LANES] = jnp.zeros((tail, LANES), jnp.bfloat16)
            vop[s, g, :, LANES:2 * LANES] = jnp.ones((KEYS, LANES), jnp.bfloat16)
        rows = slice(s * seq, (s + 1) * seq)
        kwin_ref[s, 0:WINDOW - seq, :] = ck_ref[s, seq:WINDOW, :]
        kwin_ref[s, WINDOW - seq:WINDOW, :] = kn[rows, :]
        vwin_ref[s, 0:WINDOW - seq, :] = cv_ref[s, seq:WINDOW, :]
        vwin_ref[s, WINDOW - seq:WINDOW, :] = vv[rows, :]
        for g in range(N_KV):
            kop[s, g, 0:WINDOW, :], vop[s, g, 0:WINDOW, 0:LANES] = _key_rows(ck_ref[s], cv_ref[s], g, lo_win)
            kop[s, g, WINDOW:n_keys, :], vop[s, g, WINDOW:n_keys, 0:LANES] = _key_rows(
                kn[rows, :], vv[rows, :], g, lo_new)
        for hd in range(N_HEADS):
            g, r = divmod(hd, GROUP_HEADS)
            feat = jnp.broadcast_to(qx_ref[hd], (seq, HEAD_DIM))
            qs_s[s, g, r * seq:(r + 1) * seq, :] = _bf(
                jnp.concatenate([qn[rows, hd * HEAD_DIM:(hd + 1) * HEAD_DIM], feat], axis=-1))
        for g in range(N_KV):
            out = _attend_group(qs_s[s, g], kop[s, g], vop[s, g])
            for i, pair in enumerate(_head_pairs(out, seq, lo_new)):
                col = (g * GROUP_HEADS + 2 * i) * HEAD_DIM
                att_s[rows, col:col + LANES] = pair

    o_a = _dot(att_s[...], wao_ref[...])
    macc = macc + _sigmoid(_dot(h, win_ref[:, C_G + D:C_G + 2 * D])) * o_a
    carry = _merge_tail(x, macc, wo_ref, g2_ref, wr_ref, br_ref, upper_ref,
                        jnp.zeros((N_EXPERTS, 1), jnp.float32), x1_ref, h2p_ref, rt_ref.at[0])
    cnt_ref[0] = jnp.broadcast_to(carry, (N_EXPERTS, LANES))


def _sample_mixer(x, ck, cv, sc, q_extra, wts):
    nb, seq, _ = x.shape
    rows = nb * seq
    x2 = x.reshape(rows, D)
    ins = (x2, ck, cv, sc, q_extra) + tuple(wts)
    full = lambda a: pl.BlockSpec(a.shape, lambda i, nd=a.ndim: (0,) * nd)
    out_shape = [
        jax.ShapeDtypeStruct((rows, D), jnp.float32),
        jax.ShapeDtypeStruct((rows * PK, LANES), jnp.uint32),
        jax.ShapeDtypeStruct((1, REC_ROWS, rows), jnp.float32),
        jax.ShapeDtypeStruct((1, N_EXPERTS, LANES), jnp.float32),
        jax.ShapeDtypeStruct((nb, WINDOW, LANES), jnp.float32),
        jax.ShapeDtypeStruct((nb, WINDOW, LANES), jnp.float32),
        jax.ShapeDtypeStruct((nb, CONV_HIST, D), jnp.float32),
    ]
    return pl.pallas_call(
        functools.partial(_sample_mixer_kernel, n_stream=nb, seq=seq),
        grid=(1,),
        in_specs=[full(a) for a in ins],
        out_specs=[full(a) for a in out_shape],
        out_shape=out_shape,
        scratch_shapes=[
            pltpu.VMEM((nb * (U0 + seq), D), jnp.float32),
            pltpu.VMEM((rows, D), jnp.bfloat16),
            pltpu.VMEM((nb, N_KV, GROUP_HEADS * seq, LANES), jnp.bfloat16),
            pltpu.VMEM((nb, N_KV, KEYS, LANES), jnp.bfloat16),
            pltpu.VMEM((nb, N_KV, KEYS, 2 * LANES), jnp.bfloat16),
        ],
        compiler_params=pltpu.CompilerParams(
            dimension_semantics=("arbitrary",), vmem_limit_bytes=VMEM_LIMIT),
        name="sample_mixer",
    )(*ins)


def _sorted_rows(blk_tokens, mt):
    rows = 2 * blk_tokens + N_EXPERTS * (mt // 2 - 1) + mt
    return -(-rows // mt) * mt


def _moe_kernel(pos_ref, tab_ref, stp_ref, h2p_ref, x1_ref, rt_ref, wg_ref, wu_ref, wd_ref, y_ref,
                xo, g1, g2, *, tile, n_tiles, max_tiles, mt):
    b = pl.program_id(0)
    s = pl.program_id(1)
    n_groups = tile // UNROLL
    n_disp = n_tiles

    @pl.when((b == 0) & (s == 0))
    def _():
        xo[...] = jnp.zeros_like(xo)

    @pl.when(s < n_disp)
    def _dispatch():
        base = (b * n_tiles + s) * (TOP_K * tile)

        def group(gi, carry):
            i0 = base + gi * UNROLL
            t0 = gi * (UNROLL * PK)
            for k in range(UNROLL):
                row = h2p_ref[pl.ds(pl.multiple_of(t0 + k * PK, PK), PK), :]
                xo[pl.ds(pl.multiple_of(pos_ref[i0 + k], PK), PK), :] = row
                xo[pl.ds(pl.multiple_of(pos_ref[i0 + tile + k], PK), PK), :] = row
            return carry

        lax.fori_loop(0, n_groups, group, 0)

    @pl.when((s >= n_disp) & (s < n_disp + EXPERT_STEPS))
    def _experts():
        st = b * EXPERT_STEPS + s - n_disp
        first = stp_ref[2 * st]
        last = stp_ref[2 * st + 1]
        half = mt // 2

        def entry(j):
            t = (b * max_tiles + j) * TAB_W
            return pl.multiple_of(tab_ref[t], half * PK), tab_ref[t + 1], tab_ref[t + 2]

        def up(j):
            base, k, _ = entry(j)
            xs = _bf(_unpack_rows(
                [xo[pl.ds(base + q, mt, stride=PK), :] for q in range(PK)]))
            gt = _dot(xs, wg_ref[k])
            return _bf(gt * _sigmoid(gt) * _dot(xs, wu_ref[k]))

        def down(j, hid):
            base, k, full = entry(j)
            out = _pack_rows(_dot(hid, wd_ref[k]))
            def put(lo):
                for q in range(PK):
                    xo[pl.ds(base + lo * PK + q, half, stride=PK), :] = (
                        out[lo:lo + half, q * LANES:(q + 1) * LANES])

            put(0)

            @pl.when(full > 0)
            def _():
                put(half)

        def load(j):
            base, _, _ = entry(j)
            return _bf(_unpack_rows([xo[pl.ds(base + q, mt, stride=PK), :] for q in range(PK)]))

        def swiglu(j, xs):
            _, k, _ = entry(j)
            gt = _dot(xs, wg_ref[k])
            return _bf(gt * _sigmoid(gt) * _dot(xs, wu_ref[k]))

        def store(j, hid):
            base, k, _ = entry(j)
            out = _pack_rows(_dot(hid, wd_ref[k]))
            for q in range(PK):
                xo[pl.ds(base + q, mt, stride=PK), :] = out[:, q * LANES:(q + 1) * LANES]

        n_full = 0
        for k in range(STEP_EXPERTS):
            n_full = n_full + tab_ref[(b * max_tiles + jnp.minimum(first + k, max_tiles - 1)) * TAB_W + 2]
        usual = (last - first == STEP_EXPERTS) & (n_full == STEP_EXPERTS)

        @pl.when(usual)
        def _():
            xs = [load(first + k) for k in range(STEP_EXPERTS)]
            for k in range(STEP_EXPERTS):
                store(first + k, swiglu(first + k, xs[k]))

        @pl.when(jnp.logical_not(usual) & (last > first))
        def _():
            def both(j, hid):
                nxt = up(j)
                down(j - 1, hid)
                return nxt

            down(last - 1, lax.fori_loop(first + 1, last, both, up(first)))

    @pl.when(s >= n_disp + EXPERT_STEPS)
    def _combine():
        i = s - n_disp - EXPERT_STEPS
        base = (b * n_tiles + i) * (TOP_K * tile)
        rec = jnp.concatenate([rt_ref[0], jnp.zeros((LANES - REC_ROWS, tile), jnp.float32)], axis=0)
        cw = rec.T
        part = tile // COMBINE_PARTS
        for p in range(COMBINE_PARTS):
            for k in range(part):
                t = p * part + k
                dst = pl.ds(t * PK, PK)
                g1[dst, :] = xo[pl.ds(pl.multiple_of(pos_ref[base + t], PK), PK), :]
                g2[dst, :] = xo[pl.ds(pl.multiple_of(pos_ref[base + tile + t], PK), PK), :]
            rows = pl.ds(p * part, part)
            o1 = _unpack_rows([g1[pl.ds(p * part * PK + q, part, stride=PK), :] for q in range(PK)])
            o2 = _unpack_rows([g2[pl.ds(p * part * PK + q, part, stride=PK), :] for q in range(PK)])
            w = cw[p * part:(p + 1) * part]
            y_ref[rows, :] = x1_ref[rows, :] + w[:, R_W1:R_W1 + 1] * o1 + w[:, R_W2:R_W2 + 1] * o2


def _moe(pos, tab, stp, rt, h2p, x1, wg, wu, wd, tile, n_tiles, mt):
    n = x1.shape[0]
    n_blocks = n // (tile * n_tiles)
    n_disp = n_tiles
    n_steps = n_disp + EXPERT_STEPS + n_tiles

    def disp(b, s, *_):
        return (b * n_disp + jnp.minimum(s, n_disp - 1), 0)

    def comb(b, s, *_):
        return (b * n_tiles + jnp.clip(s - n_disp - EXPERT_STEPS, 0, n_tiles - 1), 0)

    def wmap(b, s, *_):
        return (jnp.clip(s - n_disp, 0, EXPERT_STEPS - 1), 0, 0)

    grid_spec = pltpu.PrefetchScalarGridSpec(
        num_scalar_prefetch=3,
        grid=(n_blocks, n_steps),
        in_specs=[pl.BlockSpec((tile * PK, LANES), disp),
                  pl.BlockSpec((tile, D), comb),
                  pl.BlockSpec((1, REC_ROWS, tile), lambda b, s, *_: comb(b, s) + (0,)),
                  pl.BlockSpec((STEP_EXPERTS, D, D_EXPERT), wmap),
                  pl.BlockSpec((STEP_EXPERTS, D, D_EXPERT), wmap),
                  pl.BlockSpec((STEP_EXPERTS, D_EXPERT, D), wmap)],
        out_specs=pl.BlockSpec((tile, D), comb),
        scratch_shapes=[
            pltpu.VMEM((_sorted_rows(tile * n_tiles, mt) * PK, LANES), jnp.uint32),
            pltpu.VMEM((tile * PK, LANES), jnp.uint32),
            pltpu.VMEM((tile * PK, LANES), jnp.uint32),
        ])
    return pl.pallas_call(
        functools.partial(_moe_kernel, tile=tile, n_tiles=n_tiles,
                          max_tiles=_max_tiles(tile * n_tiles, mt), mt=mt),
        grid_spec=grid_spec,
        out_shape=jax.ShapeDtypeStruct((n, D), jnp.float32),
        compiler_params=pltpu.CompilerParams(
            dimension_semantics=("arbitrary", "arbitrary"), vmem_limit_bytes=VMEM_LIMIT),
        name="moe",
    )(pos, tab, stp, h2p, x1, rt, wg, wu, wd)


def _max_tiles(blk_tokens, mt):
    return N_EXPERTS + 2 * blk_tokens // mt


def _sorted_positions(rt, cnt, blk_tokens, mt):
    experts = jnp.arange(N_EXPERTS, dtype=jnp.int32)
    n_tiles, _, tile = rt.shape
    counts = cnt[:, :, 0].astype(jnp.int32)
    pad = mt // 2
    padded = (counts + pad - 1) // pad * pad
    off = jnp.cumsum(padded, axis=1) - padded
    e = rt[:, R_E1:R_E1 + TOP_K, :].astype(jnp.int32)
    rank = rt[:, R_RANK1:R_RANK1 + TOP_K, :].astype(jnp.int32)
    off_tile = jnp.repeat(off, blk_tokens // tile, axis=0)
    hit = e[:, :, None, :] == experts[:, None]
    pos = (rank + jnp.sum(jnp.where(hit, off_tile[:, None, :, None], 0), axis=2)) * PK

    tiles = (counts + mt - 1) // mt
    t_end = jnp.cumsum(tiles, axis=1)
    t_beg = t_end - tiles
    j = jnp.arange(_max_tiles(blk_tokens, mt), dtype=jnp.int32)
    e_of = jnp.minimum(jnp.sum(t_end[:, None, :] <= j[None, :, None], axis=-1), N_EXPERTS - 1)
    pick = lambda a: jnp.sum(jnp.where(e_of[:, :, None] == experts, a[:, None, :], 0), axis=-1)
    local = (j[None, :] - pick(t_beg)) * mt
    tab = jnp.stack([(pick(off) + local) * PK, e_of % STEP_EXPERTS,
                     (pick(counts) - local > pad).astype(jnp.int32)], axis=-1)
    stp = jnp.stack([t_beg[:, ::STEP_EXPERTS], t_end[:, STEP_EXPERTS - 1::STEP_EXPERTS]], axis=-1)
    return pos.reshape(-1), tab.reshape(-1), stp.reshape(-1), rt


def _segment_matrices():
    head = np.arange(D) // HEAD_DIM
    seg = head[:, None] == np.arange(LANES)[None, :]
    expm = (np.arange(2 * LANES) % LANES)[:, None] == head[None, :]
    return jnp.asarray(seg, jnp.bfloat16), jnp.asarray(expm, jnp.bfloat16)


def _strict_upper(n):
    r = np.arange(n)
    return jnp.asarray(r[:, None] < r[None, :], jnp.bfloat16)


def kernel(x_prompt, x_sample, cache_k, cache_v, state_conv, norm_mix_g, w_in, conv_w, q_norm_g,
           k_norm_g, attn_sinks, w_conv_out, w_attn_out, w_out, norm_ffn_g, w_group, b_group,
           w_router, b_router, w_gate, w_up, w_down):
    nb, seq, _ = x_prompt.shape
    ns, sseq, _ = x_sample.shape
    l = 0
    seg, expm = _segment_matrices()
    pad = RT_ROWS - N_EXPERTS - N_GROUPS
    wr = jnp.concatenate([w_router[l].T, w_group[l].T, jnp.zeros((pad, D), jnp.float32)], axis=0)
    br = jnp.concatenate([b_router[l], b_group[l], jnp.zeros((pad,), jnp.float32)]).reshape(RT_ROWS, 1)
    wts = (norm_mix_g[l].reshape(1, D), _bf(w_in[l]), conv_w[l],
           jnp.tile(q_norm_g[l], N_HEADS).reshape(1, D), jnp.tile(k_norm_g[l], N_KV).reshape(1, LANES),
           seg, expm, _bf(w_conv_out[l]), _bf(w_attn_out[l]), _bf(w_out[l]),
           norm_ffn_g[l].reshape(1, D), _bf(wr), br)
    sinks = attn_sinks[l]
    n_s = ns * sseq

    sink2 = sinks * LOG2E
    sink_hi = _bf(sink2).astype(jnp.float32)
    q_extra = jnp.concatenate(
        [jnp.ones((N_HEADS, 1), jnp.float32), sink_hi[:, None], (sink2 - sink_hi)[:, None],
         jnp.zeros((N_HEADS, HEAD_DIM - Q_FEATS), jnp.float32)], axis=1).reshape(N_HEADS, 1, HEAD_DIM)
    x1p, h2pp, rtp, cntp, kwp, vwp, cstp, wg, wu, wd = _prompt_mixer(
        x_prompt, (w_gate[l], w_up[l], w_down[l]), q_extra, wts + (_strict_upper(TOK_TILE),), tile=TOK_TILE, blk_tiles=BLOCK_TILES)
    x1s, h2ps, rts, cnts, kws, vws, csts = _sample_mixer(
        x_sample, cache_k[l].reshape(ns, WINDOW, LANES), cache_v[l].reshape(ns, WINDOW, LANES),
        state_conv[l], q_extra, wts + (_strict_upper(n_s),))

    yp = _moe(*_sorted_positions(rtp, cntp, TOK_TILE * BLOCK_TILES, MOE_TILE), h2pp, x1p.reshape(nb * seq, D), wg, wu, wd,
              tile=TOK_TILE, n_tiles=BLOCK_TILES, mt=MOE_TILE)
    ys = _moe(*_sorted_positions(rts, cnts, n_s, SAMPLE_MOE_TILE), h2ps, x1s, wg, wu, wd, tile=n_s, n_tiles=1,
              mt=SAMPLE_MOE_TILE)

    kv_shape = lambda n: (1, n, WINDOW, N_KV, HEAD_DIM)
    return (yp.reshape(nb, seq, D), ys.reshape(ns, sseq, D),
            kwp.reshape(kv_shape(nb)), vwp.reshape(kv_shape(nb)), cstp.reshape(1, nb, CONV_HIST, D),
            kws.reshape(kv_shape(ns)), vws.reshape(kv_shape(ns)), csts.reshape(1, ns, CONV_HIST, D))
```

```python
import functools
import math

import jax
import jax.numpy as jnp
import numpy as np
from jax import lax
from jax.experimental import pallas as pl
from jax.experimental.pallas import tpu as pltpu

D = 1024
CHUNK = 64
HEAD_DIM = 64
N_HEADS = 16
N_KV = 2
WINDOW = 128
N_GROUPS = 4
EPG = 8
N_EXPERTS = 32
TOP_K = 2
D_EXPERT = 256
EPS = 1e-6
NEG = -1e30

LANES = 128
SUBLANES = 8
CONV_HIST = 2
U0 = SUBLANES
SLOTS = 4
KEYS = SLOTS * CHUNK
GROUP_HEADS = N_HEADS // N_KV
STACK = GROUP_HEADS * CHUNK
BIAS_LANE = HEAD_DIM
SINK_LANE = HEAD_DIM + 1
Q_FEATS = 3
LOG2E = 1.0 / math.log(2.0)
RT_ROWS = LANES
REC_ROWS = 8
HALF = D // 2
PK = HALF // LANES

R_E1, R_E2, R_W1, R_W2, R_RANK1, R_RANK2 = 0, 1, 2, 3, 4, 5

C_B, C_C, C_X, C_Q, C_K, C_V, C_G = 0, 1024, 2048, 3072, 4096, 4224, 4352

MOE_TILE = 288
SAMPLE_MOE_TILE = 64
TOK_TILE = 512
BLOCK_TILES = 8
UNROLL = 32
COMBINE_PARTS = 4
STEP_EXPERTS = 4
TAB_W = 3
EXPERT_STEPS = N_EXPERTS // STEP_EXPERTS

VMEM_LIMIT = 60 * 1024 * 1024

_NT = (((1,), (1,)), ((), ()))


def _bf(x):
    return x.astype(jnp.bfloat16)


def _dot(a, b):
    return jnp.dot(a, b, preferred_element_type=jnp.float32)


def _rms(x, g):
    return x * lax.rsqrt(jnp.mean(x * x, axis=-1, keepdims=True) + EPS) * g


def _sigmoid(x):
    return 1.0 / (1.0 + jnp.exp(-x))


def _pack_rows(x):
    return pltpu.pack_elementwise([x[:, :HALF], x[:, HALF:]], packed_dtype=jnp.bfloat16)


def _unpack_rows(parts):
    def half(i):
        return [pltpu.unpack_elementwise(p, index=i, packed_dtype=jnp.bfloat16,
                                         unpacked_dtype=jnp.float32) for p in parts]
    return jnp.concatenate(half(0) + half(1), axis=-1)


def _head_norm_q(q, seg_ref, exp_ref, qg):
    ssq = _dot(_bf(q * q), seg_ref[...])
    inv = lax.rsqrt(ssq * (1.0 / HEAD_DIM) + EPS)
    hi = _bf(inv)
    lo = _bf(inv - hi.astype(jnp.float32))
    full = _dot(jnp.concatenate([hi, lo], axis=-1), exp_ref[...])
    return q * full * qg


def _head_norm_k(k, kg):
    lane = lax.broadcasted_iota(jnp.int32, k.shape, 1)
    lo_half = lane < HEAD_DIM
    sq = k * k
    s0 = jnp.sum(jnp.where(lo_half, sq, 0.0), axis=-1, keepdims=True)
    s1 = jnp.sum(jnp.where(lo_half, 0.0, sq), axis=-1, keepdims=True)
    i0 = lax.rsqrt(s0 * (1.0 / HEAD_DIM) + EPS)
    i1 = lax.rsqrt(s1 * (1.0 / HEAD_DIM) + EPS)
    return k * jnp.where(lo_half, i0, i1) * kg


def _attend_group(qs, keys, vals):
    s = lax.dot_general(qs, keys, _NT, preferred_element_type=jnp.float32)
    m = jnp.max(s, axis=-1, keepdims=True)
    o = _dot(_bf(jnp.exp2(s - m)), vals)
    return o[:, 0:LANES] / o[:, LANES:2 * LANES]


def _key_rows(k, v, g, lo_half):
    ksw = pltpu.roll(k, HEAD_DIM, axis=1)
    vsw = pltpu.roll(v, HEAD_DIM, axis=1)
    kb = jnp.where(lo_half, k if g == 0 else ksw, 0.0)
    vb = jnp.where(lo_half, v, vsw) if g == 0 else jnp.where(lo_half, vsw, v)
    return _bf(kb), _bf(vb)


def _sink_block(shape):
    row = lax.broadcasted_iota(jnp.int32, shape, len(shape) - 2)
    ln = lax.broadcasted_iota(jnp.int32, shape, len(shape) - 1)
    is_sink = (row == 0) & ((ln == SINK_LANE) | (ln == SINK_LANE + 1))
    return _bf(jnp.where(is_sink, 1.0, jnp.where((row > 0) & (ln == BIAS_LANE), NEG, 0.0)))


def _head_pairs(out, rows_per_head, lo_half):
    blk = lambda r: out[r * rows_per_head:(r + 1) * rows_per_head, :]
    return [_bf(jnp.where(lo_half, blk(r), blk(r + 1))) for r in range(0, GROUP_HEADS, 2)]


def _route(lt, upper_ref, carry):
    t = lt.shape[1]
    big = float(RT_ROWS)
    grow = lax.broadcasted_iota(jnp.int32, (REC_ROWS, t), 0).astype(jnp.float32)
    is_g = grow < N_GROUPS
    gl = jnp.where(is_g, lt[N_EXPERTS:N_EXPERTS + REC_ROWS], NEG)
    gmax = jnp.max(gl, axis=0, keepdims=True)
    gsum = jnp.sum(jnp.where(is_g, jnp.exp(gl - gmax), 0.0), axis=0, keepdims=True)
    g_w = 1.0 / gsum
    g_sel = jnp.min(jnp.where(is_g & (gl == gmax), grow, big), axis=0, keepdims=True)
    logits = lt[0:N_EXPERTS]
    erow = lax.broadcasted_iota(jnp.int32, (N_EXPERTS, t), 0).astype(jnp.float32)
    lo = g_sel * EPG
    in_grp = (erow >= lo) & (erow < lo + EPG)
    el = jnp.where(in_grp, logits, NEG)
    v1 = jnp.max(el, axis=0, keepdims=True)
    i1 = jnp.min(jnp.where(in_grp & (el == v1), erow, big), axis=0, keepdims=True)
    rest = in_grp & (erow != i1)
    el2 = jnp.where(rest, logits, NEG)
    v2 = jnp.max(el2, axis=0, keepdims=True)
    i2 = jnp.min(jnp.where(rest & (el2 == v2), erow, big), axis=0, keepdims=True)
    e2 = jnp.exp(v2 - v1)
    den = 1.0 + e2
    w1 = (1.0 / den) * g_w
    w2 = (e2 / den) * g_w
    oh1 = jnp.where(erow == i1, 1.0, 0.0)
    oh2 = jnp.where(erow == i2, 1.0, 0.0)
    both = oh1 + oh2
    before = _dot(_bf(both), upper_ref[...]) + carry
    r1 = jnp.sum(before * oh1, axis=0, keepdims=True)
    r2 = jnp.sum(before * oh2, axis=0, keepdims=True)
    rec = jnp.zeros((REC_ROWS, t), jnp.float32)
    for rw, val in ((R_E1, i1), (R_E2, i2), (R_W1, w1), (R_W2, w2), (R_RANK1, r1), (R_RANK2, r2)):
        rec = jnp.where(grow == rw, val, rec)
    return rec, carry + jnp.sum(both, axis=1, keepdims=True)


def _route_stage(x1, g2_ref, wr_ref, br_ref, upper_ref, carry, h2p_ref, rt_ref):
    rows = x1.shape[0]
    h2 = _rms(x1, g2_ref[...])
    pk = _pack_rows(h2)
    for j in range(PK):
        h2p_ref[pl.ds(j, rows, stride=PK), :] = pk[:, j * LANES:(j + 1) * LANES]
    lt = lax.dot_general(wr_ref[...], _bf(h2), _NT, preferred_element_type=jnp.float32) + br_ref[...]
    rec, carry = _route(lt, upper_ref, carry)
    rt_ref[...] = rec
    return carry


def _merge_tail(x, macc, wo_ref, g2_ref, wr_ref, br_ref, upper_ref, carry, x1_ref, h2p_ref, rt_ref):
    x1 = x + _dot(_bf(macc), wo_ref[...])
    x1_ref[...] = x1
    return _route_stage(x1, g2_ref, wr_ref, br_ref, upper_ref, carry, h2p_ref, rt_ref)


def _prompt_mixer_kernel(x_ref, wg32_ref, wu32_ref, wd32_ref,
                         qx_ref, g1_ref, win_ref, cw_ref, qg_ref, kg_ref, seg_ref, exp_ref,
                         wco_ref, wao_ref, wo_ref, g2_ref, wr_ref, br_ref, upper_ref,
                         x1_ref, h2p_ref, rt_ref, cnt_ref, kwin_ref, vwin_ref, cst_ref,
                         wg16_ref, wu16_ref, wd16_ref,
                         uext, qs_s, att_s, kn_s, kop, vop, khist, vhist, carry_s, x1_new, x1_old,
                         *, tile, blk_tiles, n_it, n_main):
    step = pl.program_id(0)
    it = step % n_it
    n_chunk = tile // CHUNK

    @pl.when(step == 0)
    def _():
        shape = (n_chunk, N_KV, CHUNK, LANES)
        tail = pl.ds((SLOTS - 1) * CHUNK, CHUNK)
        kop[:, :, tail, :] = _sink_block(shape)
        vop[:, :, tail, 0:LANES] = jnp.zeros(shape, jnp.bfloat16)
        vop[:, :, tail, LANES:2 * LANES] = jnp.ones(shape, jnp.bfloat16)
        x1_new[...] = jnp.zeros((tile, D), jnp.float32)
        carry_s[...] = jnp.zeros_like(carry_s)

    def route_previous(x1_prev):
        first_of_block = (step + blk_tiles - 1) % blk_tiles == 0
        carry = jnp.where(first_of_block, 0.0, carry_s[...])
        carry = _route_stage(x1_prev[...], g2_ref, wr_ref, br_ref, upper_ref, carry,
                             h2p_ref, rt_ref.at[0])
        carry_s[...] = carry
        cnt_ref[0] = jnp.broadcast_to(carry, (N_EXPERTS, LANES))

    @pl.when(step == n_main)
    def _():
        route_previous(x1_new)

    @pl.when((step < n_main) & (it == 0))
    def _():
        uext[0:U0, :] = jnp.zeros((U0, D), jnp.float32)
        ln = lax.broadcasted_iota(jnp.int32, khist.shape, 2)
        khist[...] = _bf(jnp.where(ln == BIAS_LANE, NEG, 0.0))
        vhist[...] = jnp.zeros_like(vhist)

    @pl.when(step < n_main)
    def _():
        x1_old[...] = x1_new[...]
        _mixer_tile(x_ref, wg32_ref, wu32_ref, wd32_ref, qx_ref, g1_ref, win_ref, cw_ref, qg_ref,
                    kg_ref, seg_ref, exp_ref, wco_ref, wao_ref, wo_ref, x1_ref, kwin_ref, vwin_ref,
                    cst_ref, wg16_ref, wu16_ref, wd16_ref, uext, qs_s, att_s, kn_s, kop, vop, khist,
                    vhist, x1_new, tile=tile, side_job=lambda: route_previous(x1_old))


def _mixer_tile(x_ref, wg32_ref, wu32_ref, wd32_ref, qx_ref, g1_ref, win_ref, cw_ref, qg_ref,
                kg_ref, seg_ref, exp_ref, wco_ref, wao_ref, wo_ref, x1_ref, kwin_ref, vwin_ref,
                cst_ref, wg16_ref, wu16_ref, wd16_ref, uext, qs_s, att_s, kn_s, kop, vop, khist,
                vhist, x1_keep, *, tile, side_job):
    n_chunk = tile // CHUNK
    wg16_ref[...] = _bf(wg32_ref[...])
    wu16_ref[...] = _bf(wu32_ref[...])
    wd16_ref[...] = _bf(wd32_ref[...])

    x = x_ref[0]
    h = _bf(_rms(x, g1_ref[...]))

    q = _dot(h, win_ref[:, C_Q:C_Q + D])
    qn = _head_norm_q(q, seg_ref, exp_ref, qg_ref[...]) * (LOG2E / math.sqrt(HEAD_DIM))
    for hd in range(N_HEADS):
        g, r = divmod(hd, GROUP_HEADS)
        ext = jnp.broadcast_to(qx_ref[hd], (tile, HEAD_DIM))
        piece = _bf(jnp.concatenate([qn[:, hd * HEAD_DIM:(hd + 1) * HEAD_DIM], ext], axis=-1))
        for c in range(n_chunk):
            qs_s[g, c * STACK + r * CHUNK:c * STACK + (r + 1) * CHUNK, :] = (
                piece[c * CHUNK:(c + 1) * CHUNK, :])
    kv = _dot(h, win_ref[:, C_K:C_K + 2 * LANES])
    kn = _head_norm_k(kv[:, 0:LANES], kg_ref[...])
    vv = kv[:, LANES:2 * LANES]
    kn_s[0] = kn
    kn_s[1] = vv

    kwin_ref[0] = kn[tile - WINDOW:, :]
    vwin_ref[0] = vv[tile - WINDOW:, :]

    u = _dot(h, win_ref[:, C_C:C_C + D]) * _dot(h, win_ref[:, C_X:C_X + D])
    uext[U0:U0 + tile, :] = u
    conv = (uext[U0 - 2:U0 - 2 + tile, :] * cw_ref[0:1, :] + uext[U0 - 1:U0 - 1 + tile, :] * cw_ref[1:2, :]
            + u * cw_ref[2:3, :])
    yc = _bf(_dot(h, win_ref[:, C_B:C_B + D]) * conv)
    o_c = _dot(yc, wco_ref[...])
    macc = _sigmoid(_dot(h, win_ref[:, C_G:C_G + D])) * o_c
    last2 = uext[U0 + tile - CONV_HIST:U0 + tile, :]
    uext[U0 - CONV_HIST:U0, :] = last2

    cst_ref[0] = last2
    side_job()

    lane = lax.broadcasted_iota(jnp.int32, (CHUNK, LANES), 1)
    lo_half = lane < HEAD_DIM
    ones_blk = jnp.ones((CHUNK, LANES), jnp.bfloat16)
    own = SLOTS - 2

    for c in range(own):
        for sl in range(own - c):
            src = pl.ds((c + sl) * CHUNK, CHUNK)
            kop[c, :, pl.ds(sl * CHUNK, CHUNK), :] = khist[:, src, :]
            vop[c, :, pl.ds(sl * CHUNK, CHUNK), :] = vhist[:, src, :]

    def chunk_body(c):
        r0 = c * CHUNK
        kc = kn_s[0, pl.ds(r0, CHUNK), :]
        vc = kn_s[1, pl.ds(r0, CHUNK), :]
        for g in range(N_KV):
            kb, vb = _key_rows(kc, vc, g, lo_half)
            for d in range(own + 1):
                if c + d < n_chunk:
                    dst = pl.ds((own - d) * CHUNK, CHUNK)
                    kop[c + d, g, dst, :] = kb
                    vop[c + d, g, dst, 0:LANES] = vb
                    vop[c + d, g, dst, LANES:2 * LANES] = ones_blk
            if c >= n_chunk - own:
                dst = pl.ds((c - (n_chunk - own)) * CHUNK, CHUNK)
                khist[g, dst, :] = kb
                vhist[g, dst, 0:LANES] = vb
                vhist[g, dst, LANES:2 * LANES] = ones_blk
        for g in range(N_KV):
            out = _attend_group(qs_s[g, pl.ds(c * STACK, STACK), :], kop[c, g], vop[c, g])
            for i, pair in enumerate(_head_pairs(out, CHUNK, lo_half)):
                col = (g * GROUP_HEADS + 2 * i) * HEAD_DIM
                att_s[pl.ds(r0, CHUNK), col:col + LANES] = pair

    for c in range(n_chunk):
        chunk_body(c)

    o_a = _dot(att_s[...], wao_ref[...])
    macc = macc + _sigmoid(_dot(h, win_ref[:, C_G + D:C_G + 2 * D])) * o_a
    x1 = x + _dot(_bf(macc), wo_ref[...])
    x1_ref[0] = x1
    x1_keep[...] = x1


def _const_spec(shape):
    nd = len(shape)
    return pl.BlockSpec(shape, lambda *_: (0,) * nd, pipeline_mode=pl.Buffered(1))


def _prompt_mixer(x, experts32, q_extra, wts, tile, blk_tiles):
    nb, seq, _ = x.shape
    n_it = seq // tile
    n_main = nb * n_it
    bps = n_it // blk_tiles
    parts = n_main // N_EXPERTS
    consts = (q_extra,) + tuple(wts)

    cur = lambda s: jnp.minimum(s, n_main - 1)
    prev = lambda s: jnp.maximum(s - 1, 0)
    row_spec = lambda w: pl.BlockSpec((1, tile, w), lambda s: (cur(s) // n_it, cur(s) % n_it, 0))
    str_spec = lambda r, w: pl.BlockSpec((1, r, w), lambda s: (cur(s) // n_it, 0, 0))
    part_spec = lambda r, w: pl.BlockSpec(
        (1, r // parts, w), lambda s: (cur(s) // parts, cur(s) % parts, 0))
    expert_specs = [part_spec(D, D_EXPERT), part_spec(D, D_EXPERT), part_spec(D_EXPERT, D)]
    out_shape = [
        jax.ShapeDtypeStruct((nb, seq, D), jnp.float32),
        jax.ShapeDtypeStruct((nb * seq * PK, LANES), jnp.uint32),
        jax.ShapeDtypeStruct((n_main, REC_ROWS, tile), jnp.float32),
        jax.ShapeDtypeStruct((nb * bps, N_EXPERTS, LANES), jnp.float32),
        jax.ShapeDtypeStruct((nb, WINDOW, LANES), jnp.float32),
        jax.ShapeDtypeStruct((nb, WINDOW, LANES), jnp.float32),
        jax.ShapeDtypeStruct((nb, CONV_HIST, D), jnp.float32),
    ] + [jax.ShapeDtypeStruct(w.shape, jnp.bfloat16) for w in experts32]
    return pl.pallas_call(
        functools.partial(_prompt_mixer_kernel, tile=tile, blk_tiles=blk_tiles, n_it=n_it, n_main=n_main),
        grid=(n_main + 1,),
        in_specs=[row_spec(D)] + expert_specs + [_const_spec(w.shape) for w in consts],
        out_specs=[row_spec(D),
                   pl.BlockSpec((tile * PK, LANES), lambda s: (prev(s), 0)),
                   pl.BlockSpec((1, REC_ROWS, tile), lambda s: (prev(s), 0, 0)),
                   pl.BlockSpec((1, N_EXPERTS, LANES), lambda s: (prev(s) // blk_tiles, 0, 0)),
                   str_spec(WINDOW, LANES), str_spec(WINDOW, LANES), str_spec(CONV_HIST, D)] + expert_specs,
        out_shape=out_shape,
        scratch_shapes=[
            pltpu.VMEM((U0 + tile, D), jnp.float32),
            pltpu.VMEM((N_KV, tile * GROUP_HEADS, LANES), jnp.bfloat16),
            pltpu.VMEM((tile, D), jnp.bfloat16),
            pltpu.VMEM((2, tile, LANES), jnp.float32),
            pltpu.VMEM((tile // CHUNK, N_KV, KEYS, LANES), jnp.bfloat16),
            pltpu.VMEM((tile // CHUNK, N_KV, KEYS, 2 * LANES), jnp.bfloat16),
            pltpu.VMEM((N_KV, (SLOTS - 2) * CHUNK, LANES), jnp.bfloat16),
            pltpu.VMEM((N_KV, (SLOTS - 2) * CHUNK, 2 * LANES), jnp.bfloat16),
            pltpu.VMEM((N_EXPERTS, 1), jnp.float32),
            pltpu.VMEM((tile, D), jnp.float32),
            pltpu.VMEM((tile, D), jnp.float32),
        ],
        compiler_params=pltpu.CompilerParams(
            dimension_semantics=("arbitrary",), vmem_limit_bytes=VMEM_LIMIT),
        name="prompt_mixer",
    )(x, *experts32, *consts)


def _sample_mixer_kernel(x_ref, ck_ref, cv_ref, sc_ref, qx_ref, g1_ref, win_ref, cw_ref, qg_ref,
                         kg_ref, seg_ref, exp_ref, wco_ref, wao_ref, wo_ref, g2_ref, wr_ref, br_ref,
                         upper_ref, x1_ref, h2p_ref, rt_ref, cnt_ref, kwin_ref, vwin_ref, cst_ref,
                         uext, att_s, qs_s, kop, vop, *, n_stream, seq):
    ext = U0 + seq
    n_keys = WINDOW + seq
    x = x_ref[...]
    h = _bf(_rms(x, g1_ref[...]))

    u = _dot(h, win_ref[:, C_C:C_C + D]) * _dot(h, win_ref[:, C_X:C_X + D])
    for s in range(n_stream):
        r0 = s * ext + U0
        uext[r0 - CONV_HIST:r0, :] = sc_ref[s]
        uext[r0:r0 + seq, :] = u[s * seq:(s + 1) * seq, :]
        cst_ref[s] = u[(s + 1) * seq - CONV_HIST:(s + 1) * seq, :]
    conv = jnp.concatenate(
        [uext[s * ext + U0 - 2:s * ext + U0 - 2 + seq, :] * cw_ref[0:1, :]
         + uext[s * ext + U0 - 1:s * ext + U0 - 1 + seq, :] * cw_ref[1:2, :] for s in range(n_stream)],
        axis=0) + u * cw_ref[2:3, :]
    yc = _bf(_dot(h, win_ref[:, C_B:C_B + D]) * conv)
    o_c = _dot(yc, wco_ref[...])
    macc = _sigmoid(_dot(h, win_ref[:, C_G:C_G + D])) * o_c

    q = _dot(h, win_ref[:, C_Q:C_Q + D])
    qn = _head_norm_q(q, seg_ref, exp_ref, qg_ref[...]) * (LOG2E / math.sqrt(HEAD_DIM))
    kv = _dot(h, win_ref[:, C_K:C_K + 2 * LANES])
    kn = _head_norm_k(kv[:, 0:LANES], kg_ref[...])
    vv = kv[:, LANES:2 * LANES]

    lo_win = lax.broadcasted_iota(jnp.int32, (WINDOW, LANES), 1) < HEAD_DIM
    lo_new = lax.broadcasted_iota(jnp.int32, (seq, LANES), 1) < HEAD_DIM
    tail = KEYS - n_keys
    sink_rows = _sink_block((tail, LANES))
    for s in range(n_stream):
        for g in range(N_KV):
            kop[s, g, n_keys:KEYS, :] = sink_rows
            vop[s, g, n_keys:KEYS, 0:LANES] = jnp.zeros((tail, LANES), jnp.bfloat16)
            vop[s, g, :, LANES:2 * LANES] = jnp.ones((KEYS, LANES), jnp.bfloat16)
        rows = slice(s * seq, (s + 1) * seq)
        kwin_ref[s, 0:WINDOW - seq, :] = ck_ref[s, seq:WINDOW, :]
        kwin_ref[s, WINDOW - seq:WINDOW, :] = kn[rows, :]
        vwin_ref[s, 0:WINDOW - seq, :] = cv_ref[s, seq:WINDOW, :]
        vwin_ref[s, WINDOW - seq:WINDOW, :] = vv[rows, :]
        for g in range(N_KV):
            kop[s, g, 0:WINDOW, :], vop[s, g, 0:WINDOW, 0:LANES] = _key_rows(ck_ref[s], cv_ref[s], g, lo_win)
            kop[s, g, WINDOW:n_keys, :], vop[s, g, WINDOW:n_keys, 0:LANES] = _key_rows(
                kn[rows, :], vv[rows, :], g, lo_new)
        for hd in range(N_HEADS):
            g, r = divmod(hd, GROUP_HEADS)
            feat = jnp.broadcast_to(qx_ref[hd], (seq, HEAD_DIM))
            qs_s[s, g, r * seq:(r + 1) * seq, :] = _bf(
                jnp.concatenate([qn[rows, hd * HEAD_DIM:(hd + 1) * HEAD_DIM], feat], axis=-1))
        for g in range(N_KV):
            out = _attend_group(qs_s[s, g], kop[s, g], vop[s, g])
            for i, pair in enumerate(_head_pairs(out, seq, lo_new)):
                col = (g * GROUP_HEADS + 2 * i) * HEAD_DIM
                att_s[rows, col:col + LANES] = pair

    o_a = _dot(att_s[...], wao_ref[...])
    macc = macc + _sigmoid(_dot(h, win_ref[:, C_G + D:C_G + 2 * D])) * o_a
    carry = _merge_tail(x, macc, wo_ref, g2_ref, wr_ref, br_ref, upper_ref,
                        jnp.zeros((N_EXPERTS, 1), jnp.float32), x1_ref, h2p_ref, rt_ref.at[0])
    cnt_ref[0] = jnp.broadcast_to(carry, (N_EXPERTS, LANES))


def _sample_mixer(x, ck, cv, sc, q_extra, wts):
    nb, seq, _ = x.shape
    rows = nb * seq
    x2 = x.reshape(rows, D)
    ins = (x2, ck, cv, sc, q_extra) + tuple(wts)
    full = lambda a: pl.BlockSpec(a.shape, lambda i, nd=a.ndim: (0,) * nd)
    out_shape = [
        jax.ShapeDtypeStruct((rows, D), jnp.float32),
        jax.ShapeDtypeStruct((rows * PK, LANES), jnp.uint32),
        jax.ShapeDtypeStruct((1, REC_ROWS, rows), jnp.float32),
        jax.ShapeDtypeStruct((1, N_EXPERTS, LANES), jnp.float32),
        jax.ShapeDtypeStruct((nb, WINDOW, LANES), jnp.float32),
        jax.ShapeDtypeStruct((nb, WINDOW, LANES), jnp.float32),
        jax.ShapeDtypeStruct((nb, CONV_HIST, D), jnp.float32),
    ]
    return pl.pallas_call(
        functools.partial(_sample_mixer_kernel, n_stream=nb, seq=seq),
        grid=(1,),
        in_specs=[full(a) for a in ins],
        out_specs=[full(a) for a in out_shape],
        out_shape=out_shape,
        scratch_shapes=[
            pltpu.VMEM((nb * (U0 + seq), D), jnp.float32),
            pltpu.VMEM((rows, D), jnp.bfloat16),
            pltpu.VMEM((nb, N_KV, GROUP_HEADS * seq, LANES), jnp.bfloat16),
            pltpu.VMEM((nb, N_KV, KEYS, LANES), jnp.bfloat16),
            pltpu.VMEM((nb, N_KV, KEYS, 2 * LANES), jnp.bfloat16),
        ],
        compiler_params=pltpu.CompilerParams(
            dimension_semantics=("arbitrary",), vmem_limit_bytes=VMEM_LIMIT),
        name="sample_mixer",
    )(*ins)


def _sorted_rows(blk_tokens, mt):
    rows = 2 * blk_tokens + N_EXPERTS * (mt // 2 - 1) + mt
    return -(-rows // mt) * mt


def _moe_kernel(pos_ref, tab_ref, stp_ref, h2p_ref, x1_ref, rt_ref, wg_ref, wu_ref, wd_ref, y_ref,
                xo, g1, g2, *, tile, n_tiles, max_tiles, mt):
    b = pl.program_id(0)
    s = pl.program_id(1)
    n_groups = tile // UNROLL
    n_disp = n_tiles

    @pl.when((b == 0) & (s == 0))
    def _():
        xo[...] = jnp.zeros_like(xo)

    @pl.when(s < n_disp)
    def _dispatch():
        base = (b * n_tiles + s) * (TOP_K * tile)

        def group(gi, carry):
            i0 = base + gi * UNROLL
            t0 = gi * (UNROLL * PK)
            for k in range(UNROLL):
                row = h2p_ref[pl.ds(pl.multiple_of(t0 + k * PK, PK), PK), :]
                xo[pl.ds(pl.multiple_of(pos_ref[i0 + k], PK), PK), :] = row
                xo[pl.ds(pl.multiple_of(pos_ref[i0 + tile + k], PK), PK), :] = row
            return carry

        lax.fori_loop(0, n_groups, group, 0)

    @pl.when((s >= n_disp) & (s < n_disp + EXPERT_STEPS))
    def _experts():
        st = b * EXPERT_STEPS + s - n_disp
        first = stp_ref[2 * st]
        last = stp_ref[2 * st + 1]
        half = mt // 2

        def entry(j):
            t = (b * max_tiles + j) * TAB_W
            return pl.multiple_of(tab_ref[t], half * PK), tab_ref[t + 1], tab_ref[t + 2]

        def up(j):
            base, k, _ = entry(j)
            xs = _bf(_unpack_rows(
                [xo[pl.ds(base + q, mt, stride=PK), :] for q in range(PK)]))
            gt = _dot(xs, wg_ref[k])
            return _bf(gt * _sigmoid(gt) * _dot(xs, wu_ref[k]))

        def down(j, hid):
            base, k, full = entry(j)
            out = _pack_rows(_dot(hid, wd_ref[k]))
            def put(lo):
                for q in range(PK):
                    xo[pl.ds(base + lo * PK + q, half, stride=PK), :] = (
                        out[lo:lo + half, q * LANES:(q + 1) * LANES])

            put(0)

            @pl.when(full > 0)
            def _():
                put(half)

        def load(j):
            base, _, _ = entry(j)
            return _bf(_unpack_rows([xo[pl.ds(base + q, mt, stride=PK), :] for q in range(PK)]))

        def swiglu(j, xs):
            _, k, _ = entry(j)
            gt = _dot(xs, wg_ref[k])
            return _bf(gt * _sigmoid(gt) * _dot(xs, wu_ref[k]))

        def store(j, hid):
            base, k, _ = entry(j)
            out = _pack_rows(_dot(hid, wd_ref[k]))
            for q in range(PK):
                xo[pl.ds(base + q, mt, stride=PK), :] = out[:, q * LANES:(q + 1) * LANES]

        n_full = 0
        for k in range(STEP_EXPERTS):
            n_full = n_full + tab_ref[(b * max_tiles + jnp.minimum(first + k, max_tiles - 1)) * TAB_W + 2]
        usual = (last - first == STEP_EXPERTS) & (n_full == STEP_EXPERTS)

        @pl.when(usual)
        def _():
            xs = [load(first + k) for k in range(STEP_EXPERTS)]
            for k in range(STEP_EXPERTS):
                store(first + k, swiglu(first + k, xs[k]))

        @pl.when(jnp.logical_not(usual) & (last > first))
        def _():
            def both(j, hid):
                nxt = up(j)
                down(j - 1, hid)
                return nxt

            down(last - 1, lax.fori_loop(first + 1, last, both, up(first)))

    @pl.when(s >= n_disp + EXPERT_STEPS)
    def _combine():
        i = s - n_disp - EXPERT_STEPS
        base = (b * n_tiles + i) * (TOP_K * tile)
        rec = jnp.concatenate([rt_ref[0], jnp.zeros((LANES - REC_ROWS, tile), jnp.float32)], axis=0)
        cw = rec.T
        part = tile // COMBINE_PARTS
        for p in range(COMBINE_PARTS):
            for k in range(part):
                t = p * part + k
                dst = pl.ds(t * PK, PK)
                g1[dst, :] = xo[pl.ds(pl.multiple_of(pos_ref[base + t], PK), PK), :]
                g2[dst, :] = xo[pl.ds(pl.multiple_of(pos_ref[base + tile + t], PK), PK), :]
            rows = pl.ds(p * part, part)
            o1 = _unpack_rows([g1[pl.ds(p * part * PK + q, part, stride=PK), :] for q in range(PK)])
            o2 = _unpack_rows([g2[pl.ds(p * part * PK + q, part, stride=PK), :] for q in range(PK)])
            w = cw[p * part:(p + 1) * part]
            y_ref[rows, :] = x1_ref[rows, :] + w[:, R_W1:R_W1 + 1] * o1 + w[:, R_W2:R_W2 + 1] * o2


def _moe(pos, tab, stp, rt, h2p, x1, wg, wu, wd, tile, n_tiles, mt):
    n = x1.shape[0]
    n_blocks = n // (tile * n_tiles)
    n_disp = n_tiles
    n_steps = n_disp + EXPERT_STEPS + n_tiles

    def disp(b, s, *_):
        return (b * n_disp + jnp.minimum(s, n_disp - 1), 0)

    def comb(b, s, *_):
        return (b * n_tiles + jnp.clip(s - n_disp - EXPERT_STEPS, 0, n_tiles - 1), 0)

    def wmap(b, s, *_):
        return (jnp.clip(s - n_disp, 0, EXPERT_STEPS - 1), 0, 0)

    grid_spec = pltpu.PrefetchScalarGridSpec(
        num_scalar_prefetch=3,
        grid=(n_blocks, n_steps),
        in_specs=[pl.BlockSpec((tile * PK, LANES), disp),
                  pl.BlockSpec((tile, D), comb),
                  pl.BlockSpec((1, REC_ROWS, tile), lambda b, s, *_: comb(b, s) + (0,)),
                  pl.BlockSpec((STEP_EXPERTS, D, D_EXPERT), wmap),
                  pl.BlockSpec((STEP_EXPERTS, D, D_EXPERT), wmap),
                  pl.BlockSpec((STEP_EXPERTS, D_EXPERT, D), wmap)],
        out_specs=pl.BlockSpec((tile, D), comb),
        scratch_shapes=[
            pltpu.VMEM((_sorted_rows(tile * n_tiles, mt) * PK, LANES), jnp.uint32),
            pltpu.VMEM((tile * PK, LANES), jnp.uint32),
            pltpu.VMEM((tile * PK, LANES), jnp.uint32),
        ])
    return pl.pallas_call(
        functools.partial(_moe_kernel, tile=tile, n_tiles=n_tiles,
                          max_tiles=_max_tiles(tile * n_tiles, mt), mt=mt),
        grid_spec=grid_spec,
        out_shape=jax.ShapeDtypeStruct((n, D), jnp.float32),
        compiler_params=pltpu.CompilerParams(
            dimension_semantics=("arbitrary", "arbitrary"), vmem_limit_bytes=VMEM_LIMIT),
        name="moe",
    )(pos, tab, stp, h2p, x1, rt, wg, wu, wd)


def _max_tiles(blk_tokens, mt):
    return N_EXPERTS + 2 * blk_tokens // mt


def _sorted_positions(rt, cnt, blk_tokens, mt):
    experts = jnp.arange(N_EXPERTS, dtype=jnp.int32)
    n_tiles, _, tile = rt.shape
    counts = cnt[:, :, 0].astype(jnp.int32)
    pad = mt // 2
    padded = (counts + pad - 1) // pad * pad
    off = jnp.cumsum(padded, axis=1) - padded
    e = rt[:, R_E1:R_E1 + TOP_K, :].astype(jnp.int32)
    rank = rt[:, R_RANK1:R_RANK1 + TOP_K, :].astype(jnp.int32)
    off_tile = jnp.repeat(off, blk_tokens // tile, axis=0)
    hit = e[:, :, None, :] == experts[:, None]
    pos = (rank + jnp.sum(jnp.where(hit, off_tile[:, None, :, None], 0), axis=2)) * PK

    tiles = (counts + mt - 1) // mt
    t_end = jnp.cumsum(tiles, axis=1)
    t_beg = t_end - tiles
    j = jnp.arange(_max_tiles(blk_tokens, mt), dtype=jnp.int32)
    e_of = jnp.minimum(jnp.sum(t_end[:, None, :] <= j[None, :, None], axis=-1), N_EXPERTS - 1)
    pick = lambda a: jnp.sum(jnp.where(e_of[:, :, None] == experts, a[:, None, :], 0), axis=-1)
    local = (j[None, :] - pick(t_beg)) * mt
    tab = jnp.stack([(pick(off) + local) * PK, e_of % STEP_EXPERTS,
                     (pick(counts) - local > pad).astype(jnp.int32)], axis=-1)
    stp = jnp.stack([t_beg[:, ::STEP_EXPERTS], t_end[:, STEP_EXPERTS - 1::STEP_EXPERTS]], axis=-1)
    return pos.reshape(-1), tab.reshape(-1), stp.reshape(-1), rt


def _segment_matrices():
    head = np.arange(D) // HEAD_DIM
    seg = head[:, None] == np.arange(LANES)[None, :]
    expm = (np.arange(2 * LANES) % LANES)[:, None] == head[None, :]
    return jnp.asarray(seg, jnp.bfloat16), jnp.asarray(expm, jnp.bfloat16)


def _strict_upper(n):
    r = np.arange(n)
    return jnp.asarray(r[:, None] < r[None, :], jnp.bfloat16)


def kernel(x_prompt, x_sample, cache_k, cache_v, state_conv, norm_mix_g, w_in, conv_w, q_norm_g,
           k_norm_g, attn_sinks, w_conv_out, w_attn_out, w_out, norm_ffn_g, w_group, b_group,
           w_router, b_router, w_gate, w_up, w_down):
    nb, seq, _ = x_prompt.shape
    ns, sseq, _ = x_sample.shape
    l = 0
    seg, expm = _segment_matrices()
    pad = RT_ROWS - N_EXPERTS - N_GROUPS
    wr = jnp.concatenate([w_router[l].T, w_group[l].T, jnp.zeros((pad, D), jnp.float32)], axis=0)
    br = jnp.concatenate([b_router[l], b_group[l], jnp.zeros((pad,), jnp.float32)]).reshape(RT_ROWS, 1)
    wts = (norm_mix_g[l].reshape(1, D), _bf(w_in[l]), conv_w[l],
           jnp.tile(q_norm_g[l], N_HEADS).reshape(1, D), jnp.tile(k_norm_g[l], N_KV).reshape(1, LANES),
           seg, expm, _bf(w_conv_out[l]), _bf(w_attn_out[l]), _bf(w_out[l]),
           norm_ffn_g[l].reshape(1, D), _bf(wr), br)
    sinks = attn_sinks[l]
    n_s = ns * sseq

    sink2 = sinks * LOG2E
    sink_hi = _bf(sink2).astype(jnp.float32)
    q_extra = jnp.concatenate(
        [jnp.ones((N_HEADS, 1), jnp.float32), sink_hi[:, None], (sink2 - sink_hi)[:, None],
         jnp.zeros((N_HEADS, HEAD_DIM - Q_FEATS), jnp.float32)], axis=1).reshape(N_HEADS, 1, HEAD_DIM)
    x1p, h2pp, rtp, cntp, kwp, vwp, cstp, wg, wu, wd = _prompt_mixer(
        x_prompt, (w_gate[l], w_up[l], w_down[l]), q_extra, wts + (_strict_upper(TOK_TILE),), tile=TOK_TILE, blk_tiles=BLOCK_TILES)
    x1s, h2ps, rts, cnts, kws, vws, csts = _sample_mixer(
        x_sample, cache_k[l].reshape(ns, WINDOW, LANES), cache_v[l].reshape(ns, WINDOW, LANES),
        state_conv[l], q_extra, wts + (_strict_upper(n_s),))

    yp = _moe(*_sorted_positions(rtp, cntp, TOK_TILE * BLOCK_TILES, MOE_TILE), h2pp, x1p.reshape(nb * seq, D), wg, wu, wd,
              tile=TOK_TILE, n_tiles=BLOCK_TILES, mt=MOE_TILE)
    ys = _moe(*_sorted_positions(rts, cnts, n_s, SAMPLE_MOE_TILE), h2ps, x1s, wg, wu, wd, tile=n_s, n_tiles=1,
              mt=SAMPLE_MOE_TILE)

    kv_shape = lambda n: (1, n, WINDOW, N_KV, HEAD_DIM)
    return (yp.reshape(nb, seq, D), ys.reshape(ns, sseq, D),
            kwp.reshape(kv_shape(nb)), vwp.reshape(kv_shape(nb)), cstp.reshape(1, nb, CONV_HIST, D),
            kws.reshape(kv_shape(ns)), vws.reshape(kv_shape(ns)), csts.reshape(1, ns, CONV_HIST, D))
```

```python
import functools
import math

import jax
import jax.numpy as jnp
import numpy as np
from jax import lax
from jax.experimental import pallas as pl
from jax.experimental.pallas import tpu as pltpu

D = 1024
CHUNK = 64
HEAD_DIM = 64
N_HEADS = 16
N_KV = 2
WINDOW = 128
N_GROUPS = 4
EPG = 8
N_EXPERTS = 32
TOP_K = 2
D_EXPERT = 256
EPS = 1e-6
NEG = -1e30

LANES = 128
SUBLANES = 8
CONV_HIST = 2
U0 = SUBLANES
SLOTS = 4
KEYS = SLOTS * CHUNK
GROUP_HEADS = N_HEADS // N_KV
STACK = GROUP_HEADS * CHUNK
BIAS_LANE = HEAD_DIM
SINK_LANE = HEAD_DIM + 1
Q_FEATS = 3
LOG2E = 1.0 / math.log(2.0)
RT_ROWS = LANES
REC_ROWS = 8
HALF = D // 2
PK = HALF // LANES

R_E1, R_E2, R_W1, R_W2, R_RANK1, R_RANK2 = 0, 1, 2, 3, 4, 5

C_B, C_C, C_X, C_Q, C_K, C_V, C_G = 0, 1024, 2048, 3072, 4096, 4224, 4352

MOE_TILE = 288
SAMPLE_MOE_TILE = 64
TOK_TILE = 512
BLOCK_TILES = 8
UNROLL = 32
COMBINE_PARTS = 4
STEP_EXPERTS = 4
TAB_W = 3
EXPERT_STEPS = N_EXPERTS // STEP_EXPERTS

VMEM_LIMIT = 60 * 1024 * 1024

_NT = (((1,), (1,)), ((), ()))


def _bf(x):
    return x.astype(jnp.bfloat16)


def _dot(a, b):
    return jnp.dot(a, b, preferred_element_type=jnp.float32)


def _rms(x, g):
    return x * lax.rsqrt(jnp.mean(x * x, axis=-1, keepdims=True) + EPS) * g


def _sigmoid(x):
    return 1.0 / (1.0 + jnp.exp(-x))


def _pack_rows(x):
    return pltpu.pack_elementwise([x[:, :HALF], x[:, HALF:]], packed_dtype=jnp.bfloat16)


def _unpack_rows(parts):
    def half(i):
        return [pltpu.unpack_elementwise(p, index=i, packed_dtype=jnp.bfloat16,
                                         unpacked_dtype=jnp.float32) for p in parts]
    return jnp.concatenate(half(0) + half(1), axis=-1)


def _head_norm_q(q, seg_ref, exp_ref, qg):
    ssq = _dot(_bf(q * q), seg_ref[...])
    inv = lax.rsqrt(ssq * (1.0 / HEAD_DIM) + EPS)
    hi = _bf(inv)
    lo = _bf(inv - hi.astype(jnp.float32))
    full = _dot(jnp.concatenate([hi, lo], axis=-1), exp_ref[...])
    return q * full * qg


def _head_norm_k(k, kg):
    lane = lax.broadcasted_iota(jnp.int32, k.shape, 1)
    lo_half = lane < HEAD_DIM
    sq = k * k
    s0 = jnp.sum(jnp.where(lo_half, sq, 0.0), axis=-1, keepdims=True)
    s1 = jnp.sum(jnp.where(lo_half, 0.0, sq), axis=-1, keepdims=True)
    i0 = lax.rsqrt(s0 * (1.0 / HEAD_DIM) + EPS)
    i1 = lax.rsqrt(s1 * (1.0 / HEAD_DIM) + EPS)
    return k * jnp.where(lo_half, i0, i1) * kg


def _attend_group(qs, keys, vals):
    s = lax.dot_general(qs, keys, _NT, preferred_element_type=jnp.float32)
    m = jnp.max(s, axis=-1, keepdims=True)
    o = _dot(_bf(jnp.exp2(s - m)), vals)
    return o[:, 0:LANES] / o[:, LANES:2 * LANES]


def _key_rows(k, v, g, lo_half):
    ksw = pltpu.roll(k, HEAD_DIM, axis=1)
    vsw = pltpu.roll(v, HEAD_DIM, axis=1)
    kb = jnp.where(lo_half, k if g == 0 else ksw, 0.0)
    vb = jnp.where(lo_half, v, vsw) if g == 0 else jnp.where(lo_half, vsw, v)
    return _bf(kb), _bf(vb)


def _sink_block(shape):
    row = lax.broadcasted_iota(jnp.int32, shape, len(shape) - 2)
    ln = lax.broadcasted_iota(jnp.int32, shape, len(shape) - 1)
    is_sink = (row == 0) & ((ln == SINK_LANE) | (ln == SINK_LANE + 1))
    return _bf(jnp.where(is_sink, 1.0, jnp.where((row > 0) & (ln == BIAS_LANE), NEG, 0.0)))


def _head_pairs(out, rows_per_head, lo_half):
    blk = lambda r: out[r * rows_per_head:(r + 1) * rows_per_head, :]
    return [_bf(jnp.where(lo_half, blk(r), blk(r + 1))) for r in range(0, GROUP_HEADS, 2)]


def _route(lt, upper_ref, carry):
    t = lt.shape[1]
    big = float(RT_ROWS)
    grow = lax.broadcasted_iota(jnp.int32, (REC_ROWS, t), 0).astype(jnp.float32)
    is_g = grow < N_GROUPS
    gl = jnp.where(is_g, lt[N_EXPERTS:N_EXPERTS + REC_ROWS], NEG)
    gmax = jnp.max(gl, axis=0, keepdims=True)
    gsum = jnp.sum(jnp.where(is_g, jnp.exp(gl - gmax), 0.0), axis=0, keepdims=True)
    g_w = 1.0 / gsum
    g_sel = jnp.min(jnp.where(is_g & (gl == gmax), grow, big), axis=0, keepdims=True)
    logits = lt[0:N_EXPERTS]
    erow = lax.broadcasted_iota(jnp.int32, (N_EXPERTS, t), 0).astype(jnp.float32)
    lo = g_sel * EPG
    in_grp = (erow >= lo) & (erow < lo + EPG)
    el = jnp.where(in_grp, logits, NEG)
    v1 = jnp.max(el, axis=0, keepdims=True)
    i1 = jnp.min(jnp.where(in_grp & (el == v1), erow, big), axis=0, keepdims=True)
    rest = in_grp & (erow != i1)
    el2 = jnp.where(rest, logits, NEG)
    v2 = jnp.max(el2, axis=0, keepdims=True)
    i2 = jnp.min(jnp.where(rest & (el2 == v2), erow, big), axis=0, keepdims=True)
    e2 = jnp.exp(v2 - v1)
    den = 1.0 + e2
    w1 = (1.0 / den) * g_w
    w2 = (e2 / den) * g_w
    oh1 = jnp.where(erow == i1, 1.0, 0.0)
    oh2 = jnp.where(erow == i2, 1.0, 0.0)
    both = oh1 + oh2
    before = _dot(_bf(both), upper_ref[...]) + carry
    r1 = jnp.sum(before * oh1, axis=0, keepdims=True)
    r2 = jnp.sum(before * oh2, axis=0, keepdims=True)
    rec = jnp.zeros((REC_ROWS, t), jnp.float32)
    for rw, val in ((R_E1, i1), (R_E2, i2), (R_W1, w1), (R_W2, w2), (R_RANK1, r1), (R_RANK2, r2)):
        rec = jnp.where(grow == rw, val, rec)
    return rec, carry + jnp.sum(both, axis=1, keepdims=True)


def _route_stage(x1, g2_ref, wr_ref, br_ref, upper_ref, carry, h2p_ref, rt_ref):
    rows = x1.shape[0]
    h2 = _rms(x1, g2_ref[...])
    pk = _pack_rows(h2)
    for j in range(PK):
        h2p_ref[pl.ds(j, rows, stride=PK), :] = pk[:, j * LANES:(j + 1) * LANES]
    lt = lax.dot_general(wr_ref[...], _bf(h2), _NT, preferred_element_type=jnp.float32) + br_ref[...]
    rec, carry = _route(lt, upper_ref, carry)
    rt_ref[...] = rec
    return carry


def _merge_tail(x, macc, wo_ref, g2_ref, wr_ref, br_ref, upper_ref, carry, x1_ref, h2p_ref, rt_ref):
    x1 = x + _dot(_bf(macc), wo_ref[...])
    x1_ref[...] = x1
    return _route_stage(x1, g2_ref, wr_ref, br_ref, upper_ref, carry, h2p_ref, rt_ref)


def _prompt_mixer_kernel(x_ref, wg32_ref, wu32_ref, wd32_ref,
                         qx_ref, g1_ref, win_ref, cw_ref, qg_ref, kg_ref, seg_ref, exp_ref,
                         wco_ref, wao_ref, wo_ref, g2_ref, wr_ref, br_ref, upper_ref,
                         x1_ref, h2p_ref, rt_ref, cnt_ref, kwin_ref, vwin_ref, cst_ref,
                         wg16_ref, wu16_ref, wd16_ref,
                         uext, qs_s, att_s, kn_s, kop, vop, khist, vhist, carry_s, x1_new, x1_old,
                         *, tile, blk_tiles, n_it, n_main):
    step = pl.program_id(0)
    it = step % n_it
    n_chunk = tile // CHUNK

    @pl.when(step == 0)
    def _():
        shape = (n_chunk, N_KV, CHUNK, LANES)
        tail = pl.ds((SLOTS - 1) * CHUNK, CHUNK)
        kop[:, :, tail, :] = _sink_block(shape)
        vop[:, :, tail, 0:LANES] = jnp.zeros(shape, jnp.bfloat16)
        vop[:, :, tail, LANES:2 * LANES] = jnp.ones(shape, jnp.bfloat16)
        x1_new[...] = jnp.zeros((tile, D), jnp.float32)
        carry_s[...] = jnp.zeros_like(carry_s)

    def route_previous(x1_prev):
        first_of_block = (step + blk_tiles - 1) % blk_tiles == 0
        carry = jnp.where(first_of_block, 0.0, carry_s[...])
        carry = _route_stage(x1_prev[...], g2_ref, wr_ref, br_ref, upper_ref, carry,
                             h2p_ref, rt_ref.at[0])
        carry_s[...] = carry
        cnt_ref[0] = jnp.broadcast_to(carry, (N_EXPERTS, LANES))

    @pl.when(step == n_main)
    def _():
        route_previous(x1_new)

    @pl.when((step < n_main) & (it == 0))
    def _():
        uext[0:U0, :] = jnp.zeros((U0, D), jnp.float32)
        ln = lax.broadcasted_iota(jnp.int32, khist.shape, 2)
        khist[...] = _bf(jnp.where(ln == BIAS_LANE, NEG, 0.0))
        vhist[...] = jnp.zeros_like(vhist)

    @pl.when(step < n_main)
    def _():
        x1_old[...] = x1_new[...]
        _mixer_tile(x_ref, wg32_ref, wu32_ref, wd32_ref, qx_ref, g1_ref, win_ref, cw_ref, qg_ref,
                    kg_ref, seg_ref, exp_ref, wco_ref, wao_ref, wo_ref, x1_ref, kwin_ref, vwin_ref,
                    cst_ref, wg16_ref, wu16_ref, wd16_ref, uext, qs_s, att_s, kn_s, kop, vop, khist,
                    vhist, x1_new, tile=tile, side_job=lambda: route_previous(x1_old))


def _mixer_tile(x_ref, wg32_ref, wu32_ref, wd32_ref, qx_ref, g1_ref, win_ref, cw_ref, qg_ref,
                kg_ref, seg_ref, exp_ref, wco_ref, wao_ref, wo_ref, x1_ref, kwin_ref, vwin_ref,
                cst_ref, wg16_ref, wu16_ref, wd16_ref, uext, qs_s, att_s, kn_s, kop, vop, khist,
                vhist, x1_keep, *, tile, side_job):
    n_chunk = tile // CHUNK
    wg16_ref[...] = _bf(wg32_ref[...])
    wu16_ref[...] = _bf(wu32_ref[...])
    wd16_ref[...] = _bf(wd32_ref[...])

    x = x_ref[0]
    h = _bf(_rms(x, g1_ref[...]))

    q = _dot(h, win_ref[:, C_Q:C_Q + D])
    qn = _head_norm_q(q, seg_ref, exp_ref, qg_ref[...]) * (LOG2E / math.sqrt(HEAD_DIM))
    for hd in range(N_HEADS):
        g, r = divmod(hd, GROUP_HEADS)
        ext = jnp.broadcast_to(qx_ref[hd], (tile, HEAD_DIM))
        piece = _bf(jnp.concatenate([qn[:, hd * HEAD_DIM:(hd + 1) * HEAD_DIM], ext], axis=-1))
        for c in range(n_chunk):
            qs_s[g, c * STACK + r * CHUNK:c * STACK + (r + 1) * CHUNK, :] = (
                piece[c * CHUNK:(c + 1) * CHUNK, :])
    kv = _dot(h, win_ref[:, C_K:C_K + 2 * LANES])
    kn = _head_norm_k(kv[:, 0:LANES], kg_ref[...])
    vv = kv[:, LANES:2 * LANES]
    kn_s[0] = kn
    kn_s[1] = vv

    kwin_ref[0] = kn[tile - WINDOW:, :]
    vwin_ref[0] = vv[tile - WINDOW:, :]

    u = _dot(h, win_ref[:, C_C:C_C + D]) * _dot(h, win_ref[:, C_X:C_X + D])
    uext[U0:U0 + tile, :] = u
    conv = (uext[U0 - 2:U0 - 2 + tile, :] * cw_ref[0:1, :] + uext[U0 - 1:U0 - 1 + tile, :] * cw_ref[1:2, :]
            + u * cw_ref[2:3, :])
    yc = _bf(_dot(h, win_ref[:, C_B:C_B + D]) * conv)
    o_c = _dot(yc, wco_ref[...])
    macc = _sigmoid(_dot(h, win_ref[:, C_G:C_G + D])) * o_c
    last2 = uext[U0 + tile - CONV_HIST:U0 + tile, :]
    uext[U0 - CONV_HIST:U0, :] = last2

    cst_ref[0] = last2
    side_job()

    lane = lax.broadcasted_iota(jnp.int32, (CHUNK, LANES), 1)
    lo_half = lane < HEAD_DIM
    ones_blk = jnp.ones((CHUNK, LANES), jnp.bfloat16)
    own = SLOTS - 2

    for c in range(own):
        for sl in range(own - c):
            src = pl.ds((c + sl) * CHUNK, CHUNK)
            kop[c, :, pl.ds(sl * CHUNK, CHUNK), :] = khist[:, src, :]
            vop[c, :, pl.ds(sl * CHUNK, CHUNK), :] = vhist[:, src, :]

    def chunk_body(c):
        r0 = c * CHUNK
        kc = kn_s[0, pl.ds(r0, CHUNK), :]
        vc = kn_s[1, pl.ds(r0, CHUNK), :]
        for g in range(N_KV):
            kb, vb = _key_rows(kc, vc, g, lo_half)
            for d in range(own + 1):
                if c + d < n_chunk:
                    dst = pl.ds((own - d) * CHUNK, CHUNK)
                    kop[c + d, g, dst, :] = kb
                    vop[c + d, g, dst, 0:LANES] = vb
                    vop[c + d, g, dst, LANES:2 * LANES] = ones_blk
            if c >= n_chunk - own:
                dst = pl.ds((c - (n_chunk - own)) * CHUNK, CHUNK)
                khist[g, dst, :] = kb
                vhist[g, dst, 0:LANES] = vb
                vhist[g, dst, LANES:2 * LANES] = ones_blk
        for g in range(N_KV):
            out = _attend_group(qs_s[g, pl.ds(c * STACK, STACK), :], kop[c, g], vop[c, g])
            for i, pair in enumerate(_head_pairs(out, CHUNK, lo_half)):
                col = (g * GROUP_HEADS + 2 * i) * HEAD_DIM
                att_s[pl.ds(r0, CHUNK), col:col + LANES] = pair

    for c in range(n_chunk):
        chunk_body(c)

    o_a = _dot(att_s[...], wao_ref[...])
    macc = macc + _sigmoid(_dot(h, win_ref[:, C_G + D:C_G + 2 * D])) * o_a
    x1 = x + _dot(_bf(macc), wo_ref[...])
    x1_ref[0] = x1
    x1_keep[...] = x1


def _const_spec(shape):
    nd = len(shape)
    return pl.BlockSpec(shape, lambda *_: (0,) * nd, pipeline_mode=pl.Buffered(1))


def _prompt_mixer(x, experts32, q_extra, wts, tile, blk_tiles):
    nb, seq, _ = x.shape
    n_it = seq // tile
    n_main = nb * n_it
    bps = n_it // blk_tiles
    parts = n_main // N_EXPERTS
    consts = (q_extra,) + tuple(wts)

    cur = lambda s: jnp.minimum(s, n_main - 1)
    prev = lambda s: jnp.maximum(s - 1, 0)
    row_spec = lambda w: pl.BlockSpec((1, tile, w), lambda s: (cur(s) // n_it, cur(s) % n_it, 0))
    str_spec = lambda r, w: pl.BlockSpec((1, r, w), lambda s: (cur(s) // n_it, 0, 0))
    part_spec = lambda r, w: pl.BlockSpec(
        (1, r // parts, w), lambda s: (cur(s) // parts, cur(s) % parts, 0))
    expert_specs = [part_spec(D, D_EXPERT), part_spec(D, D_EXPERT), part_spec(D_EXPERT, D)]
    out_shape = [
        jax.ShapeDtypeStruct((nb, seq, D), jnp.float32),
        jax.ShapeDtypeStruct((nb * seq * PK, LANES), jnp.uint32),
        jax.ShapeDtypeStruct((n_main, REC_ROWS, tile), jnp.float32),
        jax.ShapeDtypeStruct((nb * bps, N_EXPERTS, LANES), jnp.float32),
        jax.ShapeDtypeStruct((nb, WINDOW, LANES), jnp.float32),
        jax.ShapeDtypeStruct((nb, WINDOW, LANES), jnp.float32),
        jax.ShapeDtypeStruct((nb, CONV_HIST, D), jnp.float32),
    ] + [jax.ShapeDtypeStruct(w.shape, jnp.bfloat16) for w in experts32]
    return pl.pallas_call(
        functools.partial(_prompt_mixer_kernel, tile=tile, blk_tiles=blk_tiles, n_it=n_it, n_main=n_main),
        grid=(n_main + 1,),
        in_specs=[row_spec(D)] + expert_specs + [_const_spec(w.shape) for w in consts],
        out_specs=[row_spec(D),
                   pl.BlockSpec((tile * PK, LANES), lambda s: (prev(s), 0)),
                   pl.BlockSpec((1, REC_ROWS, tile), lambda s: (prev(s), 0, 0)),
                   pl.BlockSpec((1, N_EXPERTS, LANES), lambda s: (prev(s) // blk_tiles, 0, 0)),
                   str_spec(WINDOW, LANES), str_spec(WINDOW, LANES), str_spec(CONV_HIST, D)] + expert_specs,
        out_shape=out_shape,
        scratch_shapes=[
            pltpu.VMEM((U0 + tile, D), jnp.float32),
            pltpu.VMEM((N_KV, tile * GROUP_HEADS, LANES), jnp.bfloat16),
            pltpu.VMEM((tile, D), jnp.bfloat16),
            pltpu.VMEM((2, tile, LANES), jnp.float32),
            pltpu.VMEM((tile // CHUNK, N_KV, KEYS, LANES), jnp.bfloat16),
            pltpu.VMEM((tile // CHUNK, N_KV, KEYS, 2 * LANES), jnp.bfloat16),
            pltpu.VMEM((N_KV, (SLOTS - 2) * CHUNK, LANES), jnp.bfloat16),
            pltpu.VMEM((N_KV, (SLOTS - 2) * CHUNK, 2 * LANES), jnp.bfloat16),
            pltpu.VMEM((N_EXPERTS, 1), jnp.float32),
            pltpu.VMEM((tile, D), jnp.float32),
            pltpu.VMEM((tile, D), jnp.float32),
        ],
        compiler_params=pltpu.CompilerParams(
            dimension_semantics=("arbitrary",), vmem_limit_bytes=VMEM_LIMIT),
        name="prompt_mixer",
    )(x, *experts32, *consts)


def _sample_mixer_kernel(x_ref, ck_ref, cv_ref, sc_ref, qx_ref, g1_ref, win_ref, cw_ref, qg_ref,
                         kg_ref, seg_ref, exp_ref, wco_ref, wao_ref, wo_ref, g2_ref, wr_ref, br_ref,
                         upper_ref, x1_ref, h2p_ref, rt_ref, cnt_ref, kwin_ref, vwin_ref, cst_ref,
                         uext, att_s, qs_s, kop, vop, *, n_stream, seq):
    ext = U0 + seq
    n_keys = WINDOW + seq
    x = x_ref[...]
    h = _bf(_rms(x, g1_ref[...]))

    u = _dot(h, win_ref[:, C_C:C_C + D]) * _dot(h, win_ref[:, C_X:C_X + D])
    for s in range(n_stream):
        r0 = s * ext + U0
        uext[r0 - CONV_HIST:r0, :] = sc_ref[s]
        uext[r0:r0 + seq, :] = u[s * seq:(s + 1) * seq, :]
        cst_ref[s] = u[(s + 1) * seq - CONV_HIST:(s + 1) * seq, :]
    conv = jnp.concatenate(
        [uext[s * ext + U0 - 2:s * ext + U0 - 2 + seq, :] * cw_ref[0:1, :]
         + uext[s * ext + U0 - 1:s * ext + U0 - 1 + seq, :] * cw_ref[1:2, :] for s in range(n_stream)],
        axis=0) + u * cw_ref[2:3, :]
    yc = _bf(_dot(h, win_ref[:, C_B:C_B + D]) * conv)
    o_c = _dot(yc, wco_ref[...])
    macc = _sigmoid(_dot(h, win_ref[:, C_G:C_G + D])) * o_c

    q = _dot(h, win_ref[:, C_Q:C_Q + D])
    qn = _head_norm_q(q, seg_ref, exp_ref, qg_ref[...]) * (LOG2E / math.sqrt(HEAD_DIM))
    kv = _dot(h, win_ref[:, C_K:C_K + 2 * LANES])
    kn = _head_norm_k(kv[:, 0:LANES], kg_ref[...])
    vv = kv[:, LANES:2 * LANES]

    lo_win = lax.broadcasted_iota(jnp.int32, (WINDOW, LANES), 1) < HEAD_DIM
    lo_new = lax.broadcasted_iota(jnp.int32, (seq, LANES), 1) < HEAD_DIM
    tail = KEYS - n_keys
    sink_rows = _sink_block((tail, LANES))
    for s in range(n_stream):
        for g in range(N_KV):
            kop[s, g, n_keys:KEYS, :] = sink_rows
            vop[s, g, n_keys:KEYS, 0:LANES] = jnp.zeros((tail, LANES), jnp.bfloat16)
            vop[s, g, :, LANES:2 * LANES] = jnp.ones((KEYS, LANES), jnp.bfloat16)
        rows = slice(s * seq, (s + 1) * seq)
        kwin_ref[s, 0:WINDOW - seq, :] = ck_ref[s, seq:WINDOW, :]
        kwin_ref[s, WINDOW - seq:WINDOW, :] = kn[rows, :]
        vwin_ref[s, 0:WINDOW - seq, :] = cv_ref[s, seq:WINDOW, :]
        vwin_ref[s, WINDOW - seq:WINDOW, :] = vv[rows, :]
        for g in range(N_KV):
            kop[s, g, 0:WINDOW, :], vop[s, g, 0:WINDOW, 0:LANES] = _key_rows(ck_ref[s], cv_ref[s], g, lo_win)
            kop[s, g, WINDOW:n_keys, :], vop[s, g, WINDOW:n_keys, 0:LANES] = _key_rows(
                kn[rows, :], vv[rows, :], g, lo_new)
        for hd in range(N_HEADS):
            g, r = divmod(hd, GROUP_HEADS)
            feat = jnp.broadcast_to(qx_ref[hd], (seq, HEAD_DIM))
            qs_s[s, g, r * seq:(r + 1) * seq, :] = _bf(
                jnp.concatenate([qn[rows, hd * HEAD_DIM:(hd + 1) * HEAD_DIM], feat], axis=-1))
        for g in range(N_KV):
            out = _attend_group(qs_s[s, g], kop[s, g], vop[s, g])
            for i, pair in enumerate(_head_pairs(out, seq, lo_new)):
                col = (g * GROUP_HEADS + 2 * i) * HEAD_DIM
                att_s[rows, col:col + LANES] = pair

    o_a = _dot(att_s[...], wao_ref[...])
    macc = macc + _sigmoid(_dot(h, win_ref[:, C_G + D:C_G + 2 * D])) * o_a
    carry = _merge_tail(x, macc, wo_ref, g2_ref, wr_ref, br_ref, upper_ref,
                        jnp.zeros((N_EXPERTS, 1), jnp.float32), x1_ref, h2p_ref, rt_ref.at[0])
    cnt_ref[0] = jnp.broadcast_to(carry, (N_EXPERTS, LANES))


def _sample_mixer(x, ck, cv, sc, q_extra, wts):
    nb, seq, _ = x.shape
    rows = nb * seq
    x2 = x.reshape(rows, D)
    ins = (x2, ck, cv, sc, q_extra) + tuple(wts)
    full = lambda a: pl.BlockSpec(a.shape, lambda i, nd=a.ndim: (0,) * nd)
    out_shape = [
        jax.ShapeDtypeStruct((rows, D), jnp.float32),
        jax.ShapeDtypeStruct((rows * PK, LANES), jnp.uint32),
        jax.ShapeDtypeStruct((1, REC_ROWS, rows), jnp.float32),
        jax.ShapeDtypeStruct((1, N_EXPERTS, LANES), jnp.float32),
        jax.ShapeDtypeStruct((nb, WINDOW, LANES), jnp.float32),
        jax.ShapeDtypeStruct((nb, WINDOW, LANES), jnp.float32),
        jax.ShapeDtypeStruct((nb, CONV_HIST, D), jnp.float32),
    ]
    return pl.pallas_call(
        functools.partial(_sample_mixer_kernel, n_stream=nb, seq=seq),
        grid=(1,),
        in_specs=[full(a) for a in ins],
        out_specs=[full(a) for a in out_shape],
        out_shape=out_shape,
        scratch_shapes=[
            pltpu.VMEM((nb * (U0 + seq), D), jnp.float32),
            pltpu.VMEM((rows, D), jnp.bfloat16),
            pltpu.VMEM((nb, N_KV, GROUP_HEADS * seq, LANES), jnp.bfloat16),
            pltpu.VMEM((nb, N_KV, KEYS, LANES), jnp.bfloat16),
            pltpu.VMEM((nb, N_KV, KEYS, 2 * LANES), jnp.bfloat16),
        ],
        compiler_params=pltpu.CompilerParams(
            dimension_semantics=("arbitrary",), vmem_limit_bytes=VMEM_LIMIT),
        name="sample_mixer",
    )(*ins)


def _sorted_rows(blk_tokens, mt):
    rows = 2 * blk_tokens + N_EXPERTS * (mt // 2 - 1) + mt
    return -(-rows // mt) * mt


def _moe_kernel(pos_ref, tab_ref, stp_ref, h2p_ref, x1_ref, rt_ref, wg_ref, wu_ref, wd_ref, y_ref,
                xo, g1, g2, *, tile, n_tiles, max_tiles, mt):
    b = pl.program_id(0)
    s = pl.program_id(1)
    n_groups = tile // UNROLL
    n_disp = n_tiles

    @pl.when((b == 0) & (s == 0))
    def _():
        xo[...] = jnp.zeros_like(xo)

    @pl.when(s < n_disp)
    def _dispatch():
        base = (b * n_tiles + s) * (TOP_K * tile)

        def group(gi, carry):
            i0 = base + gi * UNROLL
            t0 = gi * (UNROLL * PK)
            for k in range(UNROLL):
                row = h2p_ref[pl.ds(pl.multiple_of(t0 + k * PK, PK), PK), :]
                xo[pl.ds(pl.multiple_of(pos_ref[i0 + k], PK), PK), :] = row
                xo[pl.ds(pl.multiple_of(pos_ref[i0 + tile + k], PK), PK), :] = row
            return carry

        lax.fori_loop(0, n_groups, group, 0)

    @pl.when((s >= n_disp) & (s < n_disp + EXPERT_STEPS))
    def _experts():
        st = b * EXPERT_STEPS + s - n_disp
        first = stp_ref[2 * st]
        last = stp_ref[2 * st + 1]
        half = mt // 2

        def entry(j):
            t = (b * max_tiles + j) * TAB_W
            return pl.multiple_of(tab_ref[t], half * PK), tab_ref[t + 1], tab_ref[t + 2]

        def load(j):
            base, _, _ = entry(j)
            return _bf(_unpack_rows([xo[pl.ds(base + q, mt, stride=PK), :] for q in range(PK)]))

        def swiglu(j, xs):
            _, k, _ = entry(j)
            gt = _dot(xs, wg_ref[k])
            return _bf(gt * _sigmoid(gt) * _dot(xs, wu_ref[k]))

        def up(j):
            return swiglu(j, load(j))

        def store(j, hid):
            base, k, _ = entry(j)
            out = _pack_rows(_dot(hid, wd_ref[k]))
            for q in range(PK):
                xo[pl.ds(base + q, mt, stride=PK), :] = out[:, q * LANES:(q + 1) * LANES]

        def down(j, hid):
            base, k, full = entry(j)
            out = _pack_rows(_dot(hid, wd_ref[k]))

            def put(lo):
                for q in range(PK):
                    xo[pl.ds(base + lo * PK + q, half, stride=PK), :] = (
                        out[lo:lo + half, q * LANES:(q + 1) * LANES])

            put(0)

            @pl.when(full > 0)
            def _():
                put(half)

        n_full = 0
        for k in range(STEP_EXPERTS):
            n_full = n_full + tab_ref[(b * max_tiles + jnp.minimum(first + k, max_tiles - 1)) * TAB_W + 2]
        usual = (last - first == STEP_EXPERTS) & (n_full == STEP_EXPERTS)

        @pl.when(usual)
        def _():
            xs = [load(first + k) for k in range(STEP_EXPERTS)]
            for k in range(STEP_EXPERTS):
                store(first + k, swiglu(first + k, xs[k]))

        @pl.when(jnp.logical_not(usual) & (last > first))
        def _():
            def both(j, hid):
                nxt = up(j)
                down(j - 1, hid)
                return nxt

            down(last - 1, lax.fori_loop(first + 1, last, both, up(first)))

    @pl.when(s >= n_disp + EXPERT_STEPS)
    def _combine():
        i = s - n_disp - EXPERT_STEPS
        base = (b * n_tiles + i) * (TOP_K * tile)
        rec = jnp.concatenate([rt_ref[0], jnp.zeros((LANES - REC_ROWS, tile), jnp.float32)], axis=0)
        cw = rec.T
        part = tile // COMBINE_PARTS
        for p in range(COMBINE_PARTS):
            for k in range(part):
                t = p * part + k
                dst = pl.ds(t * PK, PK)
                g1[dst, :] = xo[pl.ds(pl.multiple_of(pos_ref[base + t], PK), PK), :]
                g2[dst, :] = xo[pl.ds(pl.multiple_of(pos_ref[base + tile + t], PK), PK), :]
            rows = pl.ds(p * part, part)
            o1 = _unpack_rows([g1[pl.ds(p * part * PK + q, part, stride=PK), :] for q in range(PK)])
            o2 = _unpack_rows([g2[pl.ds(p * part * PK + q, part, stride=PK), :] for q in range(PK)])
            w = cw[p * part:(p + 1) * part]
            y_ref[rows, :] = x1_ref[rows, :] + w[:, R_W1:R_W1 + 1] * o1 + w[:, R_W2:R_W2 + 1] * o2


def _moe(pos, tab, stp, rt, h2p, x1, wg, wu, wd, tile, n_tiles, mt):
    n = x1.shape[0]
    n_blocks = n // (tile * n_tiles)
    n_disp = n_tiles
    n_steps = n_disp + EXPERT_STEPS + n_tiles

    def disp(b, s, *_):
        return (b * n_disp + jnp.minimum(s, n_disp - 1), 0)

    def comb(b, s, *_):
        return (b * n_tiles + jnp.clip(s - n_disp - EXPERT_STEPS, 0, n_tiles - 1), 0)

    def wmap(b, s, *_):
        return (jnp.clip(s - n_disp, 0, EXPERT_STEPS - 1), 0, 0)

    grid_spec = pltpu.PrefetchScalarGridSpec(
        num_scalar_prefetch=3,
        grid=(n_blocks, n_steps),
        in_specs=[pl.BlockSpec((tile * PK, LANES), disp),
                  pl.BlockSpec((tile, D), comb),
                  pl.BlockSpec((1, REC_ROWS, tile), lambda b, s, *_: comb(b, s) + (0,)),
                  pl.BlockSpec((STEP_EXPERTS, D, D_EXPERT), wmap),
                  pl.BlockSpec((STEP_EXPERTS, D, D_EXPERT), wmap),
                  pl.BlockSpec((STEP_EXPERTS, D_EXPERT, D), wmap)],
        out_specs=pl.BlockSpec((tile, D), comb),
        scratch_shapes=[
            pltpu.VMEM((_sorted_rows(tile * n_tiles, mt) * PK, LANES), jnp.uint32),
            pltpu.VMEM((tile * PK, LANES), jnp.uint32),
            pltpu.VMEM((tile * PK, LANES), jnp.uint32),
        ])
    return pl.pallas_call(
        functools.partial(_moe_kernel, tile=tile, n_tiles=n_tiles,
                          max_tiles=_max_tiles(tile * n_tiles, mt), mt=mt),
        grid_spec=grid_spec,
        out_shape=jax.ShapeDtypeStruct((n, D), jnp.float32),
        compiler_params=pltpu.CompilerParams(
            dimension_semantics=("arbitrary", "arbitrary"), vmem_limit_bytes=VMEM_LIMIT),
        name="moe",
    )(pos, tab, stp, h2p, x1, rt, wg, wu, wd)


def _max_tiles(blk_tokens, mt):
    return N_EXPERTS + 2 * blk_tokens // mt


def _sorted_positions(rt, cnt, blk_tokens, mt):
    experts = jnp.arange(N_EXPERTS, dtype=jnp.int32)
    n_tiles, _, tile = rt.shape
    counts = cnt[:, :, 0].astype(jnp.int32)
    pad = mt // 2
    padded = (counts + pad - 1) // pad * pad
    off = jnp.cumsum(padded, axis=1) - padded
    e = rt[:, R_E1:R_E1 + TOP_K, :].astype(jnp.int32)
    rank = rt[:, R_RANK1:R_RANK1 + TOP_K, :].astype(jnp.int32)
    off_tile = jnp.repeat(off, blk_tokens // tile, axis=0)
    hit = e[:, :, None, :] == experts[:, None]
    pos = (rank + jnp.sum(jnp.where(hit, off_tile[:, None, :, None], 0), axis=2)) * PK

    tiles = (counts + mt - 1) // mt
    t_end = jnp.cumsum(tiles, axis=1)
    t_beg = t_end - tiles
    j = jnp.arange(_max_tiles(blk_tokens, mt), dtype=jnp.int32)
    e_of = jnp.minimum(jnp.sum(t_end[:, None, :] <= j[None, :, None], axis=-1), N_EXPERTS - 1)
    pick = lambda a: jnp.sum(jnp.where(e_of[:, :, None] == experts, a[:, None, :], 0), axis=-1)
    local = (j[None, :] - pick(t_beg)) * mt
    tab = jnp.stack([(pick(off) + local) * PK, e_of % STEP_EXPERTS,
                     (pick(counts) - local > pad).astype(jnp.int32)], axis=-1)
    stp = jnp.stack([t_beg[:, ::STEP_EXPERTS], t_end[:, STEP_EXPERTS - 1::STEP_EXPERTS]], axis=-1)
    return pos.reshape(-1), tab.reshape(-1), stp.reshape(-1), rt


def _segment_matrices():
    head = np.arange(D) // HEAD_DIM
    seg = head[:, None] == np.arange(LANES)[None, :]
    expm = (np.arange(2 * LANES) % LANES)[:, None] == head[None, :]
    return jnp.asarray(seg, jnp.bfloat16), jnp.asarray(expm, jnp.bfloat16)


def _strict_upper(n):
    r = np.arange(n)
    return jnp.asarray(r[:, None] < r[None, :], jnp.bfloat16)


def kernel(x_prompt, x_sample, cache_k, cache_v, state_conv, norm_mix_g, w_in, conv_w, q_norm_g,
           k_norm_g, attn_sinks, w_conv_out, w_attn_out, w_out, norm_ffn_g, w_group, b_group,
           w_router, b_router, w_gate, w_up, w_down):
    nb, seq, _ = x_prompt.shape
    ns, sseq, _ = x_sample.shape
    l = 0
    seg, expm = _segment_matrices()
    pad = RT_ROWS - N_EXPERTS - N_GROUPS
    wr = jnp.concatenate([w_router[l].T, w_group[l].T, jnp.zeros((pad, D), jnp.float32)], axis=0)
    br = jnp.concatenate([b_router[l], b_group[l], jnp.zeros((pad,), jnp.float32)]).reshape(RT_ROWS, 1)
    wts = (norm_mix_g[l].reshape(1, D), _bf(w_in[l]), conv_w[l],
           jnp.tile(q_norm_g[l], N_HEADS).reshape(1, D), jnp.tile(k_norm_g[l], N_KV).reshape(1, LANES),
           seg, expm, _bf(w_conv_out[l]), _bf(w_attn_out[l]), _bf(w_out[l]),
           norm_ffn_g[l].reshape(1, D), _bf(wr), br)
    sinks = attn_sinks[l]
    n_s = ns * sseq

    sink2 = sinks * LOG2E
    sink_hi = _bf(sink2).astype(jnp.float32)
    q_extra = jnp.concatenate(
        [jnp.ones((N_HEADS, 1), jnp.float32), sink_hi[:, None], (sink2 - sink_hi)[:, None],
         jnp.zeros((N_HEADS, HEAD_DIM - Q_FEATS), jnp.float32)], axis=1).reshape(N_HEADS, 1, HEAD_DIM)
    x1p, h2pp, rtp, cntp, kwp, vwp, cstp, wg, wu, wd = _prompt_mixer(
        x_prompt, (w_gate[l], w_up[l], w_down[l]), q_extra, wts + (_strict_upper(TOK_TILE),), tile=TOK_TILE, blk_tiles=BLOCK_TILES)
    x1s, h2ps, rts, cnts, kws, vws, csts = _sample_mixer(
        x_sample, cache_k[l].reshape(ns, WINDOW, LANES), cache_v[l].reshape(ns, WINDOW, LANES),
        state_conv[l], q_extra, wts + (_strict_upper(n_s),))

    yp = _moe(*_sorted_positions(rtp, cntp, TOK_TILE * BLOCK_TILES, MOE_TILE), h2pp, x1p.reshape(nb * seq, D), wg, wu, wd,
              tile=TOK_TILE, n_tiles=BLOCK_TILES, mt=MOE_TILE)
    ys = _moe(*_sorted_positions(rts, cnts, n_s, SAMPLE_MOE_TILE), h2ps, x1s, wg, wu, wd, tile=n_s, n_tiles=1,
              mt=SAMPLE_MOE_TILE)

    kv_shape = lambda n: (1, n, WINDOW, N_KV, HEAD_DIM)
    return (yp.reshape(nb, seq, D), ys.reshape(ns, sseq, D),
            kwp.reshape(kv_shape(nb)), vwp.reshape(kv_shape(nb)), cstp.reshape(1, nb, CONV_HIST, D),
            kws.reshape(kv_shape(ns)), vws.reshape(kv_shape(ns)), csts.reshape(1, ns, CONV_HIST, D))
```

```python
import functools
import math

import jax
import jax.numpy as jnp
import numpy as np
from jax import lax
from jax.experimental import pallas as pl
from jax.experimental.pallas import tpu as pltpu

D = 1024
CHUNK = 64
HEAD_DIM = 64
N_HEADS = 16
N_KV = 2
WINDOW = 128
N_GROUPS = 4
EPG = 8
N_EXPERTS = 32
TOP_K = 2
D_EXPERT = 256
EPS = 1e-6
NEG = -1e30

LANES = 128
SUBLANES = 8
CONV_HIST = 2
U0 = SUBLANES
SLOTS = 4
KEYS = SLOTS * CHUNK
GROUP_HEADS = N_HEADS // N_KV
STACK = GROUP_HEADS * CHUNK
BIAS_LANE = HEAD_DIM
SINK_LANE = HEAD_DIM + 1
Q_FEATS = 3
LOG2E = 1.0 / math.log(2.0)
RT_ROWS = LANES
REC_ROWS = 8
HALF = D // 2
PK = HALF // LANES

R_E1, R_E2, R_W1, R_W2, R_RANK1, R_RANK2 = 0, 1, 2, 3, 4, 5

C_B, C_C, C_X, C_Q, C_K, C_V, C_G = 0, 1024, 2048, 3072, 4096, 4224, 4352

MOE_TILE = 288
SAMPLE_MOE_TILE = 64
TOK_TILE = 512
BLOCK_TILES = 8
UNROLL = 32
COMBINE_PARTS = 4
STEP_EXPERTS = 4
TAB_W = 3
EXPERT_STEPS = N_EXPERTS // STEP_EXPERTS

VMEM_LIMIT = 60 * 1024 * 1024

_NT = (((1,), (1,)), ((), ()))


def _bf(x):
    return x.astype(jnp.bfloat16)


def _dot(a, b):
    return jnp.dot(a, b, preferred_element_type=jnp.float32)


def _rms(x, g):
    return x * lax.rsqrt(jnp.mean(x * x, axis=-1, keepdims=True) + EPS) * g


def _sigmoid(x):
    return 1.0 / (1.0 + jnp.exp(-x))


def _pack_rows(x):
    return pltpu.pack_elementwise([x[:, :HALF], x[:, HALF:]], packed_dtype=jnp.bfloat16)


def _unpack_rows(parts):
    def half(i):
        return [pltpu.unpack_elementwise(p, index=i, packed_dtype=jnp.bfloat16,
                                         unpacked_dtype=jnp.float32) for p in parts]
    return jnp.concatenate(half(0) + half(1), axis=-1)


def _head_norm_q(q, seg_ref, exp_ref, qg):
    ssq = _dot(_bf(q * q), seg_ref[...])
    inv = lax.rsqrt(ssq * (1.0 / HEAD_DIM) + EPS)
    hi = _bf(inv)
    lo = _bf(inv - hi.astype(jnp.float32))
    full = _dot(jnp.concatenate([hi, lo], axis=-1), exp_ref[...])
    return q * full * qg


def _head_norm_k(k, kg):
    lane = lax.broadcasted_iota(jnp.int32, k.shape, 1)
    lo_half = lane < HEAD_DIM
    sq = k * k
    s0 = jnp.sum(jnp.where(lo_half, sq, 0.0), axis=-1, keepdims=True)
    s1 = jnp.sum(jnp.where(lo_half, 0.0, sq), axis=-1, keepdims=True)
    i0 = lax.rsqrt(s0 * (1.0 / HEAD_DIM) + EPS)
    i1 = lax.rsqrt(s1 * (1.0 / HEAD_DIM) + EPS)
    return k * jnp.where(lo_half, i0, i1) * kg


def _attend_group(qs, keys, vals):
    s = lax.dot_general(qs, keys, _NT, preferred_element_type=jnp.float32)
    m = jnp.max(s, axis=-1, keepdims=True)
    o = _dot(_bf(jnp.exp2(s - m)), vals)
    return o[:, 0:LANES] / o[:, LANES:2 * LANES]


def _key_rows(k, v, g, lo_half):
    ksw = pltpu.roll(k, HEAD_DIM, axis=1)
    vsw = pltpu.roll(v, HEAD_DIM, axis=1)
    kb = jnp.where(lo_half, k if g == 0 else ksw, 0.0)
    vb = jnp.where(lo_half, v, vsw) if g == 0 else jnp.where(lo_half, vsw, v)
    return _bf(kb), _bf(vb)


def _sink_block(shape):
    row = lax.broadcasted_iota(jnp.int32, shape, len(shape) - 2)
    ln = lax.broadcasted_iota(jnp.int32, shape, len(shape) - 1)
    is_sink = (row == 0) & ((ln == SINK_LANE) | (ln == SINK_LANE + 1))
    return _bf(jnp.where(is_sink, 1.0, jnp.where((row > 0) & (ln == BIAS_LANE), NEG, 0.0)))


def _head_pairs(out, rows_per_head, lo_half):
    blk = lambda r: out[r * rows_per_head:(r + 1) * rows_per_head, :]
    return [_bf(jnp.where(lo_half, blk(r), blk(r + 1))) for r in range(0, GROUP_HEADS, 2)]


def _route(lt, upper_ref, carry):
    t = lt.shape[1]
    big = float(RT_ROWS)
    grow = lax.broadcasted_iota(jnp.int32, (REC_ROWS, t), 0).astype(jnp.float32)
    is_g = grow < N_GROUPS
    gl = jnp.where(is_g, lt[N_EXPERTS:N_EXPERTS + REC_ROWS], NEG)
    gmax = jnp.max(gl, axis=0, keepdims=True)
    gsum = jnp.sum(jnp.where(is_g, jnp.exp(gl - gmax), 0.0), axis=0, keepdims=True)
    g_w = 1.0 / gsum
    g_sel = jnp.min(jnp.where(is_g & (gl == gmax), grow, big), axis=0, keepdims=True)
    logits = lt[0:N_EXPERTS]
    erow = lax.broadcasted_iota(jnp.int32, (N_EXPERTS, t), 0).astype(jnp.float32)
    lo = g_sel * EPG
    in_grp = (erow >= lo) & (erow < lo + EPG)
    el = jnp.where(in_grp, logits, NEG)
    v1 = jnp.max(el, axis=0, keepdims=True)
    i1 = jnp.min(jnp.where(in_grp & (el == v1), erow, big), axis=0, keepdims=True)
    rest = in_grp & (erow != i1)
    el2 = jnp.where(rest, logits, NEG)
    v2 = jnp.max(el2, axis=0, keepdims=True)
    i2 = jnp.min(jnp.where(rest & (el2 == v2), erow, big), axis=0, keepdims=True)
    e2 = jnp.exp(v2 - v1)
    den = 1.0 + e2
    w1 = (1.0 / den) * g_w
    w2 = (e2 / den) * g_w
    oh1 = jnp.where(erow == i1, 1.0, 0.0)
    oh2 = jnp.where(erow == i2, 1.0, 0.0)
    both = oh1 + oh2
    before = _dot(_bf(both), upper_ref[...]) + carry
    r1 = jnp.sum(before * oh1, axis=0, keepdims=True)
    r2 = jnp.sum(before * oh2, axis=0, keepdims=True)
    rec = jnp.zeros((REC_ROWS, t), jnp.float32)
    for rw, val in ((R_E1, i1), (R_E2, i2), (R_W1, w1), (R_W2, w2), (R_RANK1, r1), (R_RANK2, r2)):
        rec = jnp.where(grow == rw, val, rec)
    return rec, carry + jnp.sum(both, axis=1, keepdims=True)


def _route_stage(x1, g2_ref, wr_ref, br_ref, upper_ref, carry, h2p_ref, rt_ref):
    rows = x1.shape[0]
    h2 = _rms(x1, g2_ref[...])
    pk = _pack_rows(h2)
    for j in range(PK):
        h2p_ref[pl.ds(j, rows, stride=PK), :] = pk[:, j * LANES:(j + 1) * LANES]
    lt = lax.dot_general(wr_ref[...], _bf(h2), _NT, preferred_element_type=jnp.float32) + br_ref[...]
    rec, carry = _route(lt, upper_ref, carry)
    rt_ref[...] = rec
    return carry


def _merge_tail(x, macc, wo_ref, g2_ref, wr_ref, br_ref, upper_ref, carry, x1_ref, h2p_ref, rt_ref):
    x1 = x + _dot(_bf(macc), wo_ref[...])
    x1_ref[...] = x1
    return _route_stage(x1, g2_ref, wr_ref, br_ref, upper_ref, carry, h2p_ref, rt_ref)


def _prompt_mixer_kernel(x_ref, wg32_ref, wu32_ref, wd32_ref,
                         qx_ref, g1_ref, win_ref, cw_ref, qg_ref, kg_ref, seg_ref, exp_ref,
                         wco_ref, wao_ref, wo_ref, g2_ref, wr_ref, br_ref, upper_ref,
                         x1_ref, h2p_ref, rt_ref, cnt_ref, kwin_ref, vwin_ref, cst_ref,
                         wg16_ref, wu16_ref, wd16_ref,
                         uext, qs_s, att_s, kn_s, kop, vop, khist, vhist, carry_s, x1_new, x1_old,
                         *, tile, blk_tiles, n_it, n_main):
    step = pl.program_id(0)
    it = step % n_it
    n_chunk = tile // CHUNK

    @pl.when(step == 0)
    def _():
        shape = (n_chunk, N_KV, CHUNK, LANES)
        tail = pl.ds((SLOTS - 1) * CHUNK, CHUNK)
        kop[:, :, tail, :] = _sink_block(shape)
        vop[:, :, tail, 0:LANES] = jnp.zeros(shape, jnp.bfloat16)
        vop[:, :, tail, LANES:2 * LANES] = jnp.ones(shape, jnp.bfloat16)
        x1_new[...] = jnp.zeros((tile, D), jnp.float32)
        carry_s[...] = jnp.zeros_like(carry_s)

    def route_previous(x1_prev):
        first_of_block = (step + blk_tiles - 1) % blk_tiles == 0
        carry = jnp.where(first_of_block, 0.0, carry_s[...])
        carry = _route_stage(x1_prev[...], g2_ref, wr_ref, br_ref, upper_ref, carry,
                             h2p_ref, rt_ref.at[0])
        carry_s[...] = carry
        cnt_ref[0] = jnp.broadcast_to(carry, (N_EXPERTS, LANES))

    @pl.when(step == n_main)
    def _():
        route_previous(x1_new)

    @pl.when((step < n_main) & (it == 0))
    def _():
        uext[0:U0, :] = jnp.zeros((U0, D), jnp.float32)
        ln = lax.broadcasted_iota(jnp.int32, khist.shape, 2)
        khist[...] = _bf(jnp.where(ln == BIAS_LANE, NEG, 0.0))
        vhist[...] = jnp.zeros_like(vhist)

    @pl.when(step < n_main)
    def _():
        x1_old[...] = x1_new[...]
        _mixer_tile(x_ref, wg32_ref, wu32_ref, wd32_ref, qx_ref, g1_ref, win_ref, cw_ref, qg_ref,
                    kg_ref, seg_ref, exp_ref, wco_ref, wao_ref, wo_ref, x1_ref, kwin_ref, vwin_ref,
                    cst_ref, wg16_ref, wu16_ref, wd16_ref, uext, qs_s, att_s, kn_s, kop, vop, khist,
                    vhist, x1_new, tile=tile, side_job=lambda: route_previous(x1_old))


def _mixer_tile(x_ref, wg32_ref, wu32_ref, wd32_ref, qx_ref, g1_ref, win_ref, cw_ref, qg_ref,
                kg_ref, seg_ref, exp_ref, wco_ref, wao_ref, wo_ref, x1_ref, kwin_ref, vwin_ref,
                cst_ref, wg16_ref, wu16_ref, wd16_ref, uext, qs_s, att_s, kn_s, kop, vop, khist,
                vhist, x1_keep, *, tile, side_job):
    n_chunk = tile // CHUNK
    wg16_ref[...] = _bf(wg32_ref[...])
    wu16_ref[...] = _bf(wu32_ref[...])
    wd16_ref[...] = _bf(wd32_ref[...])

    x = x_ref[0]
    h = _bf(_rms(x, g1_ref[...]))

    q = _dot(h, win_ref[:, C_Q:C_Q + D])
    qn = _head_norm_q(q, seg_ref, exp_ref, qg_ref[...]) * (LOG2E / math.sqrt(HEAD_DIM))
    for hd in range(N_HEADS):
        g, r = divmod(hd, GROUP_HEADS)
        ext = jnp.broadcast_to(qx_ref[hd], (tile, HEAD_DIM))
        piece = _bf(jnp.concatenate([qn[:, hd * HEAD_DIM:(hd + 1) * HEAD_DIM], ext], axis=-1))
        for c in range(n_chunk):
            qs_s[g, c * STACK + r * CHUNK:c * STACK + (r + 1) * CHUNK, :] = (
                piece[c * CHUNK:(c + 1) * CHUNK, :])
    kv = _dot(h, win_ref[:, C_K:C_K + 2 * LANES])
    kn = _head_norm_k(kv[:, 0:LANES], kg_ref[...])
    vv = kv[:, LANES:2 * LANES]
    kn_s[0] = kn
    kn_s[1] = vv

    kwin_ref[0] = kn[tile - WINDOW:, :]
    vwin_ref[0] = vv[tile - WINDOW:, :]

    u = _dot(h, win_ref[:, C_C:C_C + D]) * _dot(h, win_ref[:, C_X:C_X + D])
    uext[U0:U0 + tile, :] = u
    conv = (uext[U0 - 2:U0 - 2 + tile, :] * cw_ref[0:1, :] + uext[U0 - 1:U0 - 1 + tile, :] * cw_ref[1:2, :]
            + u * cw_ref[2:3, :])
    yc = _bf(_dot(h, win_ref[:, C_B:C_B + D]) * conv)
    o_c = _dot(yc, wco_ref[...])
    macc = _sigmoid(_dot(h, win_ref[:, C_G:C_G + D])) * o_c
    last2 = uext[U0 + tile - CONV_HIST:U0 + tile, :]
    uext[U0 - CONV_HIST:U0, :] = last2

    cst_ref[0] = last2
    side_job()

    lane = lax.broadcasted_iota(jnp.int32, (CHUNK, LANES), 1)
    lo_half = lane < HEAD_DIM
    ones_blk = jnp.ones((CHUNK, LANES), jnp.bfloat16)
    own = SLOTS - 2

    for c in range(own):
        for sl in range(own - c):
            src = pl.ds((c + sl) * CHUNK, CHUNK)
            kop[c, :, pl.ds(sl * CHUNK, CHUNK), :] = khist[:, src, :]
            vop[c, :, pl.ds(sl * CHUNK, CHUNK), :] = vhist[:, src, :]

    def chunk_body(c):
        r0 = c * CHUNK
        kc = kn_s[0, pl.ds(r0, CHUNK), :]
        vc = kn_s[1, pl.ds(r0, CHUNK), :]
        for g in range(N_KV):
            kb, vb = _key_rows(kc, vc, g, lo_half)
            for d in range(own + 1):
                if c + d < n_chunk:
                    dst = pl.ds((own - d) * CHUNK, CHUNK)
                    kop[c + d, g, dst, :] = kb
                    vop[c + d, g, dst, 0:LANES] = vb
                    vop[c + d, g, dst, LANES:2 * LANES] = ones_blk
            if c >= n_chunk - own:
                dst = pl.ds((c - (n_chunk - own)) * CHUNK, CHUNK)
                khist[g, dst, :] = kb
                vhist[g, dst, 0:LANES] = vb
                vhist[g, dst, LANES:2 * LANES] = ones_blk
        for g in range(N_KV):
            out = _attend_group(qs_s[g, pl.ds(c * STACK, STACK), :], kop[c, g], vop[c, g])
            for i, pair in enumerate(_head_pairs(out, CHUNK, lo_half)):
                col = (g * GROUP_HEADS + 2 * i) * HEAD_DIM
                att_s[pl.ds(r0, CHUNK), col:col + LANES] = pair

    for c in range(n_chunk):
        chunk_body(c)

    o_a = _dot(att_s[...], wao_ref[...])
    macc = macc + _sigmoid(_dot(h, win_ref[:, C_G + D:C_G + 2 * D])) * o_a
    x1 = x + _dot(_bf(macc), wo_ref[...])
    x1_ref[0] = x1
    x1_keep[...] = x1


def _const_spec(shape):
    nd = len(shape)
    return pl.BlockSpec(shape, lambda *_: (0,) * nd, pipeline_mode=pl.Buffered(1))


def _prompt_mixer(x, experts32, q_extra, wts, tile, blk_tiles):
    nb, seq, _ = x.shape
    n_it = seq // tile
    n_main = nb * n_it
    bps = n_it // blk_tiles
    parts = n_main // N_EXPERTS
    consts = (q_extra,) + tuple(wts)

    cur = lambda s: jnp.minimum(s, n_main - 1)
    prev = lambda s: jnp.maximum(s - 1, 0)
    row_spec = lambda w: pl.BlockSpec((1, tile, w), lambda s: (cur(s) // n_it, cur(s) % n_it, 0))
    str_spec = lambda r, w: pl.BlockSpec((1, r, w), lambda s: (cur(s) // n_it, 0, 0))
    part_spec = lambda r, w: pl.BlockSpec(
        (1, r // parts, w), lambda s: (cur(s) // parts, cur(s) % parts, 0))
    expert_specs = [part_spec(D, D_EXPERT), part_spec(D, D_EXPERT), part_spec(D_EXPERT, D)]
    out_shape = [
        jax.ShapeDtypeStruct((nb, seq, D), jnp.float32),
        jax.ShapeDtypeStruct((nb * seq * PK, LANES), jnp.uint32),
        jax.ShapeDtypeStruct((n_main, REC_ROWS, tile), jnp.float32),
        jax.ShapeDtypeStruct((nb * bps, N_EXPERTS, LANES), jnp.float32),
        jax.ShapeDtypeStruct((nb, WINDOW, LANES), jnp.float32),
        jax.ShapeDtypeStruct((nb, WINDOW, LANES), jnp.float32),
        jax.ShapeDtypeStruct((nb, CONV_HIST, D), jnp.float32),
    ] + [jax.ShapeDtypeStruct(w.shape, jnp.bfloat16) for w in experts32]
    return pl.pallas_call(
        functools.partial(_prompt_mixer_kernel, tile=tile, blk_tiles=blk_tiles, n_it=n_it, n_main=n_main),
        grid=(n_main + 1,),
        in_specs=[row_spec(D)] + expert_specs + [_const_spec(w.shape) for w in consts],
        out_specs=[row_spec(D),
                   pl.BlockSpec((tile * PK, LANES), lambda s: (prev(s), 0)),
                   pl.BlockSpec((1, REC_ROWS, tile), lambda s: (prev(s), 0, 0)),
                   pl.BlockSpec((1, N_EXPERTS, LANES), lambda s: (prev(s) // blk_tiles, 0, 0)),
                   str_spec(WINDOW, LANES), str_spec(WINDOW, LANES), str_spec(CONV_HIST, D)] + expert_specs,
        out_shape=out_shape,
        scratch_shapes=[
            pltpu.VMEM((U0 + tile, D), jnp.float32),
            pltpu.VMEM((N_KV, tile * GROUP_HEADS, LANES), jnp.bfloat16),
            pltpu.VMEM((tile, D), jnp.bfloat16),
            pltpu.VMEM((2, tile, LANES), jnp.float32),
            pltpu.VMEM((tile // CHUNK, N_KV, KEYS, LANES), jnp.bfloat16),
            pltpu.VMEM((tile // CHUNK, N_KV, KEYS, 2 * LANES), jnp.bfloat16),
            pltpu.VMEM((N_KV, (SLOTS - 2) * CHUNK, LANES), jnp.bfloat16),
            pltpu.VMEM((N_KV, (SLOTS - 2) * CHUNK, 2 * LANES), jnp.bfloat16),
            pltpu.VMEM((N_EXPERTS, 1), jnp.float32),
            pltpu.VMEM((tile, D), jnp.float32),
            pltpu.VMEM((tile, D), jnp.float32),
        ],
        compiler_params=pltpu.CompilerParams(
            dimension_semantics=("arbitrary",), vmem_limit_bytes=VMEM_LIMIT),
        name="prompt_mixer",
    )(x, *experts32, *consts)


def _sample_mixer_kernel(x_ref, ck_ref, cv_ref, sc_ref, qx_ref, g1_ref, win_ref, cw_ref, qg_ref,
                         kg_ref, seg_ref, exp_ref, wco_ref, wao_ref, wo_ref, g2_ref, wr_ref, br_ref,
                         upper_ref, x1_ref, h2p_ref, rt_ref, cnt_ref, kwin_ref, vwin_ref, cst_ref,
                         uext, att_s, qs_s, kop, vop, *, n_stream, seq):
    ext = U0 + seq
    n_keys = WINDOW + seq
    x = x_ref[...]
    h = _bf(_rms(x, g1_ref[...]))

    u = _dot(h, win_ref[:, C_C:C_C + D]) * _dot(h, win_ref[:, C_X:C_X + D])
    for s in range(n_stream):
        r0 = s * ext + U0
        uext[r0 - CONV_HIST:r0, :] = sc_ref[s]
        uext[r0:r0 + seq, :] = u[s * seq:(s + 1) * seq, :]
        cst_ref[s] = u[(s + 1) * seq - CONV_HIST:(s + 1) * seq, :]
    conv = jnp.concatenate(
        [uext[s * ext + U0 - 2:s * ext + U0 - 2 + seq, :] * cw_ref[0:1, :]
         + uext[s * ext + U0 - 1:s * ext + U0 - 1 + seq, :] * cw_ref[1:2, :] for s in range(n_stream)],
        axis=0) + u * cw_ref[2:3, :]
    yc = _bf(_dot(h, win_ref[:, C_B:C_B + D]) * conv)
    o_c = _dot(yc, wco_ref[...])
    macc = _sigmoid(_dot(h, win_ref[:, C_G:C_G + D])) * o_c

    q = _dot(h, win_ref[:, C_Q:C_Q + D])
    qn = _head_norm_q(q, seg_ref, exp_ref, qg_ref[...]) * (LOG2E / math.sqrt(HEAD_DIM))
    kv = _dot(h, win_ref[:, C_K:C_K + 2 * LANES])
    kn = _head_norm_k(kv[:, 0:LANES], kg_ref[...])
    vv = kv[:, LANES:2 * LANES]

    lo_win = lax.broadcasted_iota(jnp.int32, (WINDOW, LANES), 1) < HEAD_DIM
    lo_new = lax.broadcasted_iota(jnp.int32, (seq, LANES), 1) < HEAD_DIM
    tail = KEYS - n_keys
    sink_rows = _sink_block((tail, LANES))
    for s in range(n_stream):
        for g in range(N_KV):
            kop[s, g, n_keys:KEYS, :] = sink_rows
            vop[s, g, n_keys:KEYS, 0:LANES] = jnp.zeros((tail, LANES), jnp.bfloat16)
            vop[s, g, :, LANES:2 * LANES] = jnp.ones((KEYS, LANES), jnp.bfloat16)
        rows = slice(s * seq, (s + 1) * seq)
        kwin_ref[s, 0:WINDOW - seq, :] = ck_ref[s, seq:WINDOW, :]
        kwin_ref[s, WINDOW - seq:WINDOW, :] = kn[rows, :]
        vwin_ref[s, 0:WINDOW - seq, :] = cv_ref[s, seq:WINDOW, :]
        vwin_ref[s, WINDOW - seq:WINDOW, :] = vv[rows, :]
        for g in range(N_KV):
            kop[s, g, 0:WINDOW, :], vop[s, g, 0:WINDOW, 0:LANES] = _key_rows(ck_ref[s], cv_ref[s], g, lo_win)
            kop[s, g, WINDOW:n_keys, :], vop[s, g, WINDOW:n_keys, 0:LANES] = _key_rows(
                kn[rows, :], vv[rows, :], g, lo_new)
        for hd in range(N_HEADS):
            g, r = divmod(hd, GROUP_HEADS)
            feat = jnp.broadcast_to(qx_ref[hd], (seq, HEAD_DIM))
            qs_s[s, g, r * seq:(r + 1) * seq, :] = _bf(
                jnp.concatenate([qn[rows, hd * HEAD_DIM:(hd + 1) * HEAD_DIM], feat], axis=-1))
        for g in range(N_KV):
            out = _attend_group(qs_s[s, g], kop[s, g], vop[s, g])
            for i, pair in enumerate(_head_pairs(out, seq, lo_new)):
                col = (g * GROUP_HEADS + 2 * i) * HEAD_DIM
                att_s[rows, col:col + LANES] = pair

    o_a = _dot(att_s[...], wao_ref[...])
    macc = macc + _sigmoid(_dot(h, win_ref[:, C_G + D:C_G + 2 * D])) * o_a
    carry = _merge_tail(x, macc, wo_ref, g2_ref, wr_ref, br_ref, upper_ref,
                        jnp.zeros((N_EXPERTS, 1), jnp.float32), x1_ref, h2p_ref, rt_ref.at[0])
    cnt_ref[0] = jnp.broadcast_to(carry, (N_EXPERTS, LANES))


def _sample_mixer(x, ck, cv, sc, q_extra, wts):
    nb, seq, _ = x.shape
    rows = nb * seq
    x2 = x.reshape(rows, D)
    ins = (x2, ck, cv, sc, q_extra) + tuple(wts)
    full = lambda a: pl.BlockSpec(a.shape, lambda i, nd=a.ndim: (0,) * nd)
    out_shape = [
        jax.ShapeDtypeStruct((rows, D), jnp.float32),
        jax.ShapeDtypeStruct((rows * PK, LANES), jnp.uint32),
        jax.ShapeDtypeStruct((1, REC_ROWS, rows), jnp.float32),
        jax.ShapeDtypeStruct((1, N_EXPERTS, LANES), jnp.float32),
        jax.ShapeDtypeStruct((nb, WINDOW, LANES), jnp.float32),
        jax.ShapeDtypeStruct((nb, WINDOW, LANES), jnp.float32),
        jax.ShapeDtypeStruct((nb, CONV_HIST, D), jnp.float32),
    ]
    return pl.pallas_call(
        functools.partial(_sample_mixer_kernel, n_stream=nb, seq=seq),
        grid=(1,),
        in_specs=[full(a) for a in ins],
        out_specs=[full(a) for a in out_shape],
        out_shape=out_shape,
        scratch_shapes=[
            pltpu.VMEM((nb * (U0 + seq), D), jnp.float32),
            pltpu.VMEM((rows, D), jnp.bfloat16),
            pltpu.VMEM((nb, N_KV, GROUP_HEADS * seq, LANES), jnp.bfloat16),
            pltpu.VMEM((nb, N_KV, KEYS, LANES), jnp.bfloat16),
            pltpu.VMEM((nb, N_KV, KEYS, 2 * LANES), jnp.bfloat16),
        ],
        compiler_params=pltpu.CompilerParams(
            dimension_semantics=("arbitrary",), vmem_limit_bytes=VMEM_LIMIT),
        name="sample_mixer",
    )(*ins)


def _sorted_rows(blk_tokens, mt):
    rows = 2 * blk_tokens + N_EXPERTS * (mt // 2 - 1) + mt
    return -(-rows // mt) * mt


def _moe_kernel(pos_ref, tab_ref, stp_ref, h2p_ref, x1_ref, rt_ref, wg_ref, wu_ref, wd_ref, y_ref,
                xo, g1, g2, *, tile, n_tiles, max_tiles, mt):
    b = pl.program_id(0)
    s = pl.program_id(1)
    n_groups = tile // UNROLL
    n_disp = n_tiles

    @pl.when((b == 0) & (s == 0))
    def _():
        xo[...] = jnp.zeros_like(xo)

    @pl.when(s < n_disp)
    def _dispatch():
        base = (b * n_tiles + s) * (TOP_K * tile)

        def group(gi, carry):
            i0 = base + gi * UNROLL
            t0 = gi * (UNROLL * PK)
            for k in range(UNROLL):
                row = h2p_ref[pl.ds(pl.multiple_of(t0 + k * PK, PK), PK), :]
                xo[pl.ds(pl.multiple_of(pos_ref[i0 + k], PK), PK), :] = row
                xo[pl.ds(pl.multiple_of(pos_ref[i0 + tile + k], PK), PK), :] = row
            return carry

        lax.fori_loop(0, n_groups, group, 0)

    @pl.when((s >= n_disp) & (s < n_disp + EXPERT_STEPS))
    def _experts():
        st = b * EXPERT_STEPS + s - n_disp
        first = stp_ref[2 * st]
        last = stp_ref[2 * st + 1]
        half = mt // 2

        def entry(j):
            t = (b * max_tiles + j) * TAB_W
            return pl.multiple_of(tab_ref[t], half * PK), tab_ref[t + 1], tab_ref[t + 2]

        def load(j):
            base, _, _ = entry(j)
            return _bf(_unpack_rows([xo[pl.ds(base + q, mt, stride=PK), :] for q in range(PK)]))

        def swiglu(j, xs):
            _, k, _ = entry(j)
            gt = _dot(xs, wg_ref[k])
            return _bf(gt * _sigmoid(gt) * _dot(xs, wu_ref[k]))

        def up(j):
            return swiglu(j, load(j))

        def store(j, hid):
            base, k, _ = entry(j)
            out = _pack_rows(_dot(hid, wd_ref[k]))
            for q in range(PK):
                xo[pl.ds(base + q, mt, stride=PK), :] = out[:, q * LANES:(q + 1) * LANES]

        def down(j, hid):
            base, k, full = entry(j)
            out = _pack_rows(_dot(hid, wd_ref[k]))

            def put(lo):
                for q in range(PK):
                    xo[pl.ds(base + lo * PK + q, half, stride=PK), :] = (
                        out[lo:lo + half, q * LANES:(q + 1) * LANES])

            put(0)

            @pl.when(full > 0)
            def _():
                put(half)

        n_full = 0
        for k in range(STEP_EXPERTS):
            n_full = n_full + tab_ref[(b * max_tiles + jnp.minimum(first + k, max_tiles - 1)) * TAB_W + 2]
        usual = (last - first == STEP_EXPERTS) & (n_full == STEP_EXPERTS)

        @pl.when(usual)
        def _():
            xs = [load(first + k) for k in range(STEP_EXPERTS)]
            for k in range(STEP_EXPERTS):
                store(first + k, swiglu(first + k, xs[k]))

        @pl.when(jnp.logical_not(usual) & (last > first))
        def _():
            def both(j, hid):
                nxt = up(j)
                down(j - 1, hid)
                return nxt

            down(last - 1, lax.fori_loop(first + 1, last, both, up(first)))

    @pl.when(s >= n_disp + EXPERT_STEPS)
    def _combine():
        i = s - n_disp - EXPERT_STEPS
        base = (b * n_tiles + i) * (TOP_K * tile)
        rec = jnp.concatenate([rt_ref[0], jnp.zeros((LANES - REC_ROWS, tile), jnp.float32)], axis=0)
        cw = rec.T
        part = tile // COMBINE_PARTS
        for p in range(COMBINE_PARTS):
            for k in range(part):
                t = p * part + k
                dst = pl.ds(t * PK, PK)
                g1[dst, :] = xo[pl.ds(pl.multiple_of(pos_ref[base + t], PK), PK), :]
                g2[dst, :] = xo[pl.ds(pl.multiple_of(pos_ref[base + tile + t], PK), PK), :]
            rows = pl.ds(p * part, part)
            o1 = _unpack_rows([g1[pl.ds(p * part * PK + q, part, stride=PK), :] for q in range(PK)])
            o2 = _unpack_rows([g2[pl.ds(p * part * PK + q, part, stride=PK), :] for q in range(PK)])
            w = cw[p * part:(p + 1) * part]
            y_ref[rows, :] = x1_ref[rows, :] + w[:, R_W1:R_W1 + 1] * o1 + w[:, R_W2:R_W2 + 1] * o2


def _moe(pos, tab, stp, rt, h2p, x1, wg, wu, wd, tile, n_tiles, mt):
    n = x1.shape[0]
    n_blocks = n // (tile * n_tiles)
    n_disp = n_tiles
    n_steps = n_disp + EXPERT_STEPS + n_tiles

    def disp(b, s, *_):
        return (b * n_disp + jnp.minimum(s, n_disp - 1), 0)

    def comb(b, s, *_):
        return (b * n_tiles + jnp.clip(s - n_disp - EXPERT_STEPS, 0, n_tiles - 1), 0)

    def wmap(b, s, *_):
        return (jnp.clip(s - n_disp, 0, EXPERT_STEPS - 1), 0, 0)

    grid_spec = pltpu.PrefetchScalarGridSpec(
        num_scalar_prefetch=3,
        grid=(n_blocks, n_steps),
        in_specs=[pl.BlockSpec((tile * PK, LANES), disp),
                  pl.BlockSpec((tile, D), comb),
                  pl.BlockSpec((1, REC_ROWS, tile), lambda b, s, *_: comb(b, s) + (0,)),
                  pl.BlockSpec((STEP_EXPERTS, D, D_EXPERT), wmap),
                  pl.BlockSpec((STEP_EXPERTS, D, D_EXPERT), wmap),
                  pl.BlockSpec((STEP_EXPERTS, D_EXPERT, D), wmap)],
        out_specs=pl.BlockSpec((tile, D), comb),
        scratch_shapes=[
            pltpu.VMEM((_sorted_rows(tile * n_tiles, mt) * PK, LANES), jnp.uint32),
            pltpu.VMEM((tile * PK, LANES), jnp.uint32),
            pltpu.VMEM((tile * PK, LANES), jnp.uint32),
        ])
    return pl.pallas_call(
        functools.partial(_moe_kernel, tile=tile, n_tiles=n_tiles,
                          max_tiles=_max_tiles(tile * n_tiles, mt), mt=mt),
        grid_spec=grid_spec,
        out_shape=jax.ShapeDtypeStruct((n, D), jnp.float32),
        input_output_aliases={4: 0},
        compiler_params=pltpu.CompilerParams(
            dimension_semantics=("arbitrary", "arbitrary"), vmem_limit_bytes=VMEM_LIMIT),
        name="moe",
    )(pos, tab, stp, h2p, x1, rt, wg, wu, wd)


def _max_tiles(blk_tokens, mt):
    return N_EXPERTS + 2 * blk_tokens // mt


def _sorted_positions(rt, cnt, blk_tokens, mt):
    experts = jnp.arange(N_EXPERTS, dtype=jnp.int32)
    n_tiles, _, tile = rt.shape
    counts = cnt[:, :, 0].astype(jnp.int32)
    pad = mt // 2
    padded = (counts + pad - 1) // pad * pad
    off = jnp.cumsum(padded, axis=1) - padded
    e = rt[:, R_E1:R_E1 + TOP_K, :].astype(jnp.int32)
    rank = rt[:, R_RANK1:R_RANK1 + TOP_K, :].astype(jnp.int32)
    off_tile = jnp.repeat(off, blk_tokens // tile, axis=0)
    hit = e[:, :, None, :] == experts[:, None]
    pos = (rank + jnp.sum(jnp.where(hit, off_tile[:, None, :, None], 0), axis=2)) * PK

    tiles = (counts + mt - 1) // mt
    t_end = jnp.cumsum(tiles, axis=1)
    t_beg = t_end - tiles
    j = jnp.arange(_max_tiles(blk_tokens, mt), dtype=jnp.int32)
    e_of = jnp.minimum(jnp.sum(t_end[:, None, :] <= j[None, :, None], axis=-1), N_EXPERTS - 1)
    pick = lambda a: jnp.sum(jnp.where(e_of[:, :, None] == experts, a[:, None, :], 0), axis=-1)
    local = (j[None, :] - pick(t_beg)) * mt
    tab = jnp.stack([(pick(off) + local) * PK, e_of % STEP_EXPERTS,
                     (pick(counts) - local > pad).astype(jnp.int32)], axis=-1)
    stp = jnp.stack([t_beg[:, ::STEP_EXPERTS], t_end[:, STEP_EXPERTS - 1::STEP_EXPERTS]], axis=-1)
    return pos.reshape(-1), tab.reshape(-1), stp.reshape(-1), rt


def _segment_matrices():
    head = np.arange(D) // HEAD_DIM
    seg = head[:, None] == np.arange(LANES)[None, :]
    expm = (np.arange(2 * LANES) % LANES)[:, None] == head[None, :]
    return jnp.asarray(seg, jnp.bfloat16), jnp.asarray(expm, jnp.bfloat16)


def _strict_upper(n):
    r = np.arange(n)
    return jnp.asarray(r[:, None] < r[None, :], jnp.bfloat16)


def kernel(x_prompt, x_sample, cache_k, cache_v, state_conv, norm_mix_g, w_in, conv_w, q_norm_g,
           k_norm_g, attn_sinks, w_conv_out, w_attn_out, w_out, norm_ffn_g, w_group, b_group,
           w_router, b_router, w_gate, w_up, w_down):
    nb, seq, _ = x_prompt.shape
    ns, sseq, _ = x_sample.shape
    l = 0
    seg, expm = _segment_matrices()
    pad = RT_ROWS - N_EXPERTS - N_GROUPS
    wr = jnp.concatenate([w_router[l].T, w_group[l].T, jnp.zeros((pad, D), jnp.float32)], axis=0)
    br = jnp.concatenate([b_router[l], b_group[l], jnp.zeros((pad,), jnp.float32)]).reshape(RT_ROWS, 1)
    wts = (norm_mix_g[l].reshape(1, D), _bf(w_in[l]), conv_w[l],
           jnp.tile(q_norm_g[l], N_HEADS).reshape(1, D), jnp.tile(k_norm_g[l], N_KV).reshape(1, LANES),
           seg, expm, _bf(w_conv_out[l]), _bf(w_attn_out[l]), _bf(w_out[l]),
           norm_ffn_g[l].reshape(1, D), _bf(wr), br)
    sinks = attn_sinks[l]
    n_s = ns * sseq

    sink2 = sinks * LOG2E
    sink_hi = _bf(sink2).astype(jnp.float32)
    q_extra = jnp.concatenate(
        [jnp.ones((N_HEADS, 1), jnp.float32), sink_hi[:, None], (sink2 - sink_hi)[:, None],
         jnp.zeros((N_HEADS, HEAD_DIM - Q_FEATS), jnp.float32)], axis=1).reshape(N_HEADS, 1, HEAD_DIM)
    x1p, h2pp, rtp, cntp, kwp, vwp, cstp, wg, wu, wd = _prompt_mixer(
        x_prompt, (w_gate[l], w_up[l], w_down[l]), q_extra, wts + (_strict_upper(TOK_TILE),), tile=TOK_TILE, blk_tiles=BLOCK_TILES)
    x1s, h2ps, rts, cnts, kws, vws, csts = _sample_mixer(
        x_sample, cache_k[l].reshape(ns, WINDOW, LANES), cache_v[l].reshape(ns, WINDOW, LANES),
        state_conv[l], q_extra, wts + (_strict_upper(n_s),))

    yp = _moe(*_sorted_positions(rtp, cntp, TOK_TILE * BLOCK_TILES, MOE_TILE), h2pp, x1p.reshape(nb * seq, D), wg, wu, wd,
              tile=TOK_TILE, n_tiles=BLOCK_TILES, mt=MOE_TILE)
    ys = _moe(*_sorted_positions(rts, cnts, n_s, SAMPLE_MOE_TILE), h2ps, x1s, wg, wu, wd, tile=n_s, n_tiles=1,
              mt=SAMPLE_MOE_TILE)

    kv_shape = lambda n: (1, n, WINDOW, N_KV, HEAD_DIM)
    return (yp.reshape(nb, seq, D), ys.reshape(ns, sseq, D),
            kwp.reshape(kv_shape(nb)), vwp.reshape(kv_shape(nb)), cstp.reshape(1, nb, CONV_HIST, D),
            kws.reshape(kv_shape(ns)), vws.reshape(kv_shape(ns)), csts.reshape(1, ns, CONV_HIST, D))
```

```python
import functools
import math

import jax
import jax.numpy as jnp
import numpy as np
from jax import lax
from jax.experimental import pallas as pl
from jax.experimental.pallas import tpu as pltpu

D = 1024
CHUNK = 64
HEAD_DIM = 64
N_HEADS = 16
N_KV = 2
WINDOW = 128
N_GROUPS = 4
EPG = 8
N_EXPERTS = 32
TOP_K = 2
D_EXPERT = 256
EPS = 1e-6
NEG = -1e30

LANES = 128
SUBLANES = 8
CONV_HIST = 2
U0 = SUBLANES
SLOTS = 4
KEYS = SLOTS * CHUNK
GROUP_HEADS = N_HEADS // N_KV
STACK = GROUP_HEADS * CHUNK
BIAS_LANE = HEAD_DIM
SINK_LANE = HEAD_DIM + 1
Q_FEATS = 3
LOG2E = 1.0 / math.log(2.0)
RT_ROWS = LANES
REC_ROWS = 8
HALF = D // 2
PK = HALF // LANES

R_E1, R_E2, R_W1, R_W2, R_RANK1, R_RANK2 = 0, 1, 2, 3, 4, 5

C_B, C_C, C_X, C_Q, C_K, C_V, C_G = 0, 1024, 2048, 3072, 4096, 4224, 4352

MOE_TILE = 288
SAMPLE_MOE_TILE = 64
TOK_TILE = 512
BLOCK_TILES = 8
UNROLL = 32
COMBINE_PARTS = 4
STEP_EXPERTS = 4
TAB_W = 3
EXPERT_STEPS = N_EXPERTS // STEP_EXPERTS

VMEM_LIMIT = 60 * 1024 * 1024

_NT = (((1,), (1,)), ((), ()))


def _bf(x):
    return x.astype(jnp.bfloat16)


def _dot(a, b):
    return jnp.dot(a, b, preferred_element_type=jnp.float32)


def _rms(x, g):
    return x * lax.rsqrt(jnp.mean(x * x, axis=-1, keepdims=True) + EPS) * g


def _sigmoid(x):
    return 1.0 / (1.0 + jnp.exp(-x))


def _pack_rows(x):
    return pltpu.pack_elementwise([x[:, :HALF], x[:, HALF:]], packed_dtype=jnp.bfloat16)


def _unpack_rows(parts):
    def half(i):
        return [pltpu.unpack_elementwise(p, index=i, packed_dtype=jnp.bfloat16,
                                         unpacked_dtype=jnp.float32) for p in parts]
    return jnp.concatenate(half(0) + half(1), axis=-1)


def _head_norm_q(q, seg_ref, exp_ref, qg):
    ssq = _dot(_bf(q * q), seg_ref[...])
    inv = lax.rsqrt(ssq * (1.0 / HEAD_DIM) + EPS)
    hi = _bf(inv)
    lo = _bf(inv - hi.astype(jnp.float32))
    full = _dot(jnp.concatenate([hi, lo], axis=-1), exp_ref[...])
    return q * full * qg


def _head_norm_k(k, kg):
    lane = lax.broadcasted_iota(jnp.int32, k.shape, 1)
    lo_half = lane < HEAD_DIM
    sq = k * k
    s0 = jnp.sum(jnp.where(lo_half, sq, 0.0), axis=-1, keepdims=True)
    s1 = jnp.sum(jnp.where(lo_half, 0.0, sq), axis=-1, keepdims=True)
    i0 = lax.rsqrt(s0 * (1.0 / HEAD_DIM) + EPS)
    i1 = lax.rsqrt(s1 * (1.0 / HEAD_DIM) + EPS)
    return k * jnp.where(lo_half, i0, i1) * kg


def _attend_group(qs, keys, vals):
    s = lax.dot_general(qs, keys, _NT, preferred_element_type=jnp.float32)
    m = jnp.max(s, axis=-1, keepdims=True)
    o = _dot(_bf(jnp.exp2(s - m)), vals)
    return o[:, 0:LANES] / o[:, LANES:2 * LANES]


def _key_rows(k, v, g, lo_half):
    ksw = pltpu.roll(k, HEAD_DIM, axis=1)
    vsw = pltpu.roll(v, HEAD_DIM, axis=1)
    kb = jnp.where(lo_half, k if g == 0 else ksw, 0.0)
    vb = jnp.where(lo_half, v, vsw) if g == 0 else jnp.where(lo_half, vsw, v)
    return _bf(kb), _bf(vb)


def _sink_block(shape):
    row = lax.broadcasted_iota(jnp.int32, shape, len(shape) - 2)
    ln = lax.broadcasted_iota(jnp.int32, shape, len(shape) - 1)
    is_sink = (row == 0) & ((ln == SINK_LANE) | (ln == SINK_LANE + 1))
    return _bf(jnp.where(is_sink, 1.0, jnp.where((row > 0) & (ln == BIAS_LANE), NEG, 0.0)))


def _head_pairs(out, rows_per_head, lo_half):
    blk = lambda r: out[r * rows_per_head:(r + 1) * rows_per_head, :]
    return [_bf(jnp.where(lo_half, blk(r), blk(r + 1))) for r in range(0, GROUP_HEADS, 2)]


def _route(lt, upper_ref, carry):
    t = lt.shape[1]
    big = float(RT_ROWS)
    grow = lax.broadcasted_iota(jnp.int32, (REC_ROWS, t), 0).astype(jnp.float32)
    is_g = grow < N_GROUPS
    gl = jnp.where(is_g, lt[N_EXPERTS:N_EXPERTS + REC_ROWS], NEG)
    gmax = jnp.max(gl, axis=0, keepdims=True)
    gsum = jnp.sum(jnp.where(is_g, jnp.exp(gl - gmax), 0.0), axis=0, keepdims=True)
    g_w = 1.0 / gsum
    g_sel = jnp.min(jnp.where(is_g & (gl == gmax), grow, big), axis=0, keepdims=True)
    logits = lt[0:N_EXPERTS]
    erow = lax.broadcasted_iota(jnp.int32, (N_EXPERTS, t), 0).astype(jnp.float32)
    lo = g_sel * EPG
    in_grp = (erow >= lo) & (erow < lo + EPG)
    el = jnp.where(in_grp, logits, NEG)
    v1 = jnp.max(el, axis=0, keepdims=True)
    i1 = jnp.min(jnp.where(in_grp & (el == v1), erow, big), axis=0, keepdims=True)
    rest = in_grp & (erow != i1)
    el2 = jnp.where(rest, logits, NEG)
    v2 = jnp.max(el2, axis=0, keepdims=True)
    i2 = jnp.min(jnp.where(rest & (el2 == v2), erow, big), axis=0, keepdims=True)
    e2 = jnp.exp(v2 - v1)
    den = 1.0 + e2
    w1 = (1.0 / den) * g_w
    w2 = (e2 / den) * g_w
    oh1 = jnp.where(erow == i1, 1.0, 0.0)
    oh2 = jnp.where(erow == i2, 1.0, 0.0)
    both = oh1 + oh2
    before = _dot(_bf(both), upper_ref[...]) + carry
    r1 = jnp.sum(before * oh1, axis=0, keepdims=True)
    r2 = jnp.sum(before * oh2, axis=0, keepdims=True)
    rec = jnp.zeros((REC_ROWS, t), jnp.float32)
    for rw, val in ((R_E1, i1), (R_E2, i2), (R_W1, w1), (R_W2, w2), (R_RANK1, r1), (R_RANK2, r2)):
        rec = jnp.where(grow == rw, val, rec)
    return rec, carry + jnp.sum(both, axis=1, keepdims=True)


def _route_stage(x1, g2_ref, wr_ref, br_ref, upper_ref, carry, h2p_ref, rt_ref):
    rows = x1.shape[0]
    h2 = _rms(x1, g2_ref[...])
    pk = _pack_rows(h2)
    for j in range(PK):
        h2p_ref[pl.ds(j, rows, stride=PK), :] = pk[:, j * LANES:(j + 1) * LANES]
    lt = lax.dot_general(wr_ref[...], _bf(h2), _NT, preferred_element_type=jnp.float32) + br_ref[...]
    rec, carry = _route(lt, upper_ref, carry)
    rt_ref[...] = rec
    return carry


def _merge_tail(x, macc, wo_ref, g2_ref, wr_ref, br_ref, upper_ref, carry, x1_ref, h2p_ref, rt_ref):
    x1 = x + _dot(_bf(macc), wo_ref[...])
    x1_ref[...] = x1
    return _route_stage(x1, g2_ref, wr_ref, br_ref, upper_ref, carry, h2p_ref, rt_ref)


def _prompt_mixer_kernel(x_ref, wg32_ref, wu32_ref, wd32_ref,
                         qx_ref, g1_ref, win_ref, cw_ref, qg_ref, kg_ref, seg_ref, exp_ref,
                         wco_ref, wao_ref, wo_ref, g2_ref, wr_ref, br_ref, upper_ref,
                         x1_ref, h2p_ref, rt_ref, cnt_ref, kwin_ref, vwin_ref, cst_ref,
                         wgu16_ref, wd16_ref,
                         uext, qs_s, att_s, kn_s, kop, vop, khist, vhist, carry_s, x1_new, x1_old,
                         *, tile, blk_tiles, n_it, n_main):
    step = pl.program_id(0)
    it = step % n_it
    n_chunk = tile // CHUNK

    @pl.when(step == 0)
    def _():
        shape = (n_chunk, N_KV, CHUNK, LANES)
        tail = pl.ds((SLOTS - 1) * CHUNK, CHUNK)
        kop[:, :, tail, :] = _sink_block(shape)
        vop[:, :, tail, 0:LANES] = jnp.zeros(shape, jnp.bfloat16)
        vop[:, :, tail, LANES:2 * LANES] = jnp.ones(shape, jnp.bfloat16)
        x1_new[...] = jnp.zeros((tile, D), jnp.float32)
        carry_s[...] = jnp.zeros_like(carry_s)

    def route_previous(x1_prev):
        first_of_block = (step + blk_tiles - 1) % blk_tiles == 0
        carry = jnp.where(first_of_block, 0.0, carry_s[...])
        carry = _route_stage(x1_prev[...], g2_ref, wr_ref, br_ref, upper_ref, carry,
                             h2p_ref, rt_ref.at[0])
        carry_s[...] = carry
        cnt_ref[0] = jnp.broadcast_to(carry, (N_EXPERTS, LANES))

    @pl.when(step == n_main)
    def _():
        route_previous(x1_new)

    @pl.when((step < n_main) & (it == 0))
    def _():
        uext[0:U0, :] = jnp.zeros((U0, D), jnp.float32)
        ln = lax.broadcasted_iota(jnp.int32, khist.shape, 2)
        khist[...] = _bf(jnp.where(ln == BIAS_LANE, NEG, 0.0))
        vhist[...] = jnp.zeros_like(vhist)

    @pl.when(step < n_main)
    def _():
        x1_old[...] = x1_new[...]
        _mixer_tile(x_ref, wg32_ref, wu32_ref, wd32_ref, qx_ref, g1_ref, win_ref, cw_ref, qg_ref,
                    kg_ref, seg_ref, exp_ref, wco_ref, wao_ref, wo_ref, x1_ref, kwin_ref, vwin_ref,
                    cst_ref, wgu16_ref, wd16_ref, uext, qs_s, att_s, kn_s, kop, vop, khist,
                    vhist, x1_new, tile=tile, side_job=lambda: route_previous(x1_old))


def _mixer_tile(x_ref, wg32_ref, wu32_ref, wd32_ref, qx_ref, g1_ref, win_ref, cw_ref, qg_ref,
                kg_ref, seg_ref, exp_ref, wco_ref, wao_ref, wo_ref, x1_ref, kwin_ref, vwin_ref,
                cst_ref, wgu16_ref, wd16_ref, uext, qs_s, att_s, kn_s, kop, vop, khist,
                vhist, x1_keep, *, tile, side_job):
    n_chunk = tile // CHUNK
    wgu16_ref[:, :, 0:D_EXPERT] = _bf(wg32_ref[...])
    wgu16_ref[:, :, D_EXPERT:2 * D_EXPERT] = _bf(wu32_ref[...])
    wd16_ref[...] = _bf(wd32_ref[...])

    x = x_ref[0]
    h = _bf(_rms(x, g1_ref[...]))

    q = _dot(h, win_ref[:, C_Q:C_Q + D])
    qn = _head_norm_q(q, seg_ref, exp_ref, qg_ref[...]) * (LOG2E / math.sqrt(HEAD_DIM))
    for hd in range(N_HEADS):
        g, r = divmod(hd, GROUP_HEADS)
        ext = jnp.broadcast_to(qx_ref[hd], (tile, HEAD_DIM))
        piece = _bf(jnp.concatenate([qn[:, hd * HEAD_DIM:(hd + 1) * HEAD_DIM], ext], axis=-1))
        for c in range(n_chunk):
            qs_s[g, c * STACK + r * CHUNK:c * STACK + (r + 1) * CHUNK, :] = (
                piece[c * CHUNK:(c + 1) * CHUNK, :])
    kv = _dot(h, win_ref[:, C_K:C_K + 2 * LANES])
    kn = _head_norm_k(kv[:, 0:LANES], kg_ref[...])
    vv = kv[:, LANES:2 * LANES]
    kn_s[0] = kn
    kn_s[1] = vv

    kwin_ref[0] = kn[tile - WINDOW:, :]
    vwin_ref[0] = vv[tile - WINDOW:, :]

    u = _dot(h, win_ref[:, C_C:C_C + D]) * _dot(h, win_ref[:, C_X:C_X + D])
    uext[U0:U0 + tile, :] = u
    conv = (uext[U0 - 2:U0 - 2 + tile, :] * cw_ref[0:1, :] + uext[U0 - 1:U0 - 1 + tile, :] * cw_ref[1:2, :]
            + u * cw_ref[2:3, :])
    yc = _bf(_dot(h, win_ref[:, C_B:C_B + D]) * conv)
    o_c = _dot(yc, wco_ref[...])
    macc = _sigmoid(_dot(h, win_ref[:, C_G:C_G + D])) * o_c
    last2 = uext[U0 + tile - CONV_HIST:U0 + tile, :]
    uext[U0 - CONV_HIST:U0, :] = last2

    cst_ref[0] = last2
    side_job()

    lane = lax.broadcasted_iota(jnp.int32, (CHUNK, LANES), 1)
    lo_half = lane < HEAD_DIM
    ones_blk = jnp.ones((CHUNK, LANES), jnp.bfloat16)
    own = SLOTS - 2

    for c in range(own):
        for sl in range(own - c):
            src = pl.ds((c + sl) * CHUNK, CHUNK)
            kop[c, :, pl.ds(sl * CHUNK, CHUNK), :] = khist[:, src, :]
            vop[c, :, pl.ds(sl * CHUNK, CHUNK), :] = vhist[:, src, :]

    def chunk_body(c):
        r0 = c * CHUNK
        kc = kn_s[0, pl.ds(r0, CHUNK), :]
        vc = kn_s[1, pl.ds(r0, CHUNK), :]
        for g in range(N_KV):
            kb, vb = _key_rows(kc, vc, g, lo_half)
            for d in range(own + 1):
                if c + d < n_chunk:
                    dst = pl.ds((own - d) * CHUNK, CHUNK)
                    kop[c + d, g, dst, :] = kb
                    vop[c + d, g, dst, 0:LANES] = vb
                    vop[c + d, g, dst, LANES:2 * LANES] = ones_blk
            if c >= n_chunk - own:
                dst = pl.ds((c - (n_chunk - own)) * CHUNK, CHUNK)
                khist[g, dst, :] = kb
                vhist[g, dst, 0:LANES] = vb
                vhist[g, dst, LANES:2 * LANES] = ones_blk
        for g in range(N_KV):
            out = _attend_group(qs_s[g, pl.ds(c * STACK, STACK), :], kop[c, g], vop[c, g])
            for i, pair in enumerate(_head_pairs(out, CHUNK, lo_half)):
                col = (g * GROUP_HEADS + 2 * i) * HEAD_DIM
                att_s[pl.ds(r0, CHUNK), col:col + LANES] = pair

    for c in range(n_chunk):
        chunk_body(c)

    o_a = _dot(att_s[...], wao_ref[...])
    macc = macc + _sigmoid(_dot(h, win_ref[:, C_G + D:C_G + 2 * D])) * o_a
    x1 = x + _dot(_bf(macc), wo_ref[...])
    x1_ref[0] = x1
    x1_keep[...] = x1


def _const_spec(shape):
    nd = len(shape)
    return pl.BlockSpec(shape, lambda *_: (0,) * nd, pipeline_mode=pl.Buffered(1))


def _prompt_mixer(x, experts32, q_extra, wts, tile, blk_tiles):
    nb, seq, _ = x.shape
    n_it = seq // tile
    n_main = nb * n_it
    bps = n_it // blk_tiles
    parts = n_main // N_EXPERTS
    consts = (q_extra,) + tuple(wts)

    cur = lambda s: jnp.minimum(s, n_main - 1)
    prev = lambda s: jnp.maximum(s - 1, 0)
    row_spec = lambda w: pl.BlockSpec((1, tile, w), lambda s: (cur(s) // n_it, cur(s) % n_it, 0))
    str_spec = lambda r, w: pl.BlockSpec((1, r, w), lambda s: (cur(s) // n_it, 0, 0))
    part_spec = lambda r, w: pl.BlockSpec(
        (1, r // parts, w), lambda s: (cur(s) // parts, cur(s) % parts, 0))
    expert_specs = [part_spec(D, D_EXPERT), part_spec(D, D_EXPERT), part_spec(D_EXPERT, D)]
    out_shape = [
        jax.ShapeDtypeStruct((nb, seq, D), jnp.float32),
        jax.ShapeDtypeStruct((nb * seq * PK, LANES), jnp.uint32),
        jax.ShapeDtypeStruct((n_main, REC_ROWS, tile), jnp.float32),
        jax.ShapeDtypeStruct((nb * bps, N_EXPERTS, LANES), jnp.float32),
        jax.ShapeDtypeStruct((nb, WINDOW, LANES), jnp.float32),
        jax.ShapeDtypeStruct((nb, WINDOW, LANES), jnp.float32),
        jax.ShapeDtypeStruct((nb, CONV_HIST, D), jnp.float32),
        jax.ShapeDtypeStruct((N_EXPERTS, D, 2 * D_EXPERT), jnp.bfloat16),
        jax.ShapeDtypeStruct((N_EXPERTS, D_EXPERT, D), jnp.bfloat16),
    ]
    cast_specs = [part_spec(D, 2 * D_EXPERT), part_spec(D_EXPERT, D)]
    return pl.pallas_call(
        functools.partial(_prompt_mixer_kernel, tile=tile, blk_tiles=blk_tiles, n_it=n_it, n_main=n_main),
        grid=(n_main + 1,),
        in_specs=[row_spec(D)] + expert_specs + [_const_spec(w.shape) for w in consts],
        out_specs=[row_spec(D),
                   pl.BlockSpec((tile * PK, LANES), lambda s: (prev(s), 0)),
                   pl.BlockSpec((1, REC_ROWS, tile), lambda s: (prev(s), 0, 0)),
                   pl.BlockSpec((1, N_EXPERTS, LANES), lambda s: (prev(s) // blk_tiles, 0, 0)),
                   str_spec(WINDOW, LANES), str_spec(WINDOW, LANES), str_spec(CONV_HIST, D)] + cast_specs,
        out_shape=out_shape,
        scratch_shapes=[
            pltpu.VMEM((U0 + tile, D), jnp.float32),
            pltpu.VMEM((N_KV, tile * GROUP_HEADS, LANES), jnp.bfloat16),
            pltpu.VMEM((tile, D), jnp.bfloat16),
            pltpu.VMEM((2, tile, LANES), jnp.float32),
            pltpu.VMEM((tile // CHUNK, N_KV, KEYS, LANES), jnp.bfloat16),
            pltpu.VMEM((tile // CHUNK, N_KV, KEYS, 2 * LANES), jnp.bfloat16),
            pltpu.VMEM((N_KV, (SLOTS - 2) * CHUNK, LANES), jnp.bfloat16),
            pltpu.VMEM((N_KV, (SLOTS - 2) * CHUNK, 2 * LANES), jnp.bfloat16),
            pltpu.VMEM((N_EXPERTS, 1), jnp.float32),
            pltpu.VMEM((tile, D), jnp.float32),
            pltpu.VMEM((tile, D), jnp.float32),
        ],
        compiler_params=pltpu.CompilerParams(
            dimension_semantics=("arbitrary",), vmem_limit_bytes=VMEM_LIMIT),
        name="prompt_mixer",
    )(x, *experts32, *consts)


def _sample_mixer_kernel(x_ref, ck_ref, cv_ref, sc_ref, qx_ref, g1_ref, win_ref, cw_ref, qg_ref,
                         kg_ref, seg_ref, exp_ref, wco_ref, wao_ref, wo_ref, g2_ref, wr_ref, br_ref,
                         upper_ref, x1_ref, h2p_ref, rt_ref, cnt_ref, kwin_ref, vwin_ref, cst_ref,
                         uext, att_s, qs_s, kop, vop, *, n_stream, seq):
    ext = U0 + seq
    n_keys = WINDOW + seq
    x = x_ref[...]
    h = _bf(_rms(x, g1_ref[...]))

    u = _dot(h, win_ref[:, C_C:C_C + D]) * _dot(h, win_ref[:, C_X:C_X + D])
    for s in range(n_stream):
        r0 = s * ext + U0
        uext[r0 - CONV_HIST:r0, :] = sc_ref[s]
        uext[r0:r0 + seq, :] = u[s * seq:(s + 1) * seq, :]
        cst_ref[s] = u[(s + 1) * seq - CONV_HIST:(s + 1) * seq, :]
    conv = jnp.concatenate(
        [uext[s * ext + U0 - 2:s * ext + U0 - 2 + seq, :] * cw_ref[0:1, :]
         + uext[s * ext + U0 - 1:s * ext + U0 - 1 + seq, :] * cw_ref[1:2, :] for s in range(n_stream)],
        axis=0) + u * cw_ref[2:3, :]
    yc = _bf(_dot(h, win_ref[:, C_B:C_B + D]) * conv)
    o_c = _dot(yc, wco_ref[...])
    macc = _sigmoid(_dot(h, win_ref[:, C_G:C_G + D])) * o_c

    q = _dot(h, win_ref[:, C_Q:C_Q + D])
    qn = _head_norm_q(q, seg_ref, exp_ref, qg_ref[...]) * (LOG2E / math.sqrt(HEAD_DIM))
    kv = _dot(h, win_ref[:, C_K:C_K + 2 * LANES])
    kn = _head_norm_k(kv[:, 0:LANES], kg_ref[...])
    vv = kv[:, LANES:2 * LANES]

    lo_win = lax.broadcasted_iota(jnp.int32, (WINDOW, LANES), 1) < HEAD_DIM
    lo_new = lax.broadcasted_iota(jnp.int32, (seq, LANES), 1) < HEAD_DIM
    tail = KEYS - n_keys
    sink_rows = _sink_block((tail, LANES))
    for s in range(n_stream):
        for g in range(N_KV):
            kop[s, g, n_keys:KEYS, :] = sink_rows
            vop[s, g, n_keys:KEYS, 0:LANES] = jnp.zeros((tail, LANES), jnp.bfloat16)
            vop[s, g, :, LANES:2 * LANES] = jnp.ones((KEYS, LANES), jnp.bfloat16)
        rows = slice(s * seq, (s + 1) * seq)
        kwin_ref[s, 0:WINDOW - seq, :] = ck_ref[s, seq:WINDOW, :]
        kwin_ref[s, WINDOW - seq:WINDOW, :] = kn[rows, :]
        vwin_ref[s, 0:WINDOW - seq, :] = cv_ref[s, seq:WINDOW, :]
        vwin_ref[s, WINDOW - seq:WINDOW, :] = vv[rows, :]
        for g in range(N_KV):
            kop[s, g, 0:WINDOW, :], vop[s, g, 0:WINDOW, 0:LANES] = _key_rows(ck_ref[s], cv_ref[s], g, lo_win)
            kop[s, g, WINDOW:n_keys, :], vop[s, g, WINDOW:n_keys, 0:LANES] = _key_rows(
                kn[rows, :], vv[rows, :], g, lo_new)
        for hd in range(N_HEADS):
            g, r = divmod(hd, GROUP_HEADS)
            feat = jnp.broadcast_to(qx_ref[hd], (seq, HEAD_DIM))
            qs_s[s, g, r * seq:(r + 1) * seq, :] = _bf(
                jnp.concatenate([qn[rows, hd * HEAD_DIM:(hd + 1) * HEAD_DIM], feat], axis=-1))
        for g in range(N_KV):
            out = _attend_group(qs_s[s, g], kop[s, g], vop[s, g])
            for i, pair in enumerate(_head_pairs(out, seq, lo_new)):
                col = (g * GROUP_HEADS + 2 * i) * HEAD_DIM
                att_s[rows, col:col + LANES] = pair

    o_a = _dot(att_s[...], wao_ref[...])
    macc = macc + _sigmoid(_dot(h, win_ref[:, C_G + D:C_G + 2 * D])) * o_a
    carry = _merge_tail(x, macc, wo_ref, g2_ref, wr_ref, br_ref, upper_ref,
                        jnp.zeros((N_EXPERTS, 1), jnp.float32), x1_ref, h2p_ref, rt_ref.at[0])
    cnt_ref[0] = jnp.broadcast_to(carry, (N_EXPERTS, LANES))


def _sample_mixer(x, ck, cv, sc, q_extra, wts):
    nb, seq, _ = x.shape
    rows = nb * seq
    x2 = x.reshape(rows, D)
    ins = (x2, ck, cv, sc, q_extra) + tuple(wts)
    full = lambda a: pl.BlockSpec(a.shape, lambda i, nd=a.ndim: (0,) * nd)
    out_shape = [
        jax.ShapeDtypeStruct((rows, D), jnp.float32),
        jax.ShapeDtypeStruct((rows * PK, LANES), jnp.uint32),
        jax.ShapeDtypeStruct((1, REC_ROWS, rows), jnp.float32),
        jax.ShapeDtypeStruct((1, N_EXPERTS, LANES), jnp.float32),
        jax.ShapeDtypeStruct((nb, WINDOW, LANES), jnp.float32),
        jax.ShapeDtypeStruct((nb, WINDOW, LANES), jnp.float32),
        jax.ShapeDtypeStruct((nb, CONV_HIST, D), jnp.float32),
    ]
    return pl.pallas_call(
        functools.partial(_sample_mixer_kernel, n_stream=nb, seq=seq),
        grid=(1,),
        in_specs=[full(a) for a in ins],
        out_specs=[full(a) for a in out_shape],
        out_shape=out_shape,
        scratch_shapes=[
            pltpu.VMEM((nb * (U0 + seq), D), jnp.float32),
            pltpu.VMEM((rows, D), jnp.bfloat16),
            pltpu.VMEM((nb, N_KV, GROUP_HEADS * seq, LANES), jnp.bfloat16),
            pltpu.VMEM((nb, N_KV, KEYS, LANES), jnp.bfloat16),
            pltpu.VMEM((nb, N_KV, KEYS, 2 * LANES), jnp.bfloat16),
        ],
        compiler_params=pltpu.CompilerParams(
            dimension_semantics=("arbitrary",), vmem_limit_bytes=VMEM_LIMIT),
        name="sample_mixer",
    )(*ins)


def _sorted_rows(blk_tokens, mt):
    rows = 2 * blk_tokens + N_EXPERTS * (mt // 2 - 1) + mt
    return -(-rows // mt) * mt


def _moe_kernel(pos_ref, tab_ref, stp_ref, h2p_ref, x1_ref, rt_ref, wgu_ref, wd_ref, y_ref,
                xo, g1, g2, *, tile, n_tiles, max_tiles, mt):
    b = pl.program_id(0)
    s = pl.program_id(1)
    n_groups = tile // UNROLL
    n_disp = n_tiles

    @pl.when((b == 0) & (s == 0))
    def _():
        xo[...] = jnp.zeros_like(xo)

    @pl.when(s < n_disp)
    def _dispatch():
        base = (b * n_tiles + s) * (TOP_K * tile)

        def group(gi, carry):
            i0 = base + gi * UNROLL
            t0 = gi * (UNROLL * PK)
            for k in range(UNROLL):
                row = h2p_ref[pl.ds(pl.multiple_of(t0 + k * PK, PK), PK), :]
                xo[pl.ds(pl.multiple_of(pos_ref[i0 + k], PK), PK), :] = row
                xo[pl.ds(pl.multiple_of(pos_ref[i0 + tile + k], PK), PK), :] = row
            return carry

        lax.fori_loop(0, n_groups, group, 0)

    @pl.when((s >= n_disp) & (s < n_disp + EXPERT_STEPS))
    def _experts():
        st = b * EXPERT_STEPS + s - n_disp
        first = stp_ref[2 * st]
        last = stp_ref[2 * st + 1]
        half = mt // 2

        def entry(j):
            t = (b * max_tiles + j) * TAB_W
            return pl.multiple_of(tab_ref[t], half * PK), tab_ref[t + 1], tab_ref[t + 2]

        def load(j):
            base, _, _ = entry(j)
            return _bf(_unpack_rows([xo[pl.ds(base + q, mt, stride=PK), :] for q in range(PK)]))

        def swiglu(j, xs):
            _, k, _ = entry(j)
            gu = _dot(xs, wgu_ref[k])
            gt = gu[:, 0:D_EXPERT]
            return _bf(gt * _sigmoid(gt) * gu[:, D_EXPERT:2 * D_EXPERT])

        def up(j):
            return swiglu(j, load(j))

        def store(j, hid):
            base, k, _ = entry(j)
            out = _pack_rows(_dot(hid, wd_ref[k]))
            for q in range(PK):
                xo[pl.ds(base + q, mt, stride=PK), :] = out[:, q * LANES:(q + 1) * LANES]

        def down(j, hid):
            base, k, full = entry(j)
            out = _pack_rows(_dot(hid, wd_ref[k]))

            def put(lo):
                for q in range(PK):
                    xo[pl.ds(base + lo * PK + q, half, stride=PK), :] = (
                        out[lo:lo + half, q * LANES:(q + 1) * LANES])

            put(0)

            @pl.when(full > 0)
            def _():
                put(half)

        n_full = 0
        for k in range(STEP_EXPERTS):
            n_full = n_full + tab_ref[(b * max_tiles + jnp.minimum(first + k, max_tiles - 1)) * TAB_W + 2]
        usual = (last - first == STEP_EXPERTS) & (n_full == STEP_EXPERTS)

        @pl.when(usual)
        def _():
            xs = [load(first + k) for k in range(STEP_EXPERTS)]
            for k in range(STEP_EXPERTS):
                store(first + k, swiglu(first + k, xs[k]))

        @pl.when(jnp.logical_not(usual) & (last > first))
        def _():
            def both(j, hid):
                nxt = up(j)
                down(j - 1, hid)
                return nxt

            down(last - 1, lax.fori_loop(first + 1, last, both, up(first)))

    @pl.when(s >= n_disp + EXPERT_STEPS)
    def _combine():
        i = s - n_disp - EXPERT_STEPS
        base = (b * n_tiles + i) * (TOP_K * tile)
        rec = jnp.concatenate([rt_ref[0], jnp.zeros((LANES - REC_ROWS, tile), jnp.float32)], axis=0)
        cw = rec.T
        part = tile // COMBINE_PARTS
        for p in range(COMBINE_PARTS):
            for k in range(part):
                t = p * part + k
                dst = pl.ds(t * PK, PK)
                g1[dst, :] = xo[pl.ds(pl.multiple_of(pos_ref[base + t], PK), PK), :]
                g2[dst, :] = xo[pl.ds(pl.multiple_of(pos_ref[base + tile + t], PK), PK), :]
            rows = pl.ds(p * part, part)
            o1 = _unpack_rows([g1[pl.ds(p * part * PK + q, part, stride=PK), :] for q in range(PK)])
            o2 = _unpack_rows([g2[pl.ds(p * part * PK + q, part, stride=PK), :] for q in range(PK)])
            w = cw[p * part:(p + 1) * part]
            y_ref[rows, :] = x1_ref[rows, :] + w[:, R_W1:R_W1 + 1] * o1 + w[:, R_W2:R_W2 + 1] * o2


def _moe(pos, tab, stp, rt, h2p, x1, wgu, wd, tile, n_tiles, mt):
    n = x1.shape[0]
    n_blocks = n // (tile * n_tiles)
    n_disp = n_tiles
    n_steps = n_disp + EXPERT_STEPS + n_tiles

    def disp(b, s, *_):
        return (b * n_disp + jnp.minimum(s, n_disp - 1), 0)

    def comb(b, s, *_):
        return (b * n_tiles + jnp.clip(s - n_disp - EXPERT_STEPS, 0, n_tiles - 1), 0)

    def wmap(b, s, *_):
        return (jnp.clip(s - n_disp, 0, EXPERT_STEPS - 1), 0, 0)

    grid_spec = pltpu.PrefetchScalarGridSpec(
        num_scalar_prefetch=3,
        grid=(n_blocks, n_steps),
        in_specs=[pl.BlockSpec((tile * PK, LANES), disp),
                  pl.BlockSpec((tile, D), comb),
                  pl.BlockSpec((1, REC_ROWS, tile), lambda b, s, *_: comb(b, s) + (0,)),
                  pl.BlockSpec((STEP_EXPERTS, D, 2 * D_EXPERT), wmap),
                  pl.BlockSpec((STEP_EXPERTS, D_EXPERT, D), wmap)],
        out_specs=pl.BlockSpec((tile, D), comb),
        scratch_shapes=[
            pltpu.VMEM((_sorted_rows(tile * n_tiles, mt) * PK, LANES), jnp.uint32),
            pltpu.VMEM((tile * PK, LANES), jnp.uint32),
            pltpu.VMEM((tile * PK, LANES), jnp.uint32),
        ])
    return pl.pallas_call(
        functools.partial(_moe_kernel, tile=tile, n_tiles=n_tiles,
                          max_tiles=_max_tiles(tile * n_tiles, mt), mt=mt),
        grid_spec=grid_spec,
        out_shape=jax.ShapeDtypeStruct((n, D), jnp.float32),
        compiler_params=pltpu.CompilerParams(
            dimension_semantics=("arbitrary", "arbitrary"), vmem_limit_bytes=VMEM_LIMIT),
        name="moe",
    )(pos, tab, stp, h2p, x1, rt, wgu, wd)


def _max_tiles(blk_tokens, mt):
    return N_EXPERTS + 2 * blk_tokens // mt


def _sorted_positions(rt, cnt, blk_tokens, mt):
    experts = jnp.arange(N_EXPERTS, dtype=jnp.int32)
    n_tiles, _, tile = rt.shape
    counts = cnt[:, :, 0].astype(jnp.int32)
    pad = mt // 2
    padded = (counts + pad - 1) // pad * pad
    off = jnp.cumsum(padded, axis=1) - padded
    e = rt[:, R_E1:R_E1 + TOP_K, :].astype(jnp.int32)
    rank = rt[:, R_RANK1:R_RANK1 + TOP_K, :].astype(jnp.int32)
    off_tile = jnp.repeat(off, blk_tokens // tile, axis=0)
    hit = e[:, :, None, :] == experts[:, None]
    pos = (rank + jnp.sum(jnp.where(hit, off_tile[:, None, :, None], 0), axis=2)) * PK

    tiles = (counts + mt - 1) // mt
    t_end = jnp.cumsum(tiles, axis=1)
    t_beg = t_end - tiles
    j = jnp.arange(_max_tiles(blk_tokens, mt), dtype=jnp.int32)
    e_of = jnp.minimum(jnp.sum(t_end[:, None, :] <= j[None, :, None], axis=-1), N_EXPERTS - 1)
    pick = lambda a: jnp.sum(jnp.where(e_of[:, :, None] == experts, a[:, None, :], 0), axis=-1)
    local = (j[None, :] - pick(t_beg)) * mt
    tab = jnp.stack([(pick(off) + local) * PK, e_of % STEP_EXPERTS,
                     (pick(counts) - local > pad).astype(jnp.int32)], axis=-1)
    stp = jnp.stack([t_beg[:, ::STEP_EXPERTS], t_end[:, STEP_EXPERTS - 1::STEP_EXPERTS]], axis=-1)
    return pos.reshape(-1), tab.reshape(-1), stp.reshape(-1), rt


def _segment_matrices():
    head = np.arange(D) // HEAD_DIM
    seg = head[:, None] == np.arange(LANES)[None, :]
    expm = (np.arange(2 * LANES) % LANES)[:, None] == head[None, :]
    return jnp.asarray(seg, jnp.bfloat16), jnp.asarray(expm, jnp.bfloat16)


def _strict_upper(n):
    r = np.arange(n)
    return jnp.asarray(r[:, None] < r[None, :], jnp.bfloat16)


def kernel(x_prompt, x_sample, cache_k, cache_v, state_conv, norm_mix_g, w_in, conv_w, q_norm_g,
           k_norm_g, attn_sinks, w_conv_out, w_attn_out, w_out, norm_ffn_g, w_group, b_group,
           w_router, b_router, w_gate, w_up, w_down):
    nb, seq, _ = x_prompt.shape
    ns, sseq, _ = x_sample.shape
    l = 0
    seg, expm = _segment_matrices()
    pad = RT_ROWS - N_EXPERTS - N_GROUPS
    wr = jnp.concatenate([w_router[l].T, w_group[l].T, jnp.zeros((pad, D), jnp.float32)], axis=0)
    br = jnp.concatenate([b_router[l], b_group[l], jnp.zeros((pad,), jnp.float32)]).reshape(RT_ROWS, 1)
    wts = (norm_mix_g[l].reshape(1, D), _bf(w_in[l]), conv_w[l],
           jnp.tile(q_norm_g[l], N_HEADS).reshape(1, D), jnp.tile(k_norm_g[l], N_KV).reshape(1, LANES),
           seg, expm, _bf(w_conv_out[l]), _bf(w_attn_out[l]), _bf(w_out[l]),
           norm_ffn_g[l].reshape(1, D), _bf(wr), br)
    sinks = attn_sinks[l]
    n_s = ns * sseq

    sink2 = sinks * LOG2E
    sink_hi = _bf(sink2).astype(jnp.float32)
    q_extra = jnp.concatenate(
        [jnp.ones((N_HEADS, 1), jnp.float32), sink_hi[:, None], (sink2 - sink_hi)[:, None],
         jnp.zeros((N_HEADS, HEAD_DIM - Q_FEATS), jnp.float32)], axis=1).reshape(N_HEADS, 1, HEAD_DIM)
    x1p, h2pp, rtp, cntp, kwp, vwp, cstp, wgu, wd = _prompt_mixer(
        x_prompt, (w_gate[l], w_up[l], w_down[l]), q_extra, wts + (_strict_upper(TOK_TILE),), tile=TOK_TILE, blk_tiles=BLOCK_TILES)
    x1s, h2ps, rts, cnts, kws, vws, csts = _sample_mixer(
        x_sample, cache_k[l].reshape(ns, WINDOW, LANES), cache_v[l].reshape(ns, WINDOW, LANES),
        state_conv[l], q_extra, wts + (_strict_upper(n_s),))

    yp = _moe(*_sorted_positions(rtp, cntp, TOK_TILE * BLOCK_TILES, MOE_TILE), h2pp, x1p.reshape(nb * seq, D), wgu, wd,
              tile=TOK_TILE, n_tiles=BLOCK_TILES, mt=MOE_TILE)
    ys = _moe(*_sorted_positions(rts, cnts, n_s, SAMPLE_MOE_TILE), h2ps, x1s, wgu, wd, tile=n_s, n_tiles=1,
              mt=SAMPLE_MOE_TILE)

    kv_shape = lambda n: (1, n, WINDOW, N_KV, HEAD_DIM)
    return (yp.reshape(nb, seq, D), ys.reshape(ns, sseq, D),
            kwp.reshape(kv_shape(nb)), vwp.reshape(kv_shape(nb)), cstp.reshape(1, nb, CONV_HIST, D),
            kws.reshape(kv_shape(ns)), vws.reshape(kv_shape(ns)), csts.reshape(1, ns, CONV_HIST, D))
```
